```python
import jax, jax.numpy as jnp
from jax import lax
import numpy as np

D_MODEL = 2048
BATCH = 4
SEQ = 2048
DEPTH = 1
DEC_BATCH = 32
DEC_SEQ = 1
PAST_LEN = 16384
PAGE_SIZE = 128

DN_HEADS = 8
DN_DK = 128
DN_DV = 128
DN_CONV = 4
DN_CHUNK = 64
SW_HEADS = 16
SW_KV_HEADS = 2
SW_HD = 64
SW_GROUP = SW_HEADS // SW_KV_HEADS
WINDOW = 128
N_EXPERTS = 32
TOP_K = 4
D_EXPERT = D_MODEL
SWIGLU_ALPHA = 1.702
SWIGLU_LIMIT = 7.0
MOE_BLOCK = 128
EPS = 1e-6

DN_QK = DN_HEADS * DN_DK
DN_VW = DN_HEADS * DN_DV
DN_CONV_CH = 2 * DN_QK + DN_VW
SW_QW = SW_HEADS * SW_HD
SW_KVW = SW_KV_HEADS * SW_HD
IN_SIZES = (DN_CONV_CH, DN_VW, DN_HEADS, DN_HEADS, SW_QW, SW_KVW, SW_KVW, D_MODEL, D_MODEL)

kernel_name = "hybrid_gdn_swa_moe_adaln_step"


def _rmsnorm(x, w):
    xf = x.astype(jnp.float32)
    y = xf * lax.rsqrt(jnp.mean(xf * xf, axis=-1, keepdims=True) + EPS)
    return (y * w.astype(jnp.float32)).astype(x.dtype)


def _l2norm(x):
    return x * lax.rsqrt(jnp.sum(x * x, axis=-1, keepdims=True) + EPS)


def _alibi_slopes():
    return jnp.exp2(-8.0 * jnp.arange(1, SW_HEADS + 1, dtype=jnp.float32) / SW_HEADS)


def _short_conv(u, prev, w):
    t = u.shape[1]
    full = jnp.concatenate([prev.astype(u.dtype), u], axis=1)
    y = sum(full[:, i:i + t] * w[i] for i in range(DN_CONV))
    return jax.nn.silu(y), full[:, -(DN_CONV - 1):]


def _gated_delta(q, k, v, beta, g, s0):
    b, t = q.shape[:2]
    nc = -(-t // DN_CHUNK)
    pad = nc * DN_CHUNK - t

    def prep(a):
        a = jnp.pad(a, [(0, 0), (0, pad)] + [(0, 0)] * (a.ndim - 2))
        a = a.reshape((b, nc, DN_CHUNK) + a.shape[2:])
        return jnp.moveaxis(a, 3, 1)

    q, k, v, beta, g = (prep(a) for a in (q, k, v, beta, g))
    gc = jnp.cumsum(g, axis=-1)
    idx = jnp.arange(DN_CHUNK)
    causal = idx[:, None] >= idx[None, :]
    strict = idx[:, None] > idx[None, :]
    decay = jnp.exp(jnp.where(causal, gc[..., :, None] - gc[..., None, :], -jnp.inf))
    kk = jnp.einsum('bhnid,bhnjd->bhnij', k, k)
    a_low = jnp.where(strict, beta[..., :, None] * kk * decay, 0.0)
    m = a_low + jnp.eye(DN_CHUNK, dtype=a_low.dtype)
    rhs = jnp.concatenate([v * beta[..., None], k * (beta * jnp.exp(gc))[..., None]], axis=-1)
    sol = lax.linalg.triangular_solve(m, rhs, left_side=True, lower=True)
    value, kcum = sol[..., :DN_DV], sol[..., DN_DV:]
    intra = jnp.einsum('bhnid,bhnjd->bhnij', q, k) * decay
    q_dec = q * jnp.exp(gc)[..., None]
    k_dec = k * jnp.exp(gc[..., -1:] - gc)[..., None]
    g_last = jnp.exp(gc[..., -1])
    xs = tuple(jnp.moveaxis(a, 2, 0) for a in (value, kcum, intra, q_dec, k_dec, g_last))

    def step(s, inp):
        val, kc, it, qd, kd, gl = inp
        v_new = val - jnp.einsum('bhik,bhkv->bhiv', kc, s)
        o = jnp.einsum('bhik,bhkv->bhiv', qd, s) + jnp.einsum('bhij,bhjv->bhiv', it, v_new)
        s = s * gl[..., None, None] + jnp.einsum('bhjk,bhjv->bhkv', kd, v_new)
        return s, o

    s_final, o = lax.scan(step, s0, xs)
    o = jnp.transpose(o, (1, 0, 3, 2, 4)).reshape(b, nc * DN_CHUNK, DN_HEADS, DN_DV)[:, :t]
    return o, s_final


def _deltanet(qkv_raw, z, b_raw, a_raw, conv_prev, s0, conv_w, a_log, dt_bias, norm_w):
    bsz, t = qkv_raw.shape[:2]
    qkv, conv_new = _short_conv(qkv_raw, conv_prev, conv_w)
    qkv = qkv.astype(jnp.float32)
    q, k, v = jnp.split(qkv, [DN_QK, 2 * DN_QK], axis=-1)
    q = _l2norm(q.reshape(bsz, t, DN_HEADS, DN_DK)) * (DN_DK ** -0.5)
    k = _l2norm(k.reshape(bsz, t, DN_HEADS, DN_DK))
    v = v.reshape(bsz, t, DN_HEADS, DN_DV)
    beta = jax.nn.sigmoid(b_raw.astype(jnp.float32))
    g = -jnp.exp(a_log.astype(jnp.float32)) * jax.nn.softplus(a_raw.astype(jnp.float32) + dt_bias.astype(jnp.float32))
    o, s_new = _gated_delta(q, k, v, beta, g, s0.astype(jnp.float32))
    o = _rmsnorm(o, norm_w) * jax.nn.silu(z.reshape(bsz, t, DN_HEADS, DN_DV).astype(jnp.float32))
    return o.reshape(bsz, t, DN_VW), conv_new, s_new


def _swa_attend(q, k, v, qpos, kpos, sinks, slopes):
    s = jnp.einsum('bnqhgd,bnkhd->bnhgqk', q, k).astype(jnp.float32) * (SW_HD ** -0.5)
    dist = qpos[:, :, None] - kpos[:, None, :]
    valid = (dist >= 0) & (dist < WINDOW) & (kpos[:, None, :] >= 0)
    slope = slopes.reshape(SW_KV_HEADS, SW_GROUP)[:, :, None, None]
    s = s - slope * dist.astype(jnp.float32)[:, None, None]
    s = jnp.where(valid[:, None, None], s, -jnp.inf)
    sink = jnp.broadcast_to(sinks.astype(jnp.float32).reshape(SW_KV_HEADS, SW_GROUP, 1, 1), s.shape[:-1] + (1,))
    p = jax.nn.softmax(jnp.concatenate([s, sink], axis=-1), axis=-1)[..., :-1]
    return jnp.einsum('bnhgqk,bnkhd->bnqhgd', p.astype(v.dtype), v)


def _swa_prompt(q, k, v, sinks, slopes):
    b, t = q.shape[:2]
    nb = t // WINDOW
    qb = q.reshape(b, nb, WINDOW, SW_KV_HEADS, SW_GROUP, SW_HD)

    def band(a):
        a = jnp.pad(a, ((0, 0), (WINDOW, 0), (0, 0), (0, 0)))
        a = a.reshape(b, nb + 1, WINDOW, SW_KV_HEADS, SW_HD)
        return jnp.concatenate([a[:, :-1], a[:, 1:]], axis=2)

    qpos = jnp.arange(t).reshape(nb, WINDOW)
    kpos = qpos[:, :1] - WINDOW + jnp.arange(2 * WINDOW)[None, :]
    o = _swa_attend(qb, band(k), band(v), qpos, kpos, sinks, slopes)
    return o.reshape(b, t, SW_QW)


def _swa_sample(q, k, v, k_buf, v_buf, sinks, slopes):
    b, t = q.shape[:2]
    w = k_buf.shape[1]
    k_all = jnp.concatenate([k_buf.astype(k.dtype), k], axis=1)
    v_all = jnp.concatenate([v_buf.astype(v.dtype), v], axis=1)
    qpos = (PAST_LEN + jnp.arange(t))[None]
    kpos = (PAST_LEN - w + jnp.arange(w + t))[None]
    o = _swa_attend(q.reshape(b, 1, t, SW_KV_HEADS, SW_GROUP, SW_HD), k_all[:, None], v_all[:, None],
                    qpos, kpos, sinks, slopes)
    return o.reshape(b, t, SW_QW), k_all[:, -w:], v_all[:, -w:]


def _clamped_swiglu(gu):
    glu, lin = jnp.split(gu, 2, axis=-1)
    glu = jnp.minimum(glu, SWIGLU_LIMIT)
    lin = jnp.clip(lin, -SWIGLU_LIMIT, SWIGLU_LIMIT)
    return glu * jax.nn.sigmoid(SWIGLU_ALPHA * glu) * (lin + 1.0)


def _moe(h, router_w, router_b, w_gate_up, b_gate_up, w_down, b_down):
    n = h.shape[0]
    logits = (h @ router_w + router_b).astype(jnp.float32)
    top_val, top_idx = lax.top_k(logits, TOP_K)
    top_w = jax.nn.softmax(top_val, axis=-1)
    n_assign = n * TOP_K
    n_blocks = -(-(n_assign + N_EXPERTS * (MOE_BLOCK - 1)) // MOE_BLOCK)
    n_rows = n_blocks * MOE_BLOCK
    flat_e = top_idx.reshape(-1)
    order = jnp.argsort(flat_e)
    sorted_e = flat_e[order]
    sorted_tok = (order // TOP_K).astype(jnp.int32)
    sorted_w = top_w.reshape(-1)[order]
    counts = jnp.bincount(flat_e, length=N_EXPERTS)
    padded = (counts + MOE_BLOCK - 1) // MOE_BLOCK * MOE_BLOCK
    start = jnp.cumsum(counts) - counts
    pstart = jnp.cumsum(padded) - padded
    dest = pstart[sorted_e] + jnp.arange(n_assign) - start[sorted_e]
    row_tok = jnp.zeros((n_rows,), jnp.int32).at[dest].set(sorted_tok)
    row_w = jnp.zeros((n_rows,), jnp.float32).at[dest].set(sorted_w)
    row_e = jnp.zeros((n_rows,), jnp.int32).at[dest].set(sorted_e.astype(jnp.int32))
    blk_e = row_e[::MOE_BLOCK]
    xb = h[row_tok].reshape(n_blocks, MOE_BLOCK, h.shape[-1])

    def expert_block(args):
        xblk, e = args
        gu = xblk @ w_gate_up[e] + b_gate_up[e]
        return _clamped_swiglu(gu) @ w_down[e] + b_down[e]

    yb = lax.map(expert_block, (xb, blk_e)).reshape(n_rows, h.shape[-1])
    out = jnp.zeros(h.shape, jnp.float32).at[row_tok].add(row_w[:, None] * yb.astype(jnp.float32))
    return out.astype(h.dtype)


def _layer(x, c, conv_prev, s0, k_buf, v_buf, w_ada, b_ada, ln1_w, w_in, conv_w, dn_a_log, dn_dt_bias,
           dn_norm_w, sw_q_norm_w, sw_k_norm_w, sw_sinks, w_branch_a, w_branch_b, w_out, ln2_w,
           router_w, router_b, w_gate_up, b_gate_up, w_down, b_down):
    b, t, _ = x.shape
    prompt = conv_prev is None
    if prompt:
        conv_prev = jnp.zeros((b, DN_CONV - 1, DN_CONV_CH), x.dtype)
        s0 = jnp.zeros((b, DN_HEADS, DN_DK, DN_DV), jnp.float32)
    mod = jax.nn.silu(c) @ w_ada + b_ada
    sh1, sc1, gt1, sh2, sc2, gt2 = [m[:, None] for m in jnp.split(mod, 6, axis=-1)]

    h = _rmsnorm(x, ln1_w) * (1 + sc1) + sh1
    proj = h @ w_in
    qkv_raw, dz, db, da, sq, sk, sv, ga, gb = jnp.split(proj, np.cumsum(IN_SIZES)[:-1].tolist(), axis=-1)

    ya, conv_new, s_new = _deltanet(qkv_raw, dz, db, da, conv_prev, s0, conv_w, dn_a_log, dn_dt_bias, dn_norm_w)

    q = _rmsnorm(sq.reshape(b, t, SW_HEADS, SW_HD), sw_q_norm_w)
    k = _rmsnorm(sk.reshape(b, t, SW_KV_HEADS, SW_HD), sw_k_norm_w)
    v = sv.reshape(b, t, SW_KV_HEADS, SW_HD)
    slopes = _alibi_slopes()
    if prompt:
        yb = _swa_prompt(q, k, v, sw_sinks, slopes)
        k_new, v_new = k[:, -WINDOW:], v[:, -WINDOW:]
    else:
        yb, k_new, v_new = _swa_sample(q, k, v, k_buf, v_buf, sw_sinks, slopes)

    merged = (jax.nn.sigmoid(ga) * (ya.astype(x.dtype) @ w_branch_a)
              + jax.nn.sigmoid(gb) * (yb @ w_branch_b))
    x = x + gt1 * (merged @ w_out)

    h2 = _rmsnorm(x, ln2_w) * (1 + sc2) + sh2
    y = _moe(h2.reshape(b * t, D_MODEL), router_w, router_b, w_gate_up, b_gate_up, w_down, b_down)
    x = x + gt2 * y.reshape(b, t, D_MODEL)
    return x, conv_new, s_new.astype(x.dtype), k_new, v_new


def setup_inputs(seed: int = 0) -> dict:
    key = jax.random.key(seed)
    ks = iter(jax.random.split(key, 40))

    def nrm(shape, scale):
        return jax.random.normal(next(ks), shape, jnp.float32) * scale

    L = DEPTH
    win_buf = min(WINDOW, PAST_LEN)
    in_width = sum(IN_SIZES)
    return {
        "x_prompt": nrm((BATCH, SEQ, D_MODEL), 1.0),
        "x_sample": nrm((DEC_BATCH, DEC_SEQ, D_MODEL), 1.0),
        "state_conv": nrm((L, DEC_BATCH, DN_CONV - 1, DN_CONV_CH), 1.0),
        "state_delta": nrm((L, DEC_BATCH, DN_HEADS, DN_DK, DN_DV), 0.2),
        "cache_swa_k": nrm((L, DEC_BATCH, win_buf, SW_KV_HEADS, SW_HD), 1.0),
        "cache_swa_v": nrm((L, DEC_BATCH, win_buf, SW_KV_HEADS, SW_HD), 1.0),
        "c_prompt": nrm((BATCH, D_MODEL), 1.0),
        "c_sample": nrm((DEC_BATCH, D_MODEL), 1.0),
        "w_ada": nrm((L, D_MODEL, 6 * D_MODEL), 0.5 * D_MODEL ** -0.5),
        "b_ada": nrm((L, 6 * D_MODEL), 0.02),
        "ln1_w": 1.0 + nrm((L, D_MODEL), 0.02),
        "w_in": nrm((L, D_MODEL, in_width), D_MODEL ** -0.5),
        "conv_w": nrm((L, DN_CONV, DN_CONV_CH), DN_CONV ** -0.5),
        "dn_a_log": jnp.log(jax.random.uniform(next(ks), (L, DN_HEADS), jnp.float32, 1.0, 16.0)),
        "dn_dt_bias": nrm((L, DN_HEADS), 0.5),
        "dn_norm_w": 1.0 + nrm((L, DN_DV), 0.02),
        "sw_q_norm_w": 1.0 + nrm((L, SW_HD), 0.02),
        "sw_k_norm_w": 1.0 + nrm((L, SW_HD), 0.02),
        "sw_sinks": nrm((L, SW_HEADS), 1.0),
        "w_branch_a": nrm((L, DN_VW, D_MODEL), DN_VW ** -0.5),
        "w_branch_b": nrm((L, SW_QW, D_MODEL), SW_QW ** -0.5),
        "w_out": nrm((L, D_MODEL, D_MODEL), D_MODEL ** -0.5),
        "ln2_w": 1.0 + nrm((L, D_MODEL), 0.02),
        "router_w": nrm((L, D_MODEL, N_EXPERTS), D_MODEL ** -0.5),
        "router_b": nrm((L, N_EXPERTS), 0.01),
        "w_gate_up": nrm((L, N_EXPERTS, D_MODEL, 2 * D_EXPERT), D_MODEL ** -0.5),
        "b_gate_up": nrm((L, N_EXPERTS, 2 * D_EXPERT), 0.02),
        "w_down": nrm((L, N_EXPERTS, D_EXPERT, D_MODEL), D_EXPERT ** -0.5),
        "b_down": nrm((L, N_EXPERTS, D_MODEL), 0.02),
    }


def reference(x_prompt, x_sample, state_conv, state_delta, cache_swa_k, cache_swa_v, c_prompt, c_sample,
              w_ada, b_ada, ln1_w, w_in, conv_w, dn_a_log, dn_dt_bias, dn_norm_w, sw_q_norm_w, sw_k_norm_w,
              sw_sinks, w_branch_a, w_branch_b, w_out, ln2_w, router_w, router_b, w_gate_up, b_gate_up,
              w_down, b_down):
    xp, xs = x_prompt, x_sample
    conv_p, conv_s, delta_p, delta_s, kp, ks, vp, vs = [], [], [], [], [], [], [], []
    for l in range(DEPTH):
        lw = (w_ada[l], b_ada[l], ln1_w[l], w_in[l], conv_w[l], dn_a_log[l], dn_dt_bias[l], dn_norm_w[l],
              sw_q_norm_w[l], sw_k_norm_w[l], sw_sinks[l], w_branch_a[l], w_branch_b[l], w_out[l], ln2_w[l],
              router_w[l], router_b[l], w_gate_up[l], b_gate_up[l], w_down[l], b_down[l])
        xp, cp, dp, kpl, vpl = _layer(xp, c_prompt, None, None, None, None, *lw)
        xs, cs, ds, ksl, vsl = _layer(xs, c_sample, state_conv[l], state_delta[l], cache_swa_k[l], cache_swa_v[l], *lw)
        conv_p.append(cp); conv_s.append(cs); delta_p.append(dp); delta_s.append(ds)
        kp.append(kpl); ks.append(ksl); vp.append(vpl); vs.append(vsl)
    return (xp, xs, jnp.stack(conv_p), jnp.stack(conv_s), jnp.stack(delta_p), jnp.stack(delta_s),
            jnp.stack(kp), jnp.stack(ks), jnp.stack(vp), jnp.stack(vs))
```

```python
import functools

import jax
import jax.numpy as jnp
import numpy as np
from jax import lax
from jax.experimental import pallas as pl
from jax.experimental.pallas import tpu as pltpu

F32 = jnp.float32
BF16 = jnp.bfloat16

D_MODEL = 2048
PAST_LEN = 16384
DN_HEADS = 8
DN_DK = 128
DN_DV = 128
DN_CONV = 4
SW_HEADS = 16
SW_KV_HEADS = 2
SW_HD = 64
SW_GROUP = SW_HEADS // SW_KV_HEADS
WINDOW = 128
N_EXPERTS = 32
TOP_K = 4
SWIGLU_ALPHA = 1.702
SWIGLU_LIMIT = 7.0
EPS = 1e-6

DN_QK = DN_HEADS * DN_DK
DN_VW = DN_HEADS * DN_DV
DN_CONV_CH = 2 * DN_QK + DN_VW
SW_QW = SW_HEADS * SW_HD
SW_KVW = SW_KV_HEADS * SW_HD

LANE = 128
C_QKV = 0
C_Z = DN_CONV_CH
C_SQ = C_Z + DN_VW
C_GA = C_SQ + SW_QW
C_GB = C_GA + D_MODEL
C_SK = C_GB + D_MODEL
C_SV = C_SK + SW_KVW
C_BA = C_SV + SW_KVW
PROJ_W = 10240

GDN_GROUP = 256
GDN_CHUNK = 64
MOE_ROWS = 256
MOE_SB_BLOCKS = 6
MOE_TF = 256
VMEM_LIMIT = 56 * 1024 * 1024


def _cp(sem, vmem=VMEM_LIMIT):
    return pltpu.CompilerParams(dimension_semantics=sem, vmem_limit_bytes=vmem)


def _dot(a, b):
    return jnp.dot(a.astype(BF16), b.astype(BF16), preferred_element_type=F32)


def _dot_nt(a, b):
    return lax.dot_general(a.astype(BF16), b.astype(BF16), (((1,), (1,)), ((), ())), preferred_element_type=F32)


def _split(a):
    hi = a.astype(BF16)
    lo = (a - hi.astype(F32)).astype(BF16)
    return hi, lo


def _dot3(a, b):
    ah, al = _split(a)
    bh, bl = _split(b)
    d = functools.partial(jnp.dot, preferred_element_type=F32)
    return d(ah, bh) + (d(ah, bl) + d(al, bh))


def _dot3_nt(a, b):
    ah, al = _split(a)
    bh, bl = _split(b)
    d = functools.partial(lax.dot_general, dimension_numbers=(((1,), (1,)), ((), ())), preferred_element_type=F32)
    return d(ah, bh) + (d(ah, bl) + d(al, bh))


def _dot_exact_lhs01(m01, b):
    b1 = b.astype(BF16)
    r = b - b1.astype(F32)
    b2 = r.astype(BF16)
    b3 = (r - b2.astype(F32)).astype(BF16)
    d = functools.partial(jnp.dot, preferred_element_type=F32)
    m = m01.astype(BF16)
    return d(m, b1) + (d(m, b2) + d(m, b3))


def _sigmoid(x):
    return 1.0 / (1.0 + jnp.exp(-x))


def _silu(x):
    return x * _sigmoid(x)


def _softplus(x):
    return jnp.maximum(x, 0.0) + jnp.log(1.0 + jnp.exp(-jnp.abs(x)))


def _ada_kernel(c_ref, w_ref, b_ref, o_ref):
    o_ref[...] = _dot(_silu(c_ref[...]), w_ref[...]) + b_ref[...]


def _ada_mod(c_all, w_ada, b_ada):
    m = c_all.shape[0]
    n = w_ada.shape[1]
    tn = 1024
    return pl.pallas_call(
        _ada_kernel,
        grid=(n // tn,),
        in_specs=[
            pl.BlockSpec((m, D_MODEL), lambda j: (0, 0)),
            pl.BlockSpec((D_MODEL, tn), lambda j: (0, j)),
            pl.BlockSpec((1, tn), lambda j: (0, j)),
        ],
        out_specs=pl.BlockSpec((m, tn), lambda j: (0, j)),
        out_shape=jax.ShapeDtypeStruct((m, n), F32),
        compiler_params=_cp(("arbitrary",)),
        name="ada_mod",
    )(c_all, w_ada, b_ada.reshape(1, n))


def _norm_mod(x, lnw, sc, sh):
    y = x * lax.rsqrt(jnp.mean(x * x, axis=-1, keepdims=True) + EPS)
    return (y * lnw) * (1.0 + sc) + sh


def _inproj_kernel(x_ref, lnw_ref, sc_ref, sh_ref, w_ref, o_ref, h_scr):
    @pl.when(pl.program_id(1) == 0)
    def _():
        h_scr[...] = _norm_mod(x_ref[...], lnw_ref[...], sc_ref[...], sh_ref[...]).astype(BF16)

    o_ref[...] = jnp.dot(h_scr[...], w_ref[...], preferred_element_type=F32)


def _in_proj(x2d, lnw, sc, sh, w_bf16, rows_per_mod, tm):
    m = x2d.shape[0]
    tn = 1024
    if rows_per_mod == 1:
        mod_spec = pl.BlockSpec((tm, D_MODEL), lambda i, j: (i, 0))
        sc, sh = sc.reshape(m, D_MODEL), sh.reshape(m, D_MODEL)
    else:
        assert rows_per_mod % tm == 0
        mod_spec = pl.BlockSpec((None, 1, D_MODEL), lambda i, j: (i // (rows_per_mod // tm), 0, 0))
    return pl.pallas_call(
        _inproj_kernel,
        grid=(m // tm, PROJ_W // tn),
        in_specs=[
            pl.BlockSpec((tm, D_MODEL), lambda i, j: (i, 0)),
            pl.BlockSpec((1, D_MODEL), lambda i, j: (0, 0)),
            mod_spec,
            mod_spec,
            pl.BlockSpec((D_MODEL, tn), lambda i, j: (0, j)),
        ],
        out_specs=pl.BlockSpec((tm, tn), lambda i, j: (i, j)),
        out_shape=jax.ShapeDtypeStruct((m, PROJ_W), F32),
        scratch_shapes=[pltpu.VMEM((tm, D_MODEL), BF16)],
        compiler_params=_cp(("arbitrary", "arbitrary")),
        name="in_proj",
    )(x2d, lnw.reshape(1, D_MODEL), sc, sh, w_bf16)


def _tri_masks(n, chunk):
    r = lax.broadcasted_iota(jnp.int32, (n, n), 0)
    c = lax.broadcasted_iota(jnp.int32, (n, n), 1)
    same = (r // chunk) == (c // chunk)
    return same, same & (r >= c), same & (r > c)


def _gates_kernel(ba_ref, alog_ref, dtb_ref, beta_ref, gc_ref, eg_ref, ek_ref, el_ref, gcrow_ref):
    same, causal, _ = _tri_masks(GDN_GROUP, GDN_CHUNK)
    lower01 = jnp.where(causal, 1.0, 0.0)
    ones01 = jnp.where(same, 1.0, 0.0)
    nega = -jnp.exp(alog_ref[...])
    dtb = dtb_ref[...]
    t = ba_ref.shape[0]

    def body(i, carry):
        r0 = pl.multiple_of(i * GDN_GROUP, GDN_GROUP)
        x = ba_ref[pl.ds(r0, GDN_GROUP), :]
        g = nega * _softplus(x + dtb)
        gc = _dot_exact_lhs01(lower01, g)
        gl = _dot_exact_lhs01(ones01, g)
        beta_ref[pl.ds(r0, GDN_GROUP), :] = _sigmoid(x)
        gc_ref[pl.ds(r0, GDN_GROUP), :] = gc
        eg_ref[pl.ds(r0, GDN_GROUP), :] = jnp.exp(gc)
        ek_ref[pl.ds(r0, GDN_GROUP), :] = jnp.exp(gl - gc)
        el_ref[pl.ds(r0, GDN_GROUP), :] = jnp.exp(gl)
        gct = gc.T
        for h in range(DN_HEADS):
            gcrow_ref[h, :, pl.ds(r0, GDN_GROUP)] = gct[DN_HEADS + h:DN_HEADS + h + 1, :]
        return carry

    lax.fori_loop(0, t // GDN_GROUP, body, 0)


def _gdn_gates(proj3, alog_lane, dtb_lane):
    b, t, _ = proj3.shape
    col = pl.BlockSpec((None, t, LANE), lambda i: (i, 0, 0))
    shp = jax.ShapeDtypeStruct((b, t, LANE), F32)
    return pl.pallas_call(
        _gates_kernel,
        grid=(b,),
        in_specs=[
            pl.BlockSpec((None, t, LANE), lambda i: (i, 0, C_BA // LANE)),
            pl.BlockSpec((1, LANE), lambda i: (0, 0)),
            pl.BlockSpec((1, LANE), lambda i: (0, 0)),
        ],
        out_specs=[col, col, col, col, col, pl.BlockSpec((None, DN_HEADS, 1, t), lambda i: (i, 0, 0, 0))],
        out_shape=[shp, shp, shp, shp, shp, jax.ShapeDtypeStruct((b, DN_HEADS, 1, t), F32)],
        compiler_params=_cp(("arbitrary",)),
        name="gdn_gates",
    )(proj3, alog_lane, dtb_lane)


def _l2norm(x):
    return x * lax.rsqrt(jnp.sum(x * x, axis=-1, keepdims=True) + EPS)


def _gdn_kernel(q_ref, k_ref, v_ref, z_ref, beta_ref, gc_ref, eg_ref, ek_ref, el_ref, gcrow_ref,
                cwq_ref, cwk_ref, cwv_ref, nw_ref, o_ref, s_ref, pad_scr, qn_scr, kn_scr, vn_scr, oacc_scr, s_scr):
    t = q_ref.shape[0]
    h = pl.program_id(1)
    pad = 8

    def conv_silu(u_ref, cw_ref):
        pad_scr[0:pad, :] = jnp.zeros((pad, LANE), F32)
        pad_scr[pad:pad + t, :] = u_ref[...]
        y = cw_ref[DN_CONV - 1:DN_CONV, :] * pad_scr[pad:pad + t, :]
        for i in range(DN_CONV - 1):
            off = pad - (DN_CONV - 1) + i
            y = y + cw_ref[i:i + 1, :] * pad_scr[off:off + t, :]
        return _silu(y)

    qn_scr[...] = _l2norm(conv_silu(q_ref, cwq_ref)) * (DN_DK ** -0.5)
    kn_scr[...] = _l2norm(conv_silu(k_ref, cwk_ref))
    vn_scr[...] = conv_silu(v_ref, cwv_ref)
    s_scr[...] = jnp.zeros((DN_DK, DN_DV), F32)

    n = GDN_GROUP
    c = GDN_CHUNK
    _, causal, strict = _tri_masks(n, c)
    rr = lax.broadcasted_iota(jnp.int32, (n, n), 0)
    cc = lax.broadcasted_iota(jnp.int32, (n, n), 1)
    eye = jnp.where(rr == cc, 1.0, 0.0)
    lane = lax.broadcasted_iota(jnp.int32, (n, LANE), 1)
    sel_b = lane == h
    sel_g = lane == h + DN_HEADS

    def pick(ref, r0, sel):
        return jnp.sum(jnp.where(sel, ref[pl.ds(r0, n), :], 0.0), axis=-1, keepdims=True)

    def body(i, carry):
        r0 = pl.multiple_of(i * n, n)
        q = qn_scr[pl.ds(r0, n), :]
        k = kn_scr[pl.ds(r0, n), :]
        v = vn_scr[pl.ds(r0, n), :]
        beta = pick(beta_ref, r0, sel_b)
        gc = pick(gc_ref, r0, sel_g)
        eg = pick(eg_ref, r0, sel_g)
        ek = pick(ek_ref, r0, sel_g)
        el = pick(el_ref, r0, sel_g)
        gcrow = gcrow_ref[:, pl.ds(r0, n)]
        decay = jnp.where(causal, jnp.exp(gc - gcrow), 0.0)
        a_low = jnp.where(strict, beta * _dot3_nt(k, k) * decay, 0.0)
        p = -a_low
        tinv = eye + p
        for _ in range(5):
            p = _dot3(p, p)
            tinv = tinv + _dot3(tinv, p)
        rhs = jnp.concatenate([v * beta, k * (beta * eg)], axis=1)
        sol = _dot3(tinv, rhs)
        value = sol[:, :DN_DV]
        kcum = sol[:, DN_DV:]
        intra = _dot3_nt(q, k) * decay
        q_dec = q * eg
        k_dec = k * ek
        for j in range(n // c):
            lo, hi = j * c, (j + 1) * c
            s = s_scr[...]
            r = _dot3(jnp.concatenate([kcum[lo:hi], q_dec[lo:hi]], axis=0), s)
            v_new = value[lo:hi] - r[:c]
            parts = []
            if lo:
                parts.append(jnp.zeros((lo, DN_DV), F32))
            parts.append(v_new)
            if hi < n:
                parts.append(jnp.zeros((n - hi, DN_DV), F32))
            o = r[c:] + _dot3(intra[lo:hi], jnp.concatenate(parts, axis=0) if len(parts) > 1 else v_new)
            oacc_scr[pl.ds(r0 + lo, c), :] = o
            s_scr[...] = s * el[lo:lo + 1] + _dot3(k_dec[lo:hi].T, v_new)
        return carry

    lax.fori_loop(0, t // n, body, 0)
    o = oacc_scr[...]
    y = o * lax.rsqrt(jnp.mean(o * o, axis=-1, keepdims=True) + EPS)
    o_ref[...] = (y * nw_ref[...]) * _silu(z_ref[...])
    s_ref[...] = s_scr[...]


def _gdn_prompt(proj3, gates, conv_w, norm_w):
    b, t, _ = proj3.shape
    beta, gc, eg, ek, el, gcrow = gates
    hd = DN_HEADS

    def colspec(base):
        return pl.BlockSpec((None, t, LANE), lambda i, j, base=base: (i, 0, base + j))

    gate = pl.BlockSpec((None, t, LANE), lambda i, j: (i, 0, 0))

    def cwspec(base):
        return pl.BlockSpec((DN_CONV, LANE), lambda i, j, base=base: (0, base + j))

    return pl.pallas_call(
        _gdn_kernel,
        grid=(b, hd),
        in_specs=[
            colspec(0), colspec(hd), colspec(2 * hd), colspec(C_Z // LANE),
            gate, gate, gate, gate, gate,
            pl.BlockSpec((None, None, 1, t), lambda i, j: (i, j, 0, 0)),
            cwspec(0), cwspec(hd), cwspec(2 * hd),
            pl.BlockSpec((1, DN_DV), lambda i, j: (0, 0)),
        ],
        out_specs=[
            pl.BlockSpec((None, t, DN_DV), lambda i, j: (i, 0, j)),
            pl.BlockSpec((None, None, DN_DK, DN_DV), lambda i, j: (i, j, 0, 0)),
        ],
        out_shape=[
            jax.ShapeDtypeStruct((b, t, DN_VW), F32),
            jax.ShapeDtypeStruct((b, hd, DN_DK, DN_DV), F32),
        ],
        scratch_shapes=[
            pltpu.VMEM((t + 8, LANE), F32),
            pltpu.VMEM((t, LANE), F32),
            pltpu.VMEM((t, LANE), F32),
            pltpu.VMEM((t, LANE), F32),
            pltpu.VMEM((t, LANE), F32),
            pltpu.VMEM((DN_DK, DN_DV), F32),
        ],
        compiler_params=_cp(("arbitrary", "arbitrary")),
        name="gdn_prompt",
    )(proj3, proj3, proj3, proj3, beta, gc, eg, ek, el, gcrow, conv_w, conv_w, conv_w, norm_w.reshape(1, DN_DV))


def _gdn_step_kernel(p_ref, cprev_ref, s_ref, cw_ref, alog_ref, dtb_ref, nw_ref, o_ref, cnew_ref, snew_ref):
    u = p_ref[:, C_QKV:C_QKV + DN_CONV_CH]
    prev = cprev_ref[...]
    y = cw_ref[DN_CONV - 1:DN_CONV, :] * u
    for i in range(DN_CONV - 1):
        y = y + cw_ref[i:i + 1, :] * prev[i:i + 1, :]
    y = _silu(y)
    cnew_ref[0:DN_CONV - 2, :] = prev[1:DN_CONV - 1, :]
    cnew_ref[DN_CONV - 2:DN_CONV - 1, :] = u
    ba = p_ref[:, C_BA:C_BA + LANE]
    beta_l = _sigmoid(ba)
    a_l = jnp.exp(-jnp.exp(alog_ref[...]) * _softplus(ba + dtb_ref[...]))
    lane = lax.broadcasted_iota(jnp.int32, (1, LANE), 1)
    row8 = lax.broadcasted_iota(jnp.int32, (8, LANE), 0)
    for h in range(DN_HEADS):
        q = _l2norm(y[:, h * DN_DK:(h + 1) * DN_DK]) * (DN_DK ** -0.5)
        k = _l2norm(y[:, DN_QK + h * DN_DK:DN_QK + (h + 1) * DN_DK])
        v = y[:, 2 * DN_QK + h * DN_DV:2 * DN_QK + (h + 1) * DN_DV]
        beta = jnp.sum(jnp.where(lane == h, beta_l, 0.0), axis=-1, keepdims=True)
        a = jnp.sum(jnp.where(lane == h + DN_HEADS, a_l, 0.0), axis=-1, keepdims=True)
        s = s_ref[h]
        kq = jnp.where(row8 == 0, k, jnp.where(row8 == 1, q, 0.0))
        r = _dot3(kq, s)
        v_new = beta * (v - a * r[0:1])
        o = a * r[1:2] + jnp.sum(q * k, axis=-1, keepdims=True) * v_new
        k8 = jnp.where(row8 == 0, k, 0.0)
        v8 = jnp.where(row8 == 0, v_new, 0.0)
        snew_ref[h] = s * a + _dot3(k8.T, v8)
        yo = o * lax.rsqrt(jnp.mean(o * o, axis=-1, keepdims=True) + EPS)
        z = p_ref[:, C_Z + h * DN_DV:C_Z + (h + 1) * DN_DV]
        o_ref[:, h * DN_DV:(h + 1) * DN_DV] = (yo * nw_ref[...]) * _silu(z)


def _gdn_step(proj_s, conv_prev, s0, conv_w, alog_lane, dtb_lane, norm_w):
    b = proj_s.shape[0]
    return pl.pallas_call(
        _gdn_step_kernel,
        grid=(b,),
        in_specs=[
            pl.BlockSpec((None, 1, PROJ_W), lambda i: (i, 0, 0)),
            pl.BlockSpec((None, DN_CONV - 1, DN_CONV_CH), lambda i: (i, 0, 0)),
            pl.BlockSpec((None, DN_HEADS, DN_DK, DN_DV), lambda i: (i, 0, 0, 0)),
            pl.BlockSpec((DN_CONV, DN_CONV_CH), lambda i: (0, 0)),
            pl.BlockSpec((1, LANE), lambda i: (0, 0)),
            pl.BlockSpec((1, LANE), lambda i: (0, 0)),
            pl.BlockSpec((1, DN_DV), lambda i: (0, 0)),
        ],
        out_specs=[
            pl.BlockSpec((None, 1, DN_VW), lambda i: (i, 0, 0)),
            pl.BlockSpec((None, DN_CONV - 1, DN_CONV_CH), lambda i: (i, 0, 0)),
            pl.BlockSpec((None, DN_HEADS, DN_DK, DN_DV), lambda i: (i, 0, 0, 0)),
        ],
        out_shape=[
            jax.ShapeDtypeStruct((b, 1, DN_VW), F32),
            jax.ShapeDtypeStruct((b, DN_CONV - 1, DN_CONV_CH), F32),
            jax.ShapeDtypeStruct((b, DN_HEADS, DN_DK, DN_DV), F32),
        ],
        compiler_params=_cp(("arbitrary",)),
        name="gdn_step",
    )(proj_s.reshape(b, 1, PROJ_W), conv_prev, s0, conv_w, alog_lane, dtb_lane, norm_w.reshape(1, DN_DV))


def _alibi_slope(h):
    return float(2.0 ** (-8.0 * (h + 1) / SW_HEADS))


def _head_rms(x, w):
    return (x * lax.rsqrt(jnp.mean(x * x, axis=-1, keepdims=True) + EPS)) * w


def _swa_kernel(sinks_ref, q_ref, kc_ref, kp_ref, vc_ref, vp_ref, qw_ref, kw_ref, o_ref, kn_ref):
    blk = pl.program_id(1)
    w = WINDOW
    qi = lax.broadcasted_iota(jnp.int32, (w, 2 * w), 0)
    kj = lax.broadcasted_iota(jnp.int32, (w, 2 * w), 1)
    dist = qi + w - kj
    valid = (dist >= 0) & (dist < w) & ((kj >= w) | (blk > 0))
    distf = dist.astype(F32)
    kc = kc_ref[...]
    kp = kp_ref[...]
    kbands, vbands = [], []
    for g in range(SW_KV_HEADS):
        sl = slice(g * SW_HD, (g + 1) * SW_HD)
        kcn = _head_rms(kc[:, sl], kw_ref[...])
        kn_ref[:, sl] = kcn
        kbands.append(jnp.concatenate([_head_rms(kp[:, sl], kw_ref[...]), kcn], axis=0))
        vbands.append(jnp.concatenate([vp_ref[:, sl], vc_ref[:, sl]], axis=0))
    for h in range(SW_HEADS):
        g = h // SW_GROUP
        qh = _head_rms(q_ref[:, h * SW_HD:(h + 1) * SW_HD], qw_ref[...])
        s = _dot_nt(qh, kbands[g]) * (SW_HD ** -0.5) - _alibi_slope(h) * distf
        s = jnp.where(valid, s, -jnp.inf)
        sink = sinks_ref[h]
        m = jnp.maximum(jnp.max(s, axis=-1, keepdims=True), sink)
        p = jnp.exp(s - m)
        den = jnp.sum(p, axis=-1, keepdims=True) + jnp.exp(sink - m)
        o_ref[:, h * SW_HD:(h + 1) * SW_HD] = _dot(p / den, vbands[g])


def _swa_prompt(proj3, sinks, qw, kw):
    b, t, _ = proj3.shape
    nb = t // WINDOW
    kcol, vcol = C_SK // LANE, C_SV // LANE

    def cur(col):
        return pl.BlockSpec((None, WINDOW, SW_KVW), lambda i, j, s, col=col: (i, j, col))

    def prev(col):
        return pl.BlockSpec((None, WINDOW, SW_KVW), lambda i, j, s, col=col: (i, jnp.maximum(j - 1, 0), col))

    return pl.pallas_call(
        _swa_kernel,
        grid_spec=pltpu.PrefetchScalarGridSpec(
            num_scalar_prefetch=1,
            grid=(b, nb),
            in_specs=[
                pl.BlockSpec((None, WINDOW, SW_QW), lambda i, j, s: (i, j, C_SQ // SW_QW)),
                cur(kcol), prev(kcol), cur(vcol), prev(vcol),
                pl.BlockSpec((1, SW_HD), lambda i, j, s: (0, 0)),
                pl.BlockSpec((1, SW_HD), lambda i, j, s: (0, 0)),
            ],
            out_specs=[
                pl.BlockSpec((None, WINDOW, SW_QW), lambda i, j, s: (i, j, 0)),
                pl.BlockSpec((None, WINDOW, SW_KVW), lambda i, j, s: (i, j, 0)),
            ],
        ),
        out_shape=[
            jax.ShapeDtypeStruct((b, t, SW_QW), F32),
            jax.ShapeDtypeStruct((b, t, SW_KVW), F32),
        ],
        compiler_params=_cp(("arbitrary", "arbitrary")),
        name="swa_prompt",
    )(sinks, proj3, proj3, proj3, proj3, proj3, qw.reshape(1, SW_HD), kw.reshape(1, SW_HD))


def _swa_step_kernel(sinks_ref, p_ref, kbuf_ref, vbuf_ref, qw_ref, kw_ref, o_ref, knew_ref, vnew_ref, kcat, vcat):
    w = kbuf_ref.shape[0]
    rows = kcat.shape[0]
    knew = p_ref[:, C_SK:C_SK + SW_KVW]
    vnew = p_ref[:, C_SV:C_SV + SW_KVW]
    kcat[...] = jnp.zeros(kcat.shape, F32)
    vcat[...] = jnp.zeros(vcat.shape, F32)
    kcat[0:w, :] = kbuf_ref[...]
    vcat[0:w, :] = vbuf_ref[...]
    for g in range(SW_KV_HEADS):
        sl = slice(g * SW_HD, (g + 1) * SW_HD)
        kcat[w:w + 1, sl] = _head_rms(knew[:, sl], kw_ref[...])
    vcat[w:w + 1, :] = vnew
    knew_ref[...] = kcat[1:w + 1, :]
    vnew_ref[...] = vcat[1:w + 1, :]
    j = lax.broadcasted_iota(jnp.int32, (rows, 1), 0)
    dist = w - j
    valid = (dist >= 0) & (dist < WINDOW)
    distf = dist.astype(F32)
    for h in range(SW_HEADS):
        g = h // SW_GROUP
        sl = slice(g * SW_HD, (g + 1) * SW_HD)
        qh = _head_rms(p_ref[:, C_SQ + h * SW_HD:C_SQ + (h + 1) * SW_HD], qw_ref[...])
        s = jnp.sum(kcat[:, sl] * qh, axis=-1, keepdims=True) * (SW_HD ** -0.5) - _alibi_slope(h) * distf
        s = jnp.where(valid, s, -jnp.inf)
        sink = sinks_ref[h]
        m = jnp.maximum(jnp.max(s, axis=0, keepdims=True), sink)
        p = jnp.exp(s - m)
        den = jnp.sum(p, axis=0, keepdims=True) + jnp.exp(sink - m)
        o_ref[:, h * SW_HD:(h + 1) * SW_HD] = jnp.sum((p / den) * vcat[:, sl], axis=0, keepdims=True)


def _swa_step(proj_s, kbuf, vbuf, sinks, qw, kw):
    b = proj_s.shape[0]
    w = kbuf.shape[1]
    rows = w + 8
    buf = pl.BlockSpec((None, w, SW_KVW), lambda i, s: (i, 0, 0))
    return pl.pallas_call(
        _swa_step_kernel,
        grid_spec=pltpu.PrefetchScalarGridSpec(
            num_scalar_prefetch=1,
            grid=(b,),
            in_specs=[
                pl.BlockSpec((None, 1, PROJ_W), lambda i, s: (i, 0, 0)),
                buf, buf,
                pl.BlockSpec((1, SW_HD), lambda i, s: (0, 0)),
                pl.BlockSpec((1, SW_HD), lambda i, s: (0, 0)),
            ],
            out_specs=[pl.BlockSpec((None, 1, SW_QW), lambda i, s: (i, 0, 0)), buf, buf],
            scratch_shapes=[pltpu.VMEM((rows, SW_KVW), F32), pltpu.VMEM((rows, SW_KVW), F32)],
        ),
        out_shape=[
            jax.ShapeDtypeStruct((b, 1, SW_QW), F32),
            jax.ShapeDtypeStruct((b, w, SW_KVW), F32),
            jax.ShapeDtypeStruct((b, w, SW_KVW), F32),
        ],
        compiler_params=_cp(("arbitrary",)),
        name="swa_step",
    )(sinks, proj_s.reshape(b, 1, PROJ_W), kbuf, vbuf, qw.reshape(1, SW_HD), kw.reshape(1, SW_HD))


def _merge_kernel(ya_ref, yb_ref, wa_ref, wb_ref, ga_ref, gb_ref, o_ref):
    a = _dot(ya_ref[...], wa_ref[...])
    b = _dot(yb_ref[...], wb_ref[...])
    o_ref[...] = (_sigmoid(ga_ref[...]) * a + _sigmoid(gb_ref[...]) * b).astype(BF16)


def _merge(ya, yb, wa, wb, proj, tm):
    m = ya.shape[0]
    tn = 1024
    return pl.pallas_call(
        _merge_kernel,
        grid=(m // tm, D_MODEL // tn),
        in_specs=[
            pl.BlockSpec((tm, DN_VW), lambda i, j: (i, 0)),
            pl.BlockSpec((tm, SW_QW), lambda i, j: (i, 0)),
            pl.BlockSpec((DN_VW, tn), lambda i, j: (0, j)),
            pl.BlockSpec((SW_QW, tn), lambda i, j: (0, j)),
            pl.BlockSpec((tm, tn), lambda i, j: (i, C_GA // tn + j)),
            pl.BlockSpec((tm, tn), lambda i, j: (i, C_GB // tn + j)),
        ],
        out_specs=pl.BlockSpec((tm, tn), lambda i, j: (i, j)),
        out_shape=jax.ShapeDtypeStruct((m, D_MODEL), BF16),
        compiler_params=_cp(("arbitrary", "arbitrary")),
        name="merge",
    )(ya, yb, wa, wb, proj, proj)


def _outproj_kernel(mg_ref, w_ref, x_ref, gt_ref, o_ref):
    o_ref[...] = x_ref[...] + gt_ref[...] * jnp.dot(mg_ref[...], w_ref[...], preferred_element_type=F32)


def _mod_spec(rows_per_mod, tm, tn):
    if rows_per_mod == 1:
        return pl.BlockSpec((tm, tn), lambda i, j: (i, j))
    return pl.BlockSpec((None, 1, tn), lambda i, j: (i // (rows_per_mod // tm), 0, j))


def _out_proj(merged, w_bf16, x2d, gt, rows_per_mod, tm):
    m = x2d.shape[0]
    tn = 1024
    if rows_per_mod == 1:
        gt = gt.reshape(m, D_MODEL)
    return pl.pallas_call(
        _outproj_kernel,
        grid=(m // tm, D_MODEL // tn),
        in_specs=[
            pl.BlockSpec((tm, D_MODEL), lambda i, j: (i, 0)),
            pl.BlockSpec((D_MODEL, tn), lambda i, j: (0, j)),
            pl.BlockSpec((tm, tn), lambda i, j: (i, j)),
            _mod_spec(rows_per_mod, tm, tn),
        ],
        out_specs=pl.BlockSpec((tm, tn), lambda i, j: (i, j)),
        out_shape=jax.ShapeDtypeStruct((m, D_MODEL), F32),
        compiler_params=_cp(("arbitrary", "arbitrary")),
        name="out_proj",
    )(merged, w_bf16, x2d, gt)


def _router_kernel(x_ref, lnw_ref, sc_ref, sh_ref, rw_ref, rb_ref, h_ref, idx_ref, w_ref):
    hmod = _norm_mod(x_ref[...], lnw_ref[...], sc_ref[...], sh_ref[...])
    h_ref[...] = hmod
    logits = _dot3(hmod, rw_ref[...]) + rb_ref[...]
    lane = lax.broadcasted_iota(jnp.int32, logits.shape, 1)
    cur = jnp.where(lane < N_EXPERTS, logits, -jnp.inf)
    vals, idxs = [], []
    for _ in range(TOP_K):
        m = jnp.max(cur, axis=-1, keepdims=True)
        ix = jnp.min(jnp.where(cur == m, lane, LANE), axis=-1, keepdims=True)
        vals.append(m)
        idxs.append(ix)
        cur = jnp.where(lane == ix, -jnp.inf, cur)
    es = [jnp.exp(v - vals[0]) for v in vals]
    den = es[0] + es[1] + es[2] + es[3]
    idx_out = jnp.zeros(logits.shape, jnp.int32)
    w_out = jnp.zeros(logits.shape, F32)
    for k in range(TOP_K):
        idx_out = jnp.where(lane == k, idxs[k], idx_out)
        w_out = jnp.where(lane == k, es[k] / den, w_out)
    idx_ref[...] = idx_out
    w_ref[...] = w_out


def _router(x2d, lnw, sc, sh, rw_pad, rb_pad, rows_per_mod, tm):
    m = x2d.shape[0]
    if rows_per_mod == 1:
        mod_spec = pl.BlockSpec((tm, D_MODEL), lambda i: (i, 0))
        sc, sh = sc.reshape(m, D_MODEL), sh.reshape(m, D_MODEL)
    else:
        mod_spec = pl.BlockSpec((None, 1, D_MODEL), lambda i: (i // (rows_per_mod // tm), 0, 0))
    row = pl.BlockSpec((tm, D_MODEL), lambda i: (i, 0))
    small = pl.BlockSpec((tm, LANE), lambda i: (i, 0))
    return pl.pallas_call(
        _router_kernel,
        grid=(m // tm,),
        in_specs=[
            row,
            pl.BlockSpec((1, D_MODEL), lambda i: (0, 0)),
            mod_spec, mod_spec,
            pl.BlockSpec((D_MODEL, LANE), lambda i: (0, 0)),
            pl.BlockSpec((1, LANE), lambda i: (0, 0)),
        ],
        out_specs=[row, small, small],
        out_shape=[
            jax.ShapeDtypeStruct((m, D_MODEL), F32),
            jax.ShapeDtypeStruct((m, LANE), jnp.int32),
            jax.ShapeDtypeStruct((m, LANE), F32),
        ],
        compiler_params=_cp(("arbitrary",)),
        name="router",
    )(x2d, lnw.reshape(1, D_MODEL), sc, sh, rw_pad, rb_pad)


GATHER_ROWS = 512


def _gather_kernel(tok_ref, h_hbm, xs_hbm, sem):
    base = pl.program_id(0) * GATHER_ROWS

    def copy(r):
        return pltpu.make_async_copy(h_hbm.at[pl.ds(tok_ref[0, r], 1)], xs_hbm.at[pl.ds(base + r, 1)], sem)

    def start(r, c):
        copy(r).start()
        return c

    def wait(r, c):
        copy(r).wait()
        return c

    lax.fori_loop(0, GATHER_ROWS, start, 0)
    lax.fori_loop(0, GATHER_ROWS, wait, 0)


def _gather_rows(h_all, row_tok):
    n_rows = row_tok.shape[0]
    steps = n_rows // GATHER_ROWS
    return pl.pallas_call(
        _gather_kernel,
        grid=(steps,),
        in_specs=[
            pl.BlockSpec((None, 1, GATHER_ROWS), lambda i: (i, 0, 0), memory_space=pltpu.SMEM),
            pl.BlockSpec(memory_space=pl.ANY),
        ],
        out_specs=pl.BlockSpec(memory_space=pl.ANY),
        out_shape=jax.ShapeDtypeStruct((n_rows, D_MODEL), F32),
        scratch_shapes=[pltpu.SemaphoreType.DMA(())],
        compiler_params=_cp(("arbitrary",)),
        name="moe_gather",
    )(row_tok.reshape(steps, 1, GATHER_ROWS), h_all)


def _experts_kernel(sbe_ref, sbb_ref, sbn_ref, tail_ref, xs_hbm, wg_ref, wl_ref, wd_ref, bg_ref, bl_ref, bd_ref,
                    ys_hbm, xf_scr, xb_scr, acc_scr, wg_scr, wl_scr, wd_scr, sem_in, sem_out):
    s = pl.program_id(0)
    j = pl.program_id(1)
    nblk = sbn_ref[s]
    blk0 = sbb_ref[s]
    rb = MOE_ROWS

    def in_copy(b):
        return pltpu.make_async_copy(xs_hbm.at[pl.ds((blk0 + b) * rb, rb)], xf_scr.at[pl.ds(b * rb, rb)], sem_in)

    def out_copy(b):
        return pltpu.make_async_copy(acc_scr.at[pl.ds(b * rb, rb)], ys_hbm.at[pl.ds((blk0 + b) * rb, rb)], sem_out)

    def each(fn):
        def body(b, c):
            fn(b)
            return c
        lax.fori_loop(0, nblk, body, 0)

    @pl.when(j == 0)
    def _():
        each(lambda b: in_copy(b).start())
        each(lambda b: in_copy(b).wait())

        def cast(b):
            r0 = pl.multiple_of(b * rb, rb)
            xb_scr[pl.ds(r0, rb), :] = xf_scr[pl.ds(r0, rb), :].astype(BF16)
            acc_scr[pl.ds(r0, rb), :] = jnp.broadcast_to(bd_ref[...], (rb, D_MODEL))
        each(cast)

    @pl.when(nblk > 0)
    def _():
        wg_scr[...] = wg_ref[...].astype(BF16)
        wl_scr[...] = wl_ref[...].astype(BF16)
        wd_scr[...] = wd_ref[...].astype(BF16)

        def mlp(b):
            r0 = pl.multiple_of(b * rb, rb)
            x = xb_scr[pl.ds(r0, rb), :]
            glu = jnp.dot(x, wg_scr[...], preferred_element_type=F32) + bg_ref[...]
            lin = jnp.dot(x, wl_scr[...], preferred_element_type=F32) + bl_ref[...]
            glu = jnp.minimum(glu, SWIGLU_LIMIT)
            lin = jnp.clip(lin, -SWIGLU_LIMIT, SWIGLU_LIMIT)
            act = glu * _sigmoid(SWIGLU_ALPHA * glu) * (lin + 1.0)
            acc_scr[pl.ds(r0, rb), :] += jnp.dot(act.astype(BF16), wd_scr[...], preferred_element_type=F32)
        each(mlp)

    @pl.when(j == pl.num_programs(1) - 1)
    def _():
        each(lambda b: out_copy(b).start())
        each(lambda b: out_copy(b).wait())

    @pl.when((s == pl.num_programs(0) - 1) & (j == pl.num_programs(1) - 1))
    def _():
        acc_scr[0:rb, :] = jnp.zeros((rb, D_MODEL), F32)

        def zero_copy(b):
            return pltpu.make_async_copy(acc_scr.at[pl.ds(0, rb)], ys_hbm.at[pl.ds(b * rb, rb)], sem_out)

        def start(b, c):
            zero_copy(b).start()
            return c

        def wait(b, c):
            zero_copy(b).wait()
            return c

        lax.fori_loop(tail_ref[0], tail_ref[1], start, 0)
        lax.fori_loop(tail_ref[0], tail_ref[1], wait, 0)


def _experts(xs, sb_e, sb_blk0, sb_nblk, tail, w_gate_up, b_gate_up, w_down, b_down):
    n_rows = xs.shape[0]
    n_sb = sb_e.shape[0]
    tf = MOE_TF
    nj = D_MODEL // tf
    rmax = MOE_SB_BLOCKS * MOE_ROWS

    def jj(s, j, n):
        return jnp.where(n[s] > 0, j, nj - 1)

    return pl.pallas_call(
        _experts_kernel,
        grid_spec=pltpu.PrefetchScalarGridSpec(
            num_scalar_prefetch=4,
            grid=(n_sb, nj),
            in_specs=[
                pl.BlockSpec(memory_space=pl.ANY),
                pl.BlockSpec((None, D_MODEL, tf), lambda s, j, e, b, n, tl: (e[s], 0, jj(s, j, n))),
                pl.BlockSpec((None, D_MODEL, tf), lambda s, j, e, b, n, tl: (e[s], 0, nj + jj(s, j, n))),
                pl.BlockSpec((None, tf, D_MODEL), lambda s, j, e, b, n, tl: (e[s], jj(s, j, n), 0)),
                pl.BlockSpec((None, 1, tf), lambda s, j, e, b, n, tl: (e[s], 0, jj(s, j, n))),
                pl.BlockSpec((None, 1, tf), lambda s, j, e, b, n, tl: (e[s], 0, nj + jj(s, j, n))),
                pl.BlockSpec((None, 1, D_MODEL), lambda s, j, e, b, n, tl: (e[s], 0, 0)),
            ],
            out_specs=pl.BlockSpec(memory_space=pl.ANY),
            scratch_shapes=[
                pltpu.VMEM((rmax, D_MODEL), F32),
                pltpu.VMEM((rmax, D_MODEL), BF16),
                pltpu.VMEM((rmax, D_MODEL), F32),
                pltpu.VMEM((D_MODEL, tf), BF16),
                pltpu.VMEM((D_MODEL, tf), BF16),
                pltpu.VMEM((tf, D_MODEL), BF16),
                pltpu.SemaphoreType.DMA(()),
                pltpu.SemaphoreType.DMA(()),
            ],
        ),
        out_shape=jax.ShapeDtypeStruct((n_rows, D_MODEL), F32),
        compiler_params=_cp(("arbitrary", "arbitrary")),
        name="moe_experts",
    )(sb_e, sb_blk0, sb_nblk, tail, xs, w_gate_up, w_gate_up, w_down,
      b_gate_up.reshape(N_EXPERTS, 1, 2 * D_MODEL), b_gate_up.reshape(N_EXPERTS, 1, 2 * D_MODEL),
      b_down.reshape(N_EXPERTS, 1, D_MODEL))


COMBINE_TOK = 128


def _combine_kernel(pos_ref, ys_hbm, x_ref, gt_ref, w_ref, o_ref, buf, sem):
    n = COMBINE_TOK * TOP_K

    def copy(a):
        return pltpu.make_async_copy(ys_hbm.at[pl.ds(pos_ref[0, a], 1)], buf.at[pl.ds(a, 1)], sem)

    def start(a, c):
        copy(a).start()
        return c

    def wait(a, c):
        copy(a).wait()
        return c

    lax.fori_loop(0, n, start, 0)
    lax.fori_loop(0, n, wait, 0)
    w = w_ref[...]
    lane = lax.broadcasted_iota(jnp.int32, w.shape, 1)
    y = jnp.zeros((COMBINE_TOK, D_MODEL), F32)
    for k in range(TOP_K):
        wk = jnp.sum(jnp.where(lane == k, w, 0.0), axis=-1, keepdims=True)
        y = y + wk * buf[k * COMBINE_TOK:(k + 1) * COMBINE_TOK, :]
    o_ref[...] = x_ref[...] + gt_ref[...] * y


def _combine(ys, pos_kmajor, x2d, gt, top_w, rows_per_mod):
    m = x2d.shape[0]
    tm = COMBINE_TOK
    steps = m // tm
    if rows_per_mod == 1:
        gt = gt.reshape(m, D_MODEL)
        gt_spec = pl.BlockSpec((tm, D_MODEL), lambda i: (i, 0))
    else:
        gt_spec = pl.BlockSpec((None, 1, D_MODEL), lambda i: (i // (rows_per_mod // tm), 0, 0))
    return pl.pallas_call(
        _combine_kernel,
        grid=(steps,),
        in_specs=[
            pl.BlockSpec((None, 1, TOP_K * tm), lambda i: (i, 0, 0), memory_space=pltpu.SMEM),
            pl.BlockSpec(memory_space=pl.ANY),
            pl.BlockSpec((tm, D_MODEL), lambda i: (i, 0)),
            gt_spec,
            pl.BlockSpec((tm, LANE), lambda i: (i, 0)),
        ],
        out_specs=pl.BlockSpec((tm, D_MODEL), lambda i: (i, 0)),
        out_shape=jax.ShapeDtypeStruct((m, D_MODEL), F32),
        scratch_shapes=[pltpu.VMEM((TOP_K * tm, D_MODEL), F32), pltpu.SemaphoreType.DMA(())],
        compiler_params=_cp(("arbitrary",)),
        name="moe_combine",
    )(pos_kmajor, ys, x2d, gt, top_w)


def _routing_tables(top_idx):
    n_tok = top_idx.shape[0]
    n_assign = n_tok * TOP_K
    rb = MOE_ROWS
    n_blocks = -(-(n_assign + N_EXPERTS * (rb - 1)) // rb)
    n_blocks = -(-n_blocks // 2) * 2
    n_rows = n_blocks * rb
    flat_e = top_idx.reshape(-1)
    onehot = (flat_e[:, None] == jnp.arange(N_EXPERTS, dtype=jnp.int32)[None, :]).astype(jnp.int32)
    csum = jnp.cumsum(onehot, axis=0)
    rank = jnp.sum((csum - onehot) * onehot, axis=1)
    counts = csum[-1]
    nblk_e = (counts + rb - 1) // rb
    blk_start = jnp.cumsum(nblk_e) - nblk_e
    dest = (blk_start * rb)[flat_e] + rank
    row_tok = jnp.zeros((n_rows,), jnp.int32).at[dest].set(jnp.arange(n_assign, dtype=jnp.int32) // TOP_K)
    n_sb_max = n_blocks // MOE_SB_BLOCKS + N_EXPERTS
    sb_per_e = (nblk_e + MOE_SB_BLOCKS - 1) // MOE_SB_BLOCKS
    sb_start = jnp.cumsum(sb_per_e) - sb_per_e
    total_sb = jnp.sum(sb_per_e)
    sidx = jnp.arange(n_sb_max, dtype=jnp.int32)
    e_of = jnp.clip(jnp.searchsorted(jnp.cumsum(sb_per_e), sidx, side="right"), 0, N_EXPERTS - 1).astype(jnp.int32)
    local = sidx - sb_start[e_of]
    active = sidx < total_sb
    last_e = e_of[jnp.maximum(total_sb - 1, 0)]
    sb_e = jnp.where(active, e_of, last_e).astype(jnp.int32)
    sb_blk0 = jnp.where(active, blk_start[e_of] + local * MOE_SB_BLOCKS, 0).astype(jnp.int32)
    sb_nblk = jnp.where(active, jnp.minimum(nblk_e[e_of] - local * MOE_SB_BLOCKS, MOE_SB_BLOCKS), 0).astype(jnp.int32)
    tail = jnp.stack([jnp.sum(nblk_e), jnp.int32(n_blocks)]).astype(jnp.int32)
    return dest.astype(jnp.int32), row_tok, sb_e, sb_blk0, sb_nblk, tail


def _kmajor(pos, tm):
    m = pos.shape[0]
    return pos.reshape(m // tm, tm, TOP_K).transpose(0, 2, 1).reshape(m // tm, 1, TOP_K * tm)


def _repack_w_in(w_in):
    a = DN_CONV_CH + DN_VW
    b = a + 2 * DN_HEADS
    c = b + SW_QW
    e = c + 2 * SW_KVW
    pad = jnp.zeros((D_MODEL, PROJ_W - w_in.shape[1]), w_in.dtype)
    return jnp.concatenate([w_in[:, :a], w_in[:, b:c], w_in[:, e:], w_in[:, c:e], w_in[:, a:b], pad], axis=1).astype(BF16)


def _lane_vec(v, offset):
    return jnp.zeros((1, LANE), F32).at[0, offset:offset + v.shape[0]].set(v.astype(F32))


def kernel(x_prompt, x_sample, state_conv, state_delta, cache_swa_k, cache_swa_v, c_prompt, c_sample, w_ada, b_ada, ln1_w, w_in, conv_w, dn_a_log, dn_dt_bias, dn_norm_w, sw_q_norm_w, sw_k_norm_w, sw_sinks, w_branch_a, w_branch_b, w_out, ln2_w, router_w, router_b, w_gate_up, b_gate_up, w_down, b_down):
    assert w_ada.shape[0] == 1, "single-layer step"
    bp, t, d = x_prompt.shape
    bs = x_sample.shape[0]
    np_tok = bp * t
    l = 0

    n_c = bp + bs
    c_all = jnp.concatenate([c_prompt, c_sample, jnp.zeros((-n_c % 8, d), F32)], axis=0)
    mod = _ada_mod(c_all, w_ada[l], b_ada[l])
    mods_p = [m.reshape(bp, 1, d) for m in jnp.split(mod[:bp], 6, axis=-1)]
    mods_s = [m.reshape(bs, 1, d) for m in jnp.split(mod[bp:n_c], 6, axis=-1)]

    w_in_r = _repack_w_in(w_in[l])
    wa, wb, wo = w_branch_a[l].astype(BF16), w_branch_b[l].astype(BF16), w_out[l].astype(BF16)
    alog_lane = _lane_vec(dn_a_log[l], DN_HEADS)
    dtb_lane = _lane_vec(dn_dt_bias[l], DN_HEADS)
    rw_pad = jnp.zeros((d, LANE), F32).at[:, :N_EXPERTS].set(router_w[l])
    rb_pad = jnp.zeros((1, LANE), F32).at[0, :N_EXPERTS].set(router_b[l])
    sinks = sw_sinks[l].astype(F32)

    xp = x_prompt.reshape(np_tok, d)
    proj_p = _in_proj(xp, ln1_w[l], mods_p[1], mods_p[0], w_in_r, t, 1024)
    proj3 = proj_p.reshape(bp, t, PROJ_W)
    gates = _gdn_gates(proj3, alog_lane, dtb_lane)
    ya_p, delta_p = _gdn_prompt(proj3, gates, conv_w[l], dn_norm_w[l])
    yb_p, kn_p = _swa_prompt(proj3, sinks, sw_q_norm_w[l], sw_k_norm_w[l])
    merged_p = _merge(ya_p.reshape(np_tok, DN_VW), yb_p.reshape(np_tok, SW_QW), wa, wb, proj_p, 512)
    x1_p = _out_proj(merged_p, wo, xp, mods_p[2], t, 1024)
    h2_p, idx_p, tw_p = _router(x1_p, ln2_w[l], mods_p[4], mods_p[3], rw_pad, rb_pad, t, 512)

    xs_ = x_sample.reshape(bs, d)
    proj_s = _in_proj(xs_, ln1_w[l], mods_s[1], mods_s[0], w_in_r, 1, bs)
    ya_s, conv_s, delta_s = _gdn_step(proj_s, state_conv[l], state_delta[l], conv_w[l], alog_lane, dtb_lane, dn_norm_w[l])
    w_buf = cache_swa_k.shape[2]
    yb_s, k_s, v_s = _swa_step(proj_s, cache_swa_k[l].reshape(bs, w_buf, SW_KVW), cache_swa_v[l].reshape(bs, w_buf, SW_KVW),
                               sinks, sw_q_norm_w[l], sw_k_norm_w[l])
    merged_s = _merge(ya_s.reshape(bs, DN_VW), yb_s.reshape(bs, SW_QW), wa, wb, proj_s, bs)
    x1_s = _out_proj(merged_s, wo, xs_, mods_s[2], 1, bs)
    h2_s, idx_s, tw_s = _router(x1_s, ln2_w[l], mods_s[4], mods_s[3], rw_pad, rb_pad, 1, bs)

    h_all = jnp.concatenate([h2_p, h2_s], axis=0)
    top_idx = jnp.concatenate([idx_p[:, :TOP_K], idx_s[:, :TOP_K]], axis=0)
    dest, row_tok, sb_e, sb_blk0, sb_nblk, tail = _routing_tables(top_idx)
    xs_sorted = _gather_rows(h_all, row_tok)
    ys = _experts(xs_sorted, sb_e, sb_blk0, sb_nblk, tail, w_gate_up[l], b_gate_up[l], w_down[l], b_down[l])
    pos = dest.reshape(np_tok + bs, TOP_K)
    y_p = _combine(ys, _kmajor(pos[:np_tok], COMBINE_TOK), x1_p, mods_p[5], tw_p, t)
    pad_s = COMBINE_TOK - bs
    pos_s = jnp.concatenate([pos[np_tok:], jnp.zeros((pad_s, TOP_K), jnp.int32)], axis=0)
    x1_s_pad = jnp.concatenate([x1_s, jnp.zeros((pad_s, d), F32)], axis=0)
    gt2_s_pad = jnp.concatenate([mods_s[5].reshape(bs, d), jnp.zeros((pad_s, d), F32)], axis=0)
    tw_s_pad = jnp.concatenate([tw_s, jnp.zeros((pad_s, LANE), F32)], axis=0)
    y_s = _combine(ys, _kmajor(pos_s, COMBINE_TOK), x1_s_pad, gt2_s_pad.reshape(COMBINE_TOK, 1, d), tw_s_pad, 1)[:bs]

    conv_p = proj3[:, t - (DN_CONV - 1):, C_QKV:C_QKV + DN_CONV_CH]
    kp_out = kn_p[:, t - WINDOW:].reshape(bp, WINDOW, SW_KV_HEADS, SW_HD)
    vp_out = proj3[:, t - WINDOW:, C_SV:C_SV + SW_KVW].reshape(bp, WINDOW, SW_KV_HEADS, SW_HD)
    return (
        y_p.reshape(bp, t, d),
        y_s.reshape(bs, 1, d),
        conv_p[None],
        conv_s[None],
        delta_p[None],
        delta_s[None],
        kp_out[None],
        k_s.reshape(bs, w_buf, SW_KV_HEADS, SW_HD)[None],
        vp_out[None],
        v_s.reshape(bs, w_buf, SW_KV_HEADS, SW_HD)[None],
    )
```

```python
import functools

import jax
import jax.numpy as jnp
import numpy as np
from jax import lax
from jax.experimental import pallas as pl
from jax.experimental.pallas import tpu as pltpu

F32 = jnp.float32
BF16 = jnp.bfloat16

D_MODEL = 2048
PAST_LEN = 16384
DN_HEADS = 8
DN_DK = 128
DN_DV = 128
DN_CONV = 4
SW_HEADS = 16
SW_KV_HEADS = 2
SW_HD = 64
SW_GROUP = SW_HEADS // SW_KV_HEADS
WINDOW = 128
N_EXPERTS = 32
TOP_K = 4
SWIGLU_ALPHA = 1.702
SWIGLU_LIMIT = 7.0
EPS = 1e-6

DN_QK = DN_HEADS * DN_DK
DN_VW = DN_HEADS * DN_DV
DN_CONV_CH = 2 * DN_QK + DN_VW
SW_QW = SW_HEADS * SW_HD
SW_KVW = SW_KV_HEADS * SW_HD

LANE = 128
C_QKV = 0
C_Z = DN_CONV_CH
C_SQ = C_Z + DN_VW
C_GA = C_SQ + SW_QW
C_GB = C_GA + D_MODEL
C_SK = C_GB + D_MODEL
C_SV = C_SK + SW_KVW
C_BA = C_SV + SW_KVW
PROJ_W = 10240

GDN_GROUP = 256
GDN_CHUNK = 64
MOE_ROWS = 256
MOE_SB_BLOCKS = 6
MOE_TF = 256
VMEM_LIMIT = 56 * 1024 * 1024


def _cp(sem, vmem=VMEM_LIMIT):
    return pltpu.CompilerParams(dimension_semantics=sem, vmem_limit_bytes=vmem)


def _dot(a, b):
    return jnp.dot(a.astype(BF16), b.astype(BF16), preferred_element_type=F32)


def _dot_nt(a, b):
    return lax.dot_general(a.astype(BF16), b.astype(BF16), (((1,), (1,)), ((), ())), preferred_element_type=F32)


def _split(a):
    hi = a.astype(BF16)
    lo = (a - hi.astype(F32)).astype(BF16)
    return hi, lo


def _dot3(a, b):
    ah, al = _split(a)
    bh, bl = _split(b)
    d = functools.partial(jnp.dot, preferred_element_type=F32)
    return d(ah, bh) + (d(ah, bl) + d(al, bh))


def _dot3_nt(a, b):
    ah, al = _split(a)
    bh, bl = _split(b)
    d = functools.partial(lax.dot_general, dimension_numbers=(((1,), (1,)), ((), ())), preferred_element_type=F32)
    return d(ah, bh) + (d(ah, bl) + d(al, bh))


def _dot_exact_lhs01(m01, b):
    b1 = b.astype(BF16)
    r = b - b1.astype(F32)
    b2 = r.astype(BF16)
    b3 = (r - b2.astype(F32)).astype(BF16)
    d = functools.partial(jnp.dot, preferred_element_type=F32)
    m = m01.astype(BF16)
    return d(m, b1) + (d(m, b2) + d(m, b3))


def _sigmoid(x):
    return 1.0 / (1.0 + jnp.exp(-x))


def _silu(x):
    return x * _sigmoid(x)


def _softplus(x):
    return jnp.maximum(x, 0.0) + jnp.log(1.0 + jnp.exp(-jnp.abs(x)))


def _ada_kernel(c_ref, w_ref, b_ref, o_ref):
    o_ref[...] = _dot(_silu(c_ref[...]), w_ref[...]) + b_ref[...]


def _ada_mod(c_all, w_ada, b_ada):
    m = c_all.shape[0]
    n = w_ada.shape[1]
    tn = 1024
    return pl.pallas_call(
        _ada_kernel,
        grid=(n // tn,),
        in_specs=[
            pl.BlockSpec((m, D_MODEL), lambda j: (0, 0)),
            pl.BlockSpec((D_MODEL, tn), lambda j: (0, j)),
            pl.BlockSpec((1, tn), lambda j: (0, j)),
        ],
        out_specs=pl.BlockSpec((m, tn), lambda j: (0, j)),
        out_shape=jax.ShapeDtypeStruct((m, n), F32),
        compiler_params=_cp(("arbitrary",)),
        name="ada_mod",
    )(c_all, w_ada, b_ada.reshape(1, n))


def _norm_mod(x, lnw, sc, sh):
    y = x * lax.rsqrt(jnp.mean(x * x, axis=-1, keepdims=True) + EPS)
    return (y * lnw) * (1.0 + sc) + sh


def _inproj_kernel(x_ref, lnw_ref, sc_ref, sh_ref, w_ref, o_ref, h_scr):
    @pl.when(pl.program_id(1) == 0)
    def _():
        h_scr[...] = _norm_mod(x_ref[...], lnw_ref[...], sc_ref[...], sh_ref[...]).astype(BF16)

    o_ref[...] = jnp.dot(h_scr[...], w_ref[...], preferred_element_type=F32)


def _in_proj(x2d, lnw, sc, sh, w_bf16, rows_per_mod, tm):
    m = x2d.shape[0]
    tn = 1024
    if rows_per_mod == 1:
        mod_spec = pl.BlockSpec((tm, D_MODEL), lambda i, j: (i, 0))
        sc, sh = sc.reshape(m, D_MODEL), sh.reshape(m, D_MODEL)
    else:
        assert rows_per_mod % tm == 0
        mod_spec = pl.BlockSpec((None, 1, D_MODEL), lambda i, j: (i // (rows_per_mod // tm), 0, 0))
    return pl.pallas_call(
        _inproj_kernel,
        grid=(m // tm, PROJ_W // tn),
        in_specs=[
            pl.BlockSpec((tm, D_MODEL), lambda i, j: (i, 0)),
            pl.BlockSpec((1, D_MODEL), lambda i, j: (0, 0)),
            mod_spec,
            mod_spec,
            pl.BlockSpec((D_MODEL, tn), lambda i, j: (0, j)),
        ],
        out_specs=pl.BlockSpec((tm, tn), lambda i, j: (i, j)),
        out_shape=jax.ShapeDtypeStruct((m, PROJ_W), F32),
        scratch_shapes=[pltpu.VMEM((tm, D_MODEL), BF16)],
        compiler_params=_cp(("arbitrary", "arbitrary")),
        name="in_proj",
    )(x2d, lnw.reshape(1, D_MODEL), sc, sh, w_bf16)


def _tri_masks(n, chunk):
    r = lax.broadcasted_iota(jnp.int32, (n, n), 0)
    c = lax.broadcasted_iota(jnp.int32, (n, n), 1)
    same = (r // chunk) == (c // chunk)
    return same, same & (r >= c), same & (r > c)


def _gates_kernel(ba_ref, alog_ref, dtb_ref, beta_ref, gc_ref, eg_ref, ek_ref, el_ref, gcrow_ref):
    same, causal, _ = _tri_masks(GDN_GROUP, GDN_CHUNK)
    lower01 = jnp.where(causal, 1.0, 0.0)
    ones01 = jnp.where(same, 1.0, 0.0)
    nega = -jnp.exp(alog_ref[...])
    dtb = dtb_ref[...]
    t = ba_ref.shape[0]

    def body(i, carry):
        r0 = pl.multiple_of(i * GDN_GROUP, GDN_GROUP)
        x = ba_ref[pl.ds(r0, GDN_GROUP), :]
        g = nega * _softplus(x + dtb)
        gc = _dot_exact_lhs01(lower01, g)
        gl = _dot_exact_lhs01(ones01, g)
        beta_ref[pl.ds(r0, GDN_GROUP), :] = _sigmoid(x)
        gc_ref[pl.ds(r0, GDN_GROUP), :] = gc
        eg_ref[pl.ds(r0, GDN_GROUP), :] = jnp.exp(gc)
        ek_ref[pl.ds(r0, GDN_GROUP), :] = jnp.exp(gl - gc)
        el_ref[pl.ds(r0, GDN_GROUP), :] = jnp.exp(gl)
        gct = gc.T
        for h in range(DN_HEADS):
            gcrow_ref[h, :, pl.ds(r0, GDN_GROUP)] = gct[DN_HEADS + h:DN_HEADS + h + 1, :]
        return carry

    lax.fori_loop(0, t // GDN_GROUP, body, 0)


def _gdn_gates(proj3, alog_lane, dtb_lane):
    b, t, _ = proj3.shape
    col = pl.BlockSpec((None, t, LANE), lambda i: (i, 0, 0))
    shp = jax.ShapeDtypeStruct((b, t, LANE), F32)
    return pl.pallas_call(
        _gates_kernel,
        grid=(b,),
        in_specs=[
            pl.BlockSpec((None, t, LANE), lambda i: (i, 0, C_BA // LANE)),
            pl.BlockSpec((1, LANE), lambda i: (0, 0)),
            pl.BlockSpec((1, LANE), lambda i: (0, 0)),
        ],
        out_specs=[col, col, col, col, col, pl.BlockSpec((None, DN_HEADS, 1, t), lambda i: (i, 0, 0, 0))],
        out_shape=[shp, shp, shp, shp, shp, jax.ShapeDtypeStruct((b, DN_HEADS, 1, t), F32)],
        compiler_params=_cp(("arbitrary",)),
        name="gdn_gates",
    )(proj3, alog_lane, dtb_lane)


def _l2norm(x):
    return x * lax.rsqrt(jnp.sum(x * x, axis=-1, keepdims=True) + EPS)


def _gdn_kernel(q_ref, k_ref, v_ref, z_ref, beta_ref, gc_ref, eg_ref, ek_ref, el_ref, gcrow_ref,
                cwq_ref, cwk_ref, cwv_ref, nw_ref, o_ref, s_ref, pad_scr, qn_scr, kn_scr, vn_scr, oacc_scr, s_scr):
    t = q_ref.shape[0]
    h = pl.program_id(1)
    pad = 8

    def conv_silu(u_ref, cw_ref):
        pad_scr[0:pad, :] = jnp.zeros((pad, LANE), F32)
        pad_scr[pad:pad + t, :] = u_ref[...]
        y = cw_ref[DN_CONV - 1:DN_CONV, :] * pad_scr[pad:pad + t, :]
        for i in range(DN_CONV - 1):
            off = pad - (DN_CONV - 1) + i
            y = y + cw_ref[i:i + 1, :] * pad_scr[off:off + t, :]
        return _silu(y)

    qn_scr[...] = _l2norm(conv_silu(q_ref, cwq_ref)) * (DN_DK ** -0.5)
    kn_scr[...] = _l2norm(conv_silu(k_ref, cwk_ref))
    vn_scr[...] = conv_silu(v_ref, cwv_ref)
    s_scr[...] = jnp.zeros((DN_DK, DN_DV), F32)

    n = GDN_GROUP
    c = GDN_CHUNK
    _, causal, strict = _tri_masks(n, c)
    rr = lax.broadcasted_iota(jnp.int32, (n, n), 0)
    cc = lax.broadcasted_iota(jnp.int32, (n, n), 1)
    eye = jnp.where(rr == cc, 1.0, 0.0)
    lane = lax.broadcasted_iota(jnp.int32, (n, LANE), 1)
    sel_b = lane == h
    sel_g = lane == h + DN_HEADS

    def pick(ref, r0, sel):
        return jnp.sum(jnp.where(sel, ref[pl.ds(r0, n), :], 0.0), axis=-1, keepdims=True)

    def body(i, carry):
        r0 = pl.multiple_of(i * n, n)
        q = qn_scr[pl.ds(r0, n), :]
        k = kn_scr[pl.ds(r0, n), :]
        v = vn_scr[pl.ds(r0, n), :]
        beta = pick(beta_ref, r0, sel_b)
        gc = pick(gc_ref, r0, sel_g)
        eg = pick(eg_ref, r0, sel_g)
        ek = pick(ek_ref, r0, sel_g)
        el = pick(el_ref, r0, sel_g)
        gcrow = gcrow_ref[:, pl.ds(r0, n)]
        decay = jnp.where(causal, jnp.exp(gc - gcrow), 0.0)
        a_low = jnp.where(strict, beta * _dot3_nt(k, k) * decay, 0.0)
        p = -a_low
        tinv = eye + p
        for _ in range(5):
            p = _dot3(p, p)
            tinv = tinv + _dot3(tinv, p)
        rhs = jnp.concatenate([v * beta, k * (beta * eg)], axis=1)
        sol = _dot3(tinv, rhs)
        value = sol[:, :DN_DV]
        kcum = sol[:, DN_DV:]
        intra = _dot3_nt(q, k) * decay
        q_dec = q * eg
        k_dec = k * ek
        for j in range(n // c):
            lo, hi = j * c, (j + 1) * c
            s = s_scr[...]
            r = _dot3(jnp.concatenate([kcum[lo:hi], q_dec[lo:hi]], axis=0), s)
            v_new = value[lo:hi] - r[:c]
            parts = []
            if lo:
                parts.append(jnp.zeros((lo, DN_DV), F32))
            parts.append(v_new)
            if hi < n:
                parts.append(jnp.zeros((n - hi, DN_DV), F32))
            o = r[c:] + _dot3(intra[lo:hi], jnp.concatenate(parts, axis=0) if len(parts) > 1 else v_new)
            oacc_scr[pl.ds(r0 + lo, c), :] = o
            s_scr[...] = s * el[lo:lo + 1] + _dot3(k_dec[lo:hi].T, v_new)
        return carry

    lax.fori_loop(0, t // n, body, 0)
    o = oacc_scr[...]
    y = o * lax.rsqrt(jnp.mean(o * o, axis=-1, keepdims=True) + EPS)
    o_ref[...] = (y * nw_ref[...]) * _silu(z_ref[...])
    s_ref[...] = s_scr[...]


def _gdn_prompt(proj3, gates, conv_w, norm_w):
    b, t, _ = proj3.shape
    beta, gc, eg, ek, el, gcrow = gates
    hd = DN_HEADS

    def colspec(base):
        return pl.BlockSpec((None, t, LANE), lambda i, j, base=base: (i, 0, base + j))

    gate = pl.BlockSpec((None, t, LANE), lambda i, j: (i, 0, 0))

    def cwspec(base):
        return pl.BlockSpec((DN_CONV, LANE), lambda i, j, base=base: (0, base + j))

    return pl.pallas_call(
        _gdn_kernel,
        grid=(b, hd),
        in_specs=[
            colspec(0), colspec(hd), colspec(2 * hd), colspec(C_Z // LANE),
            gate, gate, gate, gate, gate,
            pl.BlockSpec((None, None, 1, t), lambda i, j: (i, j, 0, 0)),
            cwspec(0), cwspec(hd), cwspec(2 * hd),
            pl.BlockSpec((1, DN_DV), lambda i, j: (0, 0)),
        ],
        out_specs=[
            pl.BlockSpec((None, t, DN_DV), lambda i, j: (i, 0, j)),
            pl.BlockSpec((None, None, DN_DK, DN_DV), lambda i, j: (i, j, 0, 0)),
        ],
        out_shape=[
            jax.ShapeDtypeStruct((b, t, DN_VW), F32),
            jax.ShapeDtypeStruct((b, hd, DN_DK, DN_DV), F32),
        ],
        scratch_shapes=[
            pltpu.VMEM((t + 8, LANE), F32),
            pltpu.VMEM((t, LANE), F32),
            pltpu.VMEM((t, LANE), F32),
            pltpu.VMEM((t, LANE), F32),
            pltpu.VMEM((t, LANE), F32),
            pltpu.VMEM((DN_DK, DN_DV), F32),
        ],
        compiler_params=_cp(("arbitrary", "arbitrary")),
        name="gdn_prompt",
    )(proj3, proj3, proj3, proj3, beta, gc, eg, ek, el, gcrow, conv_w, conv_w, conv_w, norm_w.reshape(1, DN_DV))


def _gdn_step_kernel(p_ref, cprev_ref, s_ref, cw_ref, alog_ref, dtb_ref, nw_ref, o_ref, cnew_ref, snew_ref):
    u = p_ref[:, C_QKV:C_QKV + DN_CONV_CH]
    prev = cprev_ref[...]
    y = cw_ref[DN_CONV - 1:DN_CONV, :] * u
    for i in range(DN_CONV - 1):
        y = y + cw_ref[i:i + 1, :] * prev[i:i + 1, :]
    y = _silu(y)
    cnew_ref[0:DN_CONV - 2, :] = prev[1:DN_CONV - 1, :]
    cnew_ref[DN_CONV - 2:DN_CONV - 1, :] = u
    ba = p_ref[:, C_BA:C_BA + LANE]
    beta_l = _sigmoid(ba)
    a_l = jnp.exp(-jnp.exp(alog_ref[...]) * _softplus(ba + dtb_ref[...]))
    lane = lax.broadcasted_iota(jnp.int32, (1, LANE), 1)
    row8 = lax.broadcasted_iota(jnp.int32, (8, LANE), 0)
    for h in range(DN_HEADS):
        q = _l2norm(y[:, h * DN_DK:(h + 1) * DN_DK]) * (DN_DK ** -0.5)
        k = _l2norm(y[:, DN_QK + h * DN_DK:DN_QK + (h + 1) * DN_DK])
        v = y[:, 2 * DN_QK + h * DN_DV:2 * DN_QK + (h + 1) * DN_DV]
        beta = jnp.sum(jnp.where(lane == h, beta_l, 0.0), axis=-1, keepdims=True)
        a = jnp.sum(jnp.where(lane == h + DN_HEADS, a_l, 0.0), axis=-1, keepdims=True)
        s = s_ref[h]
        kq = jnp.where(row8 == 0, k, jnp.where(row8 == 1, q, 0.0))
        r = _dot3(kq, s)
        v_new = beta * (v - a * r[0:1])
        o = a * r[1:2] + jnp.sum(q * k, axis=-1, keepdims=True) * v_new
        k8 = jnp.where(row8 == 0, k, 0.0)
        v8 = jnp.where(row8 == 0, v_new, 0.0)
        snew_ref[h] = s * a + _dot3(k8.T, v8)
        yo = o * lax.rsqrt(jnp.mean(o * o, axis=-1, keepdims=True) + EPS)
        z = p_ref[:, C_Z + h * DN_DV:C_Z + (h + 1) * DN_DV]
        o_ref[:, h * DN_DV:(h + 1) * DN_DV] = (yo * nw_ref[...]) * _silu(z)


def _gdn_step(proj_s, conv_prev, s0, conv_w, alog_lane, dtb_lane, norm_w):
    b = proj_s.shape[0]
    return pl.pallas_call(
        _gdn_step_kernel,
        grid=(b,),
        in_specs=[
            pl.BlockSpec((None, 1, PROJ_W), lambda i: (i, 0, 0)),
            pl.BlockSpec((None, DN_CONV - 1, DN_CONV_CH), lambda i: (i, 0, 0)),
            pl.BlockSpec((None, DN_HEADS, DN_DK, DN_DV), lambda i: (i, 0, 0, 0)),
            pl.BlockSpec((DN_CONV, DN_CONV_CH), lambda i: (0, 0)),
            pl.BlockSpec((1, LANE), lambda i: (0, 0)),
            pl.BlockSpec((1, LANE), lambda i: (0, 0)),
            pl.BlockSpec((1, DN_DV), lambda i: (0, 0)),
        ],
        out_specs=[
            pl.BlockSpec((None, 1, DN_VW), lambda i: (i, 0, 0)),
            pl.BlockSpec((None, DN_CONV - 1, DN_CONV_CH), lambda i: (i, 0, 0)),
            pl.BlockSpec((None, DN_HEADS, DN_DK, DN_DV), lambda i: (i, 0, 0, 0)),
        ],
        out_shape=[
            jax.ShapeDtypeStruct((b, 1, DN_VW), F32),
            jax.ShapeDtypeStruct((b, DN_CONV - 1, DN_CONV_CH), F32),
            jax.ShapeDtypeStruct((b, DN_HEADS, DN_DK, DN_DV), F32),
        ],
        compiler_params=_cp(("arbitrary",)),
        name="gdn_step",
    )(proj_s.reshape(b, 1, PROJ_W), conv_prev, s0, conv_w, alog_lane, dtb_lane, norm_w.reshape(1, DN_DV))


def _alibi_slope(h):
    return float(2.0 ** (-8.0 * (h + 1) / SW_HEADS))


def _head_rms(x, w):
    return (x * lax.rsqrt(jnp.mean(x * x, axis=-1, keepdims=True) + EPS)) * w


def _swa_kernel(sinks_ref, q_ref, kc_ref, kp_ref, vc_ref, vp_ref, qw_ref, kw_ref, o_ref, kn_ref):
    blk = pl.program_id(1)
    w = WINDOW
    qi = lax.broadcasted_iota(jnp.int32, (w, 2 * w), 0)
    kj = lax.broadcasted_iota(jnp.int32, (w, 2 * w), 1)
    dist = qi + w - kj
    valid = (dist >= 0) & (dist < w) & ((kj >= w) | (blk > 0))
    distf = dist.astype(F32)
    kc = kc_ref[...]
    kp = kp_ref[...]
    kbands, vbands = [], []
    for g in range(SW_KV_HEADS):
        sl = slice(g * SW_HD, (g + 1) * SW_HD)
        kcn = _head_rms(kc[:, sl], kw_ref[...])
        kn_ref[:, sl] = kcn
        kbands.append(jnp.concatenate([_head_rms(kp[:, sl], kw_ref[...]), kcn], axis=0))
        vbands.append(jnp.concatenate([vp_ref[:, sl], vc_ref[:, sl]], axis=0))
    for h in range(SW_HEADS):
        g = h // SW_GROUP
        qh = _head_rms(q_ref[:, h * SW_HD:(h + 1) * SW_HD], qw_ref[...])
        s = _dot_nt(qh, kbands[g]) * (SW_HD ** -0.5) - _alibi_slope(h) * distf
        s = jnp.where(valid, s, -jnp.inf)
        sink = sinks_ref[h]
        m = jnp.maximum(jnp.max(s, axis=-1, keepdims=True), sink)
        p = jnp.exp(s - m)
        den = jnp.sum(p, axis=-1, keepdims=True) + jnp.exp(sink - m)
        o_ref[:, h * SW_HD:(h + 1) * SW_HD] = _dot(p / den, vbands[g])


def _swa_prompt(proj3, sinks, qw, kw):
    b, t, _ = proj3.shape
    nb = t // WINDOW
    kcol, vcol = C_SK // LANE, C_SV // LANE

    def cur(col):
        return pl.BlockSpec((None, WINDOW, SW_KVW), lambda i, j, s, col=col: (i, j, col))

    def prev(col):
        return pl.BlockSpec((None, WINDOW, SW_KVW), lambda i, j, s, col=col: (i, jnp.maximum(j - 1, 0), col))

    return pl.pallas_call(
        _swa_kernel,
        grid_spec=pltpu.PrefetchScalarGridSpec(
            num_scalar_prefetch=1,
            grid=(b, nb),
            in_specs=[
                pl.BlockSpec((None, WINDOW, SW_QW), lambda i, j, s: (i, j, C_SQ // SW_QW)),
                cur(kcol), prev(kcol), cur(vcol), prev(vcol),
                pl.BlockSpec((1, SW_HD), lambda i, j, s: (0, 0)),
                pl.BlockSpec((1, SW_HD), lambda i, j, s: (0, 0)),
            ],
            out_specs=[
                pl.BlockSpec((None, WINDOW, SW_QW), lambda i, j, s: (i, j, 0)),
                pl.BlockSpec((None, WINDOW, SW_KVW), lambda i, j, s: (i, j, 0)),
            ],
        ),
        out_shape=[
            jax.ShapeDtypeStruct((b, t, SW_QW), F32),
            jax.ShapeDtypeStruct((b, t, SW_KVW), F32),
        ],
        compiler_params=_cp(("arbitrary", "arbitrary")),
        name="swa_prompt",
    )(sinks, proj3, proj3, proj3, proj3, proj3, qw.reshape(1, SW_HD), kw.reshape(1, SW_HD))


def _swa_step_kernel(sinks_ref, p_ref, kbuf_ref, vbuf_ref, qw_ref, kw_ref, o_ref, knew_ref, vnew_ref, kcat, vcat):
    w = kbuf_ref.shape[0]
    rows = kcat.shape[0]
    knew = p_ref[:, C_SK:C_SK + SW_KVW]
    vnew = p_ref[:, C_SV:C_SV + SW_KVW]
    kcat[...] = jnp.zeros(kcat.shape, F32)
    vcat[...] = jnp.zeros(vcat.shape, F32)
    kcat[0:w, :] = kbuf_ref[...]
    vcat[0:w, :] = vbuf_ref[...]
    for g in range(SW_KV_HEADS):
        sl = slice(g * SW_HD, (g + 1) * SW_HD)
        kcat[w:w + 1, sl] = _head_rms(knew[:, sl], kw_ref[...])
    vcat[w:w + 1, :] = vnew
    knew_ref[...] = kcat[1:w + 1, :]
    vnew_ref[...] = vcat[1:w + 1, :]
    j = lax.broadcasted_iota(jnp.int32, (rows, 1), 0)
    dist = w - j
    valid = (dist >= 0) & (dist < WINDOW)
    distf = dist.astype(F32)
    for h in range(SW_HEADS):
        g = h // SW_GROUP
        sl = slice(g * SW_HD, (g + 1) * SW_HD)
        qh = _head_rms(p_ref[:, C_SQ + h * SW_HD:C_SQ + (h + 1) * SW_HD], qw_ref[...])
        s = jnp.sum(kcat[:, sl] * qh, axis=-1, keepdims=True) * (SW_HD ** -0.5) - _alibi_slope(h) * distf
        s = jnp.where(valid, s, -jnp.inf)
        sink = sinks_ref[h]
        m = jnp.maximum(jnp.max(s, axis=0, keepdims=True), sink)
        p = jnp.exp(s - m)
        den = jnp.sum(p, axis=0, keepdims=True) + jnp.exp(sink - m)
        o_ref[:, h * SW_HD:(h + 1) * SW_HD] = jnp.sum((p / den) * vcat[:, sl], axis=0, keepdims=True)


def _swa_step(proj_s, kbuf, vbuf, sinks, qw, kw):
    b = proj_s.shape[0]
    w = kbuf.shape[1]
    rows = w + 8
    buf = pl.BlockSpec((None, w, SW_KVW), lambda i, s: (i, 0, 0))
    return pl.pallas_call(
        _swa_step_kernel,
        grid_spec=pltpu.PrefetchScalarGridSpec(
            num_scalar_prefetch=1,
            grid=(b,),
            in_specs=[
                pl.BlockSpec((None, 1, PROJ_W), lambda i, s: (i, 0, 0)),
                buf, buf,
                pl.BlockSpec((1, SW_HD), lambda i, s: (0, 0)),
                pl.BlockSpec((1, SW_HD), lambda i, s: (0, 0)),
            ],
            out_specs=[pl.BlockSpec((None, 1, SW_QW), lambda i, s: (i, 0, 0)), buf, buf],
            scratch_shapes=[pltpu.VMEM((rows, SW_KVW), F32), pltpu.VMEM((rows, SW_KVW), F32)],
        ),
        out_shape=[
            jax.ShapeDtypeStruct((b, 1, SW_QW), F32),
            jax.ShapeDtypeStruct((b, w, SW_KVW), F32),
            jax.ShapeDtypeStruct((b, w, SW_KVW), F32),
        ],
        compiler_params=_cp(("arbitrary",)),
        name="swa_step",
    )(sinks, proj_s.reshape(b, 1, PROJ_W), kbuf, vbuf, qw.reshape(1, SW_HD), kw.reshape(1, SW_HD))


def _merge_kernel(ya_ref, yb_ref, wa_ref, wb_ref, ga_ref, gb_ref, o_ref):
    a = _dot(ya_ref[...], wa_ref[...])
    b = _dot(yb_ref[...], wb_ref[...])
    o_ref[...] = (_sigmoid(ga_ref[...]) * a + _sigmoid(gb_ref[...]) * b).astype(BF16)


def _merge(ya, yb, wa, wb, proj, tm):
    m = ya.shape[0]
    tn = 1024
    return pl.pallas_call(
        _merge_kernel,
        grid=(m // tm, D_MODEL // tn),
        in_specs=[
            pl.BlockSpec((tm, DN_VW), lambda i, j: (i, 0)),
            pl.BlockSpec((tm, SW_QW), lambda i, j: (i, 0)),
            pl.BlockSpec((DN_VW, tn), lambda i, j: (0, j)),
            pl.BlockSpec((SW_QW, tn), lambda i, j: (0, j)),
            pl.BlockSpec((tm, tn), lambda i, j: (i, C_GA // tn + j)),
            pl.BlockSpec((tm, tn), lambda i, j: (i, C_GB // tn + j)),
        ],
        out_specs=pl.BlockSpec((tm, tn), lambda i, j: (i, j)),
        out_shape=jax.ShapeDtypeStruct((m, D_MODEL), BF16),
        compiler_params=_cp(("arbitrary", "arbitrary")),
        name="merge",
    )(ya, yb, wa, wb, proj, proj)


def _outproj_kernel(mg_ref, w_ref, x_ref, gt_ref, o_ref):
    o_ref[...] = x_ref[...] + gt_ref[...] * jnp.dot(mg_ref[...], w_ref[...], preferred_element_type=F32)


def _mod_spec(rows_per_mod, tm, tn):
    if rows_per_mod == 1:
        return pl.BlockSpec((tm, tn), lambda i, j: (i, j))
    return pl.BlockSpec((None, 1, tn), lambda i, j: (i // (rows_per_mod // tm), 0, j))


def _out_proj(merged, w_bf16, x2d, gt, rows_per_mod, tm):
    m = x2d.shape[0]
    tn = 1024
    if rows_per_mod == 1:
        gt = gt.reshape(m, D_MODEL)
    return pl.pallas_call(
        _outproj_kernel,
        grid=(m // tm, D_MODEL // tn),
        in_specs=[
            pl.BlockSpec((tm, D_MODEL), lambda i, j: (i, 0)),
            pl.BlockSpec((D_MODEL, tn), lambda i, j: (0, j)),
            pl.BlockSpec((tm, tn), lambda i, j: (i, j)),
            _mod_spec(rows_per_mod, tm, tn),
        ],
        out_specs=pl.BlockSpec((tm, tn), lambda i, j: (i, j)),
        out_shape=jax.ShapeDtypeStruct((m, D_MODEL), F32),
        compiler_params=_cp(("arbitrary", "arbitrary")),
        name="out_proj",
    )(merged, w_bf16, x2d, gt)


def _router_kernel(x_ref, lnw_ref, sc_ref, sh_ref, rw_ref, rb_ref, h_ref, idx_ref, w_ref):
    hmod = _norm_mod(x_ref[...], lnw_ref[...], sc_ref[...], sh_ref[...])
    h_ref[...] = hmod
    logits = _dot3(hmod, rw_ref[...]) + rb_ref[...]
    lane = lax.broadcasted_iota(jnp.int32, logits.shape, 1)
    cur = jnp.where(lane < N_EXPERTS, logits, -jnp.inf)
    vals, idxs = [], []
    for _ in range(TOP_K):
        m = jnp.max(cur, axis=-1, keepdims=True)
        ix = jnp.min(jnp.where(cur == m, lane, LANE), axis=-1, keepdims=True)
        vals.append(m)
        idxs.append(ix)
        cur = jnp.where(lane == ix, -jnp.inf, cur)
    es = [jnp.exp(v - vals[0]) for v in vals]
    den = es[0] + es[1] + es[2] + es[3]
    idx_out = jnp.zeros(logits.shape, jnp.int32)
    w_out = jnp.zeros(logits.shape, F32)
    for k in range(TOP_K):
        idx_out = jnp.where(lane == k, idxs[k], idx_out)
        w_out = jnp.where(lane == k, es[k] / den, w_out)
    idx_ref[...] = idx_out
    w_ref[...] = w_out


def _router(x2d, lnw, sc, sh, rw_pad, rb_pad, rows_per_mod, tm):
    m = x2d.shape[0]
    if rows_per_mod == 1:
        mod_spec = pl.BlockSpec((tm, D_MODEL), lambda i: (i, 0))
        sc, sh = sc.reshape(m, D_MODEL), sh.reshape(m, D_MODEL)
    else:
        mod_spec = pl.BlockSpec((None, 1, D_MODEL), lambda i: (i // (rows_per_mod // tm), 0, 0))
    row = pl.BlockSpec((tm, D_MODEL), lambda i: (i, 0))
    small = pl.BlockSpec((tm, LANE), lambda i: (i, 0))
    return pl.pallas_call(
        _router_kernel,
        grid=(m // tm,),
        in_specs=[
            row,
            pl.BlockSpec((1, D_MODEL), lambda i: (0, 0)),
            mod_spec, mod_spec,
            pl.BlockSpec((D_MODEL, LANE), lambda i: (0, 0)),
            pl.BlockSpec((1, LANE), lambda i: (0, 0)),
        ],
        out_specs=[row, small, small],
        out_shape=[
            jax.ShapeDtypeStruct((m, D_MODEL), F32),
            jax.ShapeDtypeStruct((m, LANE), jnp.int32),
            jax.ShapeDtypeStruct((m, LANE), F32),
        ],
        compiler_params=_cp(("arbitrary",)),
        name="router",
    )(x2d, lnw.reshape(1, D_MODEL), sc, sh, rw_pad, rb_pad)


SCATTER_TOK = 256
DMA_UNROLL = 8


def _scatter_kernel(zl_ref, dest_ref, hp_ref, hs_ref, xs_hbm, zbuf, sem, zsem):
    i = pl.program_id(0)
    rb = MOE_ROWS
    n_prompt_steps = pl.num_programs(0) - 1

    @pl.when(i == 0)
    def _():
        zbuf[...] = jnp.zeros(zbuf.shape, F32)

        def zero_copy(n):
            return pltpu.make_async_copy(zbuf, xs_hbm.at[pl.ds(zl_ref[n] * rb, rb)], zsem)

        def start(n, c):
            @pl.when(zl_ref[n] >= 0)
            def _():
                zero_copy(n).start()
            return c

        def wait(n, c):
            @pl.when(zl_ref[n] >= 0)
            def _():
                zero_copy(n).wait()
            return c

        lax.fori_loop(0, zl_ref.shape[0], start, 0)
        lax.fori_loop(0, zl_ref.shape[0], wait, 0)

    def scatter(src_ref):
        n_tok = src_ref.shape[0]

        def start(a, c):
            t = lax.shift_right_logical(a, 2)
            pltpu.make_async_copy(src_ref.at[pl.ds(t, 1)], xs_hbm.at[pl.ds(dest_ref[0, a], 1)], sem).start()
            return c

        lax.fori_loop(0, n_tok * TOP_K, start, 0, unroll=DMA_UNROLL)
        for _ in range(TOP_K):
            pltpu.make_async_copy(src_ref, xs_hbm.at[pl.ds(0, n_tok)], sem).wait()

    @pl.when(i < n_prompt_steps)
    def _():
        scatter(hp_ref)

    @pl.when(i == n_prompt_steps)
    def _():
        scatter(hs_ref)


def _scatter_rows(h_p, h_s, dest, zero_blocks, n_rows):
    assert TOP_K == 4
    n_p, n_s = h_p.shape[0], h_s.shape[0]
    steps_p = n_p // SCATTER_TOK
    per = SCATTER_TOK * TOP_K
    dest_s = jnp.concatenate([dest[n_p * TOP_K:], jnp.zeros((per - n_s * TOP_K,), jnp.int32)])
    dest3 = jnp.concatenate([dest[:n_p * TOP_K], dest_s]).reshape(steps_p + 1, 1, per)
    return pl.pallas_call(
        _scatter_kernel,
        grid_spec=pltpu.PrefetchScalarGridSpec(
            num_scalar_prefetch=1,
            grid=(steps_p + 1,),
            in_specs=[
                pl.BlockSpec((None, 1, per), lambda i, zl: (i, 0, 0), memory_space=pltpu.SMEM),
                pl.BlockSpec((SCATTER_TOK, D_MODEL), lambda i, zl: (jnp.minimum(i, steps_p - 1), 0)),
                pl.BlockSpec((n_s, D_MODEL), lambda i, zl: (0, 0)),
            ],
            out_specs=pl.BlockSpec(memory_space=pl.ANY),
            scratch_shapes=[
                pltpu.VMEM((MOE_ROWS, D_MODEL), F32),
                pltpu.SemaphoreType.DMA(()),
                pltpu.SemaphoreType.DMA(()),
            ],
        ),
        out_shape=jax.ShapeDtypeStruct((n_rows, D_MODEL), F32),
        compiler_params=_cp(("arbitrary",)),
        name="moe_scatter",
    )(zero_blocks, dest3, h_p, h_s)


def _experts_kernel(sbe_ref, sbb_ref, sbn_ref, tail_ref, xs_hbm, wg_ref, wl_ref, wd_ref, bg_ref, bl_ref, bd_ref,
                    ys_hbm, xf_scr, xb_scr, acc_scr, wg_scr, wl_scr, wd_scr, sem_in, sem_out):
    s = pl.program_id(0)
    j = pl.program_id(1)
    nblk = sbn_ref[s]
    blk0 = sbb_ref[s]
    rb = MOE_ROWS

    def in_copy(b):
        return pltpu.make_async_copy(xs_hbm.at[pl.ds((blk0 + b) * rb, rb)], xf_scr.at[pl.ds(b * rb, rb)], sem_in)

    def out_copy(b):
        return pltpu.make_async_copy(acc_scr.at[pl.ds(b * rb, rb)], ys_hbm.at[pl.ds((blk0 + b) * rb, rb)], sem_out)

    def each(fn):
        def body(b, c):
            fn(b)
            return c
        lax.fori_loop(0, nblk, body, 0)

    @pl.when(j == 0)
    def _():
        each(lambda b: in_copy(b).start())
        each(lambda b: in_copy(b).wait())

        def cast(b):
            r0 = pl.multiple_of(b * rb, rb)
            xb_scr[pl.ds(r0, rb), :] = xf_scr[pl.ds(r0, rb), :].astype(BF16)
            acc_scr[pl.ds(r0, rb), :] = jnp.broadcast_to(bd_ref[...], (rb, D_MODEL))
        each(cast)

    @pl.when(nblk > 0)
    def _():
        wg_scr[...] = wg_ref[...].astype(BF16)
        wl_scr[...] = wl_ref[...].astype(BF16)
        wd_scr[...] = wd_ref[...].astype(BF16)

        def mlp(b):
            r0 = pl.multiple_of(b * rb, rb)
            x = xb_scr[pl.ds(r0, rb), :]
            glu = jnp.dot(x, wg_scr[...], preferred_element_type=F32) + bg_ref[...]
            lin = jnp.dot(x, wl_scr[...], preferred_element_type=F32) + bl_ref[...]
            glu = jnp.minimum(glu, SWIGLU_LIMIT)
            lin = jnp.clip(lin, -SWIGLU_LIMIT, SWIGLU_LIMIT)
            act = glu * _sigmoid(SWIGLU_ALPHA * glu) * (lin + 1.0)
            acc_scr[pl.ds(r0, rb), :] += jnp.dot(act.astype(BF16), wd_scr[...], preferred_element_type=F32)
        each(mlp)

    @pl.when(j == pl.num_programs(1) - 1)
    def _():
        each(lambda b: out_copy(b).start())
        each(lambda b: out_copy(b).wait())

    @pl.when((s == pl.num_programs(0) - 1) & (j == pl.num_programs(1) - 1))
    def _():
        acc_scr[0:rb, :] = jnp.zeros((rb, D_MODEL), F32)

        def zero_copy(b):
            return pltpu.make_async_copy(acc_scr.at[pl.ds(0, rb)], ys_hbm.at[pl.ds(b * rb, rb)], sem_out)

        def start(b, c):
            zero_copy(b).start()
            return c

        def wait(b, c):
            zero_copy(b).wait()
            return c

        lax.fori_loop(tail_ref[0], tail_ref[1], start, 0)
        lax.fori_loop(tail_ref[0], tail_ref[1], wait, 0)


def _experts(xs, sb_e, sb_blk0, sb_nblk, tail, w_gate_up, b_gate_up, w_down, b_down):
    n_rows = xs.shape[0]
    n_sb = sb_e.shape[0]
    tf = MOE_TF
    nj = D_MODEL // tf
    rmax = MOE_SB_BLOCKS * MOE_ROWS

    def jj(s, j, n):
        return jnp.where(n[s] > 0, j, nj - 1)

    return pl.pallas_call(
        _experts_kernel,
        grid_spec=pltpu.PrefetchScalarGridSpec(
            num_scalar_prefetch=4,
            grid=(n_sb, nj),
            in_specs=[
                pl.BlockSpec(memory_space=pl.ANY),
                pl.BlockSpec((None, D_MODEL, tf), lambda s, j, e, b, n, tl: (e[s], 0, jj(s, j, n))),
                pl.BlockSpec((None, D_MODEL, tf), lambda s, j, e, b, n, tl: (e[s], 0, nj + jj(s, j, n))),
                pl.BlockSpec((None, tf, D_MODEL), lambda s, j, e, b, n, tl: (e[s], jj(s, j, n), 0)),
                pl.BlockSpec((None, 1, tf), lambda s, j, e, b, n, tl: (e[s], 0, jj(s, j, n))),
                pl.BlockSpec((None, 1, tf), lambda s, j, e, b, n, tl: (e[s], 0, nj + jj(s, j, n))),
                pl.BlockSpec((None, 1, D_MODEL), lambda s, j, e, b, n, tl: (e[s], 0, 0)),
            ],
            out_specs=pl.BlockSpec(memory_space=pl.ANY),
            scratch_shapes=[
                pltpu.VMEM((rmax, D_MODEL), F32),
                pltpu.VMEM((rmax, D_MODEL), BF16),
                pltpu.VMEM((rmax, D_MODEL), F32),
                pltpu.VMEM((D_MODEL, tf), BF16),
                pltpu.VMEM((D_MODEL, tf), BF16),
                pltpu.VMEM((tf, D_MODEL), BF16),
                pltpu.SemaphoreType.DMA(()),
                pltpu.SemaphoreType.DMA(()),
            ],
        ),
        out_shape=jax.ShapeDtypeStruct((n_rows, D_MODEL), F32),
        compiler_params=_cp(("arbitrary", "arbitrary")),
        name="moe_experts",
    )(sb_e, sb_blk0, sb_nblk, tail, xs, w_gate_up, w_gate_up, w_down,
      b_gate_up.reshape(N_EXPERTS, 1, 2 * D_MODEL), b_gate_up.reshape(N_EXPERTS, 1, 2 * D_MODEL),
      b_down.reshape(N_EXPERTS, 1, D_MODEL))


COMBINE_TOK = 128


def _combine_kernel(pos_ref, posn_ref, ys_hbm, x_ref, gt_ref, w_ref, o_ref, buf, sem):
    n = COMBINE_TOK * TOP_K
    i = pl.program_id(0)
    slot = lax.rem(i, 2)

    def issue(p_ref, sl):
        def start(a, c):
            pltpu.make_async_copy(ys_hbm.at[pl.ds(p_ref[0, a], 1)], buf.at[sl, pl.ds(a, 1)], sem.at[sl]).start()
            return c
        lax.fori_loop(0, n, start, 0, unroll=DMA_UNROLL)

    @pl.when(i == 0)
    def _():
        issue(pos_ref, 0)

    @pl.when(i + 1 < pl.num_programs(0))
    def _():
        issue(posn_ref, 1 - slot)

    pltpu.make_async_copy(ys_hbm.at[pl.ds(0, n)], buf.at[slot], sem.at[slot]).wait()
    w = w_ref[...]
    lane = lax.broadcasted_iota(jnp.int32, w.shape, 1)
    y = jnp.zeros((COMBINE_TOK, D_MODEL), F32)
    for k in range(TOP_K):
        wk = jnp.sum(jnp.where(lane == k, w, 0.0), axis=-1, keepdims=True)
        y = y + wk * buf[slot, k * COMBINE_TOK:(k + 1) * COMBINE_TOK, :]
    o_ref[...] = x_ref[...] + gt_ref[...] * y


def _combine(ys, pos_kmajor, x2d, gt, top_w, rows_per_mod):
    m = x2d.shape[0]
    tm = COMBINE_TOK
    steps = m // tm
    if rows_per_mod == 1:
        gt = gt.reshape(m, D_MODEL)
        gt_spec = pl.BlockSpec((tm, D_MODEL), lambda i: (i, 0))
    else:
        gt_spec = pl.BlockSpec((None, 1, D_MODEL), lambda i: (i // (rows_per_mod // tm), 0, 0))
    return pl.pallas_call(
        _combine_kernel,
        grid=(steps,),
        in_specs=[
            pl.BlockSpec((None, 1, TOP_K * tm), lambda i: (i, 0, 0), memory_space=pltpu.SMEM),
            pl.BlockSpec((None, 1, TOP_K * tm), lambda i: (jnp.minimum(i + 1, steps - 1), 0, 0), memory_space=pltpu.SMEM),
            pl.BlockSpec(memory_space=pl.ANY),
            pl.BlockSpec((tm, D_MODEL), lambda i: (i, 0)),
            gt_spec,
            pl.BlockSpec((tm, LANE), lambda i: (i, 0)),
        ],
        out_specs=pl.BlockSpec((tm, D_MODEL), lambda i: (i, 0)),
        out_shape=jax.ShapeDtypeStruct((m, D_MODEL), F32),
        scratch_shapes=[pltpu.VMEM((2, TOP_K * tm, D_MODEL), F32), pltpu.SemaphoreType.DMA((2,))],
        compiler_params=_cp(("arbitrary",)),
        name="moe_combine",
    )(pos_kmajor, pos_kmajor, ys, x2d, gt, top_w)


def _routing_tables(top_idx):
    n_tok = top_idx.shape[0]
    n_assign = n_tok * TOP_K
    rb = MOE_ROWS
    n_blocks = -(-(n_assign + N_EXPERTS * (rb - 1)) // rb)
    n_rows = n_blocks * rb
    flat_e = top_idx.reshape(-1)
    onehot = (flat_e[:, None] == jnp.arange(N_EXPERTS, dtype=jnp.int32)[None, :]).astype(jnp.int32)
    csum = jnp.cumsum(onehot, axis=0)
    rank = jnp.sum((csum - onehot) * onehot, axis=1)
    counts = csum[-1]
    nblk_e = (counts + rb - 1) // rb
    blk_start = jnp.cumsum(nblk_e) - nblk_e
    dest = (blk_start * rb)[flat_e] + rank
    total_blk = jnp.sum(nblk_e)
    last_blk = jnp.where(nblk_e > 0, blk_start + nblk_e - 1, -1)
    bidx = jnp.arange(n_blocks, dtype=jnp.int32)
    zero_blocks = jnp.concatenate([last_blk, jnp.where(bidx >= total_blk, bidx, -1)]).astype(jnp.int32)
    n_sb_max = n_blocks // MOE_SB_BLOCKS + N_EXPERTS
    sb_per_e = (nblk_e + MOE_SB_BLOCKS - 1) // MOE_SB_BLOCKS
    sb_start = jnp.cumsum(sb_per_e) - sb_per_e
    total_sb = jnp.sum(sb_per_e)
    sidx = jnp.arange(n_sb_max, dtype=jnp.int32)
    e_of = jnp.clip(jnp.searchsorted(jnp.cumsum(sb_per_e), sidx, side="right"), 0, N_EXPERTS - 1).astype(jnp.int32)
    local = sidx - sb_start[e_of]
    active = sidx < total_sb
    last_e = e_of[jnp.maximum(total_sb - 1, 0)]
    sb_e = jnp.where(active, e_of, last_e).astype(jnp.int32)
    sb_blk0 = jnp.where(active, blk_start[e_of] + local * MOE_SB_BLOCKS, 0).astype(jnp.int32)
    sb_nblk = jnp.where(active, jnp.minimum(nblk_e[e_of] - local * MOE_SB_BLOCKS, MOE_SB_BLOCKS), 0).astype(jnp.int32)
    tail = jnp.stack([total_blk, jnp.int32(n_blocks)]).astype(jnp.int32)
    return dest.astype(jnp.int32), zero_blocks, n_rows, sb_e, sb_blk0, sb_nblk, tail


def _kmajor(pos, tm):
    m = pos.shape[0]
    return pos.reshape(m // tm, tm, TOP_K).transpose(0, 2, 1).reshape(m // tm, 1, TOP_K * tm)


def _repack_w_in(w_in):
    a = DN_CONV_CH + DN_VW
    b = a + 2 * DN_HEADS
    c = b + SW_QW
    e = c + 2 * SW_KVW
    pad = jnp.zeros((D_MODEL, PROJ_W - w_in.shape[1]), w_in.dtype)
    return jnp.concatenate([w_in[:, :a], w_in[:, b:c], w_in[:, e:], w_in[:, c:e], w_in[:, a:b], pad], axis=1).astype(BF16)


def _lane_vec(v, offset):
    return jnp.zeros((1, LANE), F32).at[0, offset:offset + v.shape[0]].set(v.astype(F32))


def kernel(x_prompt, x_sample, state_conv, state_delta, cache_swa_k, cache_swa_v, c_prompt, c_sample, w_ada, b_ada, ln1_w, w_in, conv_w, dn_a_log, dn_dt_bias, dn_norm_w, sw_q_norm_w, sw_k_norm_w, sw_sinks, w_branch_a, w_branch_b, w_out, ln2_w, router_w, router_b, w_gate_up, b_gate_up, w_down, b_down):
    assert w_ada.shape[0] == 1, "single-layer step"
    bp, t, d = x_prompt.shape
    bs = x_sample.shape[0]
    np_tok = bp * t
    l = 0

    n_c = bp + bs
    c_all = jnp.concatenate([c_prompt, c_sample, jnp.zeros((-n_c % 8, d), F32)], axis=0)
    mod = _ada_mod(c_all, w_ada[l], b_ada[l])
    mods_p = [m.reshape(bp, 1, d) for m in jnp.split(mod[:bp], 6, axis=-1)]
    mods_s = [m.reshape(bs, 1, d) for m in jnp.split(mod[bp:n_c], 6, axis=-1)]

    w_in_r = _repack_w_in(w_in[l])
    wa, wb, wo = w_branch_a[l].astype(BF16), w_branch_b[l].astype(BF16), w_out[l].astype(BF16)
    alog_lane = _lane_vec(dn_a_log[l], DN_HEADS)
    dtb_lane = _lane_vec(dn_dt_bias[l], DN_HEADS)
    rw_pad = jnp.zeros((d, LANE), F32).at[:, :N_EXPERTS].set(router_w[l])
    rb_pad = jnp.zeros((1, LANE), F32).at[0, :N_EXPERTS].set(router_b[l])
    sinks = sw_sinks[l].astype(F32)

    xp = x_prompt.reshape(np_tok, d)
    proj_p = _in_proj(xp, ln1_w[l], mods_p[1], mods_p[0], w_in_r, t, 1024)
    proj3 = proj_p.reshape(bp, t, PROJ_W)
    gates = _gdn_gates(proj3, alog_lane, dtb_lane)
    ya_p, delta_p = _gdn_prompt(proj3, gates, conv_w[l], dn_norm_w[l])
    yb_p, kn_p = _swa_prompt(proj3, sinks, sw_q_norm_w[l], sw_k_norm_w[l])
    merged_p = _merge(ya_p.reshape(np_tok, DN_VW), yb_p.reshape(np_tok, SW_QW), wa, wb, proj_p, 512)
    x1_p = _out_proj(merged_p, wo, xp, mods_p[2], t, 1024)
    h2_p, idx_p, tw_p = _router(x1_p, ln2_w[l], mods_p[4], mods_p[3], rw_pad, rb_pad, t, 512)

    xs_ = x_sample.reshape(bs, d)
    proj_s = _in_proj(xs_, ln1_w[l], mods_s[1], mods_s[0], w_in_r, 1, bs)
    ya_s, conv_s, delta_s = _gdn_step(proj_s, state_conv[l], state_delta[l], conv_w[l], alog_lane, dtb_lane, dn_norm_w[l])
    w_buf = cache_swa_k.shape[2]
    yb_s, k_s, v_s = _swa_step(proj_s, cache_swa_k[l].reshape(bs, w_buf, SW_KVW), cache_swa_v[l].reshape(bs, w_buf, SW_KVW),
                               sinks, sw_q_norm_w[l], sw_k_norm_w[l])
    merged_s = _merge(ya_s.reshape(bs, DN_VW), yb_s.reshape(bs, SW_QW), wa, wb, proj_s, bs)
    x1_s = _out_proj(merged_s, wo, xs_, mods_s[2], 1, bs)
    h2_s, idx_s, tw_s = _router(x1_s, ln2_w[l], mods_s[4], mods_s[3], rw_pad, rb_pad, 1, bs)

    top_idx = jnp.concatenate([idx_p[:, :TOP_K], idx_s[:, :TOP_K]], axis=0)
    dest, zero_blocks, n_rows, sb_e, sb_blk0, sb_nblk, tail = _routing_tables(top_idx)
    xs_sorted = _scatter_rows(h2_p, h2_s, dest, zero_blocks, n_rows)
    ys = _experts(xs_sorted, sb_e, sb_blk0, sb_nblk, tail, w_gate_up[l], b_gate_up[l], w_down[l], b_down[l])
    pos = dest.reshape(np_tok + bs, TOP_K)
    y_p = _combine(ys, _kmajor(pos[:np_tok], COMBINE_TOK), x1_p, mods_p[5], tw_p, t)
    pad_s = COMBINE_TOK - bs
    pos_s = jnp.concatenate([pos[np_tok:], jnp.zeros((pad_s, TOP_K), jnp.int32)], axis=0)
    x1_s_pad = jnp.concatenate([x1_s, jnp.zeros((pad_s, d), F32)], axis=0)
    gt2_s_pad = jnp.concatenate([mods_s[5].reshape(bs, d), jnp.zeros((pad_s, d), F32)], axis=0)
    tw_s_pad = jnp.concatenate([tw_s, jnp.zeros((pad_s, LANE), F32)], axis=0)
    y_s = _combine(ys, _kmajor(pos_s, COMBINE_TOK), x1_s_pad, gt2_s_pad.reshape(COMBINE_TOK, 1, d), tw_s_pad, 1)[:bs]

    conv_p = proj3[:, t - (DN_CONV - 1):, C_QKV:C_QKV + DN_CONV_CH]
    kp_out = kn_p[:, t - WINDOW:].reshape(bp, WINDOW, SW_KV_HEADS, SW_HD)
    vp_out = proj3[:, t - WINDOW:, C_SV:C_SV + SW_KVW].reshape(bp, WINDOW, SW_KV_HEADS, SW_HD)
    return (
        y_p.reshape(bp, t, d),
        y_s.reshape(bs, 1, d),
        conv_p[None],
        conv_s[None],
        delta_p[None],
        delta_s[None],
        kp_out[None],
        k_s.reshape(bs, w_buf, SW_KV_HEADS, SW_HD)[None],
        vp_out[None],
        v_s.reshape(bs, w_buf, SW_KV_HEADS, SW_HD)[None],
    )
```

```python
import functools

import jax
import jax.numpy as jnp
import numpy as np
from jax import lax
from jax.experimental import pallas as pl
from jax.experimental.pallas import tpu as pltpu

F32 = jnp.float32
BF16 = jnp.bfloat16

D_MODEL = 2048
PAST_LEN = 16384
DN_HEADS = 8
DN_DK = 128
DN_DV = 128
DN_CONV = 4
SW_HEADS = 16
SW_KV_HEADS = 2
SW_HD = 64
SW_GROUP = SW_HEADS // SW_KV_HEADS
WINDOW = 128
N_EXPERTS = 32
TOP_K = 4
SWIGLU_ALPHA = 1.702
SWIGLU_LIMIT = 7.0
EPS = 1e-6

DN_QK = DN_HEADS * DN_DK
DN_VW = DN_HEADS * DN_DV
DN_CONV_CH = 2 * DN_QK + DN_VW
SW_QW = SW_HEADS * SW_HD
SW_KVW = SW_KV_HEADS * SW_HD

LANE = 128
C_QKV = 0
C_Z = DN_CONV_CH
C_SQ = C_Z + DN_VW
C_GA = C_SQ + SW_QW
C_GB = C_GA + D_MODEL
C_SK = C_GB + D_MODEL
C_SV = C_SK + SW_KVW
C_BA = C_SV + SW_KVW
PROJ_W = 10240

GDN_GROUP = 256
GDN_CHUNK = 64
MOE_ROWS = 256
MOE_SB_BLOCKS = 6
MOE_TF = 512
VMEM_LIMIT = 56 * 1024 * 1024


def _cp(sem, vmem=VMEM_LIMIT):
    return pltpu.CompilerParams(dimension_semantics=sem, vmem_limit_bytes=vmem)


def _dot(a, b):
    return jnp.dot(a.astype(BF16), b.astype(BF16), preferred_element_type=F32)


def _dot_nt(a, b):
    return lax.dot_general(a.astype(BF16), b.astype(BF16), (((1,), (1,)), ((), ())), preferred_element_type=F32)


def _split(a):
    hi = a.astype(BF16)
    lo = (a - hi.astype(F32)).astype(BF16)
    return hi, lo


def _dot3(a, b):
    ah, al = _split(a)
    bh, bl = _split(b)
    d = functools.partial(jnp.dot, preferred_element_type=F32)
    return d(ah, bh) + (d(ah, bl) + d(al, bh))


def _dot3_nt(a, b):
    ah, al = _split(a)
    bh, bl = _split(b)
    d = functools.partial(lax.dot_general, dimension_numbers=(((1,), (1,)), ((), ())), preferred_element_type=F32)
    return d(ah, bh) + (d(ah, bl) + d(al, bh))


def _dot_exact_lhs01(m01, b):
    b1 = b.astype(BF16)
    r = b - b1.astype(F32)
    b2 = r.astype(BF16)
    b3 = (r - b2.astype(F32)).astype(BF16)
    d = functools.partial(jnp.dot, preferred_element_type=F32)
    m = m01.astype(BF16)
    return d(m, b1) + (d(m, b2) + d(m, b3))


def _sigmoid(x):
    return 1.0 / (1.0 + jnp.exp(-x))


def _silu(x):
    return x * _sigmoid(x)


def _softplus(x):
    return jnp.maximum(x, 0.0) + jnp.log(1.0 + jnp.exp(-jnp.abs(x)))


def _ada_kernel(c_ref, w_ref, b_ref, o_ref):
    o_ref[...] = _dot(_silu(c_ref[...]), w_ref[...]) + b_ref[...]


def _ada_mod(c_all, w_ada, b_ada):
    m = c_all.shape[0]
    n = w_ada.shape[1]
    tn = 1024
    return pl.pallas_call(
        _ada_kernel,
        grid=(n // tn,),
        in_specs=[
            pl.BlockSpec((m, D_MODEL), lambda j: (0, 0)),
            pl.BlockSpec((D_MODEL, tn), lambda j: (0, j)),
            pl.BlockSpec((1, tn), lambda j: (0, j)),
        ],
        out_specs=pl.BlockSpec((m, tn), lambda j: (0, j)),
        out_shape=jax.ShapeDtypeStruct((m, n), F32),
        compiler_params=_cp(("arbitrary",)),
        name="ada_mod",
    )(c_all, w_ada, b_ada.reshape(1, n))


def _norm_mod(x, lnw, sc, sh):
    y = x * lax.rsqrt(jnp.mean(x * x, axis=-1, keepdims=True) + EPS)
    return (y * lnw) * (1.0 + sc) + sh


def _inproj_kernel(x_ref, lnw_ref, sc_ref, sh_ref, w_ref, o_ref, h_scr):
    @pl.when(pl.program_id(1) == 0)
    def _():
        h_scr[...] = _norm_mod(x_ref[...], lnw_ref[...], sc_ref[...], sh_ref[...]).astype(BF16)

    o_ref[...] = jnp.dot(h_scr[...], w_ref[...], preferred_element_type=F32)


def _in_proj(x2d, lnw, sc, sh, w_bf16, rows_per_mod, tm):
    m = x2d.shape[0]
    tn = 1024
    if rows_per_mod == 1:
        mod_spec = pl.BlockSpec((tm, D_MODEL), lambda i, j: (i, 0))
        sc, sh = sc.reshape(m, D_MODEL), sh.reshape(m, D_MODEL)
    else:
        assert rows_per_mod % tm == 0
        mod_spec = pl.BlockSpec((None, 1, D_MODEL), lambda i, j: (i // (rows_per_mod // tm), 0, 0))
    return pl.pallas_call(
        _inproj_kernel,
        grid=(m // tm, PROJ_W // tn),
        in_specs=[
            pl.BlockSpec((tm, D_MODEL), lambda i, j: (i, 0)),
            pl.BlockSpec((1, D_MODEL), lambda i, j: (0, 0)),
            mod_spec,
            mod_spec,
            pl.BlockSpec((D_MODEL, tn), lambda i, j: (0, j)),
        ],
        out_specs=pl.BlockSpec((tm, tn), lambda i, j: (i, j)),
        out_shape=jax.ShapeDtypeStruct((m, PROJ_W), F32),
        scratch_shapes=[pltpu.VMEM((tm, D_MODEL), BF16)],
        compiler_params=_cp(("arbitrary", "arbitrary")),
        name="in_proj",
    )(x2d, lnw.reshape(1, D_MODEL), sc, sh, w_bf16)


def _tri_masks(n, chunk):
    r = lax.broadcasted_iota(jnp.int32, (n, n), 0)
    c = lax.broadcasted_iota(jnp.int32, (n, n), 1)
    same = (r // chunk) == (c // chunk)
    return same, same & (r >= c), same & (r > c)


def _gates_kernel(ba_ref, alog_ref, dtb_ref, beta_ref, gc_ref, eg_ref, ek_ref, el_ref, gcrow_ref):
    same, causal, _ = _tri_masks(GDN_GROUP, GDN_CHUNK)
    lower01 = jnp.where(causal, 1.0, 0.0)
    ones01 = jnp.where(same, 1.0, 0.0)
    nega = -jnp.exp(alog_ref[...])
    dtb = dtb_ref[...]
    t = ba_ref.shape[0]

    def body(i, carry):
        r0 = pl.multiple_of(i * GDN_GROUP, GDN_GROUP)
        x = ba_ref[pl.ds(r0, GDN_GROUP), :]
        g = nega * _softplus(x + dtb)
        gc = _dot_exact_lhs01(lower01, g)
        gl = _dot_exact_lhs01(ones01, g)
        beta_ref[pl.ds(r0, GDN_GROUP), :] = _sigmoid(x)
        gc_ref[pl.ds(r0, GDN_GROUP), :] = gc
        eg_ref[pl.ds(r0, GDN_GROUP), :] = jnp.exp(gc)
        ek_ref[pl.ds(r0, GDN_GROUP), :] = jnp.exp(gl - gc)
        el_ref[pl.ds(r0, GDN_GROUP), :] = jnp.exp(gl)
        gct = gc.T
        for h in range(DN_HEADS):
            gcrow_ref[h, :, pl.ds(r0, GDN_GROUP)] = gct[DN_HEADS + h:DN_HEADS + h + 1, :]
        return carry

    lax.fori_loop(0, t // GDN_GROUP, body, 0)


def _gdn_gates(proj3, alog_lane, dtb_lane):
    b, t, _ = proj3.shape
    col = pl.BlockSpec((None, t, LANE), lambda i: (i, 0, 0))
    shp = jax.ShapeDtypeStruct((b, t, LANE), F32)
    return pl.pallas_call(
        _gates_kernel,
        grid=(b,),
        in_specs=[
            pl.BlockSpec((None, t, LANE), lambda i: (i, 0, C_BA // LANE)),
            pl.BlockSpec((1, LANE), lambda i: (0, 0)),
            pl.BlockSpec((1, LANE), lambda i: (0, 0)),
        ],
        out_specs=[col, col, col, col, col, pl.BlockSpec((None, DN_HEADS, 1, t), lambda i: (i, 0, 0, 0))],
        out_shape=[shp, shp, shp, shp, shp, jax.ShapeDtypeStruct((b, DN_HEADS, 1, t), F32)],
        compiler_params=_cp(("arbitrary",)),
        name="gdn_gates",
    )(proj3, alog_lane, dtb_lane)


def _l2norm(x):
    return x * lax.rsqrt(jnp.sum(x * x, axis=-1, keepdims=True) + EPS)


GDN_HPS = 2


def _gdn_kernel(q_ref, k_ref, v_ref, z_ref, beta_ref, gc_ref, eg_ref, ek_ref, el_ref, gcrow_ref,
                cwq_ref, cwk_ref, cwv_ref, nw_ref, o_ref, s_ref, pad_scr, qn_scr, kn_scr, vn_scr, oacc_scr, s_scr):
    t = q_ref.shape[0]
    wdt = GDN_HPS * LANE
    h0 = pl.program_id(1) * GDN_HPS
    pad = 8

    def conv_silu(u_ref, cw_ref):
        pad_scr[0:pad, :] = jnp.zeros((pad, wdt), F32)
        pad_scr[pad:pad + t, :] = u_ref[...]
        y = cw_ref[DN_CONV - 1:DN_CONV, :] * pad_scr[pad:pad + t, :]
        for i in range(DN_CONV - 1):
            off = pad - (DN_CONV - 1) + i
            y = y + cw_ref[i:i + 1, :] * pad_scr[off:off + t, :]
        return _silu(y)

    yq = conv_silu(q_ref, cwq_ref)
    for hh in range(GDN_HPS):
        sl = slice(hh * LANE, (hh + 1) * LANE)
        qn_scr[:, sl] = _l2norm(yq[:, sl]) * (DN_DK ** -0.5)
    yk = conv_silu(k_ref, cwk_ref)
    for hh in range(GDN_HPS):
        sl = slice(hh * LANE, (hh + 1) * LANE)
        kn_scr[:, sl] = _l2norm(yk[:, sl])
    vn_scr[...] = conv_silu(v_ref, cwv_ref)
    s_scr[...] = jnp.zeros(s_scr.shape, F32)

    n = GDN_GROUP
    c = GDN_CHUNK
    _, causal, strict = _tri_masks(n, c)
    rr = lax.broadcasted_iota(jnp.int32, (n, n), 0)
    cc = lax.broadcasted_iota(jnp.int32, (n, n), 1)
    eye = jnp.where(rr == cc, 1.0, 0.0)
    lane = lax.broadcasted_iota(jnp.int32, (n, LANE), 1)

    def pick(ref, r0, sel):
        return jnp.sum(jnp.where(sel, ref[pl.ds(r0, n), :], 0.0), axis=-1, keepdims=True)

    def head_group(hh, r0):
        sl = slice(hh * LANE, (hh + 1) * LANE)
        sel_b = lane == h0 + hh
        sel_g = lane == h0 + hh + DN_HEADS
        q = qn_scr[pl.ds(r0, n), sl]
        k = kn_scr[pl.ds(r0, n), sl]
        v = vn_scr[pl.ds(r0, n), sl]
        beta = pick(beta_ref, r0, sel_b)
        gc = pick(gc_ref, r0, sel_g)
        eg = pick(eg_ref, r0, sel_g)
        ek = pick(ek_ref, r0, sel_g)
        el = pick(el_ref, r0, sel_g)
        gcrow = gcrow_ref[hh, :, pl.ds(r0, n)]
        decay = jnp.where(causal, jnp.exp(gc - gcrow), 0.0)
        a_low = jnp.where(strict, beta * _dot_nt(k, k) * decay, 0.0)
        p = -a_low
        tinv = eye + p
        for _ in range(5):
            p = _dot3(p, p)
            tinv = tinv + _dot3(tinv, p)
        rhs = jnp.concatenate([v * beta, k * (beta * eg)], axis=1)
        sol = _dot3(tinv, rhs)
        value = sol[:, :DN_DV]
        kcum = sol[:, DN_DV:]
        intra = _dot_nt(q, k) * decay
        q_dec = q * eg
        k_dec = k * ek
        for j in range(n // c):
            lo, hi = j * c, (j + 1) * c
            s = s_scr[hh]
            r = _dot(jnp.concatenate([kcum[lo:hi], q_dec[lo:hi]], axis=0), s)
            v_new = value[lo:hi] - r[:c]
            parts = []
            if lo:
                parts.append(jnp.zeros((lo, DN_DV), F32))
            parts.append(v_new)
            if hi < n:
                parts.append(jnp.zeros((n - hi, DN_DV), F32))
            o = r[c:] + _dot(intra[lo:hi], jnp.concatenate(parts, axis=0))
            oacc_scr[pl.ds(r0 + lo, c), sl] = o
            s_scr[hh] = s * el[lo:lo + 1] + _dot(k_dec[lo:hi].T, v_new)

    def body(i, carry):
        r0 = pl.multiple_of(i * n, n)
        for hh in range(GDN_HPS):
            head_group(hh, r0)
        return carry

    lax.fori_loop(0, t // n, body, 0)
    for hh in range(GDN_HPS):
        sl = slice(hh * LANE, (hh + 1) * LANE)
        o = oacc_scr[:, sl]
        y = o * lax.rsqrt(jnp.mean(o * o, axis=-1, keepdims=True) + EPS)
        o_ref[:, sl] = (y * nw_ref[...]) * _silu(z_ref[:, sl])
    s_ref[...] = s_scr[...]


def _gdn_prompt(proj3, gates, conv_w, norm_w):
    b, t, _ = proj3.shape
    beta, gc, eg, ek, el, gcrow = gates
    hps = GDN_HPS
    wdt = hps * LANE
    steps = DN_HEADS // hps

    def colspec(base):
        return pl.BlockSpec((None, t, wdt), lambda i, j, base=base: (i, 0, base + j))

    gate = pl.BlockSpec((None, t, LANE), lambda i, j: (i, 0, 0))

    def cwspec(base):
        return pl.BlockSpec((DN_CONV, wdt), lambda i, j, base=base: (0, base + j))

    return pl.pallas_call(
        _gdn_kernel,
        grid=(b, steps),
        in_specs=[
            colspec(0), colspec(steps), colspec(2 * steps), colspec(C_Z // wdt),
            gate, gate, gate, gate, gate,
            pl.BlockSpec((None, hps, 1, t), lambda i, j: (i, j, 0, 0)),
            cwspec(0), cwspec(steps), cwspec(2 * steps),
            pl.BlockSpec((1, DN_DV), lambda i, j: (0, 0)),
        ],
        out_specs=[
            pl.BlockSpec((None, t, wdt), lambda i, j: (i, 0, j)),
            pl.BlockSpec((None, hps, DN_DK, DN_DV), lambda i, j: (i, j, 0, 0)),
        ],
        out_shape=[
            jax.ShapeDtypeStruct((b, t, DN_VW), F32),
            jax.ShapeDtypeStruct((b, DN_HEADS, DN_DK, DN_DV), F32),
        ],
        scratch_shapes=[
            pltpu.VMEM((t + 8, wdt), F32),
            pltpu.VMEM((t, wdt), F32),
            pltpu.VMEM((t, wdt), F32),
            pltpu.VMEM((t, wdt), F32),
            pltpu.VMEM((t, wdt), F32),
            pltpu.VMEM((hps, DN_DK, DN_DV), F32),
        ],
        compiler_params=_cp(("arbitrary", "arbitrary")),
        name="gdn_prompt",
    )(proj3, proj3, proj3, proj3, beta, gc, eg, ek, el, gcrow, conv_w, conv_w, conv_w, norm_w.reshape(1, DN_DV))


def _gdn_step_kernel(p_ref, cprev_ref, s_ref, cw_ref, alog_ref, dtb_ref, nw_ref, o_ref, cnew_ref, snew_ref):
    u = p_ref[:, C_QKV:C_QKV + DN_CONV_CH]
    prev = cprev_ref[...]
    y = cw_ref[DN_CONV - 1:DN_CONV, :] * u
    for i in range(DN_CONV - 1):
        y = y + cw_ref[i:i + 1, :] * prev[i:i + 1, :]
    y = _silu(y)
    cnew_ref[0:DN_CONV - 2, :] = prev[1:DN_CONV - 1, :]
    cnew_ref[DN_CONV - 2:DN_CONV - 1, :] = u
    ba = p_ref[:, C_BA:C_BA + LANE]
    beta_l = _sigmoid(ba)
    a_l = jnp.exp(-jnp.exp(alog_ref[...]) * _softplus(ba + dtb_ref[...]))
    lane = lax.broadcasted_iota(jnp.int32, (1, LANE), 1)
    row8 = lax.broadcasted_iota(jnp.int32, (8, LANE), 0)
    for h in range(DN_HEADS):
        q = _l2norm(y[:, h * DN_DK:(h + 1) * DN_DK]) * (DN_DK ** -0.5)
        k = _l2norm(y[:, DN_QK + h * DN_DK:DN_QK + (h + 1) * DN_DK])
        v = y[:, 2 * DN_QK + h * DN_DV:2 * DN_QK + (h + 1) * DN_DV]
        beta = jnp.sum(jnp.where(lane == h, beta_l, 0.0), axis=-1, keepdims=True)
        a = jnp.sum(jnp.where(lane == h + DN_HEADS, a_l, 0.0), axis=-1, keepdims=True)
        s = s_ref[h]
        kq = jnp.where(row8 == 0, k, jnp.where(row8 == 1, q, 0.0))
        r = _dot3(kq, s)
        v_new = beta * (v - a * r[0:1])
        o = a * r[1:2] + jnp.sum(q * k, axis=-1, keepdims=True) * v_new
        k8 = jnp.where(row8 == 0, k, 0.0)
        v8 = jnp.where(row8 == 0, v_new, 0.0)
        snew_ref[h] = s * a + _dot3(k8.T, v8)
        yo = o * lax.rsqrt(jnp.mean(o * o, axis=-1, keepdims=True) + EPS)
        z = p_ref[:, C_Z + h * DN_DV:C_Z + (h + 1) * DN_DV]
        o_ref[:, h * DN_DV:(h + 1) * DN_DV] = (yo * nw_ref[...]) * _silu(z)


def _gdn_step(proj_s, conv_prev, s0, conv_w, alog_lane, dtb_lane, norm_w):
    b = proj_s.shape[0]
    return pl.pallas_call(
        _gdn_step_kernel,
        grid=(b,),
        in_specs=[
            pl.BlockSpec((None, 1, PROJ_W), lambda i: (i, 0, 0)),
            pl.BlockSpec((None, DN_CONV - 1, DN_CONV_CH), lambda i: (i, 0, 0)),
            pl.BlockSpec((None, DN_HEADS, DN_DK, DN_DV), lambda i: (i, 0, 0, 0)),
            pl.BlockSpec((DN_CONV, DN_CONV_CH), lambda i: (0, 0)),
            pl.BlockSpec((1, LANE), lambda i: (0, 0)),
            pl.BlockSpec((1, LANE), lambda i: (0, 0)),
            pl.BlockSpec((1, DN_DV), lambda i: (0, 0)),
        ],
        out_specs=[
            pl.BlockSpec((None, 1, DN_VW), lambda i: (i, 0, 0)),
            pl.BlockSpec((None, DN_CONV - 1, DN_CONV_CH), lambda i: (i, 0, 0)),
            pl.BlockSpec((None, DN_HEADS, DN_DK, DN_DV), lambda i: (i, 0, 0, 0)),
        ],
        out_shape=[
            jax.ShapeDtypeStruct((b, 1, DN_VW), F32),
            jax.ShapeDtypeStruct((b, DN_CONV - 1, DN_CONV_CH), F32),
            jax.ShapeDtypeStruct((b, DN_HEADS, DN_DK, DN_DV), F32),
        ],
        compiler_params=_cp(("arbitrary",)),
        name="gdn_step",
    )(proj_s.reshape(b, 1, PROJ_W), conv_prev, s0, conv_w, alog_lane, dtb_lane, norm_w.reshape(1, DN_DV))


def _alibi_slope(h):
    return float(2.0 ** (-8.0 * (h + 1) / SW_HEADS))


def _head_rms(x, w):
    return (x * lax.rsqrt(jnp.mean(x * x, axis=-1, keepdims=True) + EPS)) * w


SWA_HB = 4


def _swa_kernel(sinks_ref, q_ref, kc_ref, kp_ref, vc_ref, vp_ref, qw_ref, kw_ref, o_ref, kn_ref):
    blk = pl.program_id(1)
    w = WINDOW
    rows = SWA_HB * w
    qi = lax.broadcasted_iota(jnp.int32, (rows, 2 * w), 0)
    kj = lax.broadcasted_iota(jnp.int32, (rows, 2 * w), 1)
    dist = (qi & (w - 1)) + w - kj
    valid = (dist >= 0) & (dist < w) & ((kj >= w) | (blk > 0))
    distf = dist.astype(F32)
    stripe = lax.broadcasted_iota(jnp.int32, (rows, 1), 0) // w
    kc = kc_ref[...]
    kp = kp_ref[...]
    kbands, vbands = [], []
    for g in range(SW_KV_HEADS):
        sl = slice(g * SW_HD, (g + 1) * SW_HD)
        kcn = _head_rms(kc[:, sl], kw_ref[...])
        kn_ref[:, sl] = kcn
        kbands.append(jnp.concatenate([_head_rms(kp[:, sl], kw_ref[...]), kcn], axis=0))
        vbands.append(jnp.concatenate([vp_ref[:, sl], vc_ref[:, sl]], axis=0))
    for hb in range(SW_HEADS // SWA_HB):
        heads = range(hb * SWA_HB, (hb + 1) * SWA_HB)
        g = heads[0] // SW_GROUP
        qs = jnp.concatenate([_head_rms(q_ref[:, h * SW_HD:(h + 1) * SW_HD], qw_ref[...]) for h in heads], axis=0)
        slope = jnp.zeros((rows, 1), F32)
        sink = jnp.zeros((rows, 1), F32)
        for i, h in enumerate(heads):
            slope = jnp.where(stripe == i, _alibi_slope(h), slope)
            sink = jnp.where(stripe == i, sinks_ref[h], sink)
        s = _dot_nt(qs, kbands[g]) * (SW_HD ** -0.5) - slope * distf
        s = jnp.where(valid, s, -jnp.inf)
        m = jnp.maximum(jnp.max(s, axis=-1, keepdims=True), sink)
        p = jnp.exp(s - m)
        den = jnp.sum(p, axis=-1, keepdims=True) + jnp.exp(sink - m)
        o = _dot(p / den, vbands[g])
        for i, h in enumerate(heads):
            o_ref[:, h * SW_HD:(h + 1) * SW_HD] = o[i * w:(i + 1) * w]


def _swa_prompt(proj3, sinks, qw, kw):
    b, t, _ = proj3.shape
    nb = t // WINDOW
    kcol, vcol = C_SK // LANE, C_SV // LANE

    def cur(col):
        return pl.BlockSpec((None, WINDOW, SW_KVW), lambda i, j, s, col=col: (i, j, col))

    def prev(col):
        return pl.BlockSpec((None, WINDOW, SW_KVW), lambda i, j, s, col=col: (i, jnp.maximum(j - 1, 0), col))

    return pl.pallas_call(
        _swa_kernel,
        grid_spec=pltpu.PrefetchScalarGridSpec(
            num_scalar_prefetch=1,
            grid=(b, nb),
            in_specs=[
                pl.BlockSpec((None, WINDOW, SW_QW), lambda i, j, s: (i, j, C_SQ // SW_QW)),
                cur(kcol), prev(kcol), cur(vcol), prev(vcol),
                pl.BlockSpec((1, SW_HD), lambda i, j, s: (0, 0)),
                pl.BlockSpec((1, SW_HD), lambda i, j, s: (0, 0)),
            ],
            out_specs=[
                pl.BlockSpec((None, WINDOW, SW_QW), lambda i, j, s: (i, j, 0)),
                pl.BlockSpec((None, WINDOW, SW_KVW), lambda i, j, s: (i, j, 0)),
            ],
        ),
        out_shape=[
            jax.ShapeDtypeStruct((b, t, SW_QW), F32),
            jax.ShapeDtypeStruct((b, t, SW_KVW), F32),
        ],
        compiler_params=_cp(("arbitrary", "arbitrary")),
        name="swa_prompt",
    )(sinks, proj3, proj3, proj3, proj3, proj3, qw.reshape(1, SW_HD), kw.reshape(1, SW_HD))


def _swa_step_kernel(sinks_ref, p_ref, kbuf_ref, vbuf_ref, qw_ref, kw_ref, o_ref, knew_ref, vnew_ref, kcat, vcat):
    w = kbuf_ref.shape[0]
    rows = kcat.shape[0]
    knew = p_ref[:, C_SK:C_SK + SW_KVW]
    vnew = p_ref[:, C_SV:C_SV + SW_KVW]
    kcat[...] = jnp.zeros(kcat.shape, F32)
    vcat[...] = jnp.zeros(vcat.shape, F32)
    kcat[0:w, :] = kbuf_ref[...]
    vcat[0:w, :] = vbuf_ref[...]
    for g in range(SW_KV_HEADS):
        sl = slice(g * SW_HD, (g + 1) * SW_HD)
        kcat[w:w + 1, sl] = _head_rms(knew[:, sl], kw_ref[...])
    vcat[w:w + 1, :] = vnew
    knew_ref[...] = kcat[1:w + 1, :]
    vnew_ref[...] = vcat[1:w + 1, :]
    j = lax.broadcasted_iota(jnp.int32, (rows, 1), 0)
    dist = w - j
    valid = (dist >= 0) & (dist < WINDOW)
    distf = dist.astype(F32)
    for h in range(SW_HEADS):
        g = h // SW_GROUP
        sl = slice(g * SW_HD, (g + 1) * SW_HD)
        qh = _head_rms(p_ref[:, C_SQ + h * SW_HD:C_SQ + (h + 1) * SW_HD], qw_ref[...])
        s = jnp.sum(kcat[:, sl] * qh, axis=-1, keepdims=True) * (SW_HD ** -0.5) - _alibi_slope(h) * distf
        s = jnp.where(valid, s, -jnp.inf)
        sink = sinks_ref[h]
        m = jnp.maximum(jnp.max(s, axis=0, keepdims=True), sink)
        p = jnp.exp(s - m)
        den = jnp.sum(p, axis=0, keepdims=True) + jnp.exp(sink - m)
        o_ref[:, h * SW_HD:(h + 1) * SW_HD] = jnp.sum((p / den) * vcat[:, sl], axis=0, keepdims=True)


def _swa_step(proj_s, kbuf, vbuf, sinks, qw, kw):
    b = proj_s.shape[0]
    w = kbuf.shape[1]
    rows = w + 8
    buf = pl.BlockSpec((None, w, SW_KVW), lambda i, s: (i, 0, 0))
    return pl.pallas_call(
        _swa_step_kernel,
        grid_spec=pltpu.PrefetchScalarGridSpec(
            num_scalar_prefetch=1,
            grid=(b,),
            in_specs=[
                pl.BlockSpec((None, 1, PROJ_W), lambda i, s: (i, 0, 0)),
                buf, buf,
                pl.BlockSpec((1, SW_HD), lambda i, s: (0, 0)),
                pl.BlockSpec((1, SW_HD), lambda i, s: (0, 0)),
            ],
            out_specs=[pl.BlockSpec((None, 1, SW_QW), lambda i, s: (i, 0, 0)), buf, buf],
            scratch_shapes=[pltpu.VMEM((rows, SW_KVW), F32), pltpu.VMEM((rows, SW_KVW), F32)],
        ),
        out_shape=[
            jax.ShapeDtypeStruct((b, 1, SW_QW), F32),
            jax.ShapeDtypeStruct((b, w, SW_KVW), F32),
            jax.ShapeDtypeStruct((b, w, SW_KVW), F32),
        ],
        compiler_params=_cp(("arbitrary",)),
        name="swa_step",
    )(sinks, proj_s.reshape(b, 1, PROJ_W), kbuf, vbuf, qw.reshape(1, SW_HD), kw.reshape(1, SW_HD))


def _merge_kernel(ya_ref, yb_ref, wa_ref, wb_ref, ga_ref, gb_ref, o_ref):
    a = _dot(ya_ref[...], wa_ref[...])
    b = _dot(yb_ref[...], wb_ref[...])
    o_ref[...] = (_sigmoid(ga_ref[...]) * a + _sigmoid(gb_ref[...]) * b).astype(BF16)


def _merge(ya, yb, wa, wb, proj, tm):
    m = ya.shape[0]
    tn = 1024
    return pl.pallas_call(
        _merge_kernel,
        grid=(m // tm, D_MODEL // tn),
        in_specs=[
            pl.BlockSpec((tm, DN_VW), lambda i, j: (i, 0)),
            pl.BlockSpec((tm, SW_QW), lambda i, j: (i, 0)),
            pl.BlockSpec((DN_VW, tn), lambda i, j: (0, j)),
            pl.BlockSpec((SW_QW, tn), lambda i, j: (0, j)),
            pl.BlockSpec((tm, tn), lambda i, j: (i, C_GA // tn + j)),
            pl.BlockSpec((tm, tn), lambda i, j: (i, C_GB // tn + j)),
        ],
        out_specs=pl.BlockSpec((tm, tn), lambda i, j: (i, j)),
        out_shape=jax.ShapeDtypeStruct((m, D_MODEL), BF16),
        compiler_params=_cp(("arbitrary", "arbitrary")),
        name="merge",
    )(ya, yb, wa, wb, proj, proj)


def _outproj_kernel(mg_ref, w_ref, x_ref, gt_ref, o_ref):
    o_ref[...] = x_ref[...] + gt_ref[...] * jnp.dot(mg_ref[...], w_ref[...], preferred_element_type=F32)


def _mod_spec(rows_per_mod, tm, tn):
    if rows_per_mod == 1:
        return pl.BlockSpec((tm, tn), lambda i, j: (i, j))
    return pl.BlockSpec((None, 1, tn), lambda i, j: (i // (rows_per_mod // tm), 0, j))


def _out_proj(merged, w_bf16, x2d, gt, rows_per_mod, tm):
    m = x2d.shape[0]
    tn = 1024
    if rows_per_mod == 1:
        gt = gt.reshape(m, D_MODEL)
    return pl.pallas_call(
        _outproj_kernel,
        grid=(m // tm, D_MODEL // tn),
        in_specs=[
            pl.BlockSpec((tm, D_MODEL), lambda i, j: (i, 0)),
            pl.BlockSpec((D_MODEL, tn), lambda i, j: (0, j)),
            pl.BlockSpec((tm, tn), lambda i, j: (i, j)),
            _mod_spec(rows_per_mod, tm, tn),
        ],
        out_specs=pl.BlockSpec((tm, tn), lambda i, j: (i, j)),
        out_shape=jax.ShapeDtypeStruct((m, D_MODEL), F32),
        compiler_params=_cp(("arbitrary", "arbitrary")),
        name="out_proj",
    )(merged, w_bf16, x2d, gt)


def _router_kernel(x_ref, lnw_ref, sc_ref, sh_ref, rw_ref, rb_ref, h_ref, idx_ref, w_ref):
    hmod = _norm_mod(x_ref[...], lnw_ref[...], sc_ref[...], sh_ref[...])
    h_ref[...] = hmod
    logits = _dot3(hmod, rw_ref[...]) + rb_ref[...]
    lane = lax.broadcasted_iota(jnp.int32, logits.shape, 1)
    cur = jnp.where(lane < N_EXPERTS, logits, -jnp.inf)
    vals, idxs = [], []
    for _ in range(TOP_K):
        m = jnp.max(cur, axis=-1, keepdims=True)
        ix = jnp.min(jnp.where(cur == m, lane, LANE), axis=-1, keepdims=True)
        vals.append(m)
        idxs.append(ix)
        cur = jnp.where(lane == ix, -jnp.inf, cur)
    es = [jnp.exp(v - vals[0]) for v in vals]
    den = es[0] + es[1] + es[2] + es[3]
    idx_out = jnp.zeros(logits.shape, jnp.int32)
    w_out = jnp.zeros(logits.shape, F32)
    for k in range(TOP_K):
        idx_out = jnp.where(lane == k, idxs[k], idx_out)
        w_out = jnp.where(lane == k, es[k] / den, w_out)
    idx_ref[...] = idx_out
    w_ref[...] = w_out


def _router(x2d, lnw, sc, sh, rw_pad, rb_pad, rows_per_mod, tm):
    m = x2d.shape[0]
    if rows_per_mod == 1:
        mod_spec = pl.BlockSpec((tm, D_MODEL), lambda i: (i, 0))
        sc, sh = sc.reshape(m, D_MODEL), sh.reshape(m, D_MODEL)
    else:
        mod_spec = pl.BlockSpec((None, 1, D_MODEL), lambda i: (i // (rows_per_mod // tm), 0, 0))
    row = pl.BlockSpec((tm, D_MODEL), lambda i: (i, 0))
    small = pl.BlockSpec((tm, LANE), lambda i: (i, 0))
    return pl.pallas_call(
        _router_kernel,
        grid=(m // tm,),
        in_specs=[
            row,
            pl.BlockSpec((1, D_MODEL), lambda i: (0, 0)),
            mod_spec, mod_spec,
            pl.BlockSpec((D_MODEL, LANE), lambda i: (0, 0)),
            pl.BlockSpec((1, LANE), lambda i: (0, 0)),
        ],
        out_specs=[row, small, small],
        out_shape=[
            jax.ShapeDtypeStruct((m, D_MODEL), F32),
            jax.ShapeDtypeStruct((m, LANE), jnp.int32),
            jax.ShapeDtypeStruct((m, LANE), F32),
        ],
        compiler_params=_cp(("arbitrary",)),
        name="router",
    )(x2d, lnw.reshape(1, D_MODEL), sc, sh, rw_pad, rb_pad)


SCATTER_TOK = 256
DMA_UNROLL = 8


def _scatter_kernel(zl_ref, dest_ref, hp_ref, hs_ref, xs_hbm, zbuf, sem, zsem):
    i = pl.program_id(0)
    rb = MOE_ROWS
    n_prompt_steps = pl.num_programs(0) - 1

    @pl.when(i == 0)
    def _():
        zbuf[...] = jnp.zeros(zbuf.shape, F32)

        def zero_copy(n):
            return pltpu.make_async_copy(zbuf, xs_hbm.at[pl.ds(zl_ref[n] * rb, rb)], zsem)

        def start(n, c):
            @pl.when(zl_ref[n] >= 0)
            def _():
                zero_copy(n).start()
            return c

        def wait(n, c):
            @pl.when(zl_ref[n] >= 0)
            def _():
                zero_copy(n).wait()
            return c

        lax.fori_loop(0, zl_ref.shape[0], start, 0)
        lax.fori_loop(0, zl_ref.shape[0], wait, 0)

    def scatter(src_ref):
        n_tok = src_ref.shape[0]

        def start(a, c):
            t = lax.shift_right_logical(a, 2)
            pltpu.make_async_copy(src_ref.at[pl.ds(t, 1)], xs_hbm.at[pl.ds(dest_ref[0, a], 1)], sem).start()
            return c

        lax.fori_loop(0, n_tok * TOP_K, start, 0, unroll=DMA_UNROLL)
        for _ in range(TOP_K):
            pltpu.make_async_copy(src_ref, xs_hbm.at[pl.ds(0, n_tok)], sem).wait()

    @pl.when(i < n_prompt_steps)
    def _():
        scatter(hp_ref)

    @pl.when(i == n_prompt_steps)
    def _():
        scatter(hs_ref)


def _scatter_rows(h_p, h_s, dest, zero_blocks, n_rows):
    assert TOP_K == 4
    n_p, n_s = h_p.shape[0], h_s.shape[0]
    steps_p = n_p // SCATTER_TOK
    per = SCATTER_TOK * TOP_K
    dest_s = jnp.concatenate([dest[n_p * TOP_K:], jnp.zeros((per - n_s * TOP_K,), jnp.int32)])
    dest3 = jnp.concatenate([dest[:n_p * TOP_K], dest_s]).reshape(steps_p + 1, 1, per)
    return pl.pallas_call(
        _scatter_kernel,
        grid_spec=pltpu.PrefetchScalarGridSpec(
            num_scalar_prefetch=1,
            grid=(steps_p + 1,),
            in_specs=[
                pl.BlockSpec((None, 1, per), lambda i, zl: (i, 0, 0), memory_space=pltpu.SMEM),
                pl.BlockSpec((SCATTER_TOK, D_MODEL), lambda i, zl: (jnp.minimum(i, steps_p - 1), 0)),
                pl.BlockSpec((n_s, D_MODEL), lambda i, zl: (0, 0)),
            ],
            out_specs=pl.BlockSpec(memory_space=pl.ANY),
            scratch_shapes=[
                pltpu.VMEM((MOE_ROWS, D_MODEL), F32),
                pltpu.SemaphoreType.DMA(()),
                pltpu.SemaphoreType.DMA(()),
            ],
        ),
        out_shape=jax.ShapeDtypeStruct((n_rows, D_MODEL), F32),
        compiler_params=_cp(("arbitrary",)),
        name="moe_scatter",
    )(zero_blocks, dest3, h_p, h_s)


def _experts_kernel(sbe_ref, sbb_ref, sbn_ref, tail_ref, xs_hbm, wg_ref, wl_ref, wd_ref, bg_ref, bl_ref, bd_ref,
                    ys_hbm, xb_scr, acc_scr, wg_scr, wl_scr, wd_scr, sem_in, sem_out):
    s = pl.program_id(0)
    j = pl.program_id(1)
    last_j = pl.num_programs(1) - 1
    nblk = sbn_ref[s]
    blk0 = sbb_ref[s]
    rb = MOE_ROWS

    def in_copy(b):
        return pltpu.make_async_copy(xs_hbm.at[pl.ds((blk0 + b) * rb, rb)], acc_scr.at[pl.ds(b * rb, rb)], sem_in)

    def out_copy(first_blk, b):
        return pltpu.make_async_copy(acc_scr.at[pl.ds(b * rb, rb)], ys_hbm.at[pl.ds((first_blk + b) * rb, rb)], sem_out)

    def loop(n, fn):
        def body(b, c):
            fn(b)
            return c
        lax.fori_loop(0, n, body, 0)

    @pl.when(j == 0)
    def _():
        @pl.when(s > 0)
        def _():
            prev0 = sbb_ref[s - 1]
            loop(sbn_ref[s - 1], lambda b: out_copy(prev0, b).wait())

        loop(nblk, lambda b: in_copy(b).start())
        loop(nblk, lambda b: in_copy(b).wait())

        def cast(b):
            r0 = pl.multiple_of(b * rb, rb)
            xb_scr[pl.ds(r0, rb), :] = acc_scr[pl.ds(r0, rb), :].astype(BF16)
            acc_scr[pl.ds(r0, rb), :] = jnp.broadcast_to(bd_ref[...], (rb, D_MODEL))
        loop(nblk, cast)

    @pl.when(nblk > 0)
    def _():
        wg_scr[...] = wg_ref[...].astype(BF16)
        wl_scr[...] = wl_ref[...].astype(BF16)
        wd_scr[...] = wd_ref[...].astype(BF16)

        def mlp(r0, rows):
            x = xb_scr[pl.ds(r0, rows), :]
            glu = jnp.dot(x, wg_scr[...], preferred_element_type=F32) + bg_ref[...]
            lin = jnp.dot(x, wl_scr[...], preferred_element_type=F32) + bl_ref[...]
            glu = jnp.minimum(glu, SWIGLU_LIMIT)
            lin = jnp.clip(lin, -SWIGLU_LIMIT, SWIGLU_LIMIT)
            act = glu * _sigmoid(SWIGLU_ALPHA * glu) * (lin + 1.0)
            acc_scr[pl.ds(r0, rows), :] += jnp.dot(act.astype(BF16), wd_scr[...], preferred_element_type=F32)

        loop(nblk // 2, lambda p: mlp(pl.multiple_of(p * (2 * rb), 2 * rb), 2 * rb))

        @pl.when(lax.rem(nblk, 2) == 1)
        def _():
            mlp(pl.multiple_of((nblk - 1) * rb, rb), rb)

    @pl.when(j == last_j)
    def _():
        loop(nblk, lambda b: out_copy(blk0, b).start())

    @pl.when((s == pl.num_programs(0) - 1) & (j == last_j))
    def _():
        loop(nblk, lambda b: out_copy(blk0, b).wait())
        acc_scr[0:rb, :] = jnp.zeros((rb, D_MODEL), F32)

        def zero_copy(b):
            return pltpu.make_async_copy(acc_scr.at[pl.ds(0, rb)], ys_hbm.at[pl.ds(b * rb, rb)], sem_out)

        def start(b, c):
            zero_copy(b).start()
            return c

        def wait(b, c):
            zero_copy(b).wait()
            return c

        lax.fori_loop(tail_ref[0], tail_ref[1], start, 0)
        lax.fori_loop(tail_ref[0], tail_ref[1], wait, 0)


def _experts(xs, sb_e, sb_blk0, sb_nblk, tail, w_gate_up, b_gate_up, w_down, b_down):
    n_rows = xs.shape[0]
    n_sb = sb_e.shape[0]
    tf = MOE_TF
    nj = D_MODEL // tf
    rmax = MOE_SB_BLOCKS * MOE_ROWS

    def jj(s, j, n):
        return jnp.where(n[s] > 0, j, nj - 1)

    return pl.pallas_call(
        _experts_kernel,
        grid_spec=pltpu.PrefetchScalarGridSpec(
            num_scalar_prefetch=4,
            grid=(n_sb, nj),
            in_specs=[
                pl.BlockSpec(memory_space=pl.ANY),
                pl.BlockSpec((None, D_MODEL, tf), lambda s, j, e, b, n, tl: (e[s], 0, jj(s, j, n))),
                pl.BlockSpec((None, D_MODEL, tf), lambda s, j, e, b, n, tl: (e[s], 0, nj + jj(s, j, n))),
                pl.BlockSpec((None, tf, D_MODEL), lambda s, j, e, b, n, tl: (e[s], jj(s, j, n), 0)),
                pl.BlockSpec((None, 1, tf), lambda s, j, e, b, n, tl: (e[s], 0, jj(s, j, n))),
                pl.BlockSpec((None, 1, tf), lambda s, j, e, b, n, tl: (e[s], 0, nj + jj(s, j, n))),
                pl.BlockSpec((None, 1, D_MODEL), lambda s, j, e, b, n, tl: (e[s], 0, 0)),
            ],
            out_specs=pl.BlockSpec(memory_space=pl.ANY),
            scratch_shapes=[
                pltpu.VMEM((rmax, D_MODEL), BF16),
                pltpu.VMEM((rmax, D_MODEL), F32),
                pltpu.VMEM((D_MODEL, tf), BF16),
                pltpu.VMEM((D_MODEL, tf), BF16),
                pltpu.VMEM((tf, D_MODEL), BF16),
                pltpu.SemaphoreType.DMA(()),
                pltpu.SemaphoreType.DMA(()),
            ],
        ),
        out_shape=jax.ShapeDtypeStruct((n_rows, D_MODEL), F32),
        compiler_params=_cp(("arbitrary", "arbitrary")),
        name="moe_experts",
    )(sb_e, sb_blk0, sb_nblk, tail, xs, w_gate_up, w_gate_up, w_down,
      b_gate_up.reshape(N_EXPERTS, 1, 2 * D_MODEL), b_gate_up.reshape(N_EXPERTS, 1, 2 * D_MODEL),
      b_down.reshape(N_EXPERTS, 1, D_MODEL))


COMBINE_TOK = 128


def _combine_kernel(pos_ref, posn_ref, ys_hbm, x_ref, gt_ref, w_ref, o_ref, buf, sem):
    n = COMBINE_TOK * TOP_K
    i = pl.program_id(0)
    slot = lax.rem(i, 2)

    def issue(p_ref, sl):
        def start(a, c):
            pltpu.make_async_copy(ys_hbm.at[pl.ds(p_ref[0, a], 1)], buf.at[sl, pl.ds(a, 1)], sem.at[sl]).start()
            return c
        lax.fori_loop(0, n, start, 0, unroll=DMA_UNROLL)

    @pl.when(i == 0)
    def _():
        issue(pos_ref, 0)

    @pl.when(i + 1 < pl.num_programs(0))
    def _():
        issue(posn_ref, 1 - slot)

    pltpu.make_async_copy(ys_hbm.at[pl.ds(0, n)], buf.at[slot], sem.at[slot]).wait()
    w = w_ref[...]
    lane = lax.broadcasted_iota(jnp.int32, w.shape, 1)
    y = jnp.zeros((COMBINE_TOK, D_MODEL), F32)
    for k in range(TOP_K):
        wk = jnp.sum(jnp.where(lane == k, w, 0.0), axis=-1, keepdims=True)
        y = y + wk * buf[slot, k * COMBINE_TOK:(k + 1) * COMBINE_TOK, :]
    o_ref[...] = x_ref[...] + gt_ref[...] * y


def _combine(ys, pos_kmajor, x2d, gt, top_w, rows_per_mod):
    m = x2d.shape[0]
    tm = COMBINE_TOK
    steps = m // tm
    if rows_per_mod == 1:
        gt = gt.reshape(m, D_MODEL)
        gt_spec = pl.BlockSpec((tm, D_MODEL), lambda i: (i, 0))
    else:
        gt_spec = pl.BlockSpec((None, 1, D_MODEL), lambda i: (i // (rows_per_mod // tm), 0, 0))
    return pl.pallas_call(
        _combine_kernel,
        grid=(steps,),
        in_specs=[
            pl.BlockSpec((None, 1, TOP_K * tm), lambda i: (i, 0, 0), memory_space=pltpu.SMEM),
            pl.BlockSpec((None, 1, TOP_K * tm), lambda i: (jnp.minimum(i + 1, steps - 1), 0, 0), memory_space=pltpu.SMEM),
            pl.BlockSpec(memory_space=pl.ANY),
            pl.BlockSpec((tm, D_MODEL), lambda i: (i, 0)),
            gt_spec,
            pl.BlockSpec((tm, LANE), lambda i: (i, 0)),
        ],
        out_specs=pl.BlockSpec((tm, D_MODEL), lambda i: (i, 0)),
        out_shape=jax.ShapeDtypeStruct((m, D_MODEL), F32),
        scratch_shapes=[pltpu.VMEM((2, TOP_K * tm, D_MODEL), F32), pltpu.SemaphoreType.DMA((2,))],
        compiler_params=_cp(("arbitrary",)),
        name="moe_combine",
    )(pos_kmajor, pos_kmajor, ys, x2d, gt, top_w)


def _routing_tables(top_idx):
    n_tok = top_idx.shape[0]
    n_assign = n_tok * TOP_K
    rb = MOE_ROWS
    n_blocks = -(-(n_assign + N_EXPERTS * (rb - 1)) // rb)
    n_rows = n_blocks * rb
    flat_e = top_idx.reshape(-1)
    onehot = (flat_e[:, None] == jnp.arange(N_EXPERTS, dtype=jnp.int32)[None, :]).astype(jnp.int32)
    csum = jnp.cumsum(onehot, axis=0)
    rank = jnp.sum((csum - onehot) * onehot, axis=1)
    counts = csum[-1]
    nblk_e = (counts + rb - 1) // rb
    blk_start = jnp.cumsum(nblk_e) - nblk_e
    dest = (blk_start * rb)[flat_e] + rank
    total_blk = jnp.sum(nblk_e)
    last_blk = jnp.where(nblk_e > 0, blk_start + nblk_e - 1, -1)
    bidx = jnp.arange(n_blocks, dtype=jnp.int32)
    zero_blocks = jnp.concatenate([last_blk, jnp.where(bidx >= total_blk, bidx, -1)]).astype(jnp.int32)
    n_sb_max = n_blocks // MOE_SB_BLOCKS + N_EXPERTS
    sb_per_e = (nblk_e + MOE_SB_BLOCKS - 1) // MOE_SB_BLOCKS
    sb_start = jnp.cumsum(sb_per_e) - sb_per_e
    total_sb = jnp.sum(sb_per_e)
    sidx = jnp.arange(n_sb_max, dtype=jnp.int32)
    e_of = jnp.clip(jnp.searchsorted(jnp.cumsum(sb_per_e), sidx, side="right"), 0, N_EXPERTS - 1).astype(jnp.int32)
    local = sidx - sb_start[e_of]
    active = sidx < total_sb
    last_e = e_of[jnp.maximum(total_sb - 1, 0)]
    sb_e = jnp.where(active, e_of, last_e).astype(jnp.int32)
    sb_blk0 = jnp.where(active, blk_start[e_of] + local * MOE_SB_BLOCKS, 0).astype(jnp.int32)
    sb_nblk = jnp.where(active, jnp.minimum(nblk_e[e_of] - local * MOE_SB_BLOCKS, MOE_SB_BLOCKS), 0).astype(jnp.int32)
    tail = jnp.stack([total_blk, jnp.int32(n_blocks)]).astype(jnp.int32)
    return dest.astype(jnp.int32), zero_blocks, n_rows, sb_e, sb_blk0, sb_nblk, tail


def _kmajor(pos, tm):
    m = pos.shape[0]
    return pos.reshape(m // tm, tm, TOP_K).transpose(0, 2, 1).reshape(m // tm, 1, TOP_K * tm)


def _repack_w_in(w_in):
    a = DN_CONV_CH + DN_VW
    b = a + 2 * DN_HEADS
    c = b + SW_QW
    e = c + 2 * SW_KVW
    parts = [w_in[:, :a], w_in[:, b:c], w_in[:, e:], w_in[:, c:e], w_in[:, a:b]]
    pad = jnp.zeros((D_MODEL, PROJ_W - w_in.shape[1]), BF16)
    return jnp.concatenate([p.astype(BF16) for p in parts] + [pad], axis=1)


def _lane_vec(v, offset):
    return jnp.zeros((1, LANE), F32).at[0, offset:offset + v.shape[0]].set(v.astype(F32))


def kernel(x_prompt, x_sample, state_conv, state_delta, cache_swa_k, cache_swa_v, c_prompt, c_sample, w_ada, b_ada, ln1_w, w_in, conv_w, dn_a_log, dn_dt_bias, dn_norm_w, sw_q_norm_w, sw_k_norm_w, sw_sinks, w_branch_a, w_branch_b, w_out, ln2_w, router_w, router_b, w_gate_up, b_gate_up, w_down, b_down):
    assert w_ada.shape[0] == 1, "single-layer step"
    bp, t, d = x_prompt.shape
    bs = x_sample.shape[0]
    np_tok = bp * t
    l = 0

    n_c = bp + bs
    c_all = jnp.concatenate([c_prompt, c_sample, jnp.zeros((-n_c % 8, d), F32)], axis=0)
    mod = _ada_mod(c_all, w_ada[l], b_ada[l])
    mods_p = [m.reshape(bp, 1, d) for m in jnp.split(mod[:bp], 6, axis=-1)]
    mods_s = [m.reshape(bs, 1, d) for m in jnp.split(mod[bp:n_c], 6, axis=-1)]

    w_in_r = _repack_w_in(w_in[l])
    wa, wb, wo = w_branch_a[l].astype(BF16), w_branch_b[l].astype(BF16), w_out[l].astype(BF16)
    alog_lane = _lane_vec(dn_a_log[l], DN_HEADS)
    dtb_lane = _lane_vec(dn_dt_bias[l], DN_HEADS)
    rw_pad = jnp.zeros((d, LANE), F32).at[:, :N_EXPERTS].set(router_w[l])
    rb_pad = jnp.zeros((1, LANE), F32).at[0, :N_EXPERTS].set(router_b[l])
    sinks = sw_sinks[l].astype(F32)

    xp = x_prompt.reshape(np_tok, d)
    proj_p = _in_proj(xp, ln1_w[l], mods_p[1], mods_p[0], w_in_r, t, 1024)
    proj3 = proj_p.reshape(bp, t, PROJ_W)
    gates = _gdn_gates(proj3, alog_lane, dtb_lane)
    ya_p, delta_p = _gdn_prompt(proj3, gates, conv_w[l], dn_norm_w[l])
    yb_p, kn_p = _swa_prompt(proj3, sinks, sw_q_norm_w[l], sw_k_norm_w[l])
    merged_p = _merge(ya_p.reshape(np_tok, DN_VW), yb_p.reshape(np_tok, SW_QW), wa, wb, proj_p, 512)
    x1_p = _out_proj(merged_p, wo, xp, mods_p[2], t, 1024)
    h2_p, idx_p, tw_p = _router(x1_p, ln2_w[l], mods_p[4], mods_p[3], rw_pad, rb_pad, t, 512)

    xs_ = x_sample.reshape(bs, d)
    proj_s = _in_proj(xs_, ln1_w[l], mods_s[1], mods_s[0], w_in_r, 1, bs)
    ya_s, conv_s, delta_s = _gdn_step(proj_s, state_conv[l], state_delta[l], conv_w[l], alog_lane, dtb_lane, dn_norm_w[l])
    w_buf = cache_swa_k.shape[2]
    yb_s, k_s, v_s = _swa_step(proj_s, cache_swa_k[l].reshape(bs, w_buf, SW_KVW), cache_swa_v[l].reshape(bs, w_buf, SW_KVW),
                               sinks, sw_q_norm_w[l], sw_k_norm_w[l])
    merged_s = _merge(ya_s.reshape(bs, DN_VW), yb_s.reshape(bs, SW_QW), wa, wb, proj_s, bs)
    x1_s = _out_proj(merged_s, wo, xs_, mods_s[2], 1, bs)
    h2_s, idx_s, tw_s = _router(x1_s, ln2_w[l], mods_s[4], mods_s[3], rw_pad, rb_pad, 1, bs)

    top_idx = jnp.concatenate([idx_p[:, :TOP_K], idx_s[:, :TOP_K]], axis=0)
    dest, zero_blocks, n_rows, sb_e, sb_blk0, sb_nblk, tail = _routing_tables(top_idx)
    xs_sorted = _scatter_rows(h2_p, h2_s, dest, zero_blocks, n_rows)
    ys = _experts(xs_sorted, sb_e, sb_blk0, sb_nblk, tail, w_gate_up[l], b_gate_up[l], w_down[l], b_down[l])
    pos = dest.reshape(np_tok + bs, TOP_K)
    y_p = _combine(ys, _kmajor(pos[:np_tok], COMBINE_TOK), x1_p, mods_p[5], tw_p, t)
    pad_s = COMBINE_TOK - bs
    pos_s = jnp.concatenate([pos[np_tok:], jnp.zeros((pad_s, TOP_K), jnp.int32)], axis=0)
    x1_s_pad = jnp.concatenate([x1_s, jnp.zeros((pad_s, d), F32)], axis=0)
    gt2_s_pad = jnp.concatenate([mods_s[5].reshape(bs, d), jnp.zeros((pad_s, d), F32)], axis=0)
    tw_s_pad = jnp.concatenate([tw_s, jnp.zeros((pad_s, LANE), F32)], axis=0)
    y_s = _combine(ys, _kmajor(pos_s, COMBINE_TOK), x1_s_pad, gt2_s_pad.reshape(COMBINE_TOK, 1, d), tw_s_pad, 1)[:bs]

    conv_p = proj3[:, t - (DN_CONV - 1):, C_QKV:C_QKV + DN_CONV_CH]
    kp_out = kn_p[:, t - WINDOW:].reshape(bp, WINDOW, SW_KV_HEADS, SW_HD)
    vp_out = proj3[:, t - WINDOW:, C_SV:C_SV + SW_KVW].reshape(bp, WINDOW, SW_KV_HEADS, SW_HD)
    return (
        y_p.reshape(bp, t, d),
        y_s.reshape(bs, 1, d),
        conv_p[None],
        conv_s[None],
        delta_p[None],
        delta_s[None],
        kp_out[None],
        k_s.reshape(bs, w_buf, SW_KV_HEADS, SW_HD)[None],
        vp_out[None],
        v_s.reshape(bs, w_buf, SW_KV_HEADS, SW_HD)[None],
    )
```

```python
import functools

import jax
import jax.numpy as jnp
import numpy as np
from jax import lax
from jax.experimental import pallas as pl
from jax.experimental.pallas import tpu as pltpu

F32 = jnp.float32
BF16 = jnp.bfloat16

D_MODEL = 2048
PAST_LEN = 16384
DN_HEADS = 8
DN_DK = 128
DN_DV = 128
DN_CONV = 4
SW_HEADS = 16
SW_KV_HEADS = 2
SW_HD = 64
SW_GROUP = SW_HEADS // SW_KV_HEADS
WINDOW = 128
N_EXPERTS = 32
TOP_K = 4
SWIGLU_ALPHA = 1.702
SWIGLU_LIMIT = 7.0
EPS = 1e-6

DN_QK = DN_HEADS * DN_DK
DN_VW = DN_HEADS * DN_DV
DN_CONV_CH = 2 * DN_QK + DN_VW
SW_QW = SW_HEADS * SW_HD
SW_KVW = SW_KV_HEADS * SW_HD

LANE = 128
C_QKV = 0
C_Z = DN_CONV_CH
C_SQ = C_Z + DN_VW
C_GA = C_SQ + SW_QW
C_GB = C_GA + D_MODEL
C_SK = C_GB + D_MODEL
C_SV = C_SK + SW_KVW
C_BA = C_SV + SW_KVW
PROJ_W = 10240

GDN_GROUP = 256
GDN_CHUNK = 256
GDN_LEVELS = 8
MOE_ROWS = 128
MOE_SB_BLOCKS = 12
MOE_PASS_BLOCKS = 4
MOE_TF = 512
VMEM_LIMIT = 56 * 1024 * 1024


def _cp(sem, vmem=VMEM_LIMIT):
    return pltpu.CompilerParams(dimension_semantics=sem, vmem_limit_bytes=vmem)


def _dot(a, b):
    return jnp.dot(a.astype(BF16), b.astype(BF16), preferred_element_type=F32)


def _dot_nt(a, b):
    return lax.dot_general(a.astype(BF16), b.astype(BF16), (((1,), (1,)), ((), ())), preferred_element_type=F32)


def _split(a):
    hi = a.astype(BF16)
    lo = (a - hi.astype(F32)).astype(BF16)
    return hi, lo


def _dot3(a, b):
    ah, al = _split(a)
    bh, bl = _split(b)
    d = functools.partial(jnp.dot, preferred_element_type=F32)
    return d(ah, bh) + (d(ah, bl) + d(al, bh))


def _dot3_nt(a, b):
    ah, al = _split(a)
    bh, bl = _split(b)
    d = functools.partial(lax.dot_general, dimension_numbers=(((1,), (1,)), ((), ())), preferred_element_type=F32)
    return d(ah, bh) + (d(ah, bl) + d(al, bh))


def _dot_exact_lhs01(m01, b):
    b1 = b.astype(BF16)
    r = b - b1.astype(F32)
    b2 = r.astype(BF16)
    b3 = (r - b2.astype(F32)).astype(BF16)
    d = functools.partial(jnp.dot, preferred_element_type=F32)
    m = m01.astype(BF16)
    return d(m, b1) + (d(m, b2) + d(m, b3))


def _sigmoid(x):
    return 1.0 / (1.0 + jnp.exp(-x))


def _silu(x):
    return x * _sigmoid(x)


def _softplus(x):
    return jnp.maximum(x, 0.0) + jnp.log(1.0 + jnp.exp(-jnp.abs(x)))


def _ada_kernel(c_ref, w_ref, b_ref, o_ref):
    o_ref[...] = _dot(_silu(c_ref[...]), w_ref[...]) + b_ref[...]


def _ada_mod(c_all, w_ada, b_ada):
    m = c_all.shape[0]
    n = w_ada.shape[1]
    tn = 1024
    return pl.pallas_call(
        _ada_kernel,
        grid=(n // tn,),
        in_specs=[
            pl.BlockSpec((m, D_MODEL), lambda j: (0, 0)),
            pl.BlockSpec((D_MODEL, tn), lambda j: (0, j)),
            pl.BlockSpec((1, tn), lambda j: (0, j)),
        ],
        out_specs=pl.BlockSpec((m, tn), lambda j: (0, j)),
        out_shape=jax.ShapeDtypeStruct((m, n), F32),
        compiler_params=_cp(("arbitrary",)),
        name="ada_mod",
    )(c_all, w_ada, b_ada.reshape(1, n))


def _norm_mod(x, lnw, sc, sh):
    y = x * lax.rsqrt(jnp.mean(x * x, axis=-1, keepdims=True) + EPS)
    return (y * lnw) * (1.0 + sc) + sh


def _inproj_kernel(x_ref, lnw_ref, sc_ref, sh_ref, w_ref, o_ref, h_scr):
    @pl.when(pl.program_id(1) == 0)
    def _():
        h_scr[...] = _norm_mod(x_ref[...], lnw_ref[...], sc_ref[...], sh_ref[...]).astype(BF16)

    o_ref[...] = jnp.dot(h_scr[...], w_ref[...], preferred_element_type=F32)


def _in_proj(x2d, lnw, sc, sh, w_bf16, rows_per_mod, tm):
    m = x2d.shape[0]
    tn = 1024
    if rows_per_mod == 1:
        mod_spec = pl.BlockSpec((tm, D_MODEL), lambda i, j: (i, 0))
        sc, sh = sc.reshape(m, D_MODEL), sh.reshape(m, D_MODEL)
    else:
        assert rows_per_mod % tm == 0
        mod_spec = pl.BlockSpec((None, 1, D_MODEL), lambda i, j: (i // (rows_per_mod // tm), 0, 0))
    return pl.pallas_call(
        _inproj_kernel,
        grid=(m // tm, PROJ_W // tn),
        in_specs=[
            pl.BlockSpec((tm, D_MODEL), lambda i, j: (i, 0)),
            pl.BlockSpec((1, D_MODEL), lambda i, j: (0, 0)),
            mod_spec,
            mod_spec,
            pl.BlockSpec((D_MODEL, tn), lambda i, j: (0, j)),
        ],
        out_specs=pl.BlockSpec((tm, tn), lambda i, j: (i, j)),
        out_shape=jax.ShapeDtypeStruct((m, PROJ_W), F32),
        scratch_shapes=[pltpu.VMEM((tm, D_MODEL), BF16)],
        compiler_params=_cp(("arbitrary", "arbitrary")),
        name="in_proj",
    )(x2d, lnw.reshape(1, D_MODEL), sc, sh, w_bf16)


def _tri_masks(n, chunk):
    r = lax.broadcasted_iota(jnp.int32, (n, n), 0)
    c = lax.broadcasted_iota(jnp.int32, (n, n), 1)
    same = (r // chunk) == (c // chunk)
    return same, same & (r >= c), same & (r > c)


def _gates_kernel(ba_ref, alog_ref, dtb_ref, beta_ref, gc_ref, eg_ref, ek_ref, el_ref, gcrow_ref):
    same, causal, _ = _tri_masks(GDN_GROUP, GDN_CHUNK)
    lower01 = jnp.where(causal, 1.0, 0.0)
    ones01 = jnp.where(same, 1.0, 0.0)
    nega = -jnp.exp(alog_ref[...])
    dtb = dtb_ref[...]
    t = ba_ref.shape[0]

    def body(i, carry):
        r0 = pl.multiple_of(i * GDN_GROUP, GDN_GROUP)
        x = ba_ref[pl.ds(r0, GDN_GROUP), :]
        g = nega * _softplus(x + dtb)
        gc = _dot_exact_lhs01(lower01, g)
        gl = _dot_exact_lhs01(ones01, g)
        beta_ref[pl.ds(r0, GDN_GROUP), :] = _sigmoid(x)
        gc_ref[pl.ds(r0, GDN_GROUP), :] = gc
        eg_ref[pl.ds(r0, GDN_GROUP), :] = jnp.exp(gc)
        ek_ref[pl.ds(r0, GDN_GROUP), :] = jnp.exp(gl - gc)
        el_ref[pl.ds(r0, GDN_GROUP), :] = jnp.exp(gl)
        gct = gc.T
        for h in range(DN_HEADS):
            gcrow_ref[h, :, pl.ds(r0, GDN_GROUP)] = gct[DN_HEADS + h:DN_HEADS + h + 1, :]
        return carry

    lax.fori_loop(0, t // GDN_GROUP, body, 0)


def _gdn_gates(proj3, alog_lane, dtb_lane):
    b, t, _ = proj3.shape
    col = pl.BlockSpec((None, t, LANE), lambda i: (i, 0, 0))
    shp = jax.ShapeDtypeStruct((b, t, LANE), F32)
    return pl.pallas_call(
        _gates_kernel,
        grid=(b,),
        in_specs=[
            pl.BlockSpec((None, t, LANE), lambda i: (i, 0, C_BA // LANE)),
            pl.BlockSpec((1, LANE), lambda i: (0, 0)),
            pl.BlockSpec((1, LANE), lambda i: (0, 0)),
        ],
        out_specs=[col, col, col, col, col, pl.BlockSpec((None, DN_HEADS, 1, t), lambda i: (i, 0, 0, 0))],
        out_shape=[shp, shp, shp, shp, shp, jax.ShapeDtypeStruct((b, DN_HEADS, 1, t), F32)],
        compiler_params=_cp(("arbitrary",)),
        name="gdn_gates",
    )(proj3, alog_lane, dtb_lane)


def _l2norm(x):
    return x * lax.rsqrt(jnp.sum(x * x, axis=-1, keepdims=True) + EPS)


GDN_HPS = 4
GDN_TILE = 1024


def _gdn_kernel(q_ref, k_ref, v_ref, z_ref, beta_ref, gc_ref, eg_ref, ek_ref, el_ref, gcrow_ref,
                cwq_ref, cwk_ref, cwv_ref, nw_ref, o_ref, s_ref,
                pad_scr, hist_scr, qn_scr, kn_scr, vn_scr, oacc_scr, s_scr):
    t = q_ref.shape[0]
    wdt = GDN_HPS * LANE
    h0 = pl.program_id(1) * GDN_HPS
    pad = 8

    @pl.when(pl.program_id(2) == 0)
    def _():
        hist_scr[...] = jnp.zeros(hist_scr.shape, F32)
        s_scr[...] = jnp.zeros(s_scr.shape, F32)

    def conv_silu(stream, u_ref, cw_ref):
        pad_scr[0:pad, :] = hist_scr[stream]
        pad_scr[pad:pad + t, :] = u_ref[...]
        hist_scr[stream] = pad_scr[t:t + pad, :]
        y = cw_ref[DN_CONV - 1:DN_CONV, :] * pad_scr[pad:pad + t, :]
        for i in range(DN_CONV - 1):
            off = pad - (DN_CONV - 1) + i
            y = y + cw_ref[i:i + 1, :] * pad_scr[off:off + t, :]
        return _silu(y)

    yq = conv_silu(0, q_ref, cwq_ref)
    for hh in range(GDN_HPS):
        sl = slice(hh * LANE, (hh + 1) * LANE)
        qn_scr[:, sl] = _l2norm(yq[:, sl]) * (DN_DK ** -0.5)
    yk = conv_silu(1, k_ref, cwk_ref)
    for hh in range(GDN_HPS):
        sl = slice(hh * LANE, (hh + 1) * LANE)
        kn_scr[:, sl] = _l2norm(yk[:, sl])
    vn_scr[...] = conv_silu(2, v_ref, cwv_ref)

    n = GDN_GROUP
    c = GDN_CHUNK
    _, causal, strict = _tri_masks(n, c)
    rr = lax.broadcasted_iota(jnp.int32, (n, n), 0)
    cc = lax.broadcasted_iota(jnp.int32, (n, n), 1)
    eye = jnp.where(rr == cc, 1.0, 0.0)
    lane = lax.broadcasted_iota(jnp.int32, (n, LANE), 1)

    def pick(ref, r0, sel):
        return jnp.sum(jnp.where(sel, ref[pl.ds(r0, n), :], 0.0), axis=-1, keepdims=True)

    def head_group(hh, r0):
        sl = slice(hh * LANE, (hh + 1) * LANE)
        sel_b = lane == h0 + hh
        sel_g = lane == h0 + hh + DN_HEADS
        q = qn_scr[pl.ds(r0, n), sl]
        k = kn_scr[pl.ds(r0, n), sl]
        v = vn_scr[pl.ds(r0, n), sl]
        beta = pick(beta_ref, r0, sel_b)
        gc = pick(gc_ref, r0, sel_g)
        eg = pick(eg_ref, r0, sel_g)
        ek = pick(ek_ref, r0, sel_g)
        el = pick(el_ref, r0, sel_g)
        gcrow = gcrow_ref[hh, :, pl.ds(r0, n)]
        decay = jnp.where(causal, jnp.exp(gc - gcrow), 0.0)
        a_low = jnp.where(strict, beta * _dot_nt(k, k) * decay, 0.0)
        pw = [-a_low]
        for _ in range(GDN_LEVELS - 1):
            pw.append(_dot(pw[-1], pw[-1]))
        fs = [eye + pw[i] + pw[i + 1] + _dot(pw[i], pw[i + 1]) for i in range(0, GDN_LEVELS, 2)]
        while len(fs) > 1:
            fs = [_dot(fs[i], fs[i + 1]) for i in range(0, len(fs), 2)]
        rhs = jnp.concatenate([v * beta, k * (beta * eg)], axis=1)
        sol = _dot(fs[0], rhs)
        value = sol[:, :DN_DV]
        kcum = sol[:, DN_DV:]
        intra = _dot_nt(q, k) * decay
        q_dec = q * eg
        k_dec = k * ek
        for j in range(n // c):
            lo, hi = j * c, (j + 1) * c
            s = s_scr[hh]
            r = _dot(jnp.concatenate([kcum[lo:hi], q_dec[lo:hi]], axis=0), s)
            v_new = value[lo:hi] - r[:c]
            parts = []
            if lo:
                parts.append(jnp.zeros((lo, DN_DV), F32))
            parts.append(v_new)
            if hi < n:
                parts.append(jnp.zeros((n - hi, DN_DV), F32))
            o = r[c:] + _dot(intra[lo:hi], jnp.concatenate(parts, axis=0))
            oacc_scr[pl.ds(r0 + lo, c), sl] = o
            s_scr[hh] = s * el[lo:lo + 1] + _dot(k_dec[lo:hi].T, v_new)

    def body(i, carry):
        r0 = pl.multiple_of(i * n, n)
        for hh in range(GDN_HPS):
            head_group(hh, r0)
        return carry

    lax.fori_loop(0, t // n, body, 0)
    for hh in range(GDN_HPS):
        sl = slice(hh * LANE, (hh + 1) * LANE)
        o = oacc_scr[:, sl]
        y = o * lax.rsqrt(jnp.mean(o * o, axis=-1, keepdims=True) + EPS)
        o_ref[:, sl] = (y * nw_ref[...]) * _silu(z_ref[:, sl])
    s_ref[...] = s_scr[...]


def _gdn_prompt(proj3, gates, conv_w, norm_w):
    b, t, _ = proj3.shape
    beta, gc, eg, ek, el, gcrow = gates
    hps = GDN_HPS
    wdt = hps * LANE
    steps = DN_HEADS // hps
    tt = GDN_TILE
    assert t % tt == 0 and tt % GDN_GROUP == 0

    def colspec(base):
        return pl.BlockSpec((None, tt, wdt), lambda i, j, r, base=base: (i, r, base + j))

    gate = pl.BlockSpec((None, tt, LANE), lambda i, j, r: (i, r, 0))

    def cwspec(base):
        return pl.BlockSpec((DN_CONV, wdt), lambda i, j, r, base=base: (0, base + j))

    return pl.pallas_call(
        _gdn_kernel,
        grid=(b, steps, t // tt),
        in_specs=[
            colspec(0), colspec(steps), colspec(2 * steps), colspec(C_Z // wdt),
            gate, gate, gate, gate, gate,
            pl.BlockSpec((None, hps, 1, tt), lambda i, j, r: (i, j, 0, r)),
            cwspec(0), cwspec(steps), cwspec(2 * steps),
            pl.BlockSpec((1, DN_DV), lambda i, j, r: (0, 0)),
        ],
        out_specs=[
            pl.BlockSpec((None, tt, wdt), lambda i, j, r: (i, r, j)),
            pl.BlockSpec((None, hps, DN_DK, DN_DV), lambda i, j, r: (i, j, 0, 0)),
        ],
        out_shape=[
            jax.ShapeDtypeStruct((b, t, DN_VW), F32),
            jax.ShapeDtypeStruct((b, DN_HEADS, DN_DK, DN_DV), F32),
        ],
        scratch_shapes=[
            pltpu.VMEM((tt + 8, wdt), F32),
            pltpu.VMEM((3, 8, wdt), F32),
            pltpu.VMEM((tt, wdt), F32),
            pltpu.VMEM((tt, wdt), F32),
            pltpu.VMEM((tt, wdt), F32),
            pltpu.VMEM((tt, wdt), F32),
            pltpu.VMEM((hps, DN_DK, DN_DV), F32),
        ],
        compiler_params=_cp(("arbitrary", "arbitrary", "arbitrary")),
        name="gdn_prompt",
    )(proj3, proj3, proj3, proj3, beta, gc, eg, ek, el, gcrow, conv_w, conv_w, conv_w, norm_w.reshape(1, DN_DV))


def _gdn_step_kernel(p_ref, cprev_ref, s_ref, cw_ref, alog_ref, dtb_ref, nw_ref, o_ref, cnew_ref, snew_ref):
    u = p_ref[:, C_QKV:C_QKV + DN_CONV_CH]
    prev = cprev_ref[...]
    y = cw_ref[DN_CONV - 1:DN_CONV, :] * u
    for i in range(DN_CONV - 1):
        y = y + cw_ref[i:i + 1, :] * prev[i:i + 1, :]
    y = _silu(y)
    cnew_ref[0:DN_CONV - 2, :] = prev[1:DN_CONV - 1, :]
    cnew_ref[DN_CONV - 2:DN_CONV - 1, :] = u
    ba = p_ref[:, C_BA:C_BA + LANE]
    beta_l = _sigmoid(ba)
    a_l = jnp.exp(-jnp.exp(alog_ref[...]) * _softplus(ba + dtb_ref[...]))
    lane = lax.broadcasted_iota(jnp.int32, (1, LANE), 1)
    row8 = lax.broadcasted_iota(jnp.int32, (8, LANE), 0)
    for h in range(DN_HEADS):
        q = _l2norm(y[:, h * DN_DK:(h + 1) * DN_DK]) * (DN_DK ** -0.5)
        k = _l2norm(y[:, DN_QK + h * DN_DK:DN_QK + (h + 1) * DN_DK])
        v = y[:, 2 * DN_QK + h * DN_DV:2 * DN_QK + (h + 1) * DN_DV]
        beta = jnp.sum(jnp.where(lane == h, beta_l, 0.0), axis=-1, keepdims=True)
        a = jnp.sum(jnp.where(lane == h + DN_HEADS, a_l, 0.0), axis=-1, keepdims=True)
        s = s_ref[h]
        kq = jnp.where(row8 == 0, k, jnp.where(row8 == 1, q, 0.0))
        r = _dot3(kq, s)
        v_new = beta * (v - a * r[0:1])
        o = a * r[1:2] + jnp.sum(q * k, axis=-1, keepdims=True) * v_new
        k8 = jnp.where(row8 == 0, k, 0.0)
        v8 = jnp.where(row8 == 0, v_new, 0.0)
        snew_ref[h] = s * a + _dot3(k8.T, v8)
        yo = o * lax.rsqrt(jnp.mean(o * o, axis=-1, keepdims=True) + EPS)
        z = p_ref[:, C_Z + h * DN_DV:C_Z + (h + 1) * DN_DV]
        o_ref[:, h * DN_DV:(h + 1) * DN_DV] = (yo * nw_ref[...]) * _silu(z)


def _gdn_step(proj_s, conv_prev, s0, conv_w, alog_lane, dtb_lane, norm_w):
    b = proj_s.shape[0]
    return pl.pallas_call(
        _gdn_step_kernel,
        grid=(b,),
        in_specs=[
            pl.BlockSpec((None, 1, PROJ_W), lambda i: (i, 0, 0)),
            pl.BlockSpec((None, DN_CONV - 1, DN_CONV_CH), lambda i: (i, 0, 0)),
            pl.BlockSpec((None, DN_HEADS, DN_DK, DN_DV), lambda i: (i, 0, 0, 0)),
            pl.BlockSpec((DN_CONV, DN_CONV_CH), lambda i: (0, 0)),
            pl.BlockSpec((1, LANE), lambda i: (0, 0)),
            pl.BlockSpec((1, LANE), lambda i: (0, 0)),
            pl.BlockSpec((1, DN_DV), lambda i: (0, 0)),
        ],
        out_specs=[
            pl.BlockSpec((None, 1, DN_VW), lambda i: (i, 0, 0)),
            pl.BlockSpec((None, DN_CONV - 1, DN_CONV_CH), lambda i: (i, 0, 0)),
            pl.BlockSpec((None, DN_HEADS, DN_DK, DN_DV), lambda i: (i, 0, 0, 0)),
        ],
        out_shape=[
            jax.ShapeDtypeStruct((b, 1, DN_VW), F32),
            jax.ShapeDtypeStruct((b, DN_CONV - 1, DN_CONV_CH), F32),
            jax.ShapeDtypeStruct((b, DN_HEADS, DN_DK, DN_DV), F32),
        ],
        compiler_params=_cp(("arbitrary",)),
        name="gdn_step",
    )(proj_s.reshape(b, 1, PROJ_W), conv_prev, s0, conv_w, alog_lane, dtb_lane, norm_w.reshape(1, DN_DV))


def _alibi_slope(h):
    return float(2.0 ** (-8.0 * (h + 1) / SW_HEADS))


def _head_rms(x, w):
    return (x * lax.rsqrt(jnp.mean(x * x, axis=-1, keepdims=True) + EPS)) * w


SWA_HB = 4


def _swa_kernel(sinks_ref, q_ref, kc_ref, kp_ref, vc_ref, vp_ref, qw_ref, kw_ref, o_ref, kn_ref):
    blk = pl.program_id(1)
    w = WINDOW
    rows = SWA_HB * w
    qi = lax.broadcasted_iota(jnp.int32, (rows, 2 * w), 0)
    kj = lax.broadcasted_iota(jnp.int32, (rows, 2 * w), 1)
    dist = (qi & (w - 1)) + w - kj
    valid = (dist >= 0) & (dist < w) & ((kj >= w) | (blk > 0))
    distf = dist.astype(F32)
    stripe = lax.broadcasted_iota(jnp.int32, (rows, 1), 0) // w
    kc = kc_ref[...]
    kp = kp_ref[...]
    kbands, vbands = [], []
    for g in range(SW_KV_HEADS):
        sl = slice(g * SW_HD, (g + 1) * SW_HD)
        kcn = _head_rms(kc[:, sl], kw_ref[...])
        kn_ref[:, sl] = kcn
        kbands.append(jnp.concatenate([_head_rms(kp[:, sl], kw_ref[...]), kcn], axis=0))
        vbands.append(jnp.concatenate([vp_ref[:, sl], vc_ref[:, sl]], axis=0))
    for hb in range(SW_HEADS // SWA_HB):
        heads = range(hb * SWA_HB, (hb + 1) * SWA_HB)
        g = heads[0] // SW_GROUP
        qs = jnp.concatenate([_head_rms(q_ref[:, h * SW_HD:(h + 1) * SW_HD], qw_ref[...]) for h in heads], axis=0)
        slope = jnp.zeros((rows, 1), F32)
        sink = jnp.zeros((rows, 1), F32)
        for i, h in enumerate(heads):
            slope = jnp.where(stripe == i, _alibi_slope(h), slope)
            sink = jnp.where(stripe == i, sinks_ref[h], sink)
        s = _dot_nt(qs, kbands[g]) * (SW_HD ** -0.5) - slope * distf
        s = jnp.where(valid, s, -jnp.inf)
        m = jnp.maximum(jnp.max(s, axis=-1, keepdims=True), sink)
        p = jnp.exp(s - m)
        den = jnp.sum(p, axis=-1, keepdims=True) + jnp.exp(sink - m)
        o = _dot(p / den, vbands[g])
        for i, h in enumerate(heads):
            o_ref[:, h * SW_HD:(h + 1) * SW_HD] = o[i * w:(i + 1) * w]


def _swa_prompt(proj3, sinks, qw, kw):
    b, t, _ = proj3.shape
    nb = t // WINDOW
    kcol, vcol = C_SK // LANE, C_SV // LANE

    def cur(col):
        return pl.BlockSpec((None, WINDOW, SW_KVW), lambda i, j, s, col=col: (i, j, col))

    def prev(col):
        return pl.BlockSpec((None, WINDOW, SW_KVW), lambda i, j, s, col=col: (i, jnp.maximum(j - 1, 0), col))

    return pl.pallas_call(
        _swa_kernel,
        grid_spec=pltpu.PrefetchScalarGridSpec(
            num_scalar_prefetch=1,
            grid=(b, nb),
            in_specs=[
                pl.BlockSpec((None, WINDOW, SW_QW), lambda i, j, s: (i, j, C_SQ // SW_QW)),
                cur(kcol), prev(kcol), cur(vcol), prev(vcol),
                pl.BlockSpec((1, SW_HD), lambda i, j, s: (0, 0)),
                pl.BlockSpec((1, SW_HD), lambda i, j, s: (0, 0)),
            ],
            out_specs=[
                pl.BlockSpec((None, WINDOW, SW_QW), lambda i, j, s: (i, j, 0)),
                pl.BlockSpec((None, WINDOW, SW_KVW), lambda i, j, s: (i, j, 0)),
            ],
        ),
        out_shape=[
            jax.ShapeDtypeStruct((b, t, SW_QW), F32),
            jax.ShapeDtypeStruct((b, t, SW_KVW), F32),
        ],
        compiler_params=_cp(("arbitrary", "arbitrary")),
        name="swa_prompt",
    )(sinks, proj3, proj3, proj3, proj3, proj3, qw.reshape(1, SW_HD), kw.reshape(1, SW_HD))


def _swa_step_kernel(sinks_ref, p_ref, kbuf_ref, vbuf_ref, qw_ref, kw_ref, o_ref, knew_ref, vnew_ref, kcat, vcat):
    w = kbuf_ref.shape[0]
    rows = kcat.shape[0]
    knew = p_ref[:, C_SK:C_SK + SW_KVW]
    vnew = p_ref[:, C_SV:C_SV + SW_KVW]
    kcat[...] = jnp.zeros(kcat.shape, F32)
    vcat[...] = jnp.zeros(vcat.shape, F32)
    kcat[0:w, :] = kbuf_ref[...]
    vcat[0:w, :] = vbuf_ref[...]
    for g in range(SW_KV_HEADS):
        sl = slice(g * SW_HD, (g + 1) * SW_HD)
        kcat[w:w + 1, sl] = _head_rms(knew[:, sl], kw_ref[...])
    vcat[w:w + 1, :] = vnew
    knew_ref[...] = kcat[1:w + 1, :]
    vnew_ref[...] = vcat[1:w + 1, :]
    j = lax.broadcasted_iota(jnp.int32, (rows, 1), 0)
    dist = w - j
    valid = (dist >= 0) & (dist < WINDOW)
    distf = dist.astype(F32)
    for h in range(SW_HEADS):
        g = h // SW_GROUP
        sl = slice(g * SW_HD, (g + 1) * SW_HD)
        qh = _head_rms(p_ref[:, C_SQ + h * SW_HD:C_SQ + (h + 1) * SW_HD], qw_ref[...])
        s = jnp.sum(kcat[:, sl] * qh, axis=-1, keepdims=True) * (SW_HD ** -0.5) - _alibi_slope(h) * distf
        s = jnp.where(valid, s, -jnp.inf)
        sink = sinks_ref[h]
        m = jnp.maximum(jnp.max(s, axis=0, keepdims=True), sink)
        p = jnp.exp(s - m)
        den = jnp.sum(p, axis=0, keepdims=True) + jnp.exp(sink - m)
        o_ref[:, h * SW_HD:(h + 1) * SW_HD] = jnp.sum((p / den) * vcat[:, sl], axis=0, keepdims=True)


def _swa_step(proj_s, kbuf, vbuf, sinks, qw, kw):
    b = proj_s.shape[0]
    w = kbuf.shape[1]
    rows = w + 8
    buf = pl.BlockSpec((None, w, SW_KVW), lambda i, s: (i, 0, 0))
    return pl.pallas_call(
        _swa_step_kernel,
        grid_spec=pltpu.PrefetchScalarGridSpec(
            num_scalar_prefetch=1,
            grid=(b,),
            in_specs=[
                pl.BlockSpec((None, 1, PROJ_W), lambda i, s: (i, 0, 0)),
                buf, buf,
                pl.BlockSpec((1, SW_HD), lambda i, s: (0, 0)),
                pl.BlockSpec((1, SW_HD), lambda i, s: (0, 0)),
            ],
            out_specs=[pl.BlockSpec((None, 1, SW_QW), lambda i, s: (i, 0, 0)), buf, buf],
            scratch_shapes=[pltpu.VMEM((rows, SW_KVW), F32), pltpu.VMEM((rows, SW_KVW), F32)],
        ),
        out_shape=[
            jax.ShapeDtypeStruct((b, 1, SW_QW), F32),
            jax.ShapeDtypeStruct((b, w, SW_KVW), F32),
            jax.ShapeDtypeStruct((b, w, SW_KVW), F32),
        ],
        compiler_params=_cp(("arbitrary",)),
        name="swa_step",
    )(sinks, proj_s.reshape(b, 1, PROJ_W), kbuf, vbuf, qw.reshape(1, SW_HD), kw.reshape(1, SW_HD))


def _merge_kernel(ya_ref, yb_ref, wa_ref, wb_ref, ga_ref, gb_ref, o_ref):
    a = _dot(ya_ref[...], wa_ref[...])
    b = _dot(yb_ref[...], wb_ref[...])
    o_ref[...] = (_sigmoid(ga_ref[...]) * a + _sigmoid(gb_ref[...]) * b).astype(BF16)


def _merge(ya, yb, wa, wb, proj, tm):
    m = ya.shape[0]
    tn = 1024
    return pl.pallas_call(
        _merge_kernel,
        grid=(m // tm, D_MODEL // tn),
        in_specs=[
            pl.BlockSpec((tm, DN_VW), lambda i, j: (i, 0)),
            pl.BlockSpec((tm, SW_QW), lambda i, j: (i, 0)),
            pl.BlockSpec((DN_VW, tn), lambda i, j: (0, j)),
            pl.BlockSpec((SW_QW, tn), lambda i, j: (0, j)),
            pl.BlockSpec((tm, tn), lambda i, j: (i, C_GA // tn + j)),
            pl.BlockSpec((tm, tn), lambda i, j: (i, C_GB // tn + j)),
        ],
        out_specs=pl.BlockSpec((tm, tn), lambda i, j: (i, j)),
        out_shape=jax.ShapeDtypeStruct((m, D_MODEL), BF16),
        compiler_params=_cp(("arbitrary", "arbitrary")),
        name="merge",
    )(ya, yb, wa, wb, proj, proj)


def _outproj_kernel(mg_ref, w_ref, x_ref, gt_ref, o_ref):
    o_ref[...] = x_ref[...] + gt_ref[...] * jnp.dot(mg_ref[...], w_ref[...], preferred_element_type=F32)


def _mod_spec(rows_per_mod, tm, tn):
    if rows_per_mod == 1:
        return pl.BlockSpec((tm, tn), lambda i, j: (i, j))
    return pl.BlockSpec((None, 1, tn), lambda i, j: (i // (rows_per_mod // tm), 0, j))


def _out_proj(merged, w_bf16, x2d, gt, rows_per_mod, tm):
    m = x2d.shape[0]
    tn = 1024
    if rows_per_mod == 1:
        gt = gt.reshape(m, D_MODEL)
    return pl.pallas_call(
        _outproj_kernel,
        grid=(m // tm, D_MODEL // tn),
        in_specs=[
            pl.BlockSpec((tm, D_MODEL), lambda i, j: (i, 0)),
            pl.BlockSpec((D_MODEL, tn), lambda i, j: (0, j)),
            pl.BlockSpec((tm, tn), lambda i, j: (i, j)),
            _mod_spec(rows_per_mod, tm, tn),
        ],
        out_specs=pl.BlockSpec((tm, tn), lambda i, j: (i, j)),
        out_shape=jax.ShapeDtypeStruct((m, D_MODEL), F32),
        compiler_params=_cp(("arbitrary", "arbitrary")),
        name="out_proj",
    )(merged, w_bf16, x2d, gt)


def _router_kernel(x_ref, lnw_ref, sc_ref, sh_ref, rw_ref, rb_ref, h_ref, idx_ref, w_ref):
    hmod = _norm_mod(x_ref[...], lnw_ref[...], sc_ref[...], sh_ref[...])
    h_ref[...] = hmod
    logits = _dot3(hmod, rw_ref[...]) + rb_ref[...]
    lane = lax.broadcasted_iota(jnp.int32, logits.shape, 1)
    cur = jnp.where(lane < N_EXPERTS, logits, -jnp.inf)
    vals, idxs = [], []
    for _ in range(TOP_K):
        m = jnp.max(cur, axis=-1, keepdims=True)
        ix = jnp.min(jnp.where(cur == m, lane, LANE), axis=-1, keepdims=True)
        vals.append(m)
        idxs.append(ix)
        cur = jnp.where(lane == ix, -jnp.inf, cur)
    es = [jnp.exp(v - vals[0]) for v in vals]
    den = es[0] + es[1] + es[2] + es[3]
    idx_out = jnp.zeros(logits.shape, jnp.int32)
    w_out = jnp.zeros(logits.shape, F32)
    for k in range(TOP_K):
        idx_out = jnp.where(lane == k, idxs[k], idx_out)
        w_out = jnp.where(lane == k, es[k] / den, w_out)
    idx_ref[...] = idx_out
    w_ref[...] = w_out


def _router(x2d, lnw, sc, sh, rw_pad, rb_pad, rows_per_mod, tm):
    m = x2d.shape[0]
    if rows_per_mod == 1:
        mod_spec = pl.BlockSpec((tm, D_MODEL), lambda i: (i, 0))
        sc, sh = sc.reshape(m, D_MODEL), sh.reshape(m, D_MODEL)
    else:
        mod_spec = pl.BlockSpec((None, 1, D_MODEL), lambda i: (i // (rows_per_mod // tm), 0, 0))
    row = pl.BlockSpec((tm, D_MODEL), lambda i: (i, 0))
    small = pl.BlockSpec((tm, LANE), lambda i: (i, 0))
    return pl.pallas_call(
        _router_kernel,
        grid=(m // tm,),
        in_specs=[
            row,
            pl.BlockSpec((1, D_MODEL), lambda i: (0, 0)),
            mod_spec, mod_spec,
            pl.BlockSpec((D_MODEL, LANE), lambda i: (0, 0)),
            pl.BlockSpec((1, LANE), lambda i: (0, 0)),
        ],
        out_specs=[row, small, small],
        out_shape=[
            jax.ShapeDtypeStruct((m, D_MODEL), F32),
            jax.ShapeDtypeStruct((m, LANE), jnp.int32),
            jax.ShapeDtypeStruct((m, LANE), F32),
        ],
        compiler_params=_cp(("arbitrary",)),
        name="router",
    )(x2d, lnw.reshape(1, D_MODEL), sc, sh, rw_pad, rb_pad)


SCATTER_TOK = 256
DMA_UNROLL = 8


def _scatter_kernel(zl_ref, dest_ref, hp_ref, hs_ref, xs_hbm, zbuf, sem, zsem):
    i = pl.program_id(0)
    rb = MOE_ROWS
    n_prompt_steps = pl.num_programs(0) - 1

    @pl.when(i == 0)
    def _():
        zbuf[...] = jnp.zeros(zbuf.shape, F32)

        def zero_copy(n):
            return pltpu.make_async_copy(zbuf, xs_hbm.at[pl.ds(zl_ref[n] * rb, rb)], zsem)

        def start(n, c):
            @pl.when(zl_ref[n] >= 0)
            def _():
                zero_copy(n).start()
            return c

        def wait(n, c):
            @pl.when(zl_ref[n] >= 0)
            def _():
                zero_copy(n).wait()
            return c

        lax.fori_loop(0, zl_ref.shape[0], start, 0)
        lax.fori_loop(0, zl_ref.shape[0], wait, 0)

    def scatter(src_ref):
        n_tok = src_ref.shape[0]

        def start(a, c):
            t = lax.shift_right_logical(a, 2)
            pltpu.make_async_copy(src_ref.at[pl.ds(t, 1)], xs_hbm.at[pl.ds(dest_ref[0, a], 1)], sem).start()
            return c

        lax.fori_loop(0, n_tok * TOP_K, start, 0, unroll=DMA_UNROLL)
        for _ in range(TOP_K):
            pltpu.make_async_copy(src_ref, xs_hbm.at[pl.ds(0, n_tok)], sem).wait()

    @pl.when(i < n_prompt_steps)
    def _():
        scatter(hp_ref)

    @pl.when(i == n_prompt_steps)
    def _():
        scatter(hs_ref)


def _scatter_rows(h_p, h_s, dest, zero_blocks, n_rows):
    assert TOP_K == 4
    n_p, n_s = h_p.shape[0], h_s.shape[0]
    steps_p = n_p // SCATTER_TOK
    per = SCATTER_TOK * TOP_K
    dest_s = jnp.concatenate([dest[n_p * TOP_K:], jnp.zeros((per - n_s * TOP_K,), jnp.int32)])
    dest3 = jnp.concatenate([dest[:n_p * TOP_K], dest_s]).reshape(steps_p + 1, 1, per)
    return pl.pallas_call(
        _scatter_kernel,
        grid_spec=pltpu.PrefetchScalarGridSpec(
            num_scalar_prefetch=1,
            grid=(steps_p + 1,),
            in_specs=[
                pl.BlockSpec((None, 1, per), lambda i, zl: (i, 0, 0), memory_space=pltpu.SMEM),
                pl.BlockSpec((SCATTER_TOK, D_MODEL), lambda i, zl: (jnp.minimum(i, steps_p - 1), 0)),
                pl.BlockSpec((n_s, D_MODEL), lambda i, zl: (0, 0)),
            ],
            out_specs=pl.BlockSpec(memory_space=pl.ANY),
            scratch_shapes=[
                pltpu.VMEM((MOE_ROWS, D_MODEL), F32),
                pltpu.SemaphoreType.DMA(()),
                pltpu.SemaphoreType.DMA(()),
            ],
        ),
        out_shape=jax.ShapeDtypeStruct((n_rows, D_MODEL), F32),
        compiler_params=_cp(("arbitrary",)),
        name="moe_scatter",
    )(zero_blocks, dest3, h_p, h_s)


def _experts_kernel(sbe_ref, sbb_ref, sbn_ref, tail_ref, xs_hbm, wg_ref, wl_ref, wd_ref, bg_ref, bl_ref, bd_ref,
                    ys_hbm, xb_scr, acc_scr, wg_scr, wl_scr, wd_scr, sem_in, sem_out):
    s = pl.program_id(0)
    j = pl.program_id(1)
    last_j = pl.num_programs(1) - 1
    nblk = sbn_ref[s]
    blk0 = sbb_ref[s]
    rb = MOE_ROWS

    def in_copy(b):
        return pltpu.make_async_copy(xs_hbm.at[pl.ds((blk0 + b) * rb, rb)], acc_scr.at[pl.ds(b * rb, rb)], sem_in)

    def out_copy(first_blk, b):
        return pltpu.make_async_copy(acc_scr.at[pl.ds(b * rb, rb)], ys_hbm.at[pl.ds((first_blk + b) * rb, rb)], sem_out)

    def loop(n, fn):
        def body(b, c):
            fn(b)
            return c
        lax.fori_loop(0, n, body, 0)

    @pl.when(j == 0)
    def _():
        @pl.when(s > 0)
        def _():
            prev0 = sbb_ref[s - 1]
            loop(sbn_ref[s - 1], lambda b: out_copy(prev0, b).wait())

        loop(nblk, lambda b: in_copy(b).start())
        loop(nblk, lambda b: in_copy(b).wait())

        def cast(b):
            r0 = pl.multiple_of(b * rb, rb)
            xb_scr[pl.ds(r0, rb), :] = acc_scr[pl.ds(r0, rb), :].astype(BF16)
            acc_scr[pl.ds(r0, rb), :] = jnp.broadcast_to(bd_ref[...], (rb, D_MODEL))
        loop(nblk, cast)

    @pl.when(nblk > 0)
    def _():
        wg_scr[...] = wg_ref[...].astype(BF16)
        wl_scr[...] = wl_ref[...].astype(BF16)
        wd_scr[...] = wd_ref[...].astype(BF16)

        def mlp(r0, rows):
            x = xb_scr[pl.ds(r0, rows), :]
            glu = jnp.dot(x, wg_scr[...], preferred_element_type=F32) + bg_ref[...]
            lin = jnp.dot(x, wl_scr[...], preferred_element_type=F32) + bl_ref[...]
            glu = jnp.minimum(glu, SWIGLU_LIMIT)
            lin = jnp.clip(lin, -SWIGLU_LIMIT, SWIGLU_LIMIT)
            act = glu * _sigmoid(SWIGLU_ALPHA * glu) * (lin + 1.0)
            acc_scr[pl.ds(r0, rows), :] += jnp.dot(act.astype(BF16), wd_scr[...], preferred_element_type=F32)

        full = nblk // MOE_PASS_BLOCKS
        big = MOE_PASS_BLOCKS * rb
        loop(full, lambda p: mlp(pl.multiple_of(p * big, big), big))
        done = full * MOE_PASS_BLOCKS
        part = MOE_PASS_BLOCKS // 2
        while part >= 1:
            take = ((nblk - done) // part) * part

            @pl.when(take > 0)
            def _(done=done, part=part):
                mlp(pl.multiple_of(done * rb, rb), part * rb)

            done = done + take
            part //= 2

    @pl.when(j == last_j)
    def _():
        loop(nblk, lambda b: out_copy(blk0, b).start())

    @pl.when((s == pl.num_programs(0) - 1) & (j == last_j))
    def _():
        loop(nblk, lambda b: out_copy(blk0, b).wait())
        acc_scr[0:rb, :] = jnp.zeros((rb, D_MODEL), F32)

        def zero_copy(b):
            return pltpu.make_async_copy(acc_scr.at[pl.ds(0, rb)], ys_hbm.at[pl.ds(b * rb, rb)], sem_out)

        def start(b, c):
            zero_copy(b).start()
            return c

        def wait(b, c):
            zero_copy(b).wait()
            return c

        lax.fori_loop(tail_ref[0], tail_ref[1], start, 0)
        lax.fori_loop(tail_ref[0], tail_ref[1], wait, 0)


def _experts(xs, sb_e, sb_blk0, sb_nblk, tail, w_gate_up, b_gate_up, w_down, b_down):
    n_rows = xs.shape[0]
    n_sb = sb_e.shape[0]
    tf = MOE_TF
    nj = D_MODEL // tf
    rmax = MOE_SB_BLOCKS * MOE_ROWS

    def jj(s, j, n):
        return jnp.where(n[s] > 0, j, nj - 1)

    return pl.pallas_call(
        _experts_kernel,
        grid_spec=pltpu.PrefetchScalarGridSpec(
            num_scalar_prefetch=4,
            grid=(n_sb, nj),
            in_specs=[
                pl.BlockSpec(memory_space=pl.ANY),
                pl.BlockSpec((None, D_MODEL, tf), lambda s, j, e, b, n, tl: (e[s], 0, jj(s, j, n))),
                pl.BlockSpec((None, D_MODEL, tf), lambda s, j, e, b, n, tl: (e[s], 0, nj + jj(s, j, n))),
                pl.BlockSpec((None, tf, D_MODEL), lambda s, j, e, b, n, tl: (e[s], jj(s, j, n), 0)),
                pl.BlockSpec((None, 1, tf), lambda s, j, e, b, n, tl: (e[s], 0, jj(s, j, n))),
                pl.BlockSpec((None, 1, tf), lambda s, j, e, b, n, tl: (e[s], 0, nj + jj(s, j, n))),
                pl.BlockSpec((None, 1, D_MODEL), lambda s, j, e, b, n, tl: (e[s], 0, 0)),
            ],
            out_specs=pl.BlockSpec(memory_space=pl.ANY),
            scratch_shapes=[
                pltpu.VMEM((rmax, D_MODEL), BF16),
                pltpu.VMEM((rmax, D_MODEL), F32),
                pltpu.VMEM((D_MODEL, tf), BF16),
                pltpu.VMEM((D_MODEL, tf), BF16),
                pltpu.VMEM((tf, D_MODEL), BF16),
                pltpu.SemaphoreType.DMA(()),
                pltpu.SemaphoreType.DMA(()),
            ],
        ),
        out_shape=jax.ShapeDtypeStruct((n_rows, D_MODEL), F32),
        compiler_params=_cp(("arbitrary", "arbitrary")),
        name="moe_experts",
    )(sb_e, sb_blk0, sb_nblk, tail, xs, w_gate_up, w_gate_up, w_down,
      b_gate_up.reshape(N_EXPERTS, 1, 2 * D_MODEL), b_gate_up.reshape(N_EXPERTS, 1, 2 * D_MODEL),
      b_down.reshape(N_EXPERTS, 1, D_MODEL))


COMBINE_TOK = 128


def _combine_kernel(pos_ref, posn_ref, ys_hbm, x_ref, gt_ref, w_ref, o_ref, buf, sem):
    n = COMBINE_TOK * TOP_K
    i = pl.program_id(0)
    slot = lax.rem(i, 2)

    def issue(p_ref, sl):
        def start(a, c):
            pltpu.make_async_copy(ys_hbm.at[pl.ds(p_ref[0, a], 1)], buf.at[sl, pl.ds(a, 1)], sem.at[sl]).start()
            return c
        lax.fori_loop(0, n, start, 0, unroll=DMA_UNROLL)

    @pl.when(i == 0)
    def _():
        issue(pos_ref, 0)

    @pl.when(i + 1 < pl.num_programs(0))
    def _():
        issue(posn_ref, 1 - slot)

    pltpu.make_async_copy(ys_hbm.at[pl.ds(0, n)], buf.at[slot], sem.at[slot]).wait()
    w = w_ref[...]
    lane = lax.broadcasted_iota(jnp.int32, w.shape, 1)
    y = jnp.zeros((COMBINE_TOK, D_MODEL), F32)
    for k in range(TOP_K):
        wk = jnp.sum(jnp.where(lane == k, w, 0.0), axis=-1, keepdims=True)
        y = y + wk * buf[slot, k * COMBINE_TOK:(k + 1) * COMBINE_TOK, :]
    o_ref[...] = x_ref[...] + gt_ref[...] * y


def _combine(ys, pos_kmajor, x2d, gt, top_w, rows_per_mod):
    m = x2d.shape[0]
    tm = COMBINE_TOK
    steps = m // tm
    if rows_per_mod == 1:
        gt = gt.reshape(m, D_MODEL)
        gt_spec = pl.BlockSpec((tm, D_MODEL), lambda i: (i, 0))
    else:
        gt_spec = pl.BlockSpec((None, 1, D_MODEL), lambda i: (i // (rows_per_mod // tm), 0, 0))
    return pl.pallas_call(
        _combine_kernel,
        grid=(steps,),
        in_specs=[
            pl.BlockSpec((None, 1, TOP_K * tm), lambda i: (i, 0, 0), memory_space=pltpu.SMEM),
            pl.BlockSpec((None, 1, TOP_K * tm), lambda i: (jnp.minimum(i + 1, steps - 1), 0, 0), memory_space=pltpu.SMEM),
            pl.BlockSpec(memory_space=pl.ANY),
            pl.BlockSpec((tm, D_MODEL), lambda i: (i, 0)),
            gt_spec,
            pl.BlockSpec((tm, LANE), lambda i: (i, 0)),
        ],
        out_specs=pl.BlockSpec((tm, D_MODEL), lambda i: (i, 0)),
        out_shape=jax.ShapeDtypeStruct((m, D_MODEL), F32),
        scratch_shapes=[pltpu.VMEM((2, TOP_K * tm, D_MODEL), F32), pltpu.SemaphoreType.DMA((2,))],
        compiler_params=_cp(("arbitrary",)),
        name="moe_combine",
    )(pos_kmajor, pos_kmajor, ys, x2d, gt, top_w)


def _routing_tables(top_idx):
    n_tok = top_idx.shape[0]
    n_assign = n_tok * TOP_K
    rb = MOE_ROWS
    n_blocks = -(-(n_assign + N_EXPERTS * (rb - 1)) // rb)
    n_rows = n_blocks * rb
    flat_e = top_idx.reshape(-1)
    onehot = (flat_e[:, None] == jnp.arange(N_EXPERTS, dtype=jnp.int32)[None, :]).astype(jnp.int32)
    csum = jnp.cumsum(onehot, axis=0)
    rank = jnp.sum((csum - onehot) * onehot, axis=1)
    counts = csum[-1]
    nblk_e = (counts + rb - 1) // rb
    blk_start = jnp.cumsum(nblk_e) - nblk_e
    dest = (blk_start * rb)[flat_e] + rank
    total_blk = jnp.sum(nblk_e)
    last_blk = jnp.where(nblk_e > 0, blk_start + nblk_e - 1, -1)
    bidx = jnp.arange(n_blocks, dtype=jnp.int32)
    zero_blocks = jnp.concatenate([last_blk, jnp.where(bidx >= total_blk, bidx, -1)]).astype(jnp.int32)
    n_sb_max = n_blocks // MOE_SB_BLOCKS + N_EXPERTS
    sb_per_e = (nblk_e + MOE_SB_BLOCKS - 1) // MOE_SB_BLOCKS
    sb_start = jnp.cumsum(sb_per_e) - sb_per_e
    total_sb = jnp.sum(sb_per_e)
    sidx = jnp.arange(n_sb_max, dtype=jnp.int32)
    e_of = jnp.clip(jnp.searchsorted(jnp.cumsum(sb_per_e), sidx, side="right"), 0, N_EXPERTS - 1).astype(jnp.int32)
    local = sidx - sb_start[e_of]
    active = sidx < total_sb
    last_e = e_of[jnp.maximum(total_sb - 1, 0)]
    sb_e = jnp.where(active, e_of, last_e).astype(jnp.int32)
    sb_blk0 = jnp.where(active, blk_start[e_of] + local * MOE_SB_BLOCKS, 0).astype(jnp.int32)
    sb_nblk = jnp.where(active, jnp.minimum(nblk_e[e_of] - local * MOE_SB_BLOCKS, MOE_SB_BLOCKS), 0).astype(jnp.int32)
    tail = jnp.stack([total_blk, jnp.int32(n_blocks)]).astype(jnp.int32)
    return dest.astype(jnp.int32), zero_blocks, n_rows, sb_e, sb_blk0, sb_nblk, tail


def _kmajor(pos, tm):
    m = pos.shape[0]
    return pos.reshape(m // tm, tm, TOP_K).transpose(0, 2, 1).reshape(m // tm, 1, TOP_K * tm)


def _repack_w_in(w_in):
    a = DN_CONV_CH + DN_VW
    b = a + 2 * DN_HEADS
    c = b + SW_QW
    e = c + 2 * SW_KVW
    parts = [w_in[:, :a], w_in[:, b:c], w_in[:, e:], w_in[:, c:e], w_in[:, a:b]]
    pad = jnp.zeros((D_MODEL, PROJ_W - w_in.shape[1]), BF16)
    return jnp.concatenate([p.astype(BF16) for p in parts] + [pad], axis=1)


def _lane_vec(v, offset):
    return jnp.zeros((1, LANE), F32).at[0, offset:offset + v.shape[0]].set(v.astype(F32))


def kernel(x_prompt, x_sample, state_conv, state_delta, cache_swa_k, cache_swa_v, c_prompt, c_sample, w_ada, b_ada, ln1_w, w_in, conv_w, dn_a_log, dn_dt_bias, dn_norm_w, sw_q_norm_w, sw_k_norm_w, sw_sinks, w_branch_a, w_branch_b, w_out, ln2_w, router_w, router_b, w_gate_up, b_gate_up, w_down, b_down):
    assert w_ada.shape[0] == 1, "single-layer step"
    bp, t, d = x_prompt.shape
    bs = x_sample.shape[0]
    np_tok = bp * t
    l = 0

    n_c = bp + bs
    c_all = jnp.concatenate([c_prompt, c_sample, jnp.zeros((-n_c % 8, d), F32)], axis=0)
    mod = _ada_mod(c_all, w_ada[l], b_ada[l])
    mods_p = [m.reshape(bp, 1, d) for m in jnp.split(mod[:bp], 6, axis=-1)]
    mods_s = [m.reshape(bs, 1, d) for m in jnp.split(mod[bp:n_c], 6, axis=-1)]

    w_in_r = _repack_w_in(w_in[l])
    wa, wb, wo = w_branch_a[l].astype(BF16), w_branch_b[l].astype(BF16), w_out[l].astype(BF16)
    alog_lane = _lane_vec(dn_a_log[l], DN_HEADS)
    dtb_lane = _lane_vec(dn_dt_bias[l], DN_HEADS)
    rw_pad = jnp.zeros((d, LANE), F32).at[:, :N_EXPERTS].set(router_w[l])
    rb_pad = jnp.zeros((1, LANE), F32).at[0, :N_EXPERTS].set(router_b[l])
    sinks = sw_sinks[l].astype(F32)

    xp = x_prompt.reshape(np_tok, d)
    proj_p = _in_proj(xp, ln1_w[l], mods_p[1], mods_p[0], w_in_r, t, 1024)
    proj3 = proj_p.reshape(bp, t, PROJ_W)
    gates = _gdn_gates(proj3, alog_lane, dtb_lane)
    ya_p, delta_p = _gdn_prompt(proj3, gates, conv_w[l], dn_norm_w[l])
    yb_p, kn_p = _swa_prompt(proj3, sinks, sw_q_norm_w[l], sw_k_norm_w[l])
    merged_p = _merge(ya_p.reshape(np_tok, DN_VW), yb_p.reshape(np_tok, SW_QW), wa, wb, proj_p, 512)
    x1_p = _out_proj(merged_p, wo, xp, mods_p[2], t, 1024)
    h2_p, idx_p, tw_p = _router(x1_p, ln2_w[l], mods_p[4], mods_p[3], rw_pad, rb_pad, t, 512)

    xs_ = x_sample.reshape(bs, d)
    proj_s = _in_proj(xs_, ln1_w[l], mods_s[1], mods_s[0], w_in_r, 1, bs)
    ya_s, conv_s, delta_s = _gdn_step(proj_s, state_conv[l], state_delta[l], conv_w[l], alog_lane, dtb_lane, dn_norm_w[l])
    w_buf = cache_swa_k.shape[2]
    yb_s, k_s, v_s = _swa_step(proj_s, cache_swa_k[l].reshape(bs, w_buf, SW_KVW), cache_swa_v[l].reshape(bs, w_buf, SW_KVW),
                               sinks, sw_q_norm_w[l], sw_k_norm_w[l])
    merged_s = _merge(ya_s.reshape(bs, DN_VW), yb_s.reshape(bs, SW_QW), wa, wb, proj_s, bs)
    x1_s = _out_proj(merged_s, wo, xs_, mods_s[2], 1, bs)
    h2_s, idx_s, tw_s = _router(x1_s, ln2_w[l], mods_s[4], mods_s[3], rw_pad, rb_pad, 1, bs)

    top_idx = jnp.concatenate([idx_p[:, :TOP_K], idx_s[:, :TOP_K]], axis=0)
    dest, zero_blocks, n_rows, sb_e, sb_blk0, sb_nblk, tail = _routing_tables(top_idx)
    xs_sorted = _scatter_rows(h2_p, h2_s, dest, zero_blocks, n_rows)
    ys = _experts(xs_sorted, sb_e, sb_blk0, sb_nblk, tail, w_gate_up[l], b_gate_up[l], w_down[l], b_down[l])
    pos = dest.reshape(np_tok + bs, TOP_K)
    y_p = _combine(ys, _kmajor(pos[:np_tok], COMBINE_TOK), x1_p, mods_p[5], tw_p, t)
    pad_s = COMBINE_TOK - bs
    pos_s = jnp.concatenate([pos[np_tok:], jnp.zeros((pad_s, TOP_K), jnp.int32)], axis=0)
    x1_s_pad = jnp.concatenate([x1_s, jnp.zeros((pad_s, d), F32)], axis=0)
    gt2_s_pad = jnp.concatenate([mods_s[5].reshape(bs, d), jnp.zeros((pad_s, d), F32)], axis=0)
    tw_s_pad = jnp.concatenate([tw_s, jnp.zeros((pad_s, LANE), F32)], axis=0)
    y_s = _combine(ys, _kmajor(pos_s, COMBINE_TOK), x1_s_pad, gt2_s_pad.reshape(COMBINE_TOK, 1, d), tw_s_pad, 1)[:bs]

    conv_p = proj3[:, t - (DN_CONV - 1):, C_QKV:C_QKV + DN_CONV_CH]
    kp_out = kn_p[:, t - WINDOW:].reshape(bp, WINDOW, SW_KV_HEADS, SW_HD)
    vp_out = proj3[:, t - WINDOW:, C_SV:C_SV + SW_KVW].reshape(bp, WINDOW, SW_KV_HEADS, SW_HD)
    return (
        y_p.reshape(bp, t, d),
        y_s.reshape(bs, 1, d),
        conv_p[None],
        conv_s[None],
        delta_p[None],
        delta_s[None],
        kp_out[None],
        k_s.reshape(bs, w_buf, SW_KV_HEADS, SW_HD)[None],
        vp_out[None],
        v_s.reshape(bs, w_buf, SW_KV_HEADS, SW_HD)[None],
    )
```

```python
import functools

import jax
import jax.numpy as jnp
import numpy as np
from jax import lax
from jax.experimental import pallas as pl
from jax.experimental.pallas import tpu as pltpu

F32 = jnp.float32
BF16 = jnp.bfloat16

D_MODEL = 2048
PAST_LEN = 16384
DN_HEADS = 8
DN_DK = 128
DN_DV = 128
DN_CONV = 4
SW_HEADS = 16
SW_KV_HEADS = 2
SW_HD = 64
SW_GROUP = SW_HEADS // SW_KV_HEADS
WINDOW = 128
N_EXPERTS = 32
TOP_K = 4
SWIGLU_ALPHA = 1.702
SWIGLU_LIMIT = 7.0
EPS = 1e-6

DN_QK = DN_HEADS * DN_DK
DN_VW = DN_HEADS * DN_DV
DN_CONV_CH = 2 * DN_QK + DN_VW
SW_QW = SW_HEADS * SW_HD
SW_KVW = SW_KV_HEADS * SW_HD

LANE = 128
C_QKV = 0
C_Z = DN_CONV_CH
C_SQ = C_Z + DN_VW
C_GA = C_SQ + SW_QW
C_GB = C_GA + D_MODEL
C_SK = C_GB + D_MODEL
C_SV = C_SK + SW_KVW
C_BA = C_SV + SW_KVW
PROJ_W = 10240

GDN_GROUP = 256
GDN_CHUNK = 256
GDN_LEVELS = 8
MOE_ROWS = 128
MOE_SB_BLOCKS = 10
MOE_PASS_BLOCKS = 4
MOE_TF = 512
VMEM_LIMIT = 56 * 1024 * 1024


def _cp(sem, vmem=VMEM_LIMIT):
    return pltpu.CompilerParams(dimension_semantics=sem, vmem_limit_bytes=vmem)


def _dot(a, b):
    return jnp.dot(a.astype(BF16), b.astype(BF16), preferred_element_type=F32)


def _dot_nt(a, b):
    return lax.dot_general(a.astype(BF16), b.astype(BF16), (((1,), (1,)), ((), ())), preferred_element_type=F32)


def _split(a):
    hi = a.astype(BF16)
    lo = (a - hi.astype(F32)).astype(BF16)
    return hi, lo


def _dot3(a, b):
    ah, al = _split(a)
    bh, bl = _split(b)
    d = functools.partial(jnp.dot, preferred_element_type=F32)
    return d(ah, bh) + (d(ah, bl) + d(al, bh))


def _dot3_nt(a, b):
    ah, al = _split(a)
    bh, bl = _split(b)
    d = functools.partial(lax.dot_general, dimension_numbers=(((1,), (1,)), ((), ())), preferred_element_type=F32)
    return d(ah, bh) + (d(ah, bl) + d(al, bh))


def _dot_exact_lhs01(m01, b):
    b1 = b.astype(BF16)
    r = b - b1.astype(F32)
    b2 = r.astype(BF16)
    b3 = (r - b2.astype(F32)).astype(BF16)
    d = functools.partial(jnp.dot, preferred_element_type=F32)
    m = m01.astype(BF16)
    return d(m, b1) + (d(m, b2) + d(m, b3))


def _sigmoid(x):
    return 1.0 / (1.0 + jnp.exp(-x))


def _silu(x):
    return x * _sigmoid(x)


def _softplus(x):
    return jnp.maximum(x, 0.0) + jnp.log(1.0 + jnp.exp(-jnp.abs(x)))


def _ada_kernel(c_ref, w_ref, b_ref, o_ref):
    o_ref[...] = _dot(_silu(c_ref[...]), w_ref[...]) + b_ref[...]


def _ada_mod(c_all, w_ada, b_ada):
    m = c_all.shape[0]
    n = w_ada.shape[1]
    tn = 1024
    return pl.pallas_call(
        _ada_kernel,
        grid=(n // tn,),
        in_specs=[
            pl.BlockSpec((m, D_MODEL), lambda j: (0, 0)),
            pl.BlockSpec((D_MODEL, tn), lambda j: (0, j)),
            pl.BlockSpec((1, tn), lambda j: (0, j)),
        ],
        out_specs=pl.BlockSpec((m, tn), lambda j: (0, j)),
        out_shape=jax.ShapeDtypeStruct((m, n), F32),
        compiler_params=_cp(("arbitrary",)),
        name="ada_mod",
    )(c_all, w_ada, b_ada.reshape(1, n))


def _norm_mod(x, lnw, sc, sh):
    y = x * lax.rsqrt(jnp.mean(x * x, axis=-1, keepdims=True) + EPS)
    return (y * lnw) * (1.0 + sc) + sh


def _inproj_kernel(x_ref, lnw_ref, sc_ref, sh_ref, w_ref, o_ref, h_scr):
    @pl.when(pl.program_id(1) == 0)
    def _():
        h_scr[...] = _norm_mod(x_ref[...], lnw_ref[...], sc_ref[...], sh_ref[...]).astype(BF16)

    o_ref[...] = jnp.dot(h_scr[...], w_ref[...], preferred_element_type=F32)


def _in_proj(x2d, lnw, sc, sh, w_bf16, rows_per_mod, tm):
    m = x2d.shape[0]
    tn = 1024
    if rows_per_mod == 1:
        mod_spec = pl.BlockSpec((tm, D_MODEL), lambda i, j: (i, 0))
        sc, sh = sc.reshape(m, D_MODEL), sh.reshape(m, D_MODEL)
    else:
        assert rows_per_mod % tm == 0
        mod_spec = pl.BlockSpec((None, 1, D_MODEL), lambda i, j: (i // (rows_per_mod // tm), 0, 0))
    return pl.pallas_call(
        _inproj_kernel,
        grid=(m // tm, PROJ_W // tn),
        in_specs=[
            pl.BlockSpec((tm, D_MODEL), lambda i, j: (i, 0)),
            pl.BlockSpec((1, D_MODEL), lambda i, j: (0, 0)),
            mod_spec,
            mod_spec,
            pl.BlockSpec((D_MODEL, tn), lambda i, j: (0, j)),
        ],
        out_specs=pl.BlockSpec((tm, tn), lambda i, j: (i, j)),
        out_shape=jax.ShapeDtypeStruct((m, PROJ_W), F32),
        scratch_shapes=[pltpu.VMEM((tm, D_MODEL), BF16)],
        compiler_params=_cp(("arbitrary", "arbitrary")),
        name="in_proj",
    )(x2d, lnw.reshape(1, D_MODEL), sc, sh, w_bf16)


def _tri_masks(n, chunk):
    r = lax.broadcasted_iota(jnp.int32, (n, n), 0)
    c = lax.broadcasted_iota(jnp.int32, (n, n), 1)
    same = (r // chunk) == (c // chunk)
    return same, same & (r >= c), same & (r > c)


def _gates_kernel(ba_ref, alog_ref, dtb_ref, beta_ref, gc_ref, eg_ref, ek_ref, el_ref, gcrow_ref):
    same, causal, _ = _tri_masks(GDN_GROUP, GDN_CHUNK)
    lower01 = jnp.where(causal, 1.0, 0.0)
    ones01 = jnp.where(same, 1.0, 0.0)
    nega = -jnp.exp(alog_ref[...])
    dtb = dtb_ref[...]
    t = ba_ref.shape[0]

    def body(i, carry):
        r0 = pl.multiple_of(i * GDN_GROUP, GDN_GROUP)
        x = ba_ref[pl.ds(r0, GDN_GROUP), :]
        g = nega * _softplus(x + dtb)
        gc = _dot_exact_lhs01(lower01, g)
        gl = _dot_exact_lhs01(ones01, g)
        beta_ref[pl.ds(r0, GDN_GROUP), :] = _sigmoid(x)
        gc_ref[pl.ds(r0, GDN_GROUP), :] = gc
        eg_ref[pl.ds(r0, GDN_GROUP), :] = jnp.exp(gc)
        ek_ref[pl.ds(r0, GDN_GROUP), :] = jnp.exp(gl - gc)
        el_ref[pl.ds(r0, GDN_GROUP), :] = jnp.exp(gl)
        gct = gc.T
        for h in range(DN_HEADS):
            gcrow_ref[h, :, pl.ds(r0, GDN_GROUP)] = gct[DN_HEADS + h:DN_HEADS + h + 1, :]
        return carry

    lax.fori_loop(0, t // GDN_GROUP, body, 0)


def _gdn_gates(proj3, alog_lane, dtb_lane):
    b, t, _ = proj3.shape
    col = pl.BlockSpec((None, t, LANE), lambda i: (i, 0, 0))
    shp = jax.ShapeDtypeStruct((b, t, LANE), F32)
    return pl.pallas_call(
        _gates_kernel,
        grid=(b,),
        in_specs=[
            pl.BlockSpec((None, t, LANE), lambda i: (i, 0, C_BA // LANE)),
            pl.BlockSpec((1, LANE), lambda i: (0, 0)),
            pl.BlockSpec((1, LANE), lambda i: (0, 0)),
        ],
        out_specs=[col, col, col, col, col, pl.BlockSpec((None, DN_HEADS, 1, t), lambda i: (i, 0, 0, 0))],
        out_shape=[shp, shp, shp, shp, shp, jax.ShapeDtypeStruct((b, DN_HEADS, 1, t), F32)],
        compiler_params=_cp(("arbitrary",)),
        name="gdn_gates",
    )(proj3, alog_lane, dtb_lane)


def _l2norm(x):
    return x * lax.rsqrt(jnp.sum(x * x, axis=-1, keepdims=True) + EPS)


GDN_HPS = 4
GDN_TILE = 1024


def _gdn_kernel(q_ref, k_ref, v_ref, z_ref, beta_ref, gc_ref, eg_ref, ek_ref, el_ref, gcrow_ref,
                cwq_ref, cwk_ref, cwv_ref, nw_ref, o_ref, s_ref,
                pad_scr, hist_scr, qn_scr, kn_scr, vn_scr, oacc_scr, s_scr):
    t = q_ref.shape[0]
    wdt = GDN_HPS * LANE
    h0 = pl.program_id(1) * GDN_HPS
    pad = 8

    @pl.when(pl.program_id(2) == 0)
    def _():
        hist_scr[...] = jnp.zeros(hist_scr.shape, F32)
        s_scr[...] = jnp.zeros(s_scr.shape, F32)

    def conv_silu(stream, u_ref, cw_ref):
        pad_scr[0:pad, :] = hist_scr[stream]
        pad_scr[pad:pad + t, :] = u_ref[...]
        hist_scr[stream] = pad_scr[t:t + pad, :]
        y = cw_ref[DN_CONV - 1:DN_CONV, :] * pad_scr[pad:pad + t, :]
        for i in range(DN_CONV - 1):
            off = pad - (DN_CONV - 1) + i
            y = y + cw_ref[i:i + 1, :] * pad_scr[off:off + t, :]
        return _silu(y)

    yq = conv_silu(0, q_ref, cwq_ref)
    for hh in range(GDN_HPS):
        sl = slice(hh * LANE, (hh + 1) * LANE)
        qn_scr[:, sl] = _l2norm(yq[:, sl]) * (DN_DK ** -0.5)
    yk = conv_silu(1, k_ref, cwk_ref)
    for hh in range(GDN_HPS):
        sl = slice(hh * LANE, (hh + 1) * LANE)
        kn_scr[:, sl] = _l2norm(yk[:, sl])
    vn_scr[...] = conv_silu(2, v_ref, cwv_ref)

    n = GDN_GROUP
    c = GDN_CHUNK
    _, causal, strict = _tri_masks(n, c)
    rr = lax.broadcasted_iota(jnp.int32, (n, n), 0)
    cc = lax.broadcasted_iota(jnp.int32, (n, n), 1)
    eye = jnp.where(rr == cc, 1.0, 0.0)
    lane = lax.broadcasted_iota(jnp.int32, (n, LANE), 1)

    def pick(ref, r0, sel):
        return jnp.sum(jnp.where(sel, ref[pl.ds(r0, n), :], 0.0), axis=-1, keepdims=True)

    def head_group(hh, r0):
        sl = slice(hh * LANE, (hh + 1) * LANE)
        sel_b = lane == h0 + hh
        sel_g = lane == h0 + hh + DN_HEADS
        q = qn_scr[pl.ds(r0, n), sl]
        k = kn_scr[pl.ds(r0, n), sl]
        v = vn_scr[pl.ds(r0, n), sl]
        beta = pick(beta_ref, r0, sel_b)
        gc = pick(gc_ref, r0, sel_g)
        eg = pick(eg_ref, r0, sel_g)
        ek = pick(ek_ref, r0, sel_g)
        el = pick(el_ref, r0, sel_g)
        gcrow = gcrow_ref[hh, :, pl.ds(r0, n)]
        decay = jnp.where(causal, jnp.exp(gc - gcrow), 0.0)
        a_low = jnp.where(strict, beta * _dot_nt(k, k) * decay, 0.0)
        pw = [-a_low]
        for _ in range(GDN_LEVELS - 1):
            pw.append(_dot(pw[-1], pw[-1]))
        fs = [eye + pw[i] + pw[i + 1] + _dot(pw[i], pw[i + 1]) for i in range(0, GDN_LEVELS, 2)]
        while len(fs) > 1:
            fs = [_dot(fs[i], fs[i + 1]) for i in range(0, len(fs), 2)]
        rhs = jnp.concatenate([v * beta, k * (beta * eg)], axis=1)
        sol = _dot(fs[0], rhs)
        value = sol[:, :DN_DV]
        kcum = sol[:, DN_DV:]
        intra = _dot_nt(q, k) * decay
        q_dec = q * eg
        k_dec = k * ek
        for j in range(n // c):
            lo, hi = j * c, (j + 1) * c
            s = s_scr[hh]
            r = _dot(jnp.concatenate([kcum[lo:hi], q_dec[lo:hi]], axis=0), s)
            v_new = value[lo:hi] - r[:c]
            parts = []
            if lo:
                parts.append(jnp.zeros((lo, DN_DV), F32))
            parts.append(v_new)
            if hi < n:
                parts.append(jnp.zeros((n - hi, DN_DV), F32))
            o = r[c:] + _dot(intra[lo:hi], jnp.concatenate(parts, axis=0))
            oacc_scr[pl.ds(r0 + lo, c), sl] = o
            s_scr[hh] = s * el[lo:lo + 1] + _dot(k_dec[lo:hi].T, v_new)

    def body(i, carry):
        r0 = pl.multiple_of(i * n, n)
        for hh in range(GDN_HPS):
            head_group(hh, r0)
        return carry

    lax.fori_loop(0, t // n, body, 0)
    for hh in range(GDN_HPS):
        sl = slice(hh * LANE, (hh + 1) * LANE)
        o = oacc_scr[:, sl]
        y = o * lax.rsqrt(jnp.mean(o * o, axis=-1, keepdims=True) + EPS)
        o_ref[:, sl] = (y * nw_ref[...]) * _silu(z_ref[:, sl])
    s_ref[...] = s_scr[...]


def _gdn_prompt(proj3, gates, conv_w, norm_w):
    b, t, _ = proj3.shape
    beta, gc, eg, ek, el, gcrow = gates
    hps = GDN_HPS
    wdt = hps * LANE
    steps = DN_HEADS // hps
    tt = GDN_TILE
    assert t % tt == 0 and tt % GDN_GROUP == 0

    def colspec(base):
        return pl.BlockSpec((None, tt, wdt), lambda i, j, r, base=base: (i, r, base + j))

    gate = pl.BlockSpec((None, tt, LANE), lambda i, j, r: (i, r, 0))

    def cwspec(base):
        return pl.BlockSpec((DN_CONV, wdt), lambda i, j, r, base=base: (0, base + j))

    return pl.pallas_call(
        _gdn_kernel,
        grid=(b, steps, t // tt),
        in_specs=[
            colspec(0), colspec(steps), colspec(2 * steps), colspec(C_Z // wdt),
            gate, gate, gate, gate, gate,
            pl.BlockSpec((None, hps, 1, tt), lambda i, j, r: (i, j, 0, r)),
            cwspec(0), cwspec(steps), cwspec(2 * steps),
            pl.BlockSpec((1, DN_DV), lambda i, j, r: (0, 0)),
        ],
        out_specs=[
            pl.BlockSpec((None, tt, wdt), lambda i, j, r: (i, r, j)),
            pl.BlockSpec((None, hps, DN_DK, DN_DV), lambda i, j, r: (i, j, 0, 0)),
        ],
        out_shape=[
            jax.ShapeDtypeStruct((b, t, DN_VW), F32),
            jax.ShapeDtypeStruct((b, DN_HEADS, DN_DK, DN_DV), F32),
        ],
        scratch_shapes=[
            pltpu.VMEM((tt + 8, wdt), F32),
            pltpu.VMEM((3, 8, wdt), F32),
            pltpu.VMEM((tt, wdt), F32),
            pltpu.VMEM((tt, wdt), F32),
            pltpu.VMEM((tt, wdt), F32),
            pltpu.VMEM((tt, wdt), F32),
            pltpu.VMEM((hps, DN_DK, DN_DV), F32),
        ],
        compiler_params=_cp(("arbitrary", "arbitrary", "arbitrary")),
        name="gdn_prompt",
    )(proj3, proj3, proj3, proj3, beta, gc, eg, ek, el, gcrow, conv_w, conv_w, conv_w, norm_w.reshape(1, DN_DV))


def _gdn_step_kernel(p_ref, cprev_ref, s_ref, cw_ref, alog_ref, dtb_ref, nw_ref, o_ref, cnew_ref, snew_ref):
    u = p_ref[:, C_QKV:C_QKV + DN_CONV_CH]
    prev = cprev_ref[...]
    y = cw_ref[DN_CONV - 1:DN_CONV, :] * u
    for i in range(DN_CONV - 1):
        y = y + cw_ref[i:i + 1, :] * prev[i:i + 1, :]
    y = _silu(y)
    cnew_ref[0:DN_CONV - 2, :] = prev[1:DN_CONV - 1, :]
    cnew_ref[DN_CONV - 2:DN_CONV - 1, :] = u
    ba = p_ref[:, C_BA:C_BA + LANE]
    beta_l = _sigmoid(ba)
    a_l = jnp.exp(-jnp.exp(alog_ref[...]) * _softplus(ba + dtb_ref[...]))
    lane = lax.broadcasted_iota(jnp.int32, (1, LANE), 1)
    row8 = lax.broadcasted_iota(jnp.int32, (8, LANE), 0)
    for h in range(DN_HEADS):
        q = _l2norm(y[:, h * DN_DK:(h + 1) * DN_DK]) * (DN_DK ** -0.5)
        k = _l2norm(y[:, DN_QK + h * DN_DK:DN_QK + (h + 1) * DN_DK])
        v = y[:, 2 * DN_QK + h * DN_DV:2 * DN_QK + (h + 1) * DN_DV]
        beta = jnp.sum(jnp.where(lane == h, beta_l, 0.0), axis=-1, keepdims=True)
        a = jnp.sum(jnp.where(lane == h + DN_HEADS, a_l, 0.0), axis=-1, keepdims=True)
        s = s_ref[h]
        kq = jnp.where(row8 == 0, k, jnp.where(row8 == 1, q, 0.0))
        r = _dot3(kq, s)
        v_new = beta * (v - a * r[0:1])
        o = a * r[1:2] + jnp.sum(q * k, axis=-1, keepdims=True) * v_new
        k8 = jnp.where(row8 == 0, k, 0.0)
        v8 = jnp.where(row8 == 0, v_new, 0.0)
        snew_ref[h] = s * a + _dot3(k8.T, v8)
        yo = o * lax.rsqrt(jnp.mean(o * o, axis=-1, keepdims=True) + EPS)
        z = p_ref[:, C_Z + h * DN_DV:C_Z + (h + 1) * DN_DV]
        o_ref[:, h * DN_DV:(h + 1) * DN_DV] = (yo * nw_ref[...]) * _silu(z)


def _gdn_step(proj_s, conv_prev, s0, conv_w, alog_lane, dtb_lane, norm_w):
    b = proj_s.shape[0]
    return pl.pallas_call(
        _gdn_step_kernel,
        grid=(b,),
        in_specs=[
            pl.BlockSpec((None, 1, PROJ_W), lambda i: (i, 0, 0)),
            pl.BlockSpec((None, DN_CONV - 1, DN_CONV_CH), lambda i: (i, 0, 0)),
            pl.BlockSpec((None, DN_HEADS, DN_DK, DN_DV), lambda i: (i, 0, 0, 0)),
            pl.BlockSpec((DN_CONV, DN_CONV_CH), lambda i: (0, 0)),
            pl.BlockSpec((1, LANE), lambda i: (0, 0)),
            pl.BlockSpec((1, LANE), lambda i: (0, 0)),
            pl.BlockSpec((1, DN_DV), lambda i: (0, 0)),
        ],
        out_specs=[
            pl.BlockSpec((None, 1, DN_VW), lambda i: (i, 0, 0)),
            pl.BlockSpec((None, DN_CONV - 1, DN_CONV_CH), lambda i: (i, 0, 0)),
            pl.BlockSpec((None, DN_HEADS, DN_DK, DN_DV), lambda i: (i, 0, 0, 0)),
        ],
        out_shape=[
            jax.ShapeDtypeStruct((b, 1, DN_VW), F32),
            jax.ShapeDtypeStruct((b, DN_CONV - 1, DN_CONV_CH), F32),
            jax.ShapeDtypeStruct((b, DN_HEADS, DN_DK, DN_DV), F32),
        ],
        compiler_params=_cp(("arbitrary",)),
        name="gdn_step",
    )(proj_s.reshape(b, 1, PROJ_W), conv_prev, s0, conv_w, alog_lane, dtb_lane, norm_w.reshape(1, DN_DV))


def _alibi_slope(h):
    return float(2.0 ** (-8.0 * (h + 1) / SW_HEADS))


def _head_rms(x, w):
    return (x * lax.rsqrt(jnp.mean(x * x, axis=-1, keepdims=True) + EPS)) * w


SWA_HB = 4


def _swa_kernel(sinks_ref, q_ref, kc_ref, kp_ref, vc_ref, vp_ref, qw_ref, kw_ref, o_ref, kn_ref):
    blk = pl.program_id(1)
    w = WINDOW
    rows = SWA_HB * w
    qi = lax.broadcasted_iota(jnp.int32, (rows, 2 * w), 0)
    kj = lax.broadcasted_iota(jnp.int32, (rows, 2 * w), 1)
    dist = (qi & (w - 1)) + w - kj
    valid = (dist >= 0) & (dist < w) & ((kj >= w) | (blk > 0))
    distf = dist.astype(F32)
    stripe = lax.broadcasted_iota(jnp.int32, (rows, 1), 0) // w
    kc = kc_ref[...]
    kp = kp_ref[...]
    kbands, vbands = [], []
    for g in range(SW_KV_HEADS):
        sl = slice(g * SW_HD, (g + 1) * SW_HD)
        kcn = _head_rms(kc[:, sl], kw_ref[...])
        kn_ref[:, sl] = kcn
        kbands.append(jnp.concatenate([_head_rms(kp[:, sl], kw_ref[...]), kcn], axis=0))
        vbands.append(jnp.concatenate([vp_ref[:, sl], vc_ref[:, sl]], axis=0))
    for hb in range(SW_HEADS // SWA_HB):
        heads = range(hb * SWA_HB, (hb + 1) * SWA_HB)
        g = heads[0] // SW_GROUP
        qs = jnp.concatenate([_head_rms(q_ref[:, h * SW_HD:(h + 1) * SW_HD], qw_ref[...]) for h in heads], axis=0)
        slope = jnp.zeros((rows, 1), F32)
        sink = jnp.zeros((rows, 1), F32)
        for i, h in enumerate(heads):
            slope = jnp.where(stripe == i, _alibi_slope(h), slope)
            sink = jnp.where(stripe == i, sinks_ref[h], sink)
        s = _dot_nt(qs, kbands[g]) * (SW_HD ** -0.5) - slope * distf
        s = jnp.where(valid, s, -jnp.inf)
        m = jnp.maximum(jnp.max(s, axis=-1, keepdims=True), sink)
        p = jnp.exp(s - m)
        den = jnp.sum(p, axis=-1, keepdims=True) + jnp.exp(sink - m)
        o = _dot(p / den, vbands[g])
        for i, h in enumerate(heads):
            o_ref[:, h * SW_HD:(h + 1) * SW_HD] = o[i * w:(i + 1) * w]


def _swa_prompt(proj3, sinks, qw, kw):
    b, t, _ = proj3.shape
    nb = t // WINDOW
    kcol, vcol = C_SK // LANE, C_SV // LANE

    def cur(col):
        return pl.BlockSpec((None, WINDOW, SW_KVW), lambda i, j, s, col=col: (i, j, col))

    def prev(col):
        return pl.BlockSpec((None, WINDOW, SW_KVW), lambda i, j, s, col=col: (i, jnp.maximum(j - 1, 0), col))

    return pl.pallas_call(
        _swa_kernel,
        grid_spec=pltpu.PrefetchScalarGridSpec(
            num_scalar_prefetch=1,
            grid=(b, nb),
            in_specs=[
                pl.BlockSpec((None, WINDOW, SW_QW), lambda i, j, s: (i, j, C_SQ // SW_QW)),
                cur(kcol), prev(kcol), cur(vcol), prev(vcol),
                pl.BlockSpec((1, SW_HD), lambda i, j, s: (0, 0)),
                pl.BlockSpec((1, SW_HD), lambda i, j, s: (0, 0)),
            ],
            out_specs=[
                pl.BlockSpec((None, WINDOW, SW_QW), lambda i, j, s: (i, j, 0)),
                pl.BlockSpec((None, WINDOW, SW_KVW), lambda i, j, s: (i, j, 0)),
            ],
        ),
        out_shape=[
            jax.ShapeDtypeStruct((b, t, SW_QW), F32),
            jax.ShapeDtypeStruct((b, t, SW_KVW), F32),
        ],
        compiler_params=_cp(("arbitrary", "arbitrary")),
        name="swa_prompt",
    )(sinks, proj3, proj3, proj3, proj3, proj3, qw.reshape(1, SW_HD), kw.reshape(1, SW_HD))


def _swa_step_kernel(sinks_ref, p_ref, kbuf_ref, vbuf_ref, qw_ref, kw_ref, o_ref, knew_ref, vnew_ref, kcat, vcat):
    w = kbuf_ref.shape[0]
    rows = kcat.shape[0]
    knew = p_ref[:, C_SK:C_SK + SW_KVW]
    vnew = p_ref[:, C_SV:C_SV + SW_KVW]
    kcat[...] = jnp.zeros(kcat.shape, F32)
    vcat[...] = jnp.zeros(vcat.shape, F32)
    kcat[0:w, :] = kbuf_ref[...]
    vcat[0:w, :] = vbuf_ref[...]
    for g in range(SW_KV_HEADS):
        sl = slice(g * SW_HD, (g + 1) * SW_HD)
        kcat[w:w + 1, sl] = _head_rms(knew[:, sl], kw_ref[...])
    vcat[w:w + 1, :] = vnew
    knew_ref[...] = kcat[1:w + 1, :]
    vnew_ref[...] = vcat[1:w + 1, :]
    j = lax.broadcasted_iota(jnp.int32, (rows, 1), 0)
    dist = w - j
    valid = (dist >= 0) & (dist < WINDOW)
    distf = dist.astype(F32)
    for h in range(SW_HEADS):
        g = h // SW_GROUP
        sl = slice(g * SW_HD, (g + 1) * SW_HD)
        qh = _head_rms(p_ref[:, C_SQ + h * SW_HD:C_SQ + (h + 1) * SW_HD], qw_ref[...])
        s = jnp.sum(kcat[:, sl] * qh, axis=-1, keepdims=True) * (SW_HD ** -0.5) - _alibi_slope(h) * distf
        s = jnp.where(valid, s, -jnp.inf)
        sink = sinks_ref[h]
        m = jnp.maximum(jnp.max(s, axis=0, keepdims=True), sink)
        p = jnp.exp(s - m)
        den = jnp.sum(p, axis=0, keepdims=True) + jnp.exp(sink - m)
        o_ref[:, h * SW_HD:(h + 1) * SW_HD] = jnp.sum((p / den) * vcat[:, sl], axis=0, keepdims=True)


def _swa_step(proj_s, kbuf, vbuf, sinks, qw, kw):
    b = proj_s.shape[0]
    w = kbuf.shape[1]
    rows = w + 8
    buf = pl.BlockSpec((None, w, SW_KVW), lambda i, s: (i, 0, 0))
    return pl.pallas_call(
        _swa_step_kernel,
        grid_spec=pltpu.PrefetchScalarGridSpec(
            num_scalar_prefetch=1,
            grid=(b,),
            in_specs=[
                pl.BlockSpec((None, 1, PROJ_W), lambda i, s: (i, 0, 0)),
                buf, buf,
                pl.BlockSpec((1, SW_HD), lambda i, s: (0, 0)),
                pl.BlockSpec((1, SW_HD), lambda i, s: (0, 0)),
            ],
            out_specs=[pl.BlockSpec((None, 1, SW_QW), lambda i, s: (i, 0, 0)), buf, buf],
            scratch_shapes=[pltpu.VMEM((rows, SW_KVW), F32), pltpu.VMEM((rows, SW_KVW), F32)],
        ),
        out_shape=[
            jax.ShapeDtypeStruct((b, 1, SW_QW), F32),
            jax.ShapeDtypeStruct((b, w, SW_KVW), F32),
            jax.ShapeDtypeStruct((b, w, SW_KVW), F32),
        ],
        compiler_params=_cp(("arbitrary",)),
        name="swa_step",
    )(sinks, proj_s.reshape(b, 1, PROJ_W), kbuf, vbuf, qw.reshape(1, SW_HD), kw.reshape(1, SW_HD))


def _merge_kernel(ya_ref, yb_ref, wa_ref, wb_ref, ga_ref, gb_ref, o_ref):
    a = _dot(ya_ref[...], wa_ref[...])
    b = _dot(yb_ref[...], wb_ref[...])
    o_ref[...] = (_sigmoid(ga_ref[...]) * a + _sigmoid(gb_ref[...]) * b).astype(BF16)


def _merge(ya, yb, wa, wb, proj, tm):
    m = ya.shape[0]
    tn = 1024
    return pl.pallas_call(
        _merge_kernel,
        grid=(m // tm, D_MODEL // tn),
        in_specs=[
            pl.BlockSpec((tm, DN_VW), lambda i, j: (i, 0)),
            pl.BlockSpec((tm, SW_QW), lambda i, j: (i, 0)),
            pl.BlockSpec((DN_VW, tn), lambda i, j: (0, j)),
            pl.BlockSpec((SW_QW, tn), lambda i, j: (0, j)),
            pl.BlockSpec((tm, tn), lambda i, j: (i, C_GA // tn + j)),
            pl.BlockSpec((tm, tn), lambda i, j: (i, C_GB // tn + j)),
        ],
        out_specs=pl.BlockSpec((tm, tn), lambda i, j: (i, j)),
        out_shape=jax.ShapeDtypeStruct((m, D_MODEL), BF16),
        compiler_params=_cp(("arbitrary", "arbitrary")),
        name="merge",
    )(ya, yb, wa, wb, proj, proj)


def _outproj_kernel(mg_ref, w_ref, x_ref, gt_ref, o_ref):
    o_ref[...] = x_ref[...] + gt_ref[...] * jnp.dot(mg_ref[...], w_ref[...], preferred_element_type=F32)


def _mod_spec(rows_per_mod, tm, tn):
    if rows_per_mod == 1:
        return pl.BlockSpec((tm, tn), lambda i, j: (i, j))
    return pl.BlockSpec((None, 1, tn), lambda i, j: (i // (rows_per_mod // tm), 0, j))


def _out_proj(merged, w_bf16, x2d, gt, rows_per_mod, tm):
    m = x2d.shape[0]
    tn = 1024
    if rows_per_mod == 1:
        gt = gt.reshape(m, D_MODEL)
    return pl.pallas_call(
        _outproj_kernel,
        grid=(m // tm, D_MODEL // tn),
        in_specs=[
            pl.BlockSpec((tm, D_MODEL), lambda i, j: (i, 0)),
            pl.BlockSpec((D_MODEL, tn), lambda i, j: (0, j)),
            pl.BlockSpec((tm, tn), lambda i, j: (i, j)),
            _mod_spec(rows_per_mod, tm, tn),
        ],
        out_specs=pl.BlockSpec((tm, tn), lambda i, j: (i, j)),
        out_shape=jax.ShapeDtypeStruct((m, D_MODEL), F32),
        compiler_params=_cp(("arbitrary", "arbitrary")),
        name="out_proj",
    )(merged, w_bf16, x2d, gt)


def _router_kernel(x_ref, lnw_ref, sc_ref, sh_ref, rw_ref, rb_ref, h_ref, idx_ref, w_ref):
    hmod = _norm_mod(x_ref[...], lnw_ref[...], sc_ref[...], sh_ref[...])
    h_ref[...] = hmod
    logits = _dot3(hmod, rw_ref[...]) + rb_ref[...]
    lane = lax.broadcasted_iota(jnp.int32, logits.shape, 1)
    cur = jnp.where(lane < N_EXPERTS, logits, -jnp.inf)
    vals, idxs = [], []
    for _ in range(TOP_K):
        m = jnp.max(cur, axis=-1, keepdims=True)
        ix = jnp.min(jnp.where(cur == m, lane, LANE), axis=-1, keepdims=True)
        vals.append(m)
        idxs.append(ix)
        cur = jnp.where(lane == ix, -jnp.inf, cur)
    es = [jnp.exp(v - vals[0]) for v in vals]
    den = es[0] + es[1] + es[2] + es[3]
    idx_out = jnp.zeros(logits.shape, jnp.int32)
    w_out = jnp.zeros(logits.shape, F32)
    for k in range(TOP_K):
        idx_out = jnp.where(lane == k, idxs[k], idx_out)
        w_out = jnp.where(lane == k, es[k] / den, w_out)
    idx_ref[...] = idx_out
    w_ref[...] = w_out


def _router(x2d, lnw, sc, sh, rw_pad, rb_pad, rows_per_mod, tm):
    m = x2d.shape[0]
    if rows_per_mod == 1:
        mod_spec = pl.BlockSpec((tm, D_MODEL), lambda i: (i, 0))
        sc, sh = sc.reshape(m, D_MODEL), sh.reshape(m, D_MODEL)
    else:
        mod_spec = pl.BlockSpec((None, 1, D_MODEL), lambda i: (i // (rows_per_mod // tm), 0, 0))
    row = pl.BlockSpec((tm, D_MODEL), lambda i: (i, 0))
    small = pl.BlockSpec((tm, LANE), lambda i: (i, 0))
    return pl.pallas_call(
        _router_kernel,
        grid=(m // tm,),
        in_specs=[
            row,
            pl.BlockSpec((1, D_MODEL), lambda i: (0, 0)),
            mod_spec, mod_spec,
            pl.BlockSpec((D_MODEL, LANE), lambda i: (0, 0)),
            pl.BlockSpec((1, LANE), lambda i: (0, 0)),
        ],
        out_specs=[row, small, small],
        out_shape=[
            jax.ShapeDtypeStruct((m, D_MODEL), F32),
            jax.ShapeDtypeStruct((m, LANE), jnp.int32),
            jax.ShapeDtypeStruct((m, LANE), F32),
        ],
        compiler_params=_cp(("arbitrary",)),
        name="router",
    )(x2d, lnw.reshape(1, D_MODEL), sc, sh, rw_pad, rb_pad)


SCATTER_TOK = 256
DMA_UNROLL = 16


def _scatter_kernel(zl_ref, dest_ref, hp_ref, hs_ref, xs_hbm, zbuf, sem, zsem):
    i = pl.program_id(0)
    rb = MOE_ROWS
    n_prompt_steps = pl.num_programs(0) - 1

    @pl.when(i == 0)
    def _():
        zbuf[...] = jnp.zeros(zbuf.shape, F32)

        def zero_copy(n):
            return pltpu.make_async_copy(zbuf, xs_hbm.at[pl.ds(zl_ref[n] * rb, rb)], zsem)

        def start(n, c):
            @pl.when(zl_ref[n] >= 0)
            def _():
                zero_copy(n).start()
            return c

        def wait(n, c):
            @pl.when(zl_ref[n] >= 0)
            def _():
                zero_copy(n).wait()
            return c

        lax.fori_loop(0, zl_ref.shape[0], start, 0)
        lax.fori_loop(0, zl_ref.shape[0], wait, 0)

    def scatter(src_ref):
        n_tok = src_ref.shape[0]

        def start(a, c):
            t = lax.shift_right_logical(a, 2)
            pltpu.make_async_copy(src_ref.at[pl.ds(t, 1)], xs_hbm.at[pl.ds(dest_ref[0, a], 1)], sem).start()
            return c

        lax.fori_loop(0, n_tok * TOP_K, start, 0, unroll=DMA_UNROLL)
        for _ in range(TOP_K):
            pltpu.make_async_copy(src_ref, xs_hbm.at[pl.ds(0, n_tok)], sem).wait()

    @pl.when(i < n_prompt_steps)
    def _():
        scatter(hp_ref)

    @pl.when(i == n_prompt_steps)
    def _():
        scatter(hs_ref)


def _scatter_rows(h_p, h_s, dest, zero_blocks, n_rows):
    assert TOP_K == 4
    n_p, n_s = h_p.shape[0], h_s.shape[0]
    steps_p = n_p // SCATTER_TOK
    per = SCATTER_TOK * TOP_K
    dest_s = jnp.concatenate([dest[n_p * TOP_K:], jnp.zeros((per - n_s * TOP_K,), jnp.int32)])
    dest3 = jnp.concatenate([dest[:n_p * TOP_K], dest_s]).reshape(steps_p + 1, 1, per)
    return pl.pallas_call(
        _scatter_kernel,
        grid_spec=pltpu.PrefetchScalarGridSpec(
            num_scalar_prefetch=1,
            grid=(steps_p + 1,),
            in_specs=[
                pl.BlockSpec((None, 1, per), lambda i, zl: (i, 0, 0), memory_space=pltpu.SMEM),
                pl.BlockSpec((SCATTER_TOK, D_MODEL), lambda i, zl: (jnp.minimum(i, steps_p - 1), 0)),
                pl.BlockSpec((n_s, D_MODEL), lambda i, zl: (0, 0)),
            ],
            out_specs=pl.BlockSpec(memory_space=pl.ANY),
            scratch_shapes=[
                pltpu.VMEM((MOE_ROWS, D_MODEL), F32),
                pltpu.SemaphoreType.DMA(()),
                pltpu.SemaphoreType.DMA(()),
            ],
        ),
        out_shape=jax.ShapeDtypeStruct((n_rows, D_MODEL), F32),
        compiler_params=_cp(("arbitrary",)),
        name="moe_scatter",
    )(zero_blocks, dest3, h_p, h_s)


def _experts_kernel(sbe_ref, sbb_ref, sbn_ref, tail_ref, xs_hbm, wg_ref, wl_ref, wd_ref, bg_ref, bl_ref, bd_ref,
                    ys_hbm, xf_scr, xb_scr, acc_scr, sem_in, sem_out):
    s = pl.program_id(0)
    j = pl.program_id(1)
    last_s = pl.num_programs(0) - 1
    last_j = pl.num_programs(1) - 1
    nblk = sbn_ref[s]
    blk0 = sbb_ref[s]
    rb = MOE_ROWS

    def in_copy(first_blk, b):
        return pltpu.make_async_copy(xs_hbm.at[pl.ds((first_blk + b) * rb, rb)], xf_scr.at[pl.ds(b * rb, rb)], sem_in)

    def out_copy(first_blk, b):
        return pltpu.make_async_copy(acc_scr.at[pl.ds(b * rb, rb)], ys_hbm.at[pl.ds((first_blk + b) * rb, rb)], sem_out)

    def loop(n, fn):
        def body(b, c):
            fn(b)
            return c
        lax.fori_loop(0, n, body, 0)

    @pl.when(j == 0)
    def _():
        @pl.when(s == 0)
        def _():
            loop(nblk, lambda b: in_copy(blk0, b).start())

        loop(nblk, lambda b: in_copy(blk0, b).wait())

        @pl.when(s > 0)
        def _():
            prev0 = sbb_ref[s - 1]
            loop(sbn_ref[s - 1], lambda b: out_copy(prev0, b).wait())

        def cast(b):
            r0 = pl.multiple_of(b * rb, rb)
            xb_scr[pl.ds(r0, rb), :] = xf_scr[pl.ds(r0, rb), :].astype(BF16)
            acc_scr[pl.ds(r0, rb), :] = jnp.broadcast_to(bd_ref[...], (rb, D_MODEL))
        loop(nblk, cast)

        @pl.when(s < last_s)
        def _():
            nxt0 = sbb_ref[s + 1]
            loop(sbn_ref[s + 1], lambda b: in_copy(nxt0, b).start())

    @pl.when(nblk > 0)
    def _():
        def mlp(r0, rows):
            x = xb_scr[pl.ds(r0, rows), :]
            glu = jnp.dot(x, wg_ref[...].astype(BF16), preferred_element_type=F32) + bg_ref[...]
            lin = jnp.dot(x, wl_ref[...].astype(BF16), preferred_element_type=F32) + bl_ref[...]
            glu = jnp.minimum(glu, SWIGLU_LIMIT)
            lin = jnp.clip(lin, -SWIGLU_LIMIT, SWIGLU_LIMIT)
            act = glu * _sigmoid(SWIGLU_ALPHA * glu) * (lin + 1.0)
            acc_scr[pl.ds(r0, rows), :] += jnp.dot(act.astype(BF16), wd_ref[...].astype(BF16),
                                                   preferred_element_type=F32)

        full = nblk // MOE_PASS_BLOCKS
        big = MOE_PASS_BLOCKS * rb
        loop(full, lambda p: mlp(pl.multiple_of(p * big, big), big))
        done = full * MOE_PASS_BLOCKS
        part = MOE_PASS_BLOCKS // 2
        while part >= 1:
            take = ((nblk - done) // part) * part

            @pl.when(take > 0)
            def _(done=done, part=part):
                mlp(pl.multiple_of(done * rb, rb), part * rb)

            done = done + take
            part //= 2

    @pl.when(j == last_j)
    def _():
        loop(nblk, lambda b: out_copy(blk0, b).start())

    @pl.when((s == last_s) & (j == last_j))
    def _():
        loop(nblk, lambda b: out_copy(blk0, b).wait())
        acc_scr[0:rb, :] = jnp.zeros((rb, D_MODEL), F32)

        def zero_copy(b):
            return pltpu.make_async_copy(acc_scr.at[pl.ds(0, rb)], ys_hbm.at[pl.ds(b * rb, rb)], sem_out)

        def start(b, c):
            zero_copy(b).start()
            return c

        def wait(b, c):
            zero_copy(b).wait()
            return c

        lax.fori_loop(tail_ref[0], tail_ref[1], start, 0)
        lax.fori_loop(tail_ref[0], tail_ref[1], wait, 0)


def _experts(xs, sb_e, sb_blk0, sb_nblk, tail, w_gate_up, b_gate_up, w_down, b_down):
    n_rows = xs.shape[0]
    n_sb = sb_e.shape[0]
    tf = MOE_TF
    nj = D_MODEL // tf
    rmax = MOE_SB_BLOCKS * MOE_ROWS

    def jj(s, j, n):
        return jnp.where(n[s] > 0, j, nj - 1)

    return pl.pallas_call(
        _experts_kernel,
        grid_spec=pltpu.PrefetchScalarGridSpec(
            num_scalar_prefetch=4,
            grid=(n_sb, nj),
            in_specs=[
                pl.BlockSpec(memory_space=pl.ANY),
                pl.BlockSpec((None, D_MODEL, tf), lambda s, j, e, b, n, tl: (e[s], 0, jj(s, j, n))),
                pl.BlockSpec((None, D_MODEL, tf), lambda s, j, e, b, n, tl: (e[s], 0, nj + jj(s, j, n))),
                pl.BlockSpec((None, tf, D_MODEL), lambda s, j, e, b, n, tl: (e[s], jj(s, j, n), 0)),
                pl.BlockSpec((None, 1, tf), lambda s, j, e, b, n, tl: (e[s], 0, jj(s, j, n))),
                pl.BlockSpec((None, 1, tf), lambda s, j, e, b, n, tl: (e[s], 0, nj + jj(s, j, n))),
                pl.BlockSpec((None, 1, D_MODEL), lambda s, j, e, b, n, tl: (e[s], 0, 0)),
            ],
            out_specs=pl.BlockSpec(memory_space=pl.ANY),
            scratch_shapes=[
                pltpu.VMEM((rmax, D_MODEL), F32),
                pltpu.VMEM((rmax, D_MODEL), BF16),
                pltpu.VMEM((rmax, D_MODEL), F32),
                pltpu.SemaphoreType.DMA(()),
                pltpu.SemaphoreType.DMA(()),
            ],
        ),
        out_shape=jax.ShapeDtypeStruct((n_rows, D_MODEL), F32),
        compiler_params=_cp(("arbitrary", "arbitrary")),
        name="moe_experts",
    )(sb_e, sb_blk0, sb_nblk, tail, xs, w_gate_up, w_gate_up, w_down,
      b_gate_up.reshape(N_EXPERTS, 1, 2 * D_MODEL), b_gate_up.reshape(N_EXPERTS, 1, 2 * D_MODEL),
      b_down.reshape(N_EXPERTS, 1, D_MODEL))


COMBINE_TOK = 128


def _combine_kernel(pos_ref, posn_ref, ys_hbm, x_ref, gt_ref, w_ref, o_ref, buf, sem):
    n = COMBINE_TOK * TOP_K
    i = pl.program_id(0)
    slot = lax.rem(i, 2)

    def issue(p_ref, sl):
        def start(a, c):
            pltpu.make_async_copy(ys_hbm.at[pl.ds(p_ref[0, a], 1)], buf.at[sl, pl.ds(a, 1)], sem.at[sl]).start()
            return c
        lax.fori_loop(0, n, start, 0, unroll=DMA_UNROLL)

    @pl.when(i == 0)
    def _():
        issue(pos_ref, 0)

    @pl.when(i + 1 < pl.num_programs(0))
    def _():
        issue(posn_ref, 1 - slot)

    pltpu.make_async_copy(ys_hbm.at[pl.ds(0, n)], buf.at[slot], sem.at[slot]).wait()
    w = w_ref[...]
    lane = lax.broadcasted_iota(jnp.int32, w.shape, 1)
    y = jnp.zeros((COMBINE_TOK, D_MODEL), F32)
    for k in range(TOP_K):
        wk = jnp.sum(jnp.where(lane == k, w, 0.0), axis=-1, keepdims=True)
        y = y + wk * buf[slot, k * COMBINE_TOK:(k + 1) * COMBINE_TOK, :]
    o_ref[...] = x_ref[...] + gt_ref[...] * y


def _combine(ys, pos_kmajor, x2d, gt, top_w, rows_per_mod):
    m = x2d.shape[0]
    tm = COMBINE_TOK
    steps = m // tm
    if rows_per_mod == 1:
        gt = gt.reshape(m, D_MODEL)
        gt_spec = pl.BlockSpec((tm, D_MODEL), lambda i: (i, 0))
    else:
        gt_spec = pl.BlockSpec((None, 1, D_MODEL), lambda i: (i // (rows_per_mod // tm), 0, 0))
    return pl.pallas_call(
        _combine_kernel,
        grid=(steps,),
        in_specs=[
            pl.BlockSpec((None, 1, TOP_K * tm), lambda i: (i, 0, 0), memory_space=pltpu.SMEM),
            pl.BlockSpec((None, 1, TOP_K * tm), lambda i: (jnp.minimum(i + 1, steps - 1), 0, 0), memory_space=pltpu.SMEM),
            pl.BlockSpec(memory_space=pl.ANY),
            pl.BlockSpec((tm, D_MODEL), lambda i: (i, 0)),
            gt_spec,
            pl.BlockSpec((tm, LANE), lambda i: (i, 0)),
        ],
        out_specs=pl.BlockSpec((tm, D_MODEL), lambda i: (i, 0)),
        out_shape=jax.ShapeDtypeStruct((m, D_MODEL), F32),
        scratch_shapes=[pltpu.VMEM((2, TOP_K * tm, D_MODEL), F32), pltpu.SemaphoreType.DMA((2,))],
        compiler_params=_cp(("arbitrary",)),
        name="moe_combine",
    )(pos_kmajor, pos_kmajor, ys, x2d, gt, top_w)


def _routing_tables(top_idx):
    n_tok = top_idx.shape[0]
    n_assign = n_tok * TOP_K
    rb = MOE_ROWS
    n_blocks = -(-(n_assign + N_EXPERTS * (rb - 1)) // rb)
    n_rows = n_blocks * rb
    flat_e = top_idx.reshape(-1)
    onehot = (flat_e[:, None] == jnp.arange(N_EXPERTS, dtype=jnp.int32)[None, :]).astype(jnp.int32)
    csum = jnp.cumsum(onehot, axis=0)
    rank = jnp.sum((csum - onehot) * onehot, axis=1)
    counts = csum[-1]
    nblk_e = (counts + rb - 1) // rb
    blk_start = jnp.cumsum(nblk_e) - nblk_e
    dest = (blk_start * rb)[flat_e] + rank
    total_blk = jnp.sum(nblk_e)
    last_blk = jnp.where(nblk_e > 0, blk_start + nblk_e - 1, -1)
    bidx = jnp.arange(n_blocks, dtype=jnp.int32)
    zero_blocks = jnp.concatenate([last_blk, jnp.where(bidx >= total_blk, bidx, -1)]).astype(jnp.int32)
    n_sb_max = n_blocks // MOE_SB_BLOCKS + N_EXPERTS
    sb_per_e = (nblk_e + MOE_SB_BLOCKS - 1) // MOE_SB_BLOCKS
    sb_start = jnp.cumsum(sb_per_e) - sb_per_e
    total_sb = jnp.sum(sb_per_e)
    sidx = jnp.arange(n_sb_max, dtype=jnp.int32)
    e_of = jnp.clip(jnp.searchsorted(jnp.cumsum(sb_per_e), sidx, side="right"), 0, N_EXPERTS - 1).astype(jnp.int32)
    local = sidx - sb_start[e_of]
    active = sidx < total_sb
    last_e = e_of[jnp.maximum(total_sb - 1, 0)]
    sb_e = jnp.where(active, e_of, last_e).astype(jnp.int32)
    sb_blk0 = jnp.where(active, blk_start[e_of] + local * MOE_SB_BLOCKS, 0).astype(jnp.int32)
    sb_nblk = jnp.where(active, jnp.minimum(nblk_e[e_of] - local * MOE_SB_BLOCKS, MOE_SB_BLOCKS), 0).astype(jnp.int32)
    tail = jnp.stack([total_blk, jnp.int32(n_blocks)]).astype(jnp.int32)
    return dest.astype(jnp.int32), zero_blocks, n_rows, sb_e, sb_blk0, sb_nblk, tail


def _kmajor(pos, tm):
    m = pos.shape[0]
    return pos.reshape(m // tm, tm, TOP_K).transpose(0, 2, 1).reshape(m // tm, 1, TOP_K * tm)


def _repack_w_in(w_in):
    a = DN_CONV_CH + DN_VW
    b = a + 2 * DN_HEADS
    c = b + SW_QW
    e = c + 2 * SW_KVW
    parts = [w_in[:, :a], w_in[:, b:c], w_in[:, e:], w_in[:, c:e], w_in[:, a:b]]
    pad = jnp.zeros((D_MODEL, PROJ_W - w_in.shape[1]), BF16)
    return jnp.concatenate([p.astype(BF16) for p in parts] + [pad], axis=1)


def _lane_vec(v, offset):
    return jnp.zeros((1, LANE), F32).at[0, offset:offset + v.shape[0]].set(v.astype(F32))


def kernel(x_prompt, x_sample, state_conv, state_delta, cache_swa_k, cache_swa_v, c_prompt, c_sample, w_ada, b_ada, ln1_w, w_in, conv_w, dn_a_log, dn_dt_bias, dn_norm_w, sw_q_norm_w, sw_k_norm_w, sw_sinks, w_branch_a, w_branch_b, w_out, ln2_w, router_w, router_b, w_gate_up, b_gate_up, w_down, b_down):
    assert w_ada.shape[0] == 1, "single-layer step"
    bp, t, d = x_prompt.shape
    bs = x_sample.shape[0]
    np_tok = bp * t
    l = 0

    n_c = bp + bs
    c_all = jnp.concatenate([c_prompt, c_sample, jnp.zeros((-n_c % 8, d), F32)], axis=0)
    mod = _ada_mod(c_all, w_ada[l], b_ada[l])
    mods_p = [m.reshape(bp, 1, d) for m in jnp.split(mod[:bp], 6, axis=-1)]
    mods_s = [m.reshape(bs, 1, d) for m in jnp.split(mod[bp:n_c], 6, axis=-1)]

    w_in_r = _repack_w_in(w_in[l])
    wa, wb, wo = w_branch_a[l].astype(BF16), w_branch_b[l].astype(BF16), w_out[l].astype(BF16)
    alog_lane = _lane_vec(dn_a_log[l], DN_HEADS)
    dtb_lane = _lane_vec(dn_dt_bias[l], DN_HEADS)
    rw_pad = jnp.zeros((d, LANE), F32).at[:, :N_EXPERTS].set(router_w[l])
    rb_pad = jnp.zeros((1, LANE), F32).at[0, :N_EXPERTS].set(router_b[l])
    sinks = sw_sinks[l].astype(F32)

    xp = x_prompt.reshape(np_tok, d)
    proj_p = _in_proj(xp, ln1_w[l], mods_p[1], mods_p[0], w_in_r, t, 1024)
    proj3 = proj_p.reshape(bp, t, PROJ_W)
    gates = _gdn_gates(proj3, alog_lane, dtb_lane)
    ya_p, delta_p = _gdn_prompt(proj3, gates, conv_w[l], dn_norm_w[l])
    yb_p, kn_p = _swa_prompt(proj3, sinks, sw_q_norm_w[l], sw_k_norm_w[l])
    merged_p = _merge(ya_p.reshape(np_tok, DN_VW), yb_p.reshape(np_tok, SW_QW), wa, wb, proj_p, 512)
    x1_p = _out_proj(merged_p, wo, xp, mods_p[2], t, 1024)
    h2_p, idx_p, tw_p = _router(x1_p, ln2_w[l], mods_p[4], mods_p[3], rw_pad, rb_pad, t, 512)

    xs_ = x_sample.reshape(bs, d)
    proj_s = _in_proj(xs_, ln1_w[l], mods_s[1], mods_s[0], w_in_r, 1, bs)
    ya_s, conv_s, delta_s = _gdn_step(proj_s, state_conv[l], state_delta[l], conv_w[l], alog_lane, dtb_lane, dn_norm_w[l])
    w_buf = cache_swa_k.shape[2]
    yb_s, k_s, v_s = _swa_step(proj_s, cache_swa_k[l].reshape(bs, w_buf, SW_KVW), cache_swa_v[l].reshape(bs, w_buf, SW_KVW),
                               sinks, sw_q_norm_w[l], sw_k_norm_w[l])
    merged_s = _merge(ya_s.reshape(bs, DN_VW), yb_s.reshape(bs, SW_QW), wa, wb, proj_s, bs)
    x1_s = _out_proj(merged_s, wo, xs_, mods_s[2], 1, bs)
    h2_s, idx_s, tw_s = _router(x1_s, ln2_w[l], mods_s[4], mods_s[3], rw_pad, rb_pad, 1, bs)

    top_idx = jnp.concatenate([idx_p[:, :TOP_K], idx_s[:, :TOP_K]], axis=0)
    dest, zero_blocks, n_rows, sb_e, sb_blk0, sb_nblk, tail = _routing_tables(top_idx)
    xs_sorted = _scatter_rows(h2_p, h2_s, dest, zero_blocks, n_rows)
    ys = _experts(xs_sorted, sb_e, sb_blk0, sb_nblk, tail, w_gate_up[l], b_gate_up[l], w_down[l], b_down[l])
    pos = dest.reshape(np_tok + bs, TOP_K)
    y_p = _combine(ys, _kmajor(pos[:np_tok], COMBINE_TOK), x1_p, mods_p[5], tw_p, t)
    pad_s = COMBINE_TOK - bs
    pos_s = jnp.concatenate([pos[np_tok:], jnp.zeros((pad_s, TOP_K), jnp.int32)], axis=0)
    x1_s_pad = jnp.concatenate([x1_s, jnp.zeros((pad_s, d), F32)], axis=0)
    gt2_s_pad = jnp.concatenate([mods_s[5].reshape(bs, d), jnp.zeros((pad_s, d), F32)], axis=0)
    tw_s_pad = jnp.concatenate([tw_s, jnp.zeros((pad_s, LANE), F32)], axis=0)
    y_s = _combine(ys, _kmajor(pos_s, COMBINE_TOK), x1_s_pad, gt2_s_pad.reshape(COMBINE_TOK, 1, d), tw_s_pad, 1)[:bs]

    conv_p = proj3[:, t - (DN_CONV - 1):, C_QKV:C_QKV + DN_CONV_CH]
    kp_out = kn_p[:, t - WINDOW:].reshape(bp, WINDOW, SW_KV_HEADS, SW_HD)
    vp_out = proj3[:, t - WINDOW:, C_SV:C_SV + SW_KVW].reshape(bp, WINDOW, SW_KV_HEADS, SW_HD)
    return (
        y_p.reshape(bp, t, d),
        y_s.reshape(bs, 1, d),
        conv_p[None],
        conv_s[None],
        delta_p[None],
        delta_s[None],
        kp_out[None],
        k_s.reshape(bs, w_buf, SW_KV_HEADS, SW_HD)[None],
        vp_out[None],
        v_s.reshape(bs, w_buf, SW_KV_HEADS, SW_HD)[None],
    )
```

```python
import functools

import jax
import jax.numpy as jnp
import numpy as np
from jax import lax
from jax.experimental import pallas as pl
from jax.experimental.pallas import tpu as pltpu

F32 = jnp.float32
BF16 = jnp.bfloat16

D_MODEL = 2048
PAST_LEN = 16384
DN_HEADS = 8
DN_DK = 128
DN_DV = 128
DN_CONV = 4
SW_HEADS = 16
SW_KV_HEADS = 2
SW_HD = 64
SW_GROUP = SW_HEADS // SW_KV_HEADS
WINDOW = 128
N_EXPERTS = 32
TOP_K = 4
SWIGLU_ALPHA = 1.702
SWIGLU_LIMIT = 7.0
EPS = 1e-6

DN_QK = DN_HEADS * DN_DK
DN_VW = DN_HEADS * DN_DV
DN_CONV_CH = 2 * DN_QK + DN_VW
SW_QW = SW_HEADS * SW_HD
SW_KVW = SW_KV_HEADS * SW_HD

LANE = 128
C_QKV = 0
C_Z = DN_CONV_CH
C_SQ = C_Z + DN_VW
C_GA = C_SQ + SW_QW
C_GB = C_GA + D_MODEL
C_SK = C_GB + D_MODEL
C_SV = C_SK + SW_KVW
C_BA = C_SV + SW_KVW
PROJ_W = 10240

GDN_GROUP = 256
GDN_CHUNK = 256
GDN_LEVELS = 8
MOE_ROWS = 128
MOE_SB_BLOCKS = 10
MOE_PASS_BLOCKS = 4
MOE_TF = 512
VMEM_LIMIT = 56 * 1024 * 1024


def _cp(sem, vmem=VMEM_LIMIT):
    return pltpu.CompilerParams(dimension_semantics=sem, vmem_limit_bytes=vmem)


def _dot(a, b):
    return jnp.dot(a.astype(BF16), b.astype(BF16), preferred_element_type=F32)


def _dot_nt(a, b):
    return lax.dot_general(a.astype(BF16), b.astype(BF16), (((1,), (1,)), ((), ())), preferred_element_type=F32)


def _split(a):
    hi = a.astype(BF16)
    lo = (a - hi.astype(F32)).astype(BF16)
    return hi, lo


def _dot3(a, b):
    ah, al = _split(a)
    bh, bl = _split(b)
    d = functools.partial(jnp.dot, preferred_element_type=F32)
    return d(ah, bh) + (d(ah, bl) + d(al, bh))


def _dot3_nt(a, b):
    ah, al = _split(a)
    bh, bl = _split(b)
    d = functools.partial(lax.dot_general, dimension_numbers=(((1,), (1,)), ((), ())), preferred_element_type=F32)
    return d(ah, bh) + (d(ah, bl) + d(al, bh))


def _dot_exact_lhs01(m01, b):
    b1 = b.astype(BF16)
    r = b - b1.astype(F32)
    b2 = r.astype(BF16)
    b3 = (r - b2.astype(F32)).astype(BF16)
    d = functools.partial(jnp.dot, preferred_element_type=F32)
    m = m01.astype(BF16)
    return d(m, b1) + (d(m, b2) + d(m, b3))


def _sigmoid(x):
    return 1.0 / (1.0 + jnp.exp(-x))


def _silu(x):
    return x * _sigmoid(x)


def _softplus(x):
    return jnp.maximum(x, 0.0) + jnp.log(1.0 + jnp.exp(-jnp.abs(x)))


def _ada_kernel(c_ref, w_ref, b_ref, o_ref):
    o_ref[...] = _dot(_silu(c_ref[...]), w_ref[...]) + b_ref[...]


def _ada_mod(c_all, w_ada, b_ada):
    m = c_all.shape[0]
    n = w_ada.shape[1]
    tn = 1024
    return pl.pallas_call(
        _ada_kernel,
        grid=(n // tn,),
        in_specs=[
            pl.BlockSpec((m, D_MODEL), lambda j: (0, 0)),
            pl.BlockSpec((D_MODEL, tn), lambda j: (0, j)),
            pl.BlockSpec((1, tn), lambda j: (0, j)),
        ],
        out_specs=pl.BlockSpec((m, tn), lambda j: (0, j)),
        out_shape=jax.ShapeDtypeStruct((m, n), F32),
        compiler_params=_cp(("arbitrary",)),
        name="ada_mod",
    )(c_all, w_ada, b_ada.reshape(1, n))


def _norm_mod(x, lnw, sc, sh):
    y = x * lax.rsqrt(jnp.mean(x * x, axis=-1, keepdims=True) + EPS)
    return (y * lnw) * (1.0 + sc) + sh


def _inproj_kernel(x_ref, lnw_ref, sc_ref, sh_ref, w_ref, o_ref, h_scr):
    @pl.when(pl.program_id(1) == 0)
    def _():
        h_scr[...] = _norm_mod(x_ref[...], lnw_ref[...], sc_ref[...], sh_ref[...]).astype(BF16)

    o_ref[...] = jnp.dot(h_scr[...], w_ref[...], preferred_element_type=F32)


def _in_proj(x2d, lnw, sc, sh, w_bf16, rows_per_mod, tm):
    m = x2d.shape[0]
    tn = 1024
    if rows_per_mod == 1:
        mod_spec = pl.BlockSpec((tm, D_MODEL), lambda i, j: (i, 0))
        sc, sh = sc.reshape(m, D_MODEL), sh.reshape(m, D_MODEL)
    else:
        assert rows_per_mod % tm == 0
        mod_spec = pl.BlockSpec((None, 1, D_MODEL), lambda i, j: (i // (rows_per_mod // tm), 0, 0))
    return pl.pallas_call(
        _inproj_kernel,
        grid=(m // tm, PROJ_W // tn),
        in_specs=[
            pl.BlockSpec((tm, D_MODEL), lambda i, j: (i, 0)),
            pl.BlockSpec((1, D_MODEL), lambda i, j: (0, 0)),
            mod_spec,
            mod_spec,
            pl.BlockSpec((D_MODEL, tn), lambda i, j: (0, j)),
        ],
        out_specs=pl.BlockSpec((tm, tn), lambda i, j: (i, j)),
        out_shape=jax.ShapeDtypeStruct((m, PROJ_W), F32),
        scratch_shapes=[pltpu.VMEM((tm, D_MODEL), BF16)],
        compiler_params=_cp(("arbitrary", "arbitrary")),
        name="in_proj",
    )(x2d, lnw.reshape(1, D_MODEL), sc, sh, w_bf16)


def _tri_masks(n, chunk):
    r = lax.broadcasted_iota(jnp.int32, (n, n), 0)
    c = lax.broadcasted_iota(jnp.int32, (n, n), 1)
    same = (r // chunk) == (c // chunk)
    return same, same & (r >= c), same & (r > c)


def _gates_kernel(ba_ref, alog_ref, dtb_ref, beta_ref, gc_ref, eg_ref, ek_ref, el_ref, gcrow_ref):
    same, causal, _ = _tri_masks(GDN_GROUP, GDN_CHUNK)
    lower01 = jnp.where(causal, 1.0, 0.0)
    ones01 = jnp.where(same, 1.0, 0.0)
    nega = -jnp.exp(alog_ref[...])
    dtb = dtb_ref[...]
    t = ba_ref.shape[0]

    def body(i, carry):
        r0 = pl.multiple_of(i * GDN_GROUP, GDN_GROUP)
        x = ba_ref[pl.ds(r0, GDN_GROUP), :]
        g = nega * _softplus(x + dtb)
        gc = _dot_exact_lhs01(lower01, g)
        gl = _dot_exact_lhs01(ones01, g)
        beta_ref[pl.ds(r0, GDN_GROUP), :] = _sigmoid(x)
        gc_ref[pl.ds(r0, GDN_GROUP), :] = gc
        eg_ref[pl.ds(r0, GDN_GROUP), :] = jnp.exp(gc)
        ek_ref[pl.ds(r0, GDN_GROUP), :] = jnp.exp(gl - gc)
        el_ref[pl.ds(r0, GDN_GROUP), :] = jnp.exp(gl)
        gct = gc.T
        for h in range(DN_HEADS):
            gcrow_ref[h, :, pl.ds(r0, GDN_GROUP)] = gct[DN_HEADS + h:DN_HEADS + h + 1, :]
        return carry

    lax.fori_loop(0, t // GDN_GROUP, body, 0)


def _gdn_gates(proj3, alog_lane, dtb_lane):
    b, t, _ = proj3.shape
    col = pl.BlockSpec((None, t, LANE), lambda i: (i, 0, 0))
    shp = jax.ShapeDtypeStruct((b, t, LANE), F32)
    return pl.pallas_call(
        _gates_kernel,
        grid=(b,),
        in_specs=[
            pl.BlockSpec((None, t, LANE), lambda i: (i, 0, C_BA // LANE)),
            pl.BlockSpec((1, LANE), lambda i: (0, 0)),
            pl.BlockSpec((1, LANE), lambda i: (0, 0)),
        ],
        out_specs=[col, col, col, col, col, pl.BlockSpec((None, DN_HEADS, 1, t), lambda i: (i, 0, 0, 0))],
        out_shape=[shp, shp, shp, shp, shp, jax.ShapeDtypeStruct((b, DN_HEADS, 1, t), F32)],
        compiler_params=_cp(("arbitrary",)),
        name="gdn_gates",
    )(proj3, alog_lane, dtb_lane)


def _l2norm(x):
    return x * lax.rsqrt(jnp.sum(x * x, axis=-1, keepdims=True) + EPS)


GDN_HPS = 4
GDN_TILE = 1024


def _gdn_kernel(q_ref, k_ref, v_ref, z_ref, beta_ref, gc_ref, eg_ref, ek_ref, el_ref, gcrow_ref,
                cwq_ref, cwk_ref, cwv_ref, nw_ref, o_ref, s_ref,
                pad_scr, hist_scr, qn_scr, kn_scr, vn_scr, oacc_scr, s_scr):
    t = q_ref.shape[0]
    wdt = GDN_HPS * LANE
    h0 = pl.program_id(1) * GDN_HPS
    pad = 8

    @pl.when(pl.program_id(2) == 0)
    def _():
        hist_scr[...] = jnp.zeros(hist_scr.shape, F32)
        s_scr[...] = jnp.zeros(s_scr.shape, F32)

    def conv_silu(stream, u_ref, cw_ref):
        pad_scr[0:pad, :] = hist_scr[stream]
        pad_scr[pad:pad + t, :] = u_ref[...]
        hist_scr[stream] = pad_scr[t:t + pad, :]
        y = cw_ref[DN_CONV - 1:DN_CONV, :] * pad_scr[pad:pad + t, :]
        for i in range(DN_CONV - 1):
            off = pad - (DN_CONV - 1) + i
            y = y + cw_ref[i:i + 1, :] * pad_scr[off:off + t, :]
        return _silu(y)

    yq = conv_silu(0, q_ref, cwq_ref)
    for hh in range(GDN_HPS):
        sl = slice(hh * LANE, (hh + 1) * LANE)
        qn_scr[:, sl] = _l2norm(yq[:, sl]) * (DN_DK ** -0.5)
    yk = conv_silu(1, k_ref, cwk_ref)
    for hh in range(GDN_HPS):
        sl = slice(hh * LANE, (hh + 1) * LANE)
        kn_scr[:, sl] = _l2norm(yk[:, sl])
    vn_scr[...] = conv_silu(2, v_ref, cwv_ref)

    n = GDN_GROUP
    c = GDN_CHUNK
    _, causal, strict = _tri_masks(n, c)
    rr = lax.broadcasted_iota(jnp.int32, (n, n), 0)
    cc = lax.broadcasted_iota(jnp.int32, (n, n), 1)
    eye = jnp.where(rr == cc, 1.0, 0.0)
    lane = lax.broadcasted_iota(jnp.int32, (n, LANE), 1)

    def pick(ref, r0, sel):
        return jnp.sum(jnp.where(sel, ref[pl.ds(r0, n), :], 0.0), axis=-1, keepdims=True)

    def head_group(hh, r0):
        sl = slice(hh * LANE, (hh + 1) * LANE)
        sel_b = lane == h0 + hh
        sel_g = lane == h0 + hh + DN_HEADS
        q = qn_scr[pl.ds(r0, n), sl]
        k = kn_scr[pl.ds(r0, n), sl]
        v = vn_scr[pl.ds(r0, n), sl]
        beta = pick(beta_ref, r0, sel_b)
        gc = pick(gc_ref, r0, sel_g)
        eg = pick(eg_ref, r0, sel_g)
        ek = pick(ek_ref, r0, sel_g)
        el = pick(el_ref, r0, sel_g)
        gcrow = gcrow_ref[hh, :, pl.ds(r0, n)]
        decay = jnp.where(causal, jnp.exp(gc - gcrow), 0.0)
        a_low = jnp.where(strict, beta * _dot_nt(k, k) * decay, 0.0)
        pw = [-a_low]
        for _ in range(GDN_LEVELS - 1):
            pw.append(_dot(pw[-1], pw[-1]))
        fs = [eye + pw[i] + pw[i + 1] + _dot(pw[i], pw[i + 1]) for i in range(0, GDN_LEVELS, 2)]
        while len(fs) > 1:
            fs = [_dot(fs[i], fs[i + 1]) for i in range(0, len(fs), 2)]
        rhs = jnp.concatenate([v * beta, k * (beta * eg)], axis=1)
        sol = _dot(fs[0], rhs)
        value = sol[:, :DN_DV]
        kcum = sol[:, DN_DV:]
        intra = _dot_nt(q, k) * decay
        q_dec = q * eg
        k_dec = k * ek
        for j in range(n // c):
            lo, hi = j * c, (j + 1) * c
            s = s_scr[hh]
            r = _dot(jnp.concatenate([kcum[lo:hi], q_dec[lo:hi]], axis=0), s)
            v_new = value[lo:hi] - r[:c]
            parts = []
            if lo:
                parts.append(jnp.zeros((lo, DN_DV), F32))
            parts.append(v_new)
            if hi < n:
                parts.append(jnp.zeros((n - hi, DN_DV), F32))
            o = r[c:] + _dot(intra[lo:hi], jnp.concatenate(parts, axis=0))
            oacc_scr[pl.ds(r0 + lo, c), sl] = o
            s_scr[hh] = s * el[lo:lo + 1] + _dot(k_dec[lo:hi].T, v_new)

    def body(i, carry):
        r0 = pl.multiple_of(i * n, n)
        for hh in range(GDN_HPS):
            head_group(hh, r0)
        return carry

    lax.fori_loop(0, t // n, body, 0)
    for hh in range(GDN_HPS):
        sl = slice(hh * LANE, (hh + 1) * LANE)
        o = oacc_scr[:, sl]
        y = o * lax.rsqrt(jnp.mean(o * o, axis=-1, keepdims=True) + EPS)
        o_ref[:, sl] = (y * nw_ref[...]) * _silu(z_ref[:, sl])
    s_ref[...] = s_scr[...]


def _gdn_prompt(proj3, gates, conv_w, norm_w):
    b, t, _ = proj3.shape
    beta, gc, eg, ek, el, gcrow = gates
    hps = GDN_HPS
    wdt = hps * LANE
    steps = DN_HEADS // hps
    tt = GDN_TILE
    assert t % tt == 0 and tt % GDN_GROUP == 0

    def colspec(base):
        return pl.BlockSpec((None, tt, wdt), lambda i, j, r, base=base: (i, r, base + j))

    gate = pl.BlockSpec((None, tt, LANE), lambda i, j, r: (i, r, 0))

    def cwspec(base):
        return pl.BlockSpec((DN_CONV, wdt), lambda i, j, r, base=base: (0, base + j))

    return pl.pallas_call(
        _gdn_kernel,
        grid=(b, steps, t // tt),
        in_specs=[
            colspec(0), colspec(steps), colspec(2 * steps), colspec(C_Z // wdt),
            gate, gate, gate, gate, gate,
            pl.BlockSpec((None, hps, 1, tt), lambda i, j, r: (i, j, 0, r)),
            cwspec(0), cwspec(steps), cwspec(2 * steps),
            pl.BlockSpec((1, DN_DV), lambda i, j, r: (0, 0)),
        ],
        out_specs=[
            pl.BlockSpec((None, tt, wdt), lambda i, j, r: (i, r, j)),
            pl.BlockSpec((None, hps, DN_DK, DN_DV), lambda i, j, r: (i, j, 0, 0)),
        ],
        out_shape=[
            jax.ShapeDtypeStruct((b, t, DN_VW), F32),
            jax.ShapeDtypeStruct((b, DN_HEADS, DN_DK, DN_DV), F32),
        ],
        scratch_shapes=[
            pltpu.VMEM((tt + 8, wdt), F32),
            pltpu.VMEM((3, 8, wdt), F32),
            pltpu.VMEM((tt, wdt), F32),
            pltpu.VMEM((tt, wdt), F32),
            pltpu.VMEM((tt, wdt), F32),
            pltpu.VMEM((tt, wdt), F32),
            pltpu.VMEM((hps, DN_DK, DN_DV), F32),
        ],
        compiler_params=_cp(("arbitrary", "arbitrary", "arbitrary")),
        name="gdn_prompt",
    )(proj3, proj3, proj3, proj3, beta, gc, eg, ek, el, gcrow, conv_w, conv_w, conv_w, norm_w.reshape(1, DN_DV))


def _gdn_step_kernel(p_ref, cprev_ref, s_ref, cw_ref, alog_ref, dtb_ref, nw_ref, o_ref, cnew_ref, snew_ref):
    u = p_ref[:, C_QKV:C_QKV + DN_CONV_CH]
    prev = cprev_ref[...]
    y = cw_ref[DN_CONV - 1:DN_CONV, :] * u
    for i in range(DN_CONV - 1):
        y = y + cw_ref[i:i + 1, :] * prev[i:i + 1, :]
    y = _silu(y)
    cnew_ref[0:DN_CONV - 2, :] = prev[1:DN_CONV - 1, :]
    cnew_ref[DN_CONV - 2:DN_CONV - 1, :] = u
    ba = p_ref[:, C_BA:C_BA + LANE]
    beta_l = _sigmoid(ba)
    a_l = jnp.exp(-jnp.exp(alog_ref[...]) * _softplus(ba + dtb_ref[...]))
    lane = lax.broadcasted_iota(jnp.int32, (1, LANE), 1)
    row8 = lax.broadcasted_iota(jnp.int32, (8, LANE), 0)
    for h in range(DN_HEADS):
        q = _l2norm(y[:, h * DN_DK:(h + 1) * DN_DK]) * (DN_DK ** -0.5)
        k = _l2norm(y[:, DN_QK + h * DN_DK:DN_QK + (h + 1) * DN_DK])
        v = y[:, 2 * DN_QK + h * DN_DV:2 * DN_QK + (h + 1) * DN_DV]
        beta = jnp.sum(jnp.where(lane == h, beta_l, 0.0), axis=-1, keepdims=True)
        a = jnp.sum(jnp.where(lane == h + DN_HEADS, a_l, 0.0), axis=-1, keepdims=True)
        s = s_ref[h]
        kq = jnp.where(row8 == 0, k, jnp.where(row8 == 1, q, 0.0)).T
        kcol, qcol = kq[:, 0:1], kq[:, 1:2]
        v_new = beta * (v - a * jnp.sum(s * kcol, axis=0, keepdims=True))
        o = a * jnp.sum(s * qcol, axis=0, keepdims=True) + jnp.sum(q * k, axis=-1, keepdims=True) * v_new
        snew_ref[h] = s * a + kcol * v_new
        yo = o * lax.rsqrt(jnp.mean(o * o, axis=-1, keepdims=True) + EPS)
        z = p_ref[:, C_Z + h * DN_DV:C_Z + (h + 1) * DN_DV]
        o_ref[:, h * DN_DV:(h + 1) * DN_DV] = (yo * nw_ref[...]) * _silu(z)


def _gdn_step(proj_s, conv_prev, s0, conv_w, alog_lane, dtb_lane, norm_w):
    b = proj_s.shape[0]
    return pl.pallas_call(
        _gdn_step_kernel,
        grid=(b,),
        in_specs=[
            pl.BlockSpec((None, 1, PROJ_W), lambda i: (i, 0, 0)),
            pl.BlockSpec((None, DN_CONV - 1, DN_CONV_CH), lambda i: (i, 0, 0)),
            pl.BlockSpec((None, DN_HEADS, DN_DK, DN_DV), lambda i: (i, 0, 0, 0)),
            pl.BlockSpec((DN_CONV, DN_CONV_CH), lambda i: (0, 0)),
            pl.BlockSpec((1, LANE), lambda i: (0, 0)),
            pl.BlockSpec((1, LANE), lambda i: (0, 0)),
            pl.BlockSpec((1, DN_DV), lambda i: (0, 0)),
        ],
        out_specs=[
            pl.BlockSpec((None, 1, DN_VW), lambda i: (i, 0, 0)),
            pl.BlockSpec((None, DN_CONV - 1, DN_CONV_CH), lambda i: (i, 0, 0)),
            pl.BlockSpec((None, DN_HEADS, DN_DK, DN_DV), lambda i: (i, 0, 0, 0)),
        ],
        out_shape=[
            jax.ShapeDtypeStruct((b, 1, DN_VW), F32),
            jax.ShapeDtypeStruct((b, DN_CONV - 1, DN_CONV_CH), F32),
            jax.ShapeDtypeStruct((b, DN_HEADS, DN_DK, DN_DV), F32),
        ],
        compiler_params=_cp(("arbitrary",)),
        name="gdn_step",
    )(proj_s.reshape(b, 1, PROJ_W), conv_prev, s0, conv_w, alog_lane, dtb_lane, norm_w.reshape(1, DN_DV))


def _alibi_slope(h):
    return float(2.0 ** (-8.0 * (h + 1) / SW_HEADS))


def _head_rms(x, w):
    return (x * lax.rsqrt(jnp.mean(x * x, axis=-1, keepdims=True) + EPS)) * w


SWA_HB = 4


def _swa_kernel(sinks_ref, q_ref, kc_ref, kp_ref, vc_ref, vp_ref, qw_ref, kw_ref, o_ref, kn_ref):
    blk = pl.program_id(1)
    w = WINDOW
    rows = SWA_HB * w
    qi = lax.broadcasted_iota(jnp.int32, (rows, 2 * w), 0)
    kj = lax.broadcasted_iota(jnp.int32, (rows, 2 * w), 1)
    dist = (qi & (w - 1)) + w - kj
    valid = (dist >= 0) & (dist < w) & ((kj >= w) | (blk > 0))
    distf = dist.astype(F32)
    stripe = lax.broadcasted_iota(jnp.int32, (rows, 1), 0) // w
    kc = kc_ref[...]
    kp = kp_ref[...]
    kbands, vbands = [], []
    for g in range(SW_KV_HEADS):
        sl = slice(g * SW_HD, (g + 1) * SW_HD)
        kcn = _head_rms(kc[:, sl], kw_ref[...])
        kn_ref[:, sl] = kcn
        kbands.append(jnp.concatenate([_head_rms(kp[:, sl], kw_ref[...]), kcn], axis=0))
        vbands.append(jnp.concatenate([vp_ref[:, sl], vc_ref[:, sl]], axis=0))
    for hb in range(SW_HEADS // SWA_HB):
        heads = range(hb * SWA_HB, (hb + 1) * SWA_HB)
        g = heads[0] // SW_GROUP
        qs = jnp.concatenate([_head_rms(q_ref[:, h * SW_HD:(h + 1) * SW_HD], qw_ref[...]) for h in heads], axis=0)
        slope = jnp.zeros((rows, 1), F32)
        sink = jnp.zeros((rows, 1), F32)
        for i, h in enumerate(heads):
            slope = jnp.where(stripe == i, _alibi_slope(h), slope)
            sink = jnp.where(stripe == i, sinks_ref[h], sink)
        s = _dot_nt(qs, kbands[g]) * (SW_HD ** -0.5) - slope * distf
        s = jnp.where(valid, s, -jnp.inf)
        m = jnp.maximum(jnp.max(s, axis=-1, keepdims=True), sink)
        p = jnp.exp(s - m)
        den = jnp.sum(p, axis=-1, keepdims=True) + jnp.exp(sink - m)
        o = _dot(p / den, vbands[g])
        for i, h in enumerate(heads):
            o_ref[:, h * SW_HD:(h + 1) * SW_HD] = o[i * w:(i + 1) * w]


def _swa_prompt(proj3, sinks, qw, kw):
    b, t, _ = proj3.shape
    nb = t // WINDOW
    kcol, vcol = C_SK // LANE, C_SV // LANE

    def cur(col):
        return pl.BlockSpec((None, WINDOW, SW_KVW), lambda i, j, s, col=col: (i, j, col))

    def prev(col):
        return pl.BlockSpec((None, WINDOW, SW_KVW), lambda i, j, s, col=col: (i, jnp.maximum(j - 1, 0), col))

    return pl.pallas_call(
        _swa_kernel,
        grid_spec=pltpu.PrefetchScalarGridSpec(
            num_scalar_prefetch=1,
            grid=(b, nb),
            in_specs=[
                pl.BlockSpec((None, WINDOW, SW_QW), lambda i, j, s: (i, j, C_SQ // SW_QW)),
                cur(kcol), prev(kcol), cur(vcol), prev(vcol),
                pl.BlockSpec((1, SW_HD), lambda i, j, s: (0, 0)),
                pl.BlockSpec((1, SW_HD), lambda i, j, s: (0, 0)),
            ],
            out_specs=[
                pl.BlockSpec((None, WINDOW, SW_QW), lambda i, j, s: (i, j, 0)),
                pl.BlockSpec((None, WINDOW, SW_KVW), lambda i, j, s: (i, j, 0)),
            ],
        ),
        out_shape=[
            jax.ShapeDtypeStruct((b, t, SW_QW), F32),
            jax.ShapeDtypeStruct((b, t, SW_KVW), F32),
        ],
        compiler_params=_cp(("arbitrary", "arbitrary")),
        name="swa_prompt",
    )(sinks, proj3, proj3, proj3, proj3, proj3, qw.reshape(1, SW_HD), kw.reshape(1, SW_HD))


def _swa_step_kernel(sinks_ref, p_ref, kbuf_ref, vbuf_ref, qw_ref, kw_ref, o_ref, knew_ref, vnew_ref, kcat, vcat):
    w = kbuf_ref.shape[0]
    rows = kcat.shape[0]
    knew = p_ref[:, C_SK:C_SK + SW_KVW]
    vnew = p_ref[:, C_SV:C_SV + SW_KVW]
    kcat[...] = jnp.zeros(kcat.shape, F32)
    vcat[...] = jnp.zeros(vcat.shape, F32)
    kcat[0:w, :] = kbuf_ref[...]
    vcat[0:w, :] = vbuf_ref[...]
    for g in range(SW_KV_HEADS):
        sl = slice(g * SW_HD, (g + 1) * SW_HD)
        kcat[w:w + 1, sl] = _head_rms(knew[:, sl], kw_ref[...])
    vcat[w:w + 1, :] = vnew
    knew_ref[...] = kcat[1:w + 1, :]
    vnew_ref[...] = vcat[1:w + 1, :]
    j = lax.broadcasted_iota(jnp.int32, (SW_GROUP, rows), 1)
    dist = w - j
    valid = (dist >= 0) & (dist < WINDOW)
    distf = dist.astype(F32)
    hrow = lax.broadcasted_iota(jnp.int32, (SW_GROUP, 1), 0)
    for g in range(SW_KV_HEADS):
        sl = slice(g * SW_HD, (g + 1) * SW_HD)
        qs = jnp.zeros((SW_GROUP, SW_HD), F32)
        slope = jnp.zeros((SW_GROUP, 1), F32)
        sink = jnp.zeros((SW_GROUP, 1), F32)
        for i in range(SW_GROUP):
            h = g * SW_GROUP + i
            qh = _head_rms(p_ref[:, C_SQ + h * SW_HD:C_SQ + (h + 1) * SW_HD], qw_ref[...])
            qs = jnp.where(hrow == i, qh, qs)
            slope = jnp.where(hrow == i, _alibi_slope(h), slope)
            sink = jnp.where(hrow == i, sinks_ref[h], sink)
        s = _dot_nt(qs, kcat[:, sl]) * (SW_HD ** -0.5) - slope * distf
        s = jnp.where(valid, s, -jnp.inf)
        m = jnp.maximum(jnp.max(s, axis=-1, keepdims=True), sink)
        p = jnp.exp(s - m)
        den = jnp.sum(p, axis=-1, keepdims=True) + jnp.exp(sink - m)
        o = _dot(p / den, vcat[:, sl])
        for i in range(SW_GROUP):
            h = g * SW_GROUP + i
            o_ref[:, h * SW_HD:(h + 1) * SW_HD] = o[i:i + 1]


def _swa_step(proj_s, kbuf, vbuf, sinks, qw, kw):
    b = proj_s.shape[0]
    w = kbuf.shape[1]
    rows = 2 * w
    buf = pl.BlockSpec((None, w, SW_KVW), lambda i, s: (i, 0, 0))
    return pl.pallas_call(
        _swa_step_kernel,
        grid_spec=pltpu.PrefetchScalarGridSpec(
            num_scalar_prefetch=1,
            grid=(b,),
            in_specs=[
                pl.BlockSpec((None, 1, PROJ_W), lambda i, s: (i, 0, 0)),
                buf, buf,
                pl.BlockSpec((1, SW_HD), lambda i, s: (0, 0)),
                pl.BlockSpec((1, SW_HD), lambda i, s: (0, 0)),
            ],
            out_specs=[pl.BlockSpec((None, 1, SW_QW), lambda i, s: (i, 0, 0)), buf, buf],
            scratch_shapes=[pltpu.VMEM((rows, SW_KVW), F32), pltpu.VMEM((rows, SW_KVW), F32)],
        ),
        out_shape=[
            jax.ShapeDtypeStruct((b, 1, SW_QW), F32),
            jax.ShapeDtypeStruct((b, w, SW_KVW), F32),
            jax.ShapeDtypeStruct((b, w, SW_KVW), F32),
        ],
        compiler_params=_cp(("arbitrary",)),
        name="swa_step",
    )(sinks, proj_s.reshape(b, 1, PROJ_W), kbuf, vbuf, qw.reshape(1, SW_HD), kw.reshape(1, SW_HD))


def _merge_kernel(ya_ref, yb_ref, wa_ref, wb_ref, ga_ref, gb_ref, o_ref):
    a = _dot(ya_ref[...], wa_ref[...])
    b = _dot(yb_ref[...], wb_ref[...])
    o_ref[...] = (_sigmoid(ga_ref[...]) * a + _sigmoid(gb_ref[...]) * b).astype(BF16)


def _merge(ya, yb, wa, wb, proj, tm):
    m = ya.shape[0]
    tn = 1024
    return pl.pallas_call(
        _merge_kernel,
        grid=(m // tm, D_MODEL // tn),
        in_specs=[
            pl.BlockSpec((tm, DN_VW), lambda i, j: (i, 0)),
            pl.BlockSpec((tm, SW_QW), lambda i, j: (i, 0)),
            pl.BlockSpec((DN_VW, tn), lambda i, j: (0, j)),
            pl.BlockSpec((SW_QW, tn), lambda i, j: (0, j)),
            pl.BlockSpec((tm, tn), lambda i, j: (i, C_GA // tn + j)),
            pl.BlockSpec((tm, tn), lambda i, j: (i, C_GB // tn + j)),
        ],
        out_specs=pl.BlockSpec((tm, tn), lambda i, j: (i, j)),
        out_shape=jax.ShapeDtypeStruct((m, D_MODEL), BF16),
        compiler_params=_cp(("arbitrary", "arbitrary")),
        name="merge",
    )(ya, yb, wa, wb, proj, proj)


def _outproj_kernel(mg_ref, w_ref, x_ref, gt_ref, o_ref):
    o_ref[...] = x_ref[...] + gt_ref[...] * jnp.dot(mg_ref[...], w_ref[...], preferred_element_type=F32)


def _mod_spec(rows_per_mod, tm, tn):
    if rows_per_mod == 1:
        return pl.BlockSpec((tm, tn), lambda i, j: (i, j))
    return pl.BlockSpec((None, 1, tn), lambda i, j: (i // (rows_per_mod // tm), 0, j))


def _out_proj(merged, w_bf16, x2d, gt, rows_per_mod, tm):
    m = x2d.shape[0]
    tn = 1024
    if rows_per_mod == 1:
        gt = gt.reshape(m, D_MODEL)
    return pl.pallas_call(
        _outproj_kernel,
        grid=(m // tm, D_MODEL // tn),
        in_specs=[
            pl.BlockSpec((tm, D_MODEL), lambda i, j: (i, 0)),
            pl.BlockSpec((D_MODEL, tn), lambda i, j: (0, j)),
            pl.BlockSpec((tm, tn), lambda i, j: (i, j)),
            _mod_spec(rows_per_mod, tm, tn),
        ],
        out_specs=pl.BlockSpec((tm, tn), lambda i, j: (i, j)),
        out_shape=jax.ShapeDtypeStruct((m, D_MODEL), F32),
        compiler_params=_cp(("arbitrary", "arbitrary")),
        name="out_proj",
    )(merged, w_bf16, x2d, gt)


def _router_kernel(x_ref, lnw_ref, sc_ref, sh_ref, rw_ref, rb_ref, h_ref, idx_ref, w_ref):
    hmod = _norm_mod(x_ref[...], lnw_ref[...], sc_ref[...], sh_ref[...])
    h_ref[...] = hmod
    logits = _dot3(hmod, rw_ref[...]) + rb_ref[...]
    lane = lax.broadcasted_iota(jnp.int32, logits.shape, 1)
    cur = jnp.where(lane < N_EXPERTS, logits, -jnp.inf)
    vals, idxs = [], []
    for _ in range(TOP_K):
        m = jnp.max(cur, axis=-1, keepdims=True)
        ix = jnp.min(jnp.where(cur == m, lane, LANE), axis=-1, keepdims=True)
        vals.append(m)
        idxs.append(ix)
        cur = jnp.where(lane == ix, -jnp.inf, cur)
    es = [jnp.exp(v - vals[0]) for v in vals]
    den = es[0] + es[1] + es[2] + es[3]
    idx_out = jnp.zeros(logits.shape, jnp.int32)
    w_out = jnp.zeros(logits.shape, F32)
    for k in range(TOP_K):
        idx_out = jnp.where(lane == k, idxs[k], idx_out)
        w_out = jnp.where(lane == k, es[k] / den, w_out)
    idx_ref[...] = idx_out
    w_ref[...] = w_out


def _router(x2d, lnw, sc, sh, rw_pad, rb_pad, rows_per_mod, tm):
    m = x2d.shape[0]
    if rows_per_mod == 1:
        mod_spec = pl.BlockSpec((tm, D_MODEL), lambda i: (i, 0))
        sc, sh = sc.reshape(m, D_MODEL), sh.reshape(m, D_MODEL)
    else:
        mod_spec = pl.BlockSpec((None, 1, D_MODEL), lambda i: (i // (rows_per_mod // tm), 0, 0))
    row = pl.BlockSpec((tm, D_MODEL), lambda i: (i, 0))
    small = pl.BlockSpec((tm, LANE), lambda i: (i, 0))
    return pl.pallas_call(
        _router_kernel,
        grid=(m // tm,),
        in_specs=[
            row,
            pl.BlockSpec((1, D_MODEL), lambda i: (0, 0)),
            mod_spec, mod_spec,
            pl.BlockSpec((D_MODEL, LANE), lambda i: (0, 0)),
            pl.BlockSpec((1, LANE), lambda i: (0, 0)),
        ],
        out_specs=[row, small, small],
        out_shape=[
            jax.ShapeDtypeStruct((m, D_MODEL), F32),
            jax.ShapeDtypeStruct((m, LANE), jnp.int32),
            jax.ShapeDtypeStruct((m, LANE), F32),
        ],
        compiler_params=_cp(("arbitrary",)),
        name="router",
    )(x2d, lnw.reshape(1, D_MODEL), sc, sh, rw_pad, rb_pad)


SCATTER_TOK = 256
DMA_GROUP = 8
DMA_ROWS = 16


def _scatter_kernel(zl_ref, dest_ref, hp_ref, hs_ref, xs_hbm, zbuf, sem, zsem):
    i = pl.program_id(0)
    rb = MOE_ROWS
    n_prompt_steps = pl.num_programs(0) - 1

    @pl.when(i == 0)
    def _():
        zbuf[...] = jnp.zeros(zbuf.shape, F32)

        def zero_copy(n):
            return pltpu.make_async_copy(zbuf, xs_hbm.at[pl.ds(zl_ref[n] * rb, rb)], zsem)

        def start(n, c):
            @pl.when(zl_ref[n] >= 0)
            def _():
                zero_copy(n).start()
            return c

        def wait(n, c):
            @pl.when(zl_ref[n] >= 0)
            def _():
                zero_copy(n).wait()
            return c

        lax.fori_loop(0, zl_ref.shape[0], start, 0)
        lax.fori_loop(0, zl_ref.shape[0], wait, 0)

    def scatter(src_ref):
        n_tok = src_ref.shape[0]

        def group(g, c):
            t0 = pl.multiple_of(g * DMA_GROUP, DMA_GROUP)
            for r in range(DMA_GROUP):
                for k in range(TOP_K):
                    dst = dest_ref[0, g * (DMA_GROUP * TOP_K) + (r * TOP_K + k)]
                    pltpu.make_async_copy(src_ref.at[pl.ds(t0 + r, 1)], xs_hbm.at[pl.ds(dst, 1)], sem).start()
            return c

        lax.fori_loop(0, n_tok // DMA_GROUP, group, 0)
        for _ in range(TOP_K):
            pltpu.make_async_copy(src_ref, xs_hbm.at[pl.ds(0, n_tok)], sem).wait()

    @pl.when(i < n_prompt_steps)
    def _():
        scatter(hp_ref)

    @pl.when(i == n_prompt_steps)
    def _():
        scatter(hs_ref)


def _scatter_rows(h_p, h_s, dest, zero_blocks, n_rows):
    assert TOP_K == 4
    n_p, n_s = h_p.shape[0], h_s.shape[0]
    steps_p = n_p // SCATTER_TOK
    per = SCATTER_TOK * TOP_K
    dest_s = jnp.concatenate([dest[n_p * TOP_K:], jnp.zeros((per - n_s * TOP_K,), jnp.int32)])
    dest3 = jnp.concatenate([dest[:n_p * TOP_K], dest_s]).reshape(steps_p + 1, 1, per)
    return pl.pallas_call(
        _scatter_kernel,
        grid_spec=pltpu.PrefetchScalarGridSpec(
            num_scalar_prefetch=1,
            grid=(steps_p + 1,),
            in_specs=[
                pl.BlockSpec((None, 1, per), lambda i, zl: (i, 0, 0), memory_space=pltpu.SMEM),
                pl.BlockSpec((SCATTER_TOK, D_MODEL), lambda i, zl: (jnp.minimum(i, steps_p - 1), 0)),
                pl.BlockSpec((n_s, D_MODEL), lambda i, zl: (0, 0)),
            ],
            out_specs=pl.BlockSpec(memory_space=pl.ANY),
            scratch_shapes=[
                pltpu.VMEM((MOE_ROWS, D_MODEL), F32),
                pltpu.SemaphoreType.DMA(()),
                pltpu.SemaphoreType.DMA(()),
            ],
        ),
        out_shape=jax.ShapeDtypeStruct((n_rows, D_MODEL), F32),
        compiler_params=_cp(("arbitrary",)),
        name="moe_scatter",
    )(zero_blocks, dest3, h_p, h_s)


def _experts_kernel(sbe_ref, sbb_ref, sbn_ref, tail_ref, xs_hbm, wg_ref, wl_ref, wd_ref, bg_ref, bl_ref, bd_ref,
                    ys_hbm, xf_scr, xb_scr, acc_scr, sem_in, sem_out):
    s = pl.program_id(0)
    j = pl.program_id(1)
    last_s = pl.num_programs(0) - 1
    last_j = pl.num_programs(1) - 1
    nblk = sbn_ref[s]
    blk0 = sbb_ref[s]
    rb = MOE_ROWS

    def in_copy(first_blk, b):
        return pltpu.make_async_copy(xs_hbm.at[pl.ds((first_blk + b) * rb, rb)], xf_scr.at[pl.ds(b * rb, rb)], sem_in)

    def out_copy(first_blk, b):
        return pltpu.make_async_copy(acc_scr.at[pl.ds(b * rb, rb)], ys_hbm.at[pl.ds((first_blk + b) * rb, rb)], sem_out)

    def loop(n, fn):
        def body(b, c):
            fn(b)
            return c
        lax.fori_loop(0, n, body, 0)

    @pl.when(j == 0)
    def _():
        @pl.when(s == 0)
        def _():
            loop(nblk, lambda b: in_copy(blk0, b).start())

        loop(nblk, lambda b: in_copy(blk0, b).wait())

        @pl.when(s > 0)
        def _():
            prev0 = sbb_ref[s - 1]
            loop(sbn_ref[s - 1], lambda b: out_copy(prev0, b).wait())

        def cast(b):
            r0 = pl.multiple_of(b * rb, rb)
            xb_scr[pl.ds(r0, rb), :] = xf_scr[pl.ds(r0, rb), :].astype(BF16)
            acc_scr[pl.ds(r0, rb), :] = jnp.broadcast_to(bd_ref[...], (rb, D_MODEL))
        loop(nblk, cast)

        @pl.when(s < last_s)
        def _():
            nxt0 = sbb_ref[s + 1]
            loop(sbn_ref[s + 1], lambda b: in_copy(nxt0, b).start())

    @pl.when(nblk > 0)
    def _():
        def mlp(b0, nb):
            rows = nb * rb
            r0 = pl.multiple_of(b0 * rb, rb)
            x = xb_scr[pl.ds(r0, rows), :]
            glu = jnp.dot(x, wg_ref[...].astype(BF16), preferred_element_type=F32) + bg_ref[...]
            lin = jnp.dot(x, wl_ref[...].astype(BF16), preferred_element_type=F32) + bl_ref[...]
            glu = jnp.minimum(glu, SWIGLU_LIMIT)
            lin = jnp.clip(lin, -SWIGLU_LIMIT, SWIGLU_LIMIT)
            act = glu * _sigmoid(SWIGLU_ALPHA * glu) * (lin + 1.0)
            acc_scr[pl.ds(r0, rows), :] += jnp.dot(act.astype(BF16), wd_ref[...].astype(BF16),
                                                   preferred_element_type=F32)

            @pl.when(j == last_j)
            def _():
                for b in range(nb):
                    out_copy(blk0, b0 + b).start()

        full = nblk // MOE_PASS_BLOCKS
        loop(full, lambda p: mlp(p * MOE_PASS_BLOCKS, MOE_PASS_BLOCKS))
        done = full * MOE_PASS_BLOCKS
        part = MOE_PASS_BLOCKS // 2
        while part >= 1:
            take = ((nblk - done) // part) * part

            @pl.when(take > 0)
            def _(done=done, part=part):
                mlp(done, part)

            done = done + take
            part //= 2

    @pl.when((s == last_s) & (j == last_j))
    def _():
        loop(nblk, lambda b: out_copy(blk0, b).wait())
        acc_scr[0:rb, :] = jnp.zeros((rb, D_MODEL), F32)

        def zero_copy(b):
            return pltpu.make_async_copy(acc_scr.at[pl.ds(0, rb)], ys_hbm.at[pl.ds(b * rb, rb)], sem_out)

        def start(b, c):
            zero_copy(b).start()
            return c

        def wait(b, c):
            zero_copy(b).wait()
            return c

        lax.fori_loop(tail_ref[0], tail_ref[1], start, 0)
        lax.fori_loop(tail_ref[0], tail_ref[1], wait, 0)


def _experts(xs, sb_e, sb_blk0, sb_nblk, tail, w_gate_up, b_gate_up, w_down, b_down):
    n_rows = xs.shape[0]
    n_sb = sb_e.shape[0]
    tf = MOE_TF
    nj = D_MODEL // tf
    rmax = MOE_SB_BLOCKS * MOE_ROWS

    def jj(s, j, n):
        return jnp.where(n[s] > 0, j, nj - 1)

    return pl.pallas_call(
        _experts_kernel,
        grid_spec=pltpu.PrefetchScalarGridSpec(
            num_scalar_prefetch=4,
            grid=(n_sb, nj),
            in_specs=[
                pl.BlockSpec(memory_space=pl.ANY),
                pl.BlockSpec((None, D_MODEL, tf), lambda s, j, e, b, n, tl: (e[s], 0, jj(s, j, n))),
                pl.BlockSpec((None, D_MODEL, tf), lambda s, j, e, b, n, tl: (e[s], 0, nj + jj(s, j, n))),
                pl.BlockSpec((None, tf, D_MODEL), lambda s, j, e, b, n, tl: (e[s], jj(s, j, n), 0)),
                pl.BlockSpec((None, 1, tf), lambda s, j, e, b, n, tl: (e[s], 0, jj(s, j, n))),
                pl.BlockSpec((None, 1, tf), lambda s, j, e, b, n, tl: (e[s], 0, nj + jj(s, j, n))),
                pl.BlockSpec((None, 1, D_MODEL), lambda s, j, e, b, n, tl: (e[s], 0, 0)),
            ],
            out_specs=pl.BlockSpec(memory_space=pl.ANY),
            scratch_shapes=[
                pltpu.VMEM((rmax, D_MODEL), F32),
                pltpu.VMEM((rmax, D_MODEL), BF16),
                pltpu.VMEM((rmax, D_MODEL), F32),
                pltpu.SemaphoreType.DMA(()),
                pltpu.SemaphoreType.DMA(()),
            ],
        ),
        out_shape=jax.ShapeDtypeStruct((n_rows, D_MODEL), F32),
        compiler_params=_cp(("arbitrary", "arbitrary")),
        name="moe_experts",
    )(sb_e, sb_blk0, sb_nblk, tail, xs, w_gate_up, w_gate_up, w_down,
      b_gate_up.reshape(N_EXPERTS, 1, 2 * D_MODEL), b_gate_up.reshape(N_EXPERTS, 1, 2 * D_MODEL),
      b_down.reshape(N_EXPERTS, 1, D_MODEL))


COMBINE_TOK = 128


def _combine_kernel(pos_ref, posn_ref, ys_hbm, x_ref, gt_ref, w_ref, o_ref, buf, sem):
    n = COMBINE_TOK * TOP_K
    i = pl.program_id(0)
    slot = lax.rem(i, 2)

    def issue(p_ref, sl):
        def start(a, c):
            pltpu.make_async_copy(ys_hbm.at[pl.ds(p_ref[0, a], 1)], buf.at[sl, pl.ds(a, 1)], sem.at[sl]).start()
            return c
        lax.fori_loop(0, n, start, 0, unroll=DMA_ROWS)

    @pl.when(i == 0)
    def _():
        issue(pos_ref, 0)

    @pl.when(i + 1 < pl.num_programs(0))
    def _():
        issue(posn_ref, 1 - slot)

    pltpu.make_async_copy(ys_hbm.at[pl.ds(0, n)], buf.at[slot], sem.at[slot]).wait()
    w = w_ref[...]
    lane = lax.broadcasted_iota(jnp.int32, w.shape, 1)
    y = jnp.zeros((COMBINE_TOK, D_MODEL), F32)
    for k in range(TOP_K):
        wk = jnp.sum(jnp.where(lane == k, w, 0.0), axis=-1, keepdims=True)
        y = y + wk * buf[slot, k * COMBINE_TOK:(k + 1) * COMBINE_TOK, :]
    o_ref[...] = x_ref[...] + gt_ref[...] * y


def _combine(ys, pos_kmajor, x2d, gt, top_w, rows_per_mod):
    m = x2d.shape[0]
    tm = COMBINE_TOK
    steps = m // tm
    if rows_per_mod == 1:
        gt = gt.reshape(m, D_MODEL)
        gt_spec = pl.BlockSpec((tm, D_MODEL), lambda i: (i, 0))
    else:
        gt_spec = pl.BlockSpec((None, 1, D_MODEL), lambda i: (i // (rows_per_mod // tm), 0, 0))
    return pl.pallas_call(
        _combine_kernel,
        grid=(steps,),
        in_specs=[
            pl.BlockSpec((None, 1, TOP_K * tm), lambda i: (i, 0, 0), memory_space=pltpu.SMEM),
            pl.BlockSpec((None, 1, TOP_K * tm), lambda i: (jnp.minimum(i + 1, steps - 1), 0, 0), memory_space=pltpu.SMEM),
            pl.BlockSpec(memory_space=pl.ANY),
            pl.BlockSpec((tm, D_MODEL), lambda i: (i, 0)),
            gt_spec,
            pl.BlockSpec((tm, LANE), lambda i: (i, 0)),
        ],
        out_specs=pl.BlockSpec((tm, D_MODEL), lambda i: (i, 0)),
        out_shape=jax.ShapeDtypeStruct((m, D_MODEL), F32),
        scratch_shapes=[pltpu.VMEM((2, TOP_K * tm, D_MODEL), F32), pltpu.SemaphoreType.DMA((2,))],
        compiler_params=_cp(("arbitrary",)),
        name="moe_combine",
    )(pos_kmajor, pos_kmajor, ys, x2d, gt, top_w)


def _routing_tables(top_idx):
    n_tok = top_idx.shape[0]
    n_assign = n_tok * TOP_K
    rb = MOE_ROWS
    n_blocks = -(-(n_assign + N_EXPERTS * (rb - 1)) // rb)
    n_rows = n_blocks * rb
    flat_e = top_idx.reshape(-1)
    onehot = (flat_e[:, None] == jnp.arange(N_EXPERTS, dtype=jnp.int32)[None, :]).astype(jnp.int32)
    csum = jnp.cumsum(onehot, axis=0)
    rank = jnp.sum((csum - onehot) * onehot, axis=1)
    counts = csum[-1]
    nblk_e = (counts + rb - 1) // rb
    blk_start = jnp.cumsum(nblk_e) - nblk_e
    dest = (blk_start * rb)[flat_e] + rank
    total_blk = jnp.sum(nblk_e)
    last_blk = jnp.where(nblk_e > 0, blk_start + nblk_e - 1, -1)
    bidx = jnp.arange(n_blocks, dtype=jnp.int32)
    zero_blocks = jnp.concatenate([last_blk, jnp.where(bidx >= total_blk, bidx, -1)]).astype(jnp.int32)
    n_sb_max = n_blocks // MOE_SB_BLOCKS + N_EXPERTS
    sb_per_e = (nblk_e + MOE_SB_BLOCKS - 1) // MOE_SB_BLOCKS
    sb_start = jnp.cumsum(sb_per_e) - sb_per_e
    total_sb = jnp.sum(sb_per_e)
    sidx = jnp.arange(n_sb_max, dtype=jnp.int32)
    e_of = jnp.clip(jnp.searchsorted(jnp.cumsum(sb_per_e), sidx, side="right"), 0, N_EXPERTS - 1).astype(jnp.int32)
    local = sidx - sb_start[e_of]
    active = sidx < total_sb
    last_e = e_of[jnp.maximum(total_sb - 1, 0)]
    sb_e = jnp.where(active, e_of, last_e).astype(jnp.int32)
    sb_blk0 = jnp.where(active, blk_start[e_of] + local * MOE_SB_BLOCKS, 0).astype(jnp.int32)
    sb_nblk = jnp.where(active, jnp.minimum(nblk_e[e_of] - local * MOE_SB_BLOCKS, MOE_SB_BLOCKS), 0).astype(jnp.int32)
    tail = jnp.stack([total_blk, jnp.int32(n_blocks)]).astype(jnp.int32)
    return dest.astype(jnp.int32), zero_blocks, n_rows, sb_e, sb_blk0, sb_nblk, tail


def _kmajor(pos, tm):
    m = pos.shape[0]
    return pos.reshape(m // tm, tm, TOP_K).transpose(0, 2, 1).reshape(m // tm, 1, TOP_K * tm)


def _repack_w_in(w_in):
    a = DN_CONV_CH + DN_VW
    b = a + 2 * DN_HEADS
    c = b + SW_QW
    e = c + 2 * SW_KVW
    parts = [w_in[:, :a], w_in[:, b:c], w_in[:, e:], w_in[:, c:e], w_in[:, a:b]]
    pad = jnp.zeros((D_MODEL, PROJ_W - w_in.shape[1]), BF16)
    return jnp.concatenate([p.astype(BF16) for p in parts] + [pad], axis=1)


def _lane_vec(v, offset):
    return jnp.zeros((1, LANE), F32).at[0, offset:offset + v.shape[0]].set(v.astype(F32))


def kernel(x_prompt, x_sample, state_conv, state_delta, cache_swa_k, cache_swa_v, c_prompt, c_sample, w_ada, b_ada, ln1_w, w_in, conv_w, dn_a_log, dn_dt_bias, dn_norm_w, sw_q_norm_w, sw_k_norm_w, sw_sinks, w_branch_a, w_branch_b, w_out, ln2_w, router_w, router_b, w_gate_up, b_gate_up, w_down, b_down):
    assert w_ada.shape[0] == 1, "single-layer step"
    bp, t, d = x_prompt.shape
    bs = x_sample.shape[0]
    np_tok = bp * t
    l = 0

    n_c = bp + bs
    c_all = jnp.concatenate([c_prompt, c_sample, jnp.zeros((-n_c % 8, d), F32)], axis=0)
    mod = _ada_mod(c_all, w_ada[l], b_ada[l])
    mods_p = [m.reshape(bp, 1, d) for m in jnp.split(mod[:bp], 6, axis=-1)]
    mods_s = [m.reshape(bs, 1, d) for m in jnp.split(mod[bp:n_c], 6, axis=-1)]

    w_in_r = _repack_w_in(w_in[l])
    wa, wb, wo = w_branch_a[l].astype(BF16), w_branch_b[l].astype(BF16), w_out[l].astype(BF16)
    alog_lane = _lane_vec(dn_a_log[l], DN_HEADS)
    dtb_lane = _lane_vec(dn_dt_bias[l], DN_HEADS)
    rw_pad = jnp.zeros((d, LANE), F32).at[:, :N_EXPERTS].set(router_w[l])
    rb_pad = jnp.zeros((1, LANE), F32).at[0, :N_EXPERTS].set(router_b[l])
    sinks = sw_sinks[l].astype(F32)

    xp = x_prompt.reshape(np_tok, d)
    proj_p = _in_proj(xp, ln1_w[l], mods_p[1], mods_p[0], w_in_r, t, 1024)
    proj3 = proj_p.reshape(bp, t, PROJ_W)
    gates = _gdn_gates(proj3, alog_lane, dtb_lane)
    ya_p, delta_p = _gdn_prompt(proj3, gates, conv_w[l], dn_norm_w[l])
    yb_p, kn_p = _swa_prompt(proj3, sinks, sw_q_norm_w[l], sw_k_norm_w[l])
    merged_p = _merge(ya_p.reshape(np_tok, DN_VW), yb_p.reshape(np_tok, SW_QW), wa, wb, proj_p, 512)
    x1_p = _out_proj(merged_p, wo, xp, mods_p[2], t, 1024)
    h2_p, idx_p, tw_p = _router(x1_p, ln2_w[l], mods_p[4], mods_p[3], rw_pad, rb_pad, t, 512)

    xs_ = x_sample.reshape(bs, d)
    proj_s = _in_proj(xs_, ln1_w[l], mods_s[1], mods_s[0], w_in_r, 1, bs)
    ya_s, conv_s, delta_s = _gdn_step(proj_s, state_conv[l], state_delta[l], conv_w[l], alog_lane, dtb_lane, dn_norm_w[l])
    w_buf = cache_swa_k.shape[2]
    yb_s, k_s, v_s = _swa_step(proj_s, cache_swa_k[l].reshape(bs, w_buf, SW_KVW), cache_swa_v[l].reshape(bs, w_buf, SW_KVW),
                               sinks, sw_q_norm_w[l], sw_k_norm_w[l])
    merged_s = _merge(ya_s.reshape(bs, DN_VW), yb_s.reshape(bs, SW_QW), wa, wb, proj_s, bs)
    x1_s = _out_proj(merged_s, wo, xs_, mods_s[2], 1, bs)
    h2_s, idx_s, tw_s = _router(x1_s, ln2_w[l], mods_s[4], mods_s[3], rw_pad, rb_pad, 1, bs)

    top_idx = jnp.concatenate([idx_p[:, :TOP_K], idx_s[:, :TOP_K]], axis=0)
    dest, zero_blocks, n_rows, sb_e, sb_blk0, sb_nblk, tail = _routing_tables(top_idx)
    xs_sorted = _scatter_rows(h2_p, h2_s, dest, zero_blocks, n_rows)
    ys = _experts(xs_sorted, sb_e, sb_blk0, sb_nblk, tail, w_gate_up[l], b_gate_up[l], w_down[l], b_down[l])
    pos = dest.reshape(np_tok + bs, TOP_K)
    y_p = _combine(ys, _kmajor(pos[:np_tok], COMBINE_TOK), x1_p, mods_p[5], tw_p, t)
    pad_s = COMBINE_TOK - bs
    pos_s = jnp.concatenate([pos[np_tok:], jnp.zeros((pad_s, TOP_K), jnp.int32)], axis=0)
    x1_s_pad = jnp.concatenate([x1_s, jnp.zeros((pad_s, d), F32)], axis=0)
    gt2_s_pad = jnp.concatenate([mods_s[5].reshape(bs, d), jnp.zeros((pad_s, d), F32)], axis=0)
    tw_s_pad = jnp.concatenate([tw_s, jnp.zeros((pad_s, LANE), F32)], axis=0)
    y_s = _combine(ys, _kmajor(pos_s, COMBINE_TOK), x1_s_pad, gt2_s_pad.reshape(COMBINE_TOK, 1, d), tw_s_pad, 1)[:bs]

    conv_p = proj3[:, t - (DN_CONV - 1):, C_QKV:C_QKV + DN_CONV_CH]
    kp_out = kn_p[:, t - WINDOW:].reshape(bp, WINDOW, SW_KV_HEADS, SW_HD)
    vp_out = proj3[:, t - WINDOW:, C_SV:C_SV + SW_KVW].reshape(bp, WINDOW, SW_KV_HEADS, SW_HD)
    return (
        y_p.reshape(bp, t, d),
        y_s.reshape(bs, 1, d),
        conv_p[None],
        conv_s[None],
        delta_p[None],
        delta_s[None],
        kp_out[None],
        k_s.reshape(bs, w_buf, SW_KV_HEADS, SW_HD)[None],
        vp_out[None],
        v_s.reshape(bs, w_buf, SW_KV_HEADS, SW_HD)[None],
    )
```

```python
import functools

import jax
import jax.numpy as jnp
import numpy as np
from jax import lax
from jax.experimental import pallas as pl
from jax.experimental.pallas import tpu as pltpu

F32 = jnp.float32
BF16 = jnp.bfloat16

D_MODEL = 2048
PAST_LEN = 16384
DN_HEADS = 8
DN_DK = 128
DN_DV = 128
DN_CONV = 4
SW_HEADS = 16
SW_KV_HEADS = 2
SW_HD = 64
SW_GROUP = SW_HEADS // SW_KV_HEADS
WINDOW = 128
N_EXPERTS = 32
TOP_K = 4
SWIGLU_ALPHA = 1.702
SWIGLU_LIMIT = 7.0
EPS = 1e-6

DN_QK = DN_HEADS * DN_DK
DN_VW = DN_HEADS * DN_DV
DN_CONV_CH = 2 * DN_QK + DN_VW
SW_QW = SW_HEADS * SW_HD
SW_KVW = SW_KV_HEADS * SW_HD

LANE = 128
C_QKV = 0
C_Z = DN_CONV_CH
C_SQ = C_Z + DN_VW
C_GA = C_SQ + SW_QW
C_GB = C_GA + D_MODEL
C_SK = C_GB + D_MODEL
C_SV = C_SK + SW_KVW
C_BA = C_SV + SW_KVW
PROJ_W = 10240

GDN_GROUP = 256
GDN_CHUNK = 256
GDN_LEVELS = 8
MOE_ROWS = 128
MOE_SB_BLOCKS = 10
MOE_PASS_BLOCKS = 8
MOE_TF = 512
POST_TM = 256
VMEM_LIMIT = 56 * 1024 * 1024


def _cp(sem, vmem=VMEM_LIMIT):
    return pltpu.CompilerParams(dimension_semantics=sem, vmem_limit_bytes=vmem)


def _dot(a, b):
    return jnp.dot(a.astype(BF16), b.astype(BF16), preferred_element_type=F32)


def _dot_nt(a, b):
    return lax.dot_general(a.astype(BF16), b.astype(BF16), (((1,), (1,)), ((), ())), preferred_element_type=F32)


def _split(a):
    hi = a.astype(BF16)
    lo = (a - hi.astype(F32)).astype(BF16)
    return hi, lo


def _dot3(a, b):
    ah, al = _split(a)
    bh, bl = _split(b)
    d = functools.partial(jnp.dot, preferred_element_type=F32)
    return d(ah, bh) + (d(ah, bl) + d(al, bh))


def _dot3_nt(a, b):
    ah, al = _split(a)
    bh, bl = _split(b)
    d = functools.partial(lax.dot_general, dimension_numbers=(((1,), (1,)), ((), ())), preferred_element_type=F32)
    return d(ah, bh) + (d(ah, bl) + d(al, bh))


def _dot_exact_lhs01(m01, b):
    b1 = b.astype(BF16)
    r = b - b1.astype(F32)
    b2 = r.astype(BF16)
    b3 = (r - b2.astype(F32)).astype(BF16)
    d = functools.partial(jnp.dot, preferred_element_type=F32)
    m = m01.astype(BF16)
    return d(m, b1) + (d(m, b2) + d(m, b3))


def _sigmoid(x):
    return 1.0 / (1.0 + jnp.exp(-x))


def _silu(x):
    return x * _sigmoid(x)


def _softplus(x):
    return jnp.maximum(x, 0.0) + jnp.log(1.0 + jnp.exp(-jnp.abs(x)))


def _ada_kernel(c_ref, w_ref, b_ref, o_ref):
    o_ref[...] = _dot(_silu(c_ref[...]), w_ref[...]) + b_ref[...]


def _ada_mod(c_all, w_ada, b_ada):
    m = c_all.shape[0]
    n = w_ada.shape[1]
    tn = 1024
    return pl.pallas_call(
        _ada_kernel,
        grid=(n // tn,),
        in_specs=[
            pl.BlockSpec((m, D_MODEL), lambda j: (0, 0)),
            pl.BlockSpec((D_MODEL, tn), lambda j: (0, j)),
            pl.BlockSpec((1, tn), lambda j: (0, j)),
        ],
        out_specs=pl.BlockSpec((m, tn), lambda j: (0, j)),
        out_shape=jax.ShapeDtypeStruct((m, n), F32),
        compiler_params=_cp(("arbitrary",)),
        name="ada_mod",
    )(c_all, w_ada, b_ada.reshape(1, n))


def _norm_mod(x, lnw, sc, sh):
    y = x * lax.rsqrt(jnp.mean(x * x, axis=-1, keepdims=True) + EPS)
    return (y * lnw) * (1.0 + sc) + sh


def _inproj_kernel(x_ref, lnw_ref, sc_ref, sh_ref, w_ref, o_ref, h_scr):
    @pl.when(pl.program_id(1) == 0)
    def _():
        h_scr[...] = _norm_mod(x_ref[...], lnw_ref[...], sc_ref[...], sh_ref[...]).astype(BF16)

    o_ref[...] = jnp.dot(h_scr[...], w_ref[...], preferred_element_type=F32)


def _in_proj(x2d, lnw, sc, sh, w_bf16, rows_per_mod, tm):
    m = x2d.shape[0]
    tn = 1024
    if rows_per_mod == 1:
        mod_spec = pl.BlockSpec((tm, D_MODEL), lambda i, j: (i, 0))
        sc, sh = sc.reshape(m, D_MODEL), sh.reshape(m, D_MODEL)
    else:
        assert rows_per_mod % tm == 0
        mod_spec = pl.BlockSpec((None, 1, D_MODEL), lambda i, j: (i // (rows_per_mod // tm), 0, 0))
    return pl.pallas_call(
        _inproj_kernel,
        grid=(m // tm, PROJ_W // tn),
        in_specs=[
            pl.BlockSpec((tm, D_MODEL), lambda i, j: (i, 0)),
            pl.BlockSpec((1, D_MODEL), lambda i, j: (0, 0)),
            mod_spec,
            mod_spec,
            pl.BlockSpec((D_MODEL, tn), lambda i, j: (0, j)),
        ],
        out_specs=pl.BlockSpec((tm, tn), lambda i, j: (i, j)),
        out_shape=jax.ShapeDtypeStruct((m, PROJ_W), F32),
        scratch_shapes=[pltpu.VMEM((tm, D_MODEL), BF16)],
        compiler_params=_cp(("arbitrary", "arbitrary")),
        name="in_proj",
    )(x2d, lnw.reshape(1, D_MODEL), sc, sh, w_bf16)


def _tri_masks(n, chunk):
    r = lax.broadcasted_iota(jnp.int32, (n, n), 0)
    c = lax.broadcasted_iota(jnp.int32, (n, n), 1)
    same = (r // chunk) == (c // chunk)
    return same, same & (r >= c), same & (r > c)


def _gates_kernel(ba_ref, alog_ref, dtb_ref, beta_ref, gc_ref, eg_ref, ek_ref, el_ref, gcrow_ref):
    same, causal, _ = _tri_masks(GDN_GROUP, GDN_CHUNK)
    lower01 = jnp.where(causal, 1.0, 0.0)
    ones01 = jnp.where(same, 1.0, 0.0)
    nega = -jnp.exp(alog_ref[...])
    dtb = dtb_ref[...]
    t = ba_ref.shape[0]

    def body(i, carry):
        r0 = pl.multiple_of(i * GDN_GROUP, GDN_GROUP)
        x = ba_ref[pl.ds(r0, GDN_GROUP), :]
        g = nega * _softplus(x + dtb)
        gc = _dot_exact_lhs01(lower01, g)
        gl = _dot_exact_lhs01(ones01, g)
        beta_ref[pl.ds(r0, GDN_GROUP), :] = _sigmoid(x)
        gc_ref[pl.ds(r0, GDN_GROUP), :] = gc
        eg_ref[pl.ds(r0, GDN_GROUP), :] = jnp.exp(gc)
        ek_ref[pl.ds(r0, GDN_GROUP), :] = jnp.exp(gl - gc)
        el_ref[pl.ds(r0, GDN_GROUP), :] = jnp.exp(gl)
        gct = gc.T
        for h in range(DN_HEADS):
            gcrow_ref[h, :, pl.ds(r0, GDN_GROUP)] = gct[DN_HEADS + h:DN_HEADS + h + 1, :]
        return carry

    lax.fori_loop(0, t // GDN_GROUP, body, 0)


def _gdn_gates(proj3, alog_lane, dtb_lane):
    b, t, _ = proj3.shape
    col = pl.BlockSpec((None, t, LANE), lambda i: (i, 0, 0))
    shp = jax.ShapeDtypeStruct((b, t, LANE), F32)
    return pl.pallas_call(
        _gates_kernel,
        grid=(b,),
        in_specs=[
            pl.BlockSpec((None, t, LANE), lambda i: (i, 0, C_BA // LANE)),
            pl.BlockSpec((1, LANE), lambda i: (0, 0)),
            pl.BlockSpec((1, LANE), lambda i: (0, 0)),
        ],
        out_specs=[col, col, col, col, col, pl.BlockSpec((None, DN_HEADS, 1, t), lambda i: (i, 0, 0, 0))],
        out_shape=[shp, shp, shp, shp, shp, jax.ShapeDtypeStruct((b, DN_HEADS, 1, t), F32)],
        compiler_params=_cp(("arbitrary",)),
        name="gdn_gates",
    )(proj3, alog_lane, dtb_lane)


def _l2norm(x):
    return x * lax.rsqrt(jnp.sum(x * x, axis=-1, keepdims=True) + EPS)


GDN_HPS = 4
GDN_TILE = 1024


def _gdn_kernel(q_ref, k_ref, v_ref, z_ref, beta_ref, gc_ref, eg_ref, ek_ref, el_ref, gcrow_ref,
                cwq_ref, cwk_ref, cwv_ref, nw_ref, o_ref, s_ref,
                pad_scr, hist_scr, qn_scr, kn_scr, vn_scr, oacc_scr, s_scr):
    t = q_ref.shape[0]
    wdt = GDN_HPS * LANE
    h0 = pl.program_id(1) * GDN_HPS
    pad = 8

    @pl.when(pl.program_id(2) == 0)
    def _():
        hist_scr[...] = jnp.zeros(hist_scr.shape, F32)
        s_scr[...] = jnp.zeros(s_scr.shape, F32)

    def conv_silu(stream, u_ref, cw_ref):
        pad_scr[0:pad, :] = hist_scr[stream]
        pad_scr[pad:pad + t, :] = u_ref[...]
        hist_scr[stream] = pad_scr[t:t + pad, :]
        y = cw_ref[DN_CONV - 1:DN_CONV, :] * pad_scr[pad:pad + t, :]
        for i in range(DN_CONV - 1):
            off = pad - (DN_CONV - 1) + i
            y = y + cw_ref[i:i + 1, :] * pad_scr[off:off + t, :]
        return _silu(y)

    yq = conv_silu(0, q_ref, cwq_ref)
    for hh in range(GDN_HPS):
        sl = slice(hh * LANE, (hh + 1) * LANE)
        qn_scr[:, sl] = _l2norm(yq[:, sl]) * (DN_DK ** -0.5)
    yk = conv_silu(1, k_ref, cwk_ref)
    for hh in range(GDN_HPS):
        sl = slice(hh * LANE, (hh + 1) * LANE)
        kn_scr[:, sl] = _l2norm(yk[:, sl])
    vn_scr[...] = conv_silu(2, v_ref, cwv_ref)

    n = GDN_GROUP
    c = GDN_CHUNK
    _, causal, strict = _tri_masks(n, c)
    rr = lax.broadcasted_iota(jnp.int32, (n, n), 0)
    cc = lax.broadcasted_iota(jnp.int32, (n, n), 1)
    eye = jnp.where(rr == cc, 1.0, 0.0)
    lane = lax.broadcasted_iota(jnp.int32, (n, LANE), 1)

    def pick(ref, r0, sel):
        return jnp.sum(jnp.where(sel, ref[pl.ds(r0, n), :], 0.0), axis=-1, keepdims=True)

    def head_group(hh, r0):
        sl = slice(hh * LANE, (hh + 1) * LANE)
        sel_b = lane == h0 + hh
        sel_g = lane == h0 + hh + DN_HEADS
        q = qn_scr[pl.ds(r0, n), sl]
        k = kn_scr[pl.ds(r0, n), sl]
        v = vn_scr[pl.ds(r0, n), sl]
        beta = pick(beta_ref, r0, sel_b)
        gc = pick(gc_ref, r0, sel_g)
        eg = pick(eg_ref, r0, sel_g)
        ek = pick(ek_ref, r0, sel_g)
        el = pick(el_ref, r0, sel_g)
        gcrow = gcrow_ref[hh, :, pl.ds(r0, n)]
        decay = jnp.where(causal, jnp.exp(gc - gcrow), 0.0)
        a_low = jnp.where(strict, beta * _dot_nt(k, k) * decay, 0.0)
        pw = [-a_low]
        for _ in range(GDN_LEVELS - 1):
            pw.append(_dot(pw[-1], pw[-1]))
        fs = [eye + pw[i] + pw[i + 1] + _dot(pw[i], pw[i + 1]) for i in range(0, GDN_LEVELS, 2)]
        while len(fs) > 1:
            fs = [_dot(fs[i], fs[i + 1]) for i in range(0, len(fs), 2)]
        rhs = jnp.concatenate([v * beta, k * (beta * eg)], axis=1)
        sol = _dot(fs[0], rhs)
        value = sol[:, :DN_DV]
        kcum = sol[:, DN_DV:]
        intra = _dot_nt(q, k) * decay
        q_dec = q * eg
        k_dec = k * ek
        for j in range(n // c):
            lo, hi = j * c, (j + 1) * c
            s = s_scr[hh]
            r = _dot(jnp.concatenate([kcum[lo:hi], q_dec[lo:hi]], axis=0), s)
            v_new = value[lo:hi] - r[:c]
            parts = []
            if lo:
                parts.append(jnp.zeros((lo, DN_DV), F32))
            parts.append(v_new)
            if hi < n:
                parts.append(jnp.zeros((n - hi, DN_DV), F32))
            o = r[c:] + _dot(intra[lo:hi], jnp.concatenate(parts, axis=0))
            oacc_scr[pl.ds(r0 + lo, c), sl] = o
            s_scr[hh] = s * el[lo:lo + 1] + _dot(k_dec[lo:hi].T, v_new)

    def body(i, carry):
        r0 = pl.multiple_of(i * n, n)
        for hh in range(GDN_HPS):
            head_group(hh, r0)
        return carry

    lax.fori_loop(0, t // n, body, 0)
    for hh in range(GDN_HPS):
        sl = slice(hh * LANE, (hh + 1) * LANE)
        o = oacc_scr[:, sl]
        y = o * lax.rsqrt(jnp.mean(o * o, axis=-1, keepdims=True) + EPS)
        o_ref[:, sl] = (y * nw_ref[...]) * _silu(z_ref[:, sl])
    s_ref[...] = s_scr[...]


def _gdn_prompt(proj3, gates, conv_w, norm_w):
    b, t, _ = proj3.shape
    beta, gc, eg, ek, el, gcrow = gates
    hps = GDN_HPS
    wdt = hps * LANE
    steps = DN_HEADS // hps
    tt = GDN_TILE
    assert t % tt == 0 and tt % GDN_GROUP == 0

    def colspec(base):
        return pl.BlockSpec((None, tt, wdt), lambda i, j, r, base=base: (i, r, base + j))

    gate = pl.BlockSpec((None, tt, LANE), lambda i, j, r: (i, r, 0))

    def cwspec(base):
        return pl.BlockSpec((DN_CONV, wdt), lambda i, j, r, base=base: (0, base + j))

    return pl.pallas_call(
        _gdn_kernel,
        grid=(b, steps, t // tt),
        in_specs=[
            colspec(0), colspec(steps), colspec(2 * steps), colspec(C_Z // wdt),
            gate, gate, gate, gate, gate,
            pl.BlockSpec((None, hps, 1, tt), lambda i, j, r: (i, j, 0, r)),
            cwspec(0), cwspec(steps), cwspec(2 * steps),
            pl.BlockSpec((1, DN_DV), lambda i, j, r: (0, 0)),
        ],
        out_specs=[
            pl.BlockSpec((None, tt, wdt), lambda i, j, r: (i, r, j)),
            pl.BlockSpec((None, hps, DN_DK, DN_DV), lambda i, j, r: (i, j, 0, 0)),
        ],
        out_shape=[
            jax.ShapeDtypeStruct((b, t, DN_VW), F32),
            jax.ShapeDtypeStruct((b, DN_HEADS, DN_DK, DN_DV), F32),
        ],
        scratch_shapes=[
            pltpu.VMEM((tt + 8, wdt), F32),
            pltpu.VMEM((3, 8, wdt), F32),
            pltpu.VMEM((tt, wdt), F32),
            pltpu.VMEM((tt, wdt), F32),
            pltpu.VMEM((tt, wdt), F32),
            pltpu.VMEM((tt, wdt), F32),
            pltpu.VMEM((hps, DN_DK, DN_DV), F32),
        ],
        compiler_params=_cp(("arbitrary", "arbitrary", "arbitrary")),
        name="gdn_prompt",
    )(proj3, proj3, proj3, proj3, beta, gc, eg, ek, el, gcrow, conv_w, conv_w, conv_w, norm_w.reshape(1, DN_DV))


def _gdn_step_kernel(p_ref, cprev_ref, s_ref, cw_ref, alog_ref, dtb_ref, nw_ref, o_ref, cnew_ref, snew_ref):
    u = p_ref[:, C_QKV:C_QKV + DN_CONV_CH]
    prev = cprev_ref[...]
    y = cw_ref[DN_CONV - 1:DN_CONV, :] * u
    for i in range(DN_CONV - 1):
        y = y + cw_ref[i:i + 1, :] * prev[i:i + 1, :]
    y = _silu(y)
    cnew_ref[0:DN_CONV - 2, :] = prev[1:DN_CONV - 1, :]
    cnew_ref[DN_CONV - 2:DN_CONV - 1, :] = u
    ba = p_ref[:, C_BA:C_BA + LANE]
    beta_l = _sigmoid(ba)
    a_l = jnp.exp(-jnp.exp(alog_ref[...]) * _softplus(ba + dtb_ref[...]))
    lane = lax.broadcasted_iota(jnp.int32, (1, LANE), 1)
    row8 = lax.broadcasted_iota(jnp.int32, (8, LANE), 0)
    for h in range(DN_HEADS):
        q = _l2norm(y[:, h * DN_DK:(h + 1) * DN_DK]) * (DN_DK ** -0.5)
        k = _l2norm(y[:, DN_QK + h * DN_DK:DN_QK + (h + 1) * DN_DK])
        v = y[:, 2 * DN_QK + h * DN_DV:2 * DN_QK + (h + 1) * DN_DV]
        beta = jnp.sum(jnp.where(lane == h, beta_l, 0.0), axis=-1, keepdims=True)
        a = jnp.sum(jnp.where(lane == h + DN_HEADS, a_l, 0.0), axis=-1, keepdims=True)
        s = s_ref[h]
        kq = jnp.where(row8 == 0, k, jnp.where(row8 == 1, q, 0.0)).T
        kcol, qcol = kq[:, 0:1], kq[:, 1:2]
        v_new = beta * (v - a * jnp.sum(s * kcol, axis=0, keepdims=True))
        o = a * jnp.sum(s * qcol, axis=0, keepdims=True) + jnp.sum(q * k, axis=-1, keepdims=True) * v_new
        snew_ref[h] = s * a + kcol * v_new
        yo = o * lax.rsqrt(jnp.mean(o * o, axis=-1, keepdims=True) + EPS)
        z = p_ref[:, C_Z + h * DN_DV:C_Z + (h + 1) * DN_DV]
        o_ref[:, h * DN_DV:(h + 1) * DN_DV] = (yo * nw_ref[...]) * _silu(z)


def _gdn_step(proj_s, conv_prev, s0, conv_w, alog_lane, dtb_lane, norm_w):
    b = proj_s.shape[0]
    return pl.pallas_call(
        _gdn_step_kernel,
        grid=(b,),
        in_specs=[
            pl.BlockSpec((None, 1, PROJ_W), lambda i: (i, 0, 0)),
            pl.BlockSpec((None, DN_CONV - 1, DN_CONV_CH), lambda i: (i, 0, 0)),
            pl.BlockSpec((None, DN_HEADS, DN_DK, DN_DV), lambda i: (i, 0, 0, 0)),
            pl.BlockSpec((DN_CONV, DN_CONV_CH), lambda i: (0, 0)),
            pl.BlockSpec((1, LANE), lambda i: (0, 0)),
            pl.BlockSpec((1, LANE), lambda i: (0, 0)),
            pl.BlockSpec((1, DN_DV), lambda i: (0, 0)),
        ],
        out_specs=[
            pl.BlockSpec((None, 1, DN_VW), lambda i: (i, 0, 0)),
            pl.BlockSpec((None, DN_CONV - 1, DN_CONV_CH), lambda i: (i, 0, 0)),
            pl.BlockSpec((None, DN_HEADS, DN_DK, DN_DV), lambda i: (i, 0, 0, 0)),
        ],
        out_shape=[
            jax.ShapeDtypeStruct((b, 1, DN_VW), F32),
            jax.ShapeDtypeStruct((b, DN_CONV - 1, DN_CONV_CH), F32),
            jax.ShapeDtypeStruct((b, DN_HEADS, DN_DK, DN_DV), F32),
        ],
        compiler_params=_cp(("arbitrary",)),
        name="gdn_step",
    )(proj_s.reshape(b, 1, PROJ_W), conv_prev, s0, conv_w, alog_lane, dtb_lane, norm_w.reshape(1, DN_DV))


def _alibi_slope(h):
    return float(2.0 ** (-8.0 * (h + 1) / SW_HEADS))


def _head_rms(x, w):
    return (x * lax.rsqrt(jnp.mean(x * x, axis=-1, keepdims=True) + EPS)) * w


SWA_HB = 4


def _swa_kernel(sinks_ref, q_ref, kc_ref, kp_ref, vc_ref, vp_ref, qw_ref, kw_ref, o_ref, kn_ref):
    blk = pl.program_id(1)
    w = WINDOW
    rows = SWA_HB * w
    qi = lax.broadcasted_iota(jnp.int32, (rows, 2 * w), 0)
    kj = lax.broadcasted_iota(jnp.int32, (rows, 2 * w), 1)
    dist = (qi & (w - 1)) + w - kj
    valid = (dist >= 0) & (dist < w) & ((kj >= w) | (blk > 0))
    distf = dist.astype(F32)
    stripe = lax.broadcasted_iota(jnp.int32, (rows, 1), 0) // w
    kc = kc_ref[...]
    kp = kp_ref[...]
    kbands, vbands = [], []
    for g in range(SW_KV_HEADS):
        sl = slice(g * SW_HD, (g + 1) * SW_HD)
        kcn = _head_rms(kc[:, sl], kw_ref[...])
        kn_ref[:, sl] = kcn
        kbands.append(jnp.concatenate([_head_rms(kp[:, sl], kw_ref[...]), kcn], axis=0))
        vbands.append(jnp.concatenate([vp_ref[:, sl], vc_ref[:, sl]], axis=0))
    for hb in range(SW_HEADS // SWA_HB):
        heads = range(hb * SWA_HB, (hb + 1) * SWA_HB)
        g = heads[0] // SW_GROUP
        qs = jnp.concatenate([_head_rms(q_ref[:, h * SW_HD:(h + 1) * SW_HD], qw_ref[...]) for h in heads], axis=0)
        slope = jnp.zeros((rows, 1), F32)
        sink = jnp.zeros((rows, 1), F32)
        for i, h in enumerate(heads):
            slope = jnp.where(stripe == i, _alibi_slope(h), slope)
            sink = jnp.where(stripe == i, sinks_ref[h], sink)
        s = _dot_nt(qs, kbands[g]) * (SW_HD ** -0.5) - slope * distf
        s = jnp.where(valid, s, -jnp.inf)
        m = jnp.maximum(jnp.max(s, axis=-1, keepdims=True), sink)
        p = jnp.exp(s - m)
        den = jnp.sum(p, axis=-1, keepdims=True) + jnp.exp(sink - m)
        o = _dot(p / den, vbands[g])
        for i, h in enumerate(heads):
            o_ref[:, h * SW_HD:(h + 1) * SW_HD] = o[i * w:(i + 1) * w]


def _swa_prompt(proj3, sinks, qw, kw):
    b, t, _ = proj3.shape
    nb = t // WINDOW
    kcol, vcol = C_SK // LANE, C_SV // LANE

    def cur(col):
        return pl.BlockSpec((None, WINDOW, SW_KVW), lambda i, j, s, col=col: (i, j, col))

    def prev(col):
        return pl.BlockSpec((None, WINDOW, SW_KVW), lambda i, j, s, col=col: (i, jnp.maximum(j - 1, 0), col))

    return pl.pallas_call(
        _swa_kernel,
        grid_spec=pltpu.PrefetchScalarGridSpec(
            num_scalar_prefetch=1,
            grid=(b, nb),
            in_specs=[
                pl.BlockSpec((None, WINDOW, SW_QW), lambda i, j, s: (i, j, C_SQ // SW_QW)),
                cur(kcol), prev(kcol), cur(vcol), prev(vcol),
                pl.BlockSpec((1, SW_HD), lambda i, j, s: (0, 0)),
                pl.BlockSpec((1, SW_HD), lambda i, j, s: (0, 0)),
            ],
            out_specs=[
                pl.BlockSpec((None, WINDOW, SW_QW), lambda i, j, s: (i, j, 0)),
                pl.BlockSpec((None, WINDOW, SW_KVW), lambda i, j, s: (i, j, 0)),
            ],
        ),
        out_shape=[
            jax.ShapeDtypeStruct((b, t, SW_QW), F32),
            jax.ShapeDtypeStruct((b, t, SW_KVW), F32),
        ],
        compiler_params=_cp(("arbitrary", "arbitrary")),
        name="swa_prompt",
    )(sinks, proj3, proj3, proj3, proj3, proj3, qw.reshape(1, SW_HD), kw.reshape(1, SW_HD))


def _swa_step_kernel(sinks_ref, p_ref, kbuf_ref, vbuf_ref, qw_ref, kw_ref, o_ref, knew_ref, vnew_ref, kcat, vcat):
    w = kbuf_ref.shape[0]
    rows = kcat.shape[0]
    knew = p_ref[:, C_SK:C_SK + SW_KVW]
    vnew = p_ref[:, C_SV:C_SV + SW_KVW]
    kcat[...] = jnp.zeros(kcat.shape, F32)
    vcat[...] = jnp.zeros(vcat.shape, F32)
    kcat[0:w, :] = kbuf_ref[...]
    vcat[0:w, :] = vbuf_ref[...]
    for g in range(SW_KV_HEADS):
        sl = slice(g * SW_HD, (g + 1) * SW_HD)
        kcat[w:w + 1, sl] = _head_rms(knew[:, sl], kw_ref[...])
    vcat[w:w + 1, :] = vnew
    knew_ref[...] = kcat[1:w + 1, :]
    vnew_ref[...] = vcat[1:w + 1, :]
    j = lax.broadcasted_iota(jnp.int32, (SW_GROUP, rows), 1)
    dist = w - j
    valid = (dist >= 0) & (dist < WINDOW)
    distf = dist.astype(F32)
    hrow = lax.broadcasted_iota(jnp.int32, (SW_GROUP, 1), 0)
    for g in range(SW_KV_HEADS):
        sl = slice(g * SW_HD, (g + 1) * SW_HD)
        qs = jnp.zeros((SW_GROUP, SW_HD), F32)
        slope = jnp.zeros((SW_GROUP, 1), F32)
        sink = jnp.zeros((SW_GROUP, 1), F32)
        for i in range(SW_GROUP):
            h = g * SW_GROUP + i
            qh = _head_rms(p_ref[:, C_SQ + h * SW_HD:C_SQ + (h + 1) * SW_HD], qw_ref[...])
            qs = jnp.where(hrow == i, qh, qs)
            slope = jnp.where(hrow == i, _alibi_slope(h), slope)
            sink = jnp.where(hrow == i, sinks_ref[h], sink)
        s = _dot_nt(qs, kcat[:, sl]) * (SW_HD ** -0.5) - slope * distf
        s = jnp.where(valid, s, -jnp.inf)
        m = jnp.maximum(jnp.max(s, axis=-1, keepdims=True), sink)
        p = jnp.exp(s - m)
        den = jnp.sum(p, axis=-1, keepdims=True) + jnp.exp(sink - m)
        o = _dot(p / den, vcat[:, sl])
        for i in range(SW_GROUP):
            h = g * SW_GROUP + i
            o_ref[:, h * SW_HD:(h + 1) * SW_HD] = o[i:i + 1]


def _swa_step(proj_s, kbuf, vbuf, sinks, qw, kw):
    b = proj_s.shape[0]
    w = kbuf.shape[1]
    rows = 2 * w
    buf = pl.BlockSpec((None, w, SW_KVW), lambda i, s: (i, 0, 0))
    return pl.pallas_call(
        _swa_step_kernel,
        grid_spec=pltpu.PrefetchScalarGridSpec(
            num_scalar_prefetch=1,
            grid=(b,),
            in_specs=[
                pl.BlockSpec((None, 1, PROJ_W), lambda i, s: (i, 0, 0)),
                buf, buf,
                pl.BlockSpec((1, SW_HD), lambda i, s: (0, 0)),
                pl.BlockSpec((1, SW_HD), lambda i, s: (0, 0)),
            ],
            out_specs=[pl.BlockSpec((None, 1, SW_QW), lambda i, s: (i, 0, 0)), buf, buf],
            scratch_shapes=[pltpu.VMEM((rows, SW_KVW), F32), pltpu.VMEM((rows, SW_KVW), F32)],
        ),
        out_shape=[
            jax.ShapeDtypeStruct((b, 1, SW_QW), F32),
            jax.ShapeDtypeStruct((b, w, SW_KVW), F32),
            jax.ShapeDtypeStruct((b, w, SW_KVW), F32),
        ],
        compiler_params=_cp(("arbitrary",)),
        name="swa_step",
    )(sinks, proj_s.reshape(b, 1, PROJ_W), kbuf, vbuf, qw.reshape(1, SW_HD), kw.reshape(1, SW_HD))


def _route_top_k(hmod, rw_ref, rb_ref, idx_ref, w_ref):
    logits = _dot3(hmod, rw_ref[...]) + rb_ref[...]
    lane = lax.broadcasted_iota(jnp.int32, logits.shape, 1)
    cur = jnp.where(lane < N_EXPERTS, logits, -jnp.inf)
    vals, idxs = [], []
    for _ in range(TOP_K):
        m = jnp.max(cur, axis=-1, keepdims=True)
        ix = jnp.min(jnp.where(cur == m, lane, LANE), axis=-1, keepdims=True)
        vals.append(m)
        idxs.append(ix)
        cur = jnp.where(lane == ix, -jnp.inf, cur)
    es = [jnp.exp(v - vals[0]) for v in vals]
    den = es[0] + es[1] + es[2] + es[3]
    idx_out = jnp.zeros(logits.shape, jnp.int32)
    w_out = jnp.zeros(logits.shape, F32)
    for k in range(TOP_K):
        idx_out = jnp.where(lane == k, idxs[k], idx_out)
        w_out = jnp.where(lane == k, es[k] / den, w_out)
    idx_ref[...] = idx_out
    w_ref[...] = w_out


def _post_kernel(ya_ref, yb_ref, ga0_ref, ga1_ref, gb0_ref, gb1_ref, x_ref, gt_ref, lnw_ref, sc_ref, sh_ref,
                 wa_ref, wb_ref, wo_ref, rw_ref, rb_ref, x1_ref, h_ref, idx_ref, w_ref):
    a = _dot(ya_ref[...], wa_ref[...])
    b = _dot(yb_ref[...], wb_ref[...])
    ga = jnp.concatenate([ga0_ref[...], ga1_ref[...]], axis=1)
    gb = jnp.concatenate([gb0_ref[...], gb1_ref[...]], axis=1)
    merged = _sigmoid(ga) * a + _sigmoid(gb) * b
    x1 = x_ref[...] + gt_ref[...] * _dot(merged, wo_ref[...])
    x1_ref[...] = x1
    hmod = _norm_mod(x1, lnw_ref[...], sc_ref[...], sh_ref[...])
    h_ref[...] = hmod
    _route_top_k(hmod, rw_ref, rb_ref, idx_ref, w_ref)


def _post_attention(ya, yb, proj, x2d, gt, lnw, sc, sh, wa, wb, wo, rw_pad, rb_pad, rows_per_mod, tm):
    m = x2d.shape[0]
    half = D_MODEL // 2
    assert C_GA % half == 0 and C_GB % half == 0
    if rows_per_mod == 1:
        mod_spec = pl.BlockSpec((tm, D_MODEL), lambda i: (i, 0))
        gt, sc, sh = (v.reshape(m, D_MODEL) for v in (gt, sc, sh))
    else:
        mod_spec = pl.BlockSpec((None, 1, D_MODEL), lambda i: (i // (rows_per_mod // tm), 0, 0))
    row = pl.BlockSpec((tm, D_MODEL), lambda i: (i, 0))
    small = pl.BlockSpec((tm, LANE), lambda i: (i, 0))

    def gate(col):
        return pl.BlockSpec((tm, half), lambda i, col=col: (i, col))

    def resident(shape):
        return pl.BlockSpec(shape, lambda i: (0, 0), pipeline_mode=pl.Buffered(1))

    return pl.pallas_call(
        _post_kernel,
        grid=(m // tm,),
        in_specs=[
            pl.BlockSpec((tm, DN_VW), lambda i: (i, 0)),
            pl.BlockSpec((tm, SW_QW), lambda i: (i, 0)),
            gate(C_GA // half), gate(C_GA // half + 1), gate(C_GB // half), gate(C_GB // half + 1),
            row, mod_spec,
            pl.BlockSpec((1, D_MODEL), lambda i: (0, 0)),
            mod_spec, mod_spec,
            resident((DN_VW, D_MODEL)), resident((SW_QW, D_MODEL)), resident((D_MODEL, D_MODEL)),
            pl.BlockSpec((D_MODEL, LANE), lambda i: (0, 0)),
            pl.BlockSpec((1, LANE), lambda i: (0, 0)),
        ],
        out_specs=[row, row, small, small],
        out_shape=[
            jax.ShapeDtypeStruct((m, D_MODEL), F32),
            jax.ShapeDtypeStruct((m, D_MODEL), F32),
            jax.ShapeDtypeStruct((m, LANE), jnp.int32),
            jax.ShapeDtypeStruct((m, LANE), F32),
        ],
        compiler_params=_cp(("arbitrary",)),
        name="post_attention",
    )(ya, yb, proj, proj, proj, proj, x2d, gt, lnw.reshape(1, D_MODEL), sc, sh, wa, wb, wo, rw_pad, rb_pad)


SCATTER_TOK = 256
DMA_GROUP = 8
DMA_ROWS = 16


def _scatter_kernel(zl_ref, dest_ref, hp_ref, hs_ref, xs_hbm, zbuf, sem, zsem):
    i = pl.program_id(0)
    rb = MOE_ROWS
    n_prompt_steps = pl.num_programs(0) - 1

    @pl.when(i == 0)
    def _():
        zbuf[...] = jnp.zeros(zbuf.shape, F32)

        def zero_copy(n):
            return pltpu.make_async_copy(zbuf, xs_hbm.at[pl.ds(zl_ref[n] * rb, rb)], zsem)

        def start(n, c):
            @pl.when(zl_ref[n] >= 0)
            def _():
                zero_copy(n).start()
            return c

        def wait(n, c):
            @pl.when(zl_ref[n] >= 0)
            def _():
                zero_copy(n).wait()
            return c

        lax.fori_loop(0, zl_ref.shape[0], start, 0)
        lax.fori_loop(0, zl_ref.shape[0], wait, 0)

    def scatter(src_ref):
        n_tok = src_ref.shape[0]

        def group(g, c):
            t0 = pl.multiple_of(g * DMA_GROUP, DMA_GROUP)
            for r in range(DMA_GROUP):
                for k in range(TOP_K):
                    dst = dest_ref[0, g * (DMA_GROUP * TOP_K) + (r * TOP_K + k)]
                    pltpu.make_async_copy(src_ref.at[pl.ds(t0 + r, 1)], xs_hbm.at[pl.ds(dst, 1)], sem).start()
            return c

        lax.fori_loop(0, n_tok // DMA_GROUP, group, 0)
        for _ in range(TOP_K):
            pltpu.make_async_copy(src_ref, xs_hbm.at[pl.ds(0, n_tok)], sem).wait()

    @pl.when(i < n_prompt_steps)
    def _():
        scatter(hp_ref)

    @pl.when(i == n_prompt_steps)
    def _():
        scatter(hs_ref)


def _scatter_rows(h_p, h_s, dest, zero_blocks, n_rows):
    assert TOP_K == 4
    n_p, n_s = h_p.shape[0], h_s.shape[0]
    steps_p = n_p // SCATTER_TOK
    per = SCATTER_TOK * TOP_K
    dest_s = jnp.concatenate([dest[n_p * TOP_K:], jnp.zeros((per - n_s * TOP_K,), jnp.int32)])
    dest3 = jnp.concatenate([dest[:n_p * TOP_K], dest_s]).reshape(steps_p + 1, 1, per)
    return pl.pallas_call(
        _scatter_kernel,
        grid_spec=pltpu.PrefetchScalarGridSpec(
            num_scalar_prefetch=1,
            grid=(steps_p + 1,),
            in_specs=[
                pl.BlockSpec((None, 1, per), lambda i, zl: (i, 0, 0), memory_space=pltpu.SMEM),
                pl.BlockSpec((SCATTER_TOK, D_MODEL), lambda i, zl: (jnp.minimum(i, steps_p - 1), 0)),
                pl.BlockSpec((n_s, D_MODEL), lambda i, zl: (0, 0)),
            ],
            out_specs=pl.BlockSpec(memory_space=pl.ANY),
            scratch_shapes=[
                pltpu.VMEM((MOE_ROWS, D_MODEL), F32),
                pltpu.SemaphoreType.DMA(()),
                pltpu.SemaphoreType.DMA(()),
            ],
        ),
        out_shape=jax.ShapeDtypeStruct((n_rows, D_MODEL), F32),
        compiler_params=_cp(("arbitrary",)),
        name="moe_scatter",
    )(zero_blocks, dest3, h_p, h_s)


def _experts_kernel(sbe_ref, sbb_ref, sbn_ref, tail_ref, xs_hbm, wg_ref, wl_ref, wd_ref, bg_ref, bl_ref, bd_ref,
                    ys_hbm, xf_scr, xb_scr, acc_scr, sem_in, sem_out):
    s = pl.program_id(0)
    j = pl.program_id(1)
    last_s = pl.num_programs(0) - 1
    last_j = pl.num_programs(1) - 1
    nblk = sbn_ref[s]
    blk0 = sbb_ref[s]
    rb = MOE_ROWS

    def in_copy(first_blk, b):
        return pltpu.make_async_copy(xs_hbm.at[pl.ds((first_blk + b) * rb, rb)], xf_scr.at[pl.ds(b * rb, rb)], sem_in)

    def out_copy(first_blk, b):
        return pltpu.make_async_copy(acc_scr.at[pl.ds(b * rb, rb)], ys_hbm.at[pl.ds((first_blk + b) * rb, rb)], sem_out)

    def loop(n, fn):
        def body(b, c):
            fn(b)
            return c
        lax.fori_loop(0, n, body, 0)

    @pl.when(j == 0)
    def _():
        @pl.when(s == 0)
        def _():
            loop(nblk, lambda b: in_copy(blk0, b).start())

        loop(nblk, lambda b: in_copy(blk0, b).wait())

        @pl.when(s > 0)
        def _():
            prev0 = sbb_ref[s - 1]
            loop(sbn_ref[s - 1], lambda b: out_copy(prev0, b).wait())

        def cast(b):
            r0 = pl.multiple_of(b * rb, rb)
            xb_scr[pl.ds(r0, rb), :] = xf_scr[pl.ds(r0, rb), :].astype(BF16)
            acc_scr[pl.ds(r0, rb), :] = jnp.broadcast_to(bd_ref[...], (rb, D_MODEL))
        loop(nblk, cast)

        @pl.when(s < last_s)
        def _():
            nxt0 = sbb_ref[s + 1]
            loop(sbn_ref[s + 1], lambda b: in_copy(nxt0, b).start())

    @pl.when(nblk > 0)
    def _():
        def mlp(b0, nb):
            rows = nb * rb
            r0 = pl.multiple_of(b0 * rb, rb)
            x = xb_scr[pl.ds(r0, rows), :]
            glu = jnp.dot(x, wg_ref[...].astype(BF16), preferred_element_type=F32) + bg_ref[...]
            lin = jnp.dot(x, wl_ref[...].astype(BF16), preferred_element_type=F32) + bl_ref[...]
            glu = jnp.minimum(glu, SWIGLU_LIMIT)
            lin = jnp.clip(lin, -SWIGLU_LIMIT, SWIGLU_LIMIT)
            act = glu * _sigmoid(SWIGLU_ALPHA * glu) * (lin + 1.0)
            acc_scr[pl.ds(r0, rows), :] += jnp.dot(act.astype(BF16), wd_ref[...].astype(BF16),
                                                   preferred_element_type=F32)

            @pl.when(j == last_j)
            def _():
                for b in range(nb):
                    out_copy(blk0, b0 + b).start()

        full = nblk // MOE_PASS_BLOCKS
        loop(full, lambda p: mlp(p * MOE_PASS_BLOCKS, MOE_PASS_BLOCKS))
        done = full * MOE_PASS_BLOCKS
        part = MOE_PASS_BLOCKS // 2
        while part >= 1:
            take = ((nblk - done) // part) * part

            @pl.when(take > 0)
            def _(done=done, part=part):
                mlp(done, part)

            done = done + take
            part //= 2

    @pl.when((s == last_s) & (j == last_j))
    def _():
        loop(nblk, lambda b: out_copy(blk0, b).wait())
        acc_scr[0:rb, :] = jnp.zeros((rb, D_MODEL), F32)

        def zero_copy(b):
            return pltpu.make_async_copy(acc_scr.at[pl.ds(0, rb)], ys_hbm.at[pl.ds(b * rb, rb)], sem_out)

        def start(b, c):
            zero_copy(b).start()
            return c

        def wait(b, c):
            zero_copy(b).wait()
            return c

        lax.fori_loop(tail_ref[0], tail_ref[1], start, 0)
        lax.fori_loop(tail_ref[0], tail_ref[1], wait, 0)


def _experts(xs, sb_e, sb_blk0, sb_nblk, tail, w_gate_up, b_gate_up, w_down, b_down):
    n_rows = xs.shape[0]
    n_sb = sb_e.shape[0]
    tf = MOE_TF
    nj = D_MODEL // tf
    rmax = MOE_SB_BLOCKS * MOE_ROWS

    def jj(s, j, n):
        return jnp.where(n[s] > 0, j, nj - 1)

    return pl.pallas_call(
        _experts_kernel,
        grid_spec=pltpu.PrefetchScalarGridSpec(
            num_scalar_prefetch=4,
            grid=(n_sb, nj),
            in_specs=[
                pl.BlockSpec(memory_space=pl.ANY),
                pl.BlockSpec((None, D_MODEL, tf), lambda s, j, e, b, n, tl: (e[s], 0, jj(s, j, n))),
                pl.BlockSpec((None, D_MODEL, tf), lambda s, j, e, b, n, tl: (e[s], 0, nj + jj(s, j, n))),
                pl.BlockSpec((None, tf, D_MODEL), lambda s, j, e, b, n, tl: (e[s], jj(s, j, n), 0)),
                pl.BlockSpec((None, 1, tf), lambda s, j, e, b, n, tl: (e[s], 0, jj(s, j, n))),
                pl.BlockSpec((None, 1, tf), lambda s, j, e, b, n, tl: (e[s], 0, nj + jj(s, j, n))),
                pl.BlockSpec((None, 1, D_MODEL), lambda s, j, e, b, n, tl: (e[s], 0, 0)),
            ],
            out_specs=pl.BlockSpec(memory_space=pl.ANY),
            scratch_shapes=[
                pltpu.VMEM((rmax, D_MODEL), F32),
                pltpu.VMEM((rmax, D_MODEL), BF16),
                pltpu.VMEM((rmax, D_MODEL), F32),
                pltpu.SemaphoreType.DMA(()),
                pltpu.SemaphoreType.DMA(()),
            ],
        ),
        out_shape=jax.ShapeDtypeStruct((n_rows, D_MODEL), F32),
        compiler_params=_cp(("arbitrary", "arbitrary")),
        name="moe_experts",
    )(sb_e, sb_blk0, sb_nblk, tail, xs, w_gate_up, w_gate_up, w_down,
      b_gate_up.reshape(N_EXPERTS, 1, 2 * D_MODEL), b_gate_up.reshape(N_EXPERTS, 1, 2 * D_MODEL),
      b_down.reshape(N_EXPERTS, 1, D_MODEL))


COMBINE_TOK = 128


def _combine_kernel(pos_ref, posn_ref, ys_hbm, x_ref, gt_ref, w_ref, o_ref, buf, sem):
    n = COMBINE_TOK * TOP_K
    i = pl.program_id(0)
    slot = lax.rem(i, 2)

    def issue(p_ref, sl):
        def start(a, c):
            pltpu.make_async_copy(ys_hbm.at[pl.ds(p_ref[0, a], 1)], buf.at[sl, pl.ds(a, 1)], sem.at[sl]).start()
            return c
        lax.fori_loop(0, n, start, 0, unroll=DMA_ROWS)

    @pl.when(i == 0)
    def _():
        issue(pos_ref, 0)

    @pl.when(i + 1 < pl.num_programs(0))
    def _():
        issue(posn_ref, 1 - slot)

    pltpu.make_async_copy(ys_hbm.at[pl.ds(0, n)], buf.at[slot], sem.at[slot]).wait()
    w = w_ref[...]
    lane = lax.broadcasted_iota(jnp.int32, w.shape, 1)
    y = jnp.zeros((COMBINE_TOK, D_MODEL), F32)
    for k in range(TOP_K):
        wk = jnp.sum(jnp.where(lane == k, w, 0.0), axis=-1, keepdims=True)
        y = y + wk * buf[slot, k * COMBINE_TOK:(k + 1) * COMBINE_TOK, :]
    o_ref[...] = x_ref[...] + gt_ref[...] * y


def _combine(ys, pos_kmajor, x2d, gt, top_w, rows_per_mod):
    m = x2d.shape[0]
    tm = COMBINE_TOK
    steps = m // tm
    if rows_per_mod == 1:
        gt = gt.reshape(m, D_MODEL)
        gt_spec = pl.BlockSpec((tm, D_MODEL), lambda i: (i, 0))
    else:
        gt_spec = pl.BlockSpec((None, 1, D_MODEL), lambda i: (i // (rows_per_mod // tm), 0, 0))
    return pl.pallas_call(
        _combine_kernel,
        grid=(steps,),
        in_specs=[
            pl.BlockSpec((None, 1, TOP_K * tm), lambda i: (i, 0, 0), memory_space=pltpu.SMEM),
            pl.BlockSpec((None, 1, TOP_K * tm), lambda i: (jnp.minimum(i + 1, steps - 1), 0, 0), memory_space=pltpu.SMEM),
            pl.BlockSpec(memory_space=pl.ANY),
            pl.BlockSpec((tm, D_MODEL), lambda i: (i, 0)),
            gt_spec,
            pl.BlockSpec((tm, LANE), lambda i: (i, 0)),
        ],
        out_specs=pl.BlockSpec((tm, D_MODEL), lambda i: (i, 0)),
        out_shape=jax.ShapeDtypeStruct((m, D_MODEL), F32),
        scratch_shapes=[pltpu.VMEM((2, TOP_K * tm, D_MODEL), F32), pltpu.SemaphoreType.DMA((2,))],
        compiler_params=_cp(("arbitrary",)),
        name="moe_combine",
    )(pos_kmajor, pos_kmajor, ys, x2d, gt, top_w)


def _routing_tables(top_idx):
    n_tok = top_idx.shape[0]
    n_assign = n_tok * TOP_K
    rb = MOE_ROWS
    n_blocks = -(-(n_assign + N_EXPERTS * (rb - 1)) // rb)
    n_rows = n_blocks * rb
    flat_e = top_idx.reshape(-1)
    onehot = (flat_e[:, None] == jnp.arange(N_EXPERTS, dtype=jnp.int32)[None, :]).astype(jnp.int32)
    csum = jnp.cumsum(onehot, axis=0)
    rank = jnp.sum((csum - onehot) * onehot, axis=1)
    counts = csum[-1]
    nblk_e = (counts + rb - 1) // rb
    blk_start = jnp.cumsum(nblk_e) - nblk_e
    dest = (blk_start * rb)[flat_e] + rank
    total_blk = jnp.sum(nblk_e)
    last_blk = jnp.where(nblk_e > 0, blk_start + nblk_e - 1, -1)
    bidx = jnp.arange(n_blocks, dtype=jnp.int32)
    zero_blocks = jnp.concatenate([last_blk, jnp.where(bidx >= total_blk, bidx, -1)]).astype(jnp.int32)
    n_sb_max = n_blocks // MOE_SB_BLOCKS + N_EXPERTS
    sb_per_e = (nblk_e + MOE_SB_BLOCKS - 1) // MOE_SB_BLOCKS
    sb_start = jnp.cumsum(sb_per_e) - sb_per_e
    total_sb = jnp.sum(sb_per_e)
    sidx = jnp.arange(n_sb_max, dtype=jnp.int32)
    e_of = jnp.clip(jnp.searchsorted(jnp.cumsum(sb_per_e), sidx, side="right"), 0, N_EXPERTS - 1).astype(jnp.int32)
    local = sidx - sb_start[e_of]
    active = sidx < total_sb
    last_e = e_of[jnp.maximum(total_sb - 1, 0)]
    sb_e = jnp.where(active, e_of, last_e).astype(jnp.int32)
    sb_blk0 = jnp.where(active, blk_start[e_of] + local * MOE_SB_BLOCKS, 0).astype(jnp.int32)
    sb_nblk = jnp.where(active, jnp.minimum(nblk_e[e_of] - local * MOE_SB_BLOCKS, MOE_SB_BLOCKS), 0).astype(jnp.int32)
    tail = jnp.stack([total_blk, jnp.int32(n_blocks)]).astype(jnp.int32)
    return dest.astype(jnp.int32), zero_blocks, n_rows, sb_e, sb_blk0, sb_nblk, tail


def _kmajor(pos, tm):
    m = pos.shape[0]
    return pos.reshape(m // tm, tm, TOP_K).transpose(0, 2, 1).reshape(m // tm, 1, TOP_K * tm)


def _repack_w_in(w_in):
    a = DN_CONV_CH + DN_VW
    b = a + 2 * DN_HEADS
    c = b + SW_QW
    e = c + 2 * SW_KVW
    parts = [w_in[:, :a], w_in[:, b:c], w_in[:, e:], w_in[:, c:e], w_in[:, a:b]]
    pad = jnp.zeros((D_MODEL, PROJ_W - w_in.shape[1]), BF16)
    return jnp.concatenate([p.astype(BF16) for p in parts] + [pad], axis=1)


def _lane_vec(v, offset):
    return jnp.zeros((1, LANE), F32).at[0, offset:offset + v.shape[0]].set(v.astype(F32))


def kernel(x_prompt, x_sample, state_conv, state_delta, cache_swa_k, cache_swa_v, c_prompt, c_sample, w_ada, b_ada, ln1_w, w_in, conv_w, dn_a_log, dn_dt_bias, dn_norm_w, sw_q_norm_w, sw_k_norm_w, sw_sinks, w_branch_a, w_branch_b, w_out, ln2_w, router_w, router_b, w_gate_up, b_gate_up, w_down, b_down):
    assert w_ada.shape[0] == 1, "single-layer step"
    bp, t, d = x_prompt.shape
    bs = x_sample.shape[0]
    np_tok = bp * t
    l = 0

    n_c = bp + bs
    c_all = jnp.concatenate([c_prompt, c_sample, jnp.zeros((-n_c % 8, d), F32)], axis=0)
    mod = _ada_mod(c_all, w_ada[l], b_ada[l])
    mods_p = [m.reshape(bp, 1, d) for m in jnp.split(mod[:bp], 6, axis=-1)]
    mods_s = [m.reshape(bs, 1, d) for m in jnp.split(mod[bp:n_c], 6, axis=-1)]

    w_in_r = _repack_w_in(w_in[l])
    wa, wb, wo = w_branch_a[l].astype(BF16), w_branch_b[l].astype(BF16), w_out[l].astype(BF16)
    alog_lane = _lane_vec(dn_a_log[l], DN_HEADS)
    dtb_lane = _lane_vec(dn_dt_bias[l], DN_HEADS)
    rw_pad = jnp.zeros((d, LANE), F32).at[:, :N_EXPERTS].set(router_w[l])
    rb_pad = jnp.zeros((1, LANE), F32).at[0, :N_EXPERTS].set(router_b[l])
    sinks = sw_sinks[l].astype(F32)

    xp = x_prompt.reshape(np_tok, d)
    proj_p = _in_proj(xp, ln1_w[l], mods_p[1], mods_p[0], w_in_r, t, 1024)
    proj3 = proj_p.reshape(bp, t, PROJ_W)
    gates = _gdn_gates(proj3, alog_lane, dtb_lane)
    ya_p, delta_p = _gdn_prompt(proj3, gates, conv_w[l], dn_norm_w[l])
    yb_p, kn_p = _swa_prompt(proj3, sinks, sw_q_norm_w[l], sw_k_norm_w[l])
    x1_p, h2_p, idx_p, tw_p = _post_attention(
        ya_p.reshape(np_tok, DN_VW), yb_p.reshape(np_tok, SW_QW), proj_p, xp, mods_p[2], ln2_w[l], mods_p[4], mods_p[3],
        wa, wb, wo, rw_pad, rb_pad, t, POST_TM)

    xs_ = x_sample.reshape(bs, d)
    proj_s = _in_proj(xs_, ln1_w[l], mods_s[1], mods_s[0], w_in_r, 1, bs)
    ya_s, conv_s, delta_s = _gdn_step(proj_s, state_conv[l], state_delta[l], conv_w[l], alog_lane, dtb_lane, dn_norm_w[l])
    w_buf = cache_swa_k.shape[2]
    yb_s, k_s, v_s = _swa_step(proj_s, cache_swa_k[l].reshape(bs, w_buf, SW_KVW), cache_swa_v[l].reshape(bs, w_buf, SW_KVW),
                               sinks, sw_q_norm_w[l], sw_k_norm_w[l])
    x1_s, h2_s, idx_s, tw_s = _post_attention(
        ya_s.reshape(bs, DN_VW), yb_s.reshape(bs, SW_QW), proj_s, xs_, mods_s[2], ln2_w[l], mods_s[4], mods_s[3],
        wa, wb, wo, rw_pad, rb_pad, 1, bs)

    top_idx = jnp.concatenate([idx_p[:, :TOP_K], idx_s[:, :TOP_K]], axis=0)
    dest, zero_blocks, n_rows, sb_e, sb_blk0, sb_nblk, tail = _routing_tables(top_idx)
    xs_sorted = _scatter_rows(h2_p, h2_s, dest, zero_blocks, n_rows)
    ys = _experts(xs_sorted, sb_e, sb_blk0, sb_nblk, tail, w_gate_up[l], b_gate_up[l], w_down[l], b_down[l])
    pos = dest.reshape(np_tok + bs, TOP_K)
    y_p = _combine(ys, _kmajor(pos[:np_tok], COMBINE_TOK), x1_p, mods_p[5], tw_p, t)
    pad_s = COMBINE_TOK - bs
    pos_s = jnp.concatenate([pos[np_tok:], jnp.zeros((pad_s, TOP_K), jnp.int32)], axis=0)
    x1_s_pad = jnp.concatenate([x1_s, jnp.zeros((pad_s, d), F32)], axis=0)
    gt2_s_pad = jnp.concatenate([mods_s[5].reshape(bs, d), jnp.zeros((pad_s, d), F32)], axis=0)
    tw_s_pad = jnp.concatenate([tw_s, jnp.zeros((pad_s, LANE), F32)], axis=0)
    y_s = _combine(ys, _kmajor(pos_s, COMBINE_TOK), x1_s_pad, gt2_s_pad.reshape(COMBINE_TOK, 1, d), tw_s_pad, 1)[:bs]

    conv_p = proj3[:, t - (DN_CONV - 1):, C_QKV:C_QKV + DN_CONV_CH]
    kp_out = kn_p[:, t - WINDOW:].reshape(bp, WINDOW, SW_KV_HEADS, SW_HD)
    vp_out = proj3[:, t - WINDOW:, C_SV:C_SV + SW_KVW].reshape(bp, WINDOW, SW_KV_HEADS, SW_HD)
    return (
        y_p.reshape(bp, t, d),
        y_s.reshape(bs, 1, d),
        conv_p[None],
        conv_s[None],
        delta_p[None],
        delta_s[None],
        kp_out[None],
        k_s.reshape(bs, w_buf, SW_KV_HEADS, SW_HD)[None],
        vp_out[None],
        v_s.reshape(bs, w_buf, SW_KV_HEADS, SW_HD)[None],
    )
```

```python
import functools

import jax
import jax.numpy as jnp
import numpy as np
from jax import lax
from jax.experimental import pallas as pl
from jax.experimental.pallas import tpu as pltpu

F32 = jnp.float32
BF16 = jnp.bfloat16

D_MODEL = 2048
PAST_LEN = 16384
DN_HEADS = 8
DN_DK = 128
DN_DV = 128
DN_CONV = 4
SW_HEADS = 16
SW_KV_HEADS = 2
SW_HD = 64
SW_GROUP = SW_HEADS // SW_KV_HEADS
WINDOW = 128
N_EXPERTS = 32
TOP_K = 4
SWIGLU_ALPHA = 1.702
SWIGLU_LIMIT = 7.0
EPS = 1e-6

DN_QK = DN_HEADS * DN_DK
DN_VW = DN_HEADS * DN_DV
DN_CONV_CH = 2 * DN_QK + DN_VW
SW_QW = SW_HEADS * SW_HD
SW_KVW = SW_KV_HEADS * SW_HD

LANE = 128
C_QKV = 0
C_Z = DN_CONV_CH
C_SQ = C_Z + DN_VW
C_GA = C_SQ + SW_QW
C_GB = C_GA + D_MODEL
C_SK = C_GB + D_MODEL
C_SV = C_SK + SW_KVW
C_BA = C_SV + SW_KVW
PROJ_W = 10240

GDN_GROUP = 256
GDN_CHUNK = 256
GDN_LEVELS = 8
MOE_ROWS = 128
MOE_SB_BLOCKS = 10
MOE_PASS_BLOCKS = 8
MOE_TF = 512
STEP_B = 4
POST_TM = 256
VMEM_LIMIT = 56 * 1024 * 1024


def _cp(sem, vmem=VMEM_LIMIT):
    return pltpu.CompilerParams(dimension_semantics=sem, vmem_limit_bytes=vmem)


def _dot(a, b):
    return jnp.dot(a.astype(BF16), b.astype(BF16), preferred_element_type=F32)


def _dot_nt(a, b):
    return lax.dot_general(a.astype(BF16), b.astype(BF16), (((1,), (1,)), ((), ())), preferred_element_type=F32)


def _split(a):
    hi = a.astype(BF16)
    lo = (a - hi.astype(F32)).astype(BF16)
    return hi, lo


def _dot3(a, b):
    ah, al = _split(a)
    bh, bl = _split(b)
    d = functools.partial(jnp.dot, preferred_element_type=F32)
    return d(ah, bh) + (d(ah, bl) + d(al, bh))


def _dot3_nt(a, b):
    ah, al = _split(a)
    bh, bl = _split(b)
    d = functools.partial(lax.dot_general, dimension_numbers=(((1,), (1,)), ((), ())), preferred_element_type=F32)
    return d(ah, bh) + (d(ah, bl) + d(al, bh))


def _dot_exact_lhs01(m01, b):
    b1 = b.astype(BF16)
    r = b - b1.astype(F32)
    b2 = r.astype(BF16)
    b3 = (r - b2.astype(F32)).astype(BF16)
    d = functools.partial(jnp.dot, preferred_element_type=F32)
    m = m01.astype(BF16)
    return d(m, b1) + (d(m, b2) + d(m, b3))


def _sigmoid(x):
    return 1.0 / (1.0 + jnp.exp(-x))


def _silu(x):
    return x * _sigmoid(x)


def _softplus(x):
    return jnp.maximum(x, 0.0) + jnp.log(1.0 + jnp.exp(-jnp.abs(x)))


def _ada_kernel(c_ref, w_ref, b_ref, o_ref):
    o_ref[...] = _dot(_silu(c_ref[...]), w_ref[...]) + b_ref[...]


def _ada_mod(c_all, w_ada, b_ada):
    m = c_all.shape[0]
    n = w_ada.shape[1]
    tn = 1024
    return pl.pallas_call(
        _ada_kernel,
        grid=(n // tn,),
        in_specs=[
            pl.BlockSpec((m, D_MODEL), lambda j: (0, 0)),
            pl.BlockSpec((D_MODEL, tn), lambda j: (0, j)),
            pl.BlockSpec((1, tn), lambda j: (0, j)),
        ],
        out_specs=pl.BlockSpec((m, tn), lambda j: (0, j)),
        out_shape=jax.ShapeDtypeStruct((m, n), F32),
        compiler_params=_cp(("arbitrary",)),
        name="ada_mod",
    )(c_all, w_ada, b_ada.reshape(1, n))


def _norm_mod(x, lnw, sc, sh):
    y = x * lax.rsqrt(jnp.mean(x * x, axis=-1, keepdims=True) + EPS)
    return (y * lnw) * (1.0 + sc) + sh


def _inproj_kernel(x_ref, lnw_ref, sc_ref, sh_ref, w_ref, o_ref, h_scr):
    @pl.when(pl.program_id(1) == 0)
    def _():
        h_scr[...] = _norm_mod(x_ref[...], lnw_ref[...], sc_ref[...], sh_ref[...]).astype(BF16)

    o_ref[...] = jnp.dot(h_scr[...], w_ref[...], preferred_element_type=F32)


def _in_proj(x2d, lnw, sc, sh, w_bf16, rows_per_mod, tm):
    m = x2d.shape[0]
    tn = 1024
    if rows_per_mod == 1:
        mod_spec = pl.BlockSpec((tm, D_MODEL), lambda i, j: (i, 0))
        sc, sh = sc.reshape(m, D_MODEL), sh.reshape(m, D_MODEL)
    else:
        assert rows_per_mod % tm == 0
        mod_spec = pl.BlockSpec((None, 1, D_MODEL), lambda i, j: (i // (rows_per_mod // tm), 0, 0))
    return pl.pallas_call(
        _inproj_kernel,
        grid=(m // tm, PROJ_W // tn),
        in_specs=[
            pl.BlockSpec((tm, D_MODEL), lambda i, j: (i, 0)),
            pl.BlockSpec((1, D_MODEL), lambda i, j: (0, 0)),
            mod_spec,
            mod_spec,
            pl.BlockSpec((D_MODEL, tn), lambda i, j: (0, j)),
        ],
        out_specs=pl.BlockSpec((tm, tn), lambda i, j: (i, j)),
        out_shape=jax.ShapeDtypeStruct((m, PROJ_W), F32),
        scratch_shapes=[pltpu.VMEM((tm, D_MODEL), BF16)],
        compiler_params=_cp(("arbitrary", "arbitrary")),
        name="in_proj",
    )(x2d, lnw.reshape(1, D_MODEL), sc, sh, w_bf16)


def _tri_masks(n, chunk):
    r = lax.broadcasted_iota(jnp.int32, (n, n), 0)
    c = lax.broadcasted_iota(jnp.int32, (n, n), 1)
    same = (r // chunk) == (c // chunk)
    return same, same & (r >= c), same & (r > c)


def _gates_kernel(ba_ref, alog_ref, dtb_ref, beta_ref, gc_ref, eg_ref, ek_ref, el_ref, gcrow_ref):
    same, causal, _ = _tri_masks(GDN_GROUP, GDN_CHUNK)
    lower01 = jnp.where(causal, 1.0, 0.0)
    ones01 = jnp.where(same, 1.0, 0.0)
    nega = -jnp.exp(alog_ref[...])
    dtb = dtb_ref[...]
    t = ba_ref.shape[0]

    def body(i, carry):
        r0 = pl.multiple_of(i * GDN_GROUP, GDN_GROUP)
        x = ba_ref[pl.ds(r0, GDN_GROUP), :]
        g = nega * _softplus(x + dtb)
        gc = _dot_exact_lhs01(lower01, g)
        gl = _dot_exact_lhs01(ones01, g)
        beta_ref[pl.ds(r0, GDN_GROUP), :] = _sigmoid(x)
        gc_ref[pl.ds(r0, GDN_GROUP), :] = gc
        eg_ref[pl.ds(r0, GDN_GROUP), :] = jnp.exp(gc)
        ek_ref[pl.ds(r0, GDN_GROUP), :] = jnp.exp(gl - gc)
        el_ref[pl.ds(r0, GDN_GROUP), :] = jnp.exp(gl)
        gct = gc.T
        for h in range(DN_HEADS):
            gcrow_ref[h, :, pl.ds(r0, GDN_GROUP)] = gct[DN_HEADS + h:DN_HEADS + h + 1, :]
        return carry

    lax.fori_loop(0, t // GDN_GROUP, body, 0)


def _gdn_gates(proj3, alog_lane, dtb_lane):
    b, t, _ = proj3.shape
    col = pl.BlockSpec((None, t, LANE), lambda i: (i, 0, 0))
    shp = jax.ShapeDtypeStruct((b, t, LANE), F32)
    return pl.pallas_call(
        _gates_kernel,
        grid=(b,),
        in_specs=[
            pl.BlockSpec((None, t, LANE), lambda i: (i, 0, C_BA // LANE)),
            pl.BlockSpec((1, LANE), lambda i: (0, 0)),
            pl.BlockSpec((1, LANE), lambda i: (0, 0)),
        ],
        out_specs=[col, col, col, col, col, pl.BlockSpec((None, DN_HEADS, 1, t), lambda i: (i, 0, 0, 0))],
        out_shape=[shp, shp, shp, shp, shp, jax.ShapeDtypeStruct((b, DN_HEADS, 1, t), F32)],
        compiler_params=_cp(("arbitrary",)),
        name="gdn_gates",
    )(proj3, alog_lane, dtb_lane)


def _l2norm(x):
    return x * lax.rsqrt(jnp.sum(x * x, axis=-1, keepdims=True) + EPS)


GDN_HPS = 4
GDN_TILE = 1024


def _gdn_kernel(q_ref, k_ref, v_ref, z_ref, beta_ref, gc_ref, eg_ref, ek_ref, el_ref, gcrow_ref,
                cwq_ref, cwk_ref, cwv_ref, nw_ref, o_ref, s_ref,
                pad_scr, hist_scr, qn_scr, kn_scr, vn_scr, oacc_scr, s_scr):
    t = q_ref.shape[0]
    wdt = GDN_HPS * LANE
    h0 = pl.program_id(1) * GDN_HPS
    pad = 8

    @pl.when(pl.program_id(2) == 0)
    def _():
        hist_scr[...] = jnp.zeros(hist_scr.shape, F32)
        s_scr[...] = jnp.zeros(s_scr.shape, F32)

    def conv_silu(stream, u_ref, cw_ref):
        pad_scr[0:pad, :] = hist_scr[stream]
        pad_scr[pad:pad + t, :] = u_ref[...]
        hist_scr[stream] = pad_scr[t:t + pad, :]
        y = cw_ref[DN_CONV - 1:DN_CONV, :] * pad_scr[pad:pad + t, :]
        for i in range(DN_CONV - 1):
            off = pad - (DN_CONV - 1) + i
            y = y + cw_ref[i:i + 1, :] * pad_scr[off:off + t, :]
        return _silu(y)

    yq = conv_silu(0, q_ref, cwq_ref)
    for hh in range(GDN_HPS):
        sl = slice(hh * LANE, (hh + 1) * LANE)
        qn_scr[:, sl] = _l2norm(yq[:, sl]) * (DN_DK ** -0.5)
    yk = conv_silu(1, k_ref, cwk_ref)
    for hh in range(GDN_HPS):
        sl = slice(hh * LANE, (hh + 1) * LANE)
        kn_scr[:, sl] = _l2norm(yk[:, sl])
    vn_scr[...] = conv_silu(2, v_ref, cwv_ref)

    n = GDN_GROUP
    c = GDN_CHUNK
    _, causal, strict = _tri_masks(n, c)
    rr = lax.broadcasted_iota(jnp.int32, (n, n), 0)
    cc = lax.broadcasted_iota(jnp.int32, (n, n), 1)
    eye = jnp.where(rr == cc, 1.0, 0.0)
    lane = lax.broadcasted_iota(jnp.int32, (n, LANE), 1)

    def pick(ref, r0, sel):
        return jnp.sum(jnp.where(sel, ref[pl.ds(r0, n), :], 0.0), axis=-1, keepdims=True)

    def head_group(hh, r0):
        sl = slice(hh * LANE, (hh + 1) * LANE)
        sel_b = lane == h0 + hh
        sel_g = lane == h0 + hh + DN_HEADS
        q = qn_scr[pl.ds(r0, n), sl]
        k = kn_scr[pl.ds(r0, n), sl]
        v = vn_scr[pl.ds(r0, n), sl]
        beta = pick(beta_ref, r0, sel_b)
        gc = pick(gc_ref, r0, sel_g)
        eg = pick(eg_ref, r0, sel_g)
        ek = pick(ek_ref, r0, sel_g)
        el = pick(el_ref, r0, sel_g)
        gcrow = gcrow_ref[hh, :, pl.ds(r0, n)]
        decay = jnp.where(causal, jnp.exp(gc - gcrow), 0.0)
        a_low = jnp.where(strict, beta * _dot_nt(k, k) * decay, 0.0)
        pw = [-a_low]
        for _ in range(GDN_LEVELS - 1):
            pw.append(_dot(pw[-1], pw[-1]))
        fs = [eye + pw[i] + pw[i + 1] + _dot(pw[i], pw[i + 1]) for i in range(0, GDN_LEVELS, 2)]
        while len(fs) > 1:
            fs = [_dot(fs[i], fs[i + 1]) for i in range(0, len(fs), 2)]
        rhs = jnp.concatenate([v * beta, k * (beta * eg)], axis=1)
        sol = _dot(fs[0], rhs)
        value = sol[:, :DN_DV]
        kcum = sol[:, DN_DV:]
        intra = _dot_nt(q, k) * decay
        q_dec = q * eg
        k_dec = k * ek
        for j in range(n // c):
            lo, hi = j * c, (j + 1) * c
            s = s_scr[hh]
            r = _dot(jnp.concatenate([kcum[lo:hi], q_dec[lo:hi]], axis=0), s)
            v_new = value[lo:hi] - r[:c]
            parts = []
            if lo:
                parts.append(jnp.zeros((lo, DN_DV), F32))
            parts.append(v_new)
            if hi < n:
                parts.append(jnp.zeros((n - hi, DN_DV), F32))
            o = r[c:] + _dot(intra[lo:hi], jnp.concatenate(parts, axis=0))
            oacc_scr[pl.ds(r0 + lo, c), sl] = o
            s_scr[hh] = s * el[lo:lo + 1] + _dot(k_dec[lo:hi].T, v_new)

    def body(i, carry):
        r0 = pl.multiple_of(i * n, n)
        for hh in range(GDN_HPS):
            head_group(hh, r0)
        return carry

    lax.fori_loop(0, t // n, body, 0)
    for hh in range(GDN_HPS):
        sl = slice(hh * LANE, (hh + 1) * LANE)
        o = oacc_scr[:, sl]
        y = o * lax.rsqrt(jnp.mean(o * o, axis=-1, keepdims=True) + EPS)
        o_ref[:, sl] = (y * nw_ref[...]) * _silu(z_ref[:, sl])
    s_ref[...] = s_scr[...]


def _gdn_prompt(proj3, gates, conv_w, norm_w):
    b, t, _ = proj3.shape
    beta, gc, eg, ek, el, gcrow = gates
    hps = GDN_HPS
    wdt = hps * LANE
    steps = DN_HEADS // hps
    tt = GDN_TILE
    assert t % tt == 0 and tt % GDN_GROUP == 0

    def colspec(base):
        return pl.BlockSpec((None, tt, wdt), lambda i, j, r, base=base: (i, r, base + j))

    gate = pl.BlockSpec((None, tt, LANE), lambda i, j, r: (i, r, 0))

    def cwspec(base):
        return pl.BlockSpec((DN_CONV, wdt), lambda i, j, r, base=base: (0, base + j))

    return pl.pallas_call(
        _gdn_kernel,
        grid=(b, steps, t // tt),
        in_specs=[
            colspec(0), colspec(steps), colspec(2 * steps), colspec(C_Z // wdt),
            gate, gate, gate, gate, gate,
            pl.BlockSpec((None, hps, 1, tt), lambda i, j, r: (i, j, 0, r)),
            cwspec(0), cwspec(steps), cwspec(2 * steps),
            pl.BlockSpec((1, DN_DV), lambda i, j, r: (0, 0)),
        ],
        out_specs=[
            pl.BlockSpec((None, tt, wdt), lambda i, j, r: (i, r, j)),
            pl.BlockSpec((None, hps, DN_DK, DN_DV), lambda i, j, r: (i, j, 0, 0)),
        ],
        out_shape=[
            jax.ShapeDtypeStruct((b, t, DN_VW), F32),
            jax.ShapeDtypeStruct((b, DN_HEADS, DN_DK, DN_DV), F32),
        ],
        scratch_shapes=[
            pltpu.VMEM((tt + 8, wdt), F32),
            pltpu.VMEM((3, 8, wdt), F32),
            pltpu.VMEM((tt, wdt), F32),
            pltpu.VMEM((tt, wdt), F32),
            pltpu.VMEM((tt, wdt), F32),
            pltpu.VMEM((tt, wdt), F32),
            pltpu.VMEM((hps, DN_DK, DN_DV), F32),
        ],
        compiler_params=_cp(("arbitrary", "arbitrary", "arbitrary")),
        name="gdn_prompt",
    )(proj3, proj3, proj3, proj3, beta, gc, eg, ek, el, gcrow, conv_w, conv_w, conv_w, norm_w.reshape(1, DN_DV))


def _gdn_step_one(p_ref, cprev_ref, s_ref, cw_ref, alog_ref, dtb_ref, nw_ref, o_ref, cnew_ref, snew_ref):
    u = p_ref[:, C_QKV:C_QKV + DN_CONV_CH]
    prev = cprev_ref[...]
    y = cw_ref[DN_CONV - 1:DN_CONV, :] * u
    for i in range(DN_CONV - 1):
        y = y + cw_ref[i:i + 1, :] * prev[i:i + 1, :]
    y = _silu(y)
    cnew_ref[0:DN_CONV - 2, :] = prev[1:DN_CONV - 1, :]
    cnew_ref[DN_CONV - 2:DN_CONV - 1, :] = u
    ba = p_ref[:, C_BA:C_BA + LANE]
    beta_l = _sigmoid(ba)
    a_l = jnp.exp(-jnp.exp(alog_ref[...]) * _softplus(ba + dtb_ref[...]))
    lane = lax.broadcasted_iota(jnp.int32, (1, LANE), 1)
    row8 = lax.broadcasted_iota(jnp.int32, (8, LANE), 0)
    for h in range(DN_HEADS):
        q = _l2norm(y[:, h * DN_DK:(h + 1) * DN_DK]) * (DN_DK ** -0.5)
        k = _l2norm(y[:, DN_QK + h * DN_DK:DN_QK + (h + 1) * DN_DK])
        v = y[:, 2 * DN_QK + h * DN_DV:2 * DN_QK + (h + 1) * DN_DV]
        beta = jnp.sum(jnp.where(lane == h, beta_l, 0.0), axis=-1, keepdims=True)
        a = jnp.sum(jnp.where(lane == h + DN_HEADS, a_l, 0.0), axis=-1, keepdims=True)
        s = s_ref[h]
        kq = jnp.where(row8 == 0, k, jnp.where(row8 == 1, q, 0.0)).T
        kcol, qcol = kq[:, 0:1], kq[:, 1:2]
        v_new = beta * (v - a * jnp.sum(s * kcol, axis=0, keepdims=True))
        o = a * jnp.sum(s * qcol, axis=0, keepdims=True) + jnp.sum(q * k, axis=-1, keepdims=True) * v_new
        snew_ref[h] = s * a + kcol * v_new
        yo = o * lax.rsqrt(jnp.mean(o * o, axis=-1, keepdims=True) + EPS)
        z = p_ref[:, C_Z + h * DN_DV:C_Z + (h + 1) * DN_DV]
        o_ref[:, h * DN_DV:(h + 1) * DN_DV] = (yo * nw_ref[...]) * _silu(z)


def _gdn_step_kernel(p_ref, cprev_ref, s_ref, cw_ref, alog_ref, dtb_ref, nw_ref, o_ref, cnew_ref, snew_ref):
    for bi in range(p_ref.shape[0]):
        _gdn_step_one(p_ref.at[bi], cprev_ref.at[bi], s_ref.at[bi], cw_ref, alog_ref, dtb_ref, nw_ref,
                      o_ref.at[bi], cnew_ref.at[bi], snew_ref.at[bi])


def _gdn_step(proj_s, conv_prev, s0, conv_w, alog_lane, dtb_lane, norm_w):
    b = proj_s.shape[0]
    sb = STEP_B
    assert b % sb == 0
    return pl.pallas_call(
        _gdn_step_kernel,
        grid=(b // sb,),
        in_specs=[
            pl.BlockSpec((sb, 1, PROJ_W), lambda i: (i, 0, 0)),
            pl.BlockSpec((sb, DN_CONV - 1, DN_CONV_CH), lambda i: (i, 0, 0)),
            pl.BlockSpec((sb, DN_HEADS, DN_DK, DN_DV), lambda i: (i, 0, 0, 0)),
            pl.BlockSpec((DN_CONV, DN_CONV_CH), lambda i: (0, 0)),
            pl.BlockSpec((1, LANE), lambda i: (0, 0)),
            pl.BlockSpec((1, LANE), lambda i: (0, 0)),
            pl.BlockSpec((1, DN_DV), lambda i: (0, 0)),
        ],
        out_specs=[
            pl.BlockSpec((sb, 1, DN_VW), lambda i: (i, 0, 0)),
            pl.BlockSpec((sb, DN_CONV - 1, DN_CONV_CH), lambda i: (i, 0, 0)),
            pl.BlockSpec((sb, DN_HEADS, DN_DK, DN_DV), lambda i: (i, 0, 0, 0)),
        ],
        out_shape=[
            jax.ShapeDtypeStruct((b, 1, DN_VW), F32),
            jax.ShapeDtypeStruct((b, DN_CONV - 1, DN_CONV_CH), F32),
            jax.ShapeDtypeStruct((b, DN_HEADS, DN_DK, DN_DV), F32),
        ],
        compiler_params=_cp(("arbitrary",)),
        name="gdn_step",
    )(proj_s.reshape(b, 1, PROJ_W), conv_prev, s0, conv_w, alog_lane, dtb_lane, norm_w.reshape(1, DN_DV))


def _alibi_slope(h):
    return float(2.0 ** (-8.0 * (h + 1) / SW_HEADS))


def _head_rms(x, w):
    return (x * lax.rsqrt(jnp.mean(x * x, axis=-1, keepdims=True) + EPS)) * w


SWA_HB = 4


def _swa_kernel(sinks_ref, q_ref, kc_ref, kp_ref, vc_ref, vp_ref, qw_ref, kw_ref, o_ref, kn_ref):
    blk = pl.program_id(1)
    w = WINDOW
    rows = SWA_HB * w
    qi = lax.broadcasted_iota(jnp.int32, (rows, 2 * w), 0)
    kj = lax.broadcasted_iota(jnp.int32, (rows, 2 * w), 1)
    dist = (qi & (w - 1)) + w - kj
    valid = (dist >= 0) & (dist < w) & ((kj >= w) | (blk > 0))
    distf = dist.astype(F32)
    stripe = lax.broadcasted_iota(jnp.int32, (rows, 1), 0) // w
    kc = kc_ref[...]
    kp = kp_ref[...]
    kbands, vbands = [], []
    for g in range(SW_KV_HEADS):
        sl = slice(g * SW_HD, (g + 1) * SW_HD)
        kcn = _head_rms(kc[:, sl], kw_ref[...])
        kn_ref[:, sl] = kcn
        kbands.append(jnp.concatenate([_head_rms(kp[:, sl], kw_ref[...]), kcn], axis=0))
        vbands.append(jnp.concatenate([vp_ref[:, sl], vc_ref[:, sl]], axis=0))
    for hb in range(SW_HEADS // SWA_HB):
        heads = range(hb * SWA_HB, (hb + 1) * SWA_HB)
        g = heads[0] // SW_GROUP
        qs = jnp.concatenate([_head_rms(q_ref[:, h * SW_HD:(h + 1) * SW_HD], qw_ref[...]) for h in heads], axis=0)
        slope = jnp.zeros((rows, 1), F32)
        sink = jnp.zeros((rows, 1), F32)
        for i, h in enumerate(heads):
            slope = jnp.where(stripe == i, _alibi_slope(h), slope)
            sink = jnp.where(stripe == i, sinks_ref[h], sink)
        s = _dot_nt(qs, kbands[g]) * (SW_HD ** -0.5) - slope * distf
        s = jnp.where(valid, s, -jnp.inf)
        m = jnp.maximum(jnp.max(s, axis=-1, keepdims=True), sink)
        p = jnp.exp(s - m)
        den = jnp.sum(p, axis=-1, keepdims=True) + jnp.exp(sink - m)
        o = _dot(p / den, vbands[g])
        for i, h in enumerate(heads):
            o_ref[:, h * SW_HD:(h + 1) * SW_HD] = o[i * w:(i + 1) * w]


def _swa_prompt(proj3, sinks, qw, kw):
    b, t, _ = proj3.shape
    nb = t // WINDOW
    kcol, vcol = C_SK // LANE, C_SV // LANE

    def cur(col):
        return pl.BlockSpec((None, WINDOW, SW_KVW), lambda i, j, s, col=col: (i, j, col))

    def prev(col):
        return pl.BlockSpec((None, WINDOW, SW_KVW), lambda i, j, s, col=col: (i, jnp.maximum(j - 1, 0), col))

    return pl.pallas_call(
        _swa_kernel,
        grid_spec=pltpu.PrefetchScalarGridSpec(
            num_scalar_prefetch=1,
            grid=(b, nb),
            in_specs=[
                pl.BlockSpec((None, WINDOW, SW_QW), lambda i, j, s: (i, j, C_SQ // SW_QW)),
                cur(kcol), prev(kcol), cur(vcol), prev(vcol),
                pl.BlockSpec((1, SW_HD), lambda i, j, s: (0, 0)),
                pl.BlockSpec((1, SW_HD), lambda i, j, s: (0, 0)),
            ],
            out_specs=[
                pl.BlockSpec((None, WINDOW, SW_QW), lambda i, j, s: (i, j, 0)),
                pl.BlockSpec((None, WINDOW, SW_KVW), lambda i, j, s: (i, j, 0)),
            ],
        ),
        out_shape=[
            jax.ShapeDtypeStruct((b, t, SW_QW), F32),
            jax.ShapeDtypeStruct((b, t, SW_KVW), F32),
        ],
        compiler_params=_cp(("arbitrary", "arbitrary")),
        name="swa_prompt",
    )(sinks, proj3, proj3, proj3, proj3, proj3, qw.reshape(1, SW_HD), kw.reshape(1, SW_HD))


def _swa_step_one(sinks_ref, p_ref, kbuf_ref, vbuf_ref, qw_ref, kw_ref, o_ref, knew_ref, vnew_ref, kcat, vcat):
    w = kbuf_ref.shape[0]
    rows = kcat.shape[0]
    knew = p_ref[:, C_SK:C_SK + SW_KVW]
    vnew = p_ref[:, C_SV:C_SV + SW_KVW]
    kcat[...] = jnp.zeros(kcat.shape, F32)
    vcat[...] = jnp.zeros(vcat.shape, F32)
    kcat[0:w, :] = kbuf_ref[...]
    vcat[0:w, :] = vbuf_ref[...]
    for g in range(SW_KV_HEADS):
        sl = slice(g * SW_HD, (g + 1) * SW_HD)
        kcat[w:w + 1, sl] = _head_rms(knew[:, sl], kw_ref[...])
    vcat[w:w + 1, :] = vnew
    knew_ref[...] = kcat[1:w + 1, :]
    vnew_ref[...] = vcat[1:w + 1, :]
    j = lax.broadcasted_iota(jnp.int32, (SW_GROUP, rows), 1)
    dist = w - j
    valid = (dist >= 0) & (dist < WINDOW)
    distf = dist.astype(F32)
    hrow = lax.broadcasted_iota(jnp.int32, (SW_GROUP, 1), 0)
    for g in range(SW_KV_HEADS):
        sl = slice(g * SW_HD, (g + 1) * SW_HD)
        qs = jnp.zeros((SW_GROUP, SW_HD), F32)
        slope = jnp.zeros((SW_GROUP, 1), F32)
        sink = jnp.zeros((SW_GROUP, 1), F32)
        for i in range(SW_GROUP):
            h = g * SW_GROUP + i
            qh = _head_rms(p_ref[:, C_SQ + h * SW_HD:C_SQ + (h + 1) * SW_HD], qw_ref[...])
            qs = jnp.where(hrow == i, qh, qs)
            slope = jnp.where(hrow == i, _alibi_slope(h), slope)
            sink = jnp.where(hrow == i, sinks_ref[h], sink)
        s = _dot_nt(qs, kcat[:, sl]) * (SW_HD ** -0.5) - slope * distf
        s = jnp.where(valid, s, -jnp.inf)
        m = jnp.maximum(jnp.max(s, axis=-1, keepdims=True), sink)
        p = jnp.exp(s - m)
        den = jnp.sum(p, axis=-1, keepdims=True) + jnp.exp(sink - m)
        o = _dot(p / den, vcat[:, sl])
        for i in range(SW_GROUP):
            h = g * SW_GROUP + i
            o_ref[:, h * SW_HD:(h + 1) * SW_HD] = o[i:i + 1]


def _swa_step_kernel(sinks_ref, p_ref, kbuf_ref, vbuf_ref, qw_ref, kw_ref, o_ref, knew_ref, vnew_ref, kcat, vcat):
    for bi in range(p_ref.shape[0]):
        _swa_step_one(sinks_ref, p_ref.at[bi], kbuf_ref.at[bi], vbuf_ref.at[bi], qw_ref, kw_ref,
                      o_ref.at[bi], knew_ref.at[bi], vnew_ref.at[bi], kcat.at[bi], vcat.at[bi])


def _swa_step(proj_s, kbuf, vbuf, sinks, qw, kw):
    b = proj_s.shape[0]
    w = kbuf.shape[1]
    sb = STEP_B
    assert b % sb == 0
    rows = 2 * w
    buf = pl.BlockSpec((sb, w, SW_KVW), lambda i, s: (i, 0, 0))
    return pl.pallas_call(
        _swa_step_kernel,
        grid_spec=pltpu.PrefetchScalarGridSpec(
            num_scalar_prefetch=1,
            grid=(b // sb,),
            in_specs=[
                pl.BlockSpec((sb, 1, PROJ_W), lambda i, s: (i, 0, 0)),
                buf, buf,
                pl.BlockSpec((1, SW_HD), lambda i, s: (0, 0)),
                pl.BlockSpec((1, SW_HD), lambda i, s: (0, 0)),
            ],
            out_specs=[pl.BlockSpec((sb, 1, SW_QW), lambda i, s: (i, 0, 0)), buf, buf],
            scratch_shapes=[pltpu.VMEM((sb, rows, SW_KVW), F32), pltpu.VMEM((sb, rows, SW_KVW), F32)],
        ),
        out_shape=[
            jax.ShapeDtypeStruct((b, 1, SW_QW), F32),
            jax.ShapeDtypeStruct((b, w, SW_KVW), F32),
            jax.ShapeDtypeStruct((b, w, SW_KVW), F32),
        ],
        compiler_params=_cp(("arbitrary",)),
        name="swa_step",
    )(sinks, proj_s.reshape(b, 1, PROJ_W), kbuf, vbuf, qw.reshape(1, SW_HD), kw.reshape(1, SW_HD))


def _route_top_k(hmod, rw_ref, rb_ref, idx_ref, w_ref):
    logits = _dot3(hmod, rw_ref[...]) + rb_ref[...]
    lane = lax.broadcasted_iota(jnp.int32, logits.shape, 1)
    cur = jnp.where(lane < N_EXPERTS, logits, -jnp.inf)
    vals, idxs = [], []
    for _ in range(TOP_K):
        m = jnp.max(cur, axis=-1, keepdims=True)
        ix = jnp.min(jnp.where(cur == m, lane, LANE), axis=-1, keepdims=True)
        vals.append(m)
        idxs.append(ix)
        cur = jnp.where(lane == ix, -jnp.inf, cur)
    es = [jnp.exp(v - vals[0]) for v in vals]
    den = es[0] + es[1] + es[2] + es[3]
    idx_out = jnp.zeros(logits.shape, jnp.int32)
    w_out = jnp.zeros(logits.shape, F32)
    for k in range(TOP_K):
        idx_out = jnp.where(lane == k, idxs[k], idx_out)
        w_out = jnp.where(lane == k, es[k] / den, w_out)
    idx_ref[...] = idx_out
    w_ref[...] = w_out


def _post_kernel(ya_ref, yb_ref, ga0_ref, ga1_ref, gb0_ref, gb1_ref, x_ref, gt_ref, lnw_ref, sc_ref, sh_ref,
                 wa_ref, wb_ref, wo_ref, rw_ref, rb_ref, x1_ref, h_ref, idx_ref, w_ref):
    a = _dot(ya_ref[...], wa_ref[...])
    b = _dot(yb_ref[...], wb_ref[...])
    ga = jnp.concatenate([ga0_ref[...], ga1_ref[...]], axis=1)
    gb = jnp.concatenate([gb0_ref[...], gb1_ref[...]], axis=1)
    merged = _sigmoid(ga) * a + _sigmoid(gb) * b
    x1 = x_ref[...] + gt_ref[...] * _dot(merged, wo_ref[...])
    x1_ref[...] = x1
    hmod = _norm_mod(x1, lnw_ref[...], sc_ref[...], sh_ref[...])
    h_ref[...] = hmod
    _route_top_k(hmod, rw_ref, rb_ref, idx_ref, w_ref)


def _post_attention(ya, yb, proj, x2d, gt, lnw, sc, sh, wa, wb, wo, rw_pad, rb_pad, rows_per_mod, tm):
    m = x2d.shape[0]
    half = D_MODEL // 2
    assert C_GA % half == 0 and C_GB % half == 0
    if rows_per_mod == 1:
        mod_spec = pl.BlockSpec((tm, D_MODEL), lambda i: (i, 0))
        gt, sc, sh = (v.reshape(m, D_MODEL) for v in (gt, sc, sh))
    else:
        mod_spec = pl.BlockSpec((None, 1, D_MODEL), lambda i: (i // (rows_per_mod // tm), 0, 0))
    row = pl.BlockSpec((tm, D_MODEL), lambda i: (i, 0))
    small = pl.BlockSpec((tm, LANE), lambda i: (i, 0))

    def gate(col):
        return pl.BlockSpec((tm, half), lambda i, col=col: (i, col))

    def resident(shape):
        return pl.BlockSpec(shape, lambda i: (0, 0), pipeline_mode=pl.Buffered(1))

    return pl.pallas_call(
        _post_kernel,
        grid=(m // tm,),
        in_specs=[
            pl.BlockSpec((tm, DN_VW), lambda i: (i, 0)),
            pl.BlockSpec((tm, SW_QW), lambda i: (i, 0)),
            gate(C_GA // half), gate(C_GA // half + 1), gate(C_GB // half), gate(C_GB // half + 1),
            row, mod_spec,
            pl.BlockSpec((1, D_MODEL), lambda i: (0, 0)),
            mod_spec, mod_spec,
            resident((DN_VW, D_MODEL)), resident((SW_QW, D_MODEL)), resident((D_MODEL, D_MODEL)),
            pl.BlockSpec((D_MODEL, LANE), lambda i: (0, 0)),
            pl.BlockSpec((1, LANE), lambda i: (0, 0)),
        ],
        out_specs=[row, row, small, small],
        out_shape=[
            jax.ShapeDtypeStruct((m, D_MODEL), F32),
            jax.ShapeDtypeStruct((m, D_MODEL), F32),
            jax.ShapeDtypeStruct((m, LANE), jnp.int32),
            jax.ShapeDtypeStruct((m, LANE), F32),
        ],
        compiler_params=_cp(("arbitrary",)),
        name="post_attention",
    )(ya, yb, proj, proj, proj, proj, x2d, gt, lnw.reshape(1, D_MODEL), sc, sh, wa, wb, wo, rw_pad, rb_pad)


SCATTER_TOK = 256
DMA_GROUP = 8
DMA_ROWS = 16


def _scatter_kernel(zl_ref, dest_ref, hp_ref, hs_ref, xs_hbm, zbuf, sem, zsem):
    i = pl.program_id(0)
    rb = MOE_ROWS
    n_prompt_steps = pl.num_programs(0) - 1

    @pl.when(i == 0)
    def _():
        zbuf[...] = jnp.zeros(zbuf.shape, F32)

        def zero_copy(n):
            return pltpu.make_async_copy(zbuf, xs_hbm.at[pl.ds(zl_ref[n] * rb, rb)], zsem)

        def start(n, c):
            @pl.when(zl_ref[n] >= 0)
            def _():
                zero_copy(n).start()
            return c

        def wait(n, c):
            @pl.when(zl_ref[n] >= 0)
            def _():
                zero_copy(n).wait()
            return c

        lax.fori_loop(0, zl_ref.shape[0], start, 0)
        lax.fori_loop(0, zl_ref.shape[0], wait, 0)

    def scatter(src_ref):
        n_tok = src_ref.shape[0]

        def group(g, c):
            t0 = pl.multiple_of(g * DMA_GROUP, DMA_GROUP)
            for r in range(DMA_GROUP):
                for k in range(TOP_K):
                    dst = dest_ref[0, g * (DMA_GROUP * TOP_K) + (r * TOP_K + k)]
                    pltpu.make_async_copy(src_ref.at[pl.ds(t0 + r, 1)], xs_hbm.at[pl.ds(dst, 1)], sem).start()
            return c

        lax.fori_loop(0, n_tok // DMA_GROUP, group, 0)
        for _ in range(TOP_K):
            pltpu.make_async_copy(src_ref, xs_hbm.at[pl.ds(0, n_tok)], sem).wait()

    @pl.when(i < n_prompt_steps)
    def _():
        scatter(hp_ref)

    @pl.when(i == n_prompt_steps)
    def _():
        scatter(hs_ref)


def _scatter_rows(h_p, h_s, dest, zero_blocks, n_rows):
    assert TOP_K == 4
    n_p, n_s = h_p.shape[0], h_s.shape[0]
    steps_p = n_p // SCATTER_TOK
    per = SCATTER_TOK * TOP_K
    dest_s = jnp.concatenate([dest[n_p * TOP_K:], jnp.zeros((per - n_s * TOP_K,), jnp.int32)])
    dest3 = jnp.concatenate([dest[:n_p * TOP_K], dest_s]).reshape(steps_p + 1, 1, per)
    return pl.pallas_call(
        _scatter_kernel,
        grid_spec=pltpu.PrefetchScalarGridSpec(
            num_scalar_prefetch=1,
            grid=(steps_p + 1,),
            in_specs=[
                pl.BlockSpec((None, 1, per), lambda i, zl: (i, 0, 0), memory_space=pltpu.SMEM),
                pl.BlockSpec((SCATTER_TOK, D_MODEL), lambda i, zl: (jnp.minimum(i, steps_p - 1), 0)),
                pl.BlockSpec((n_s, D_MODEL), lambda i, zl: (0, 0)),
            ],
            out_specs=pl.BlockSpec(memory_space=pl.ANY),
            scratch_shapes=[
                pltpu.VMEM((MOE_ROWS, D_MODEL), F32),
                pltpu.SemaphoreType.DMA(()),
                pltpu.SemaphoreType.DMA(()),
            ],
        ),
        out_shape=jax.ShapeDtypeStruct((n_rows, D_MODEL), F32),
        compiler_params=_cp(("arbitrary",)),
        name="moe_scatter",
    )(zero_blocks, dest3, h_p, h_s)


def _experts_kernel(sbe_ref, sbb_ref, sbn_ref, tail_ref, xs_hbm, wg_ref, wl_ref, wd_ref, bg_ref, bl_ref, bd_ref,
                    ys_hbm, xf_scr, xb_scr, acc_scr, sem_in, sem_out):
    s = pl.program_id(0)
    j = pl.program_id(1)
    last_s = pl.num_programs(0) - 1
    last_j = pl.num_programs(1) - 1
    nblk = sbn_ref[s]
    blk0 = sbb_ref[s]
    rb = MOE_ROWS

    def in_copy(first_blk, b):
        return pltpu.make_async_copy(xs_hbm.at[pl.ds((first_blk + b) * rb, rb)], xf_scr.at[pl.ds(b * rb, rb)], sem_in)

    def out_copy(first_blk, b):
        return pltpu.make_async_copy(acc_scr.at[pl.ds(b * rb, rb)], ys_hbm.at[pl.ds((first_blk + b) * rb, rb)], sem_out)

    def loop(n, fn):
        def body(b, c):
            fn(b)
            return c
        lax.fori_loop(0, n, body, 0)

    @pl.when(j == 0)
    def _():
        @pl.when(s == 0)
        def _():
            loop(nblk, lambda b: in_copy(blk0, b).start())

        loop(nblk, lambda b: in_copy(blk0, b).wait())

        @pl.when(s > 0)
        def _():
            prev0 = sbb_ref[s - 1]
            loop(sbn_ref[s - 1], lambda b: out_copy(prev0, b).wait())

        def cast(b):
            r0 = pl.multiple_of(b * rb, rb)
            xb_scr[pl.ds(r0, rb), :] = xf_scr[pl.ds(r0, rb), :].astype(BF16)
            acc_scr[pl.ds(r0, rb), :] = jnp.broadcast_to(bd_ref[...], (rb, D_MODEL))
        loop(nblk, cast)

        @pl.when(s < last_s)
        def _():
            nxt0 = sbb_ref[s + 1]
            loop(sbn_ref[s + 1], lambda b: in_copy(nxt0, b).start())

    @pl.when(nblk > 0)
    def _():
        def mlp(b0, nb):
            rows = nb * rb
            r0 = pl.multiple_of(b0 * rb, rb)
            x = xb_scr[pl.ds(r0, rows), :]
            glu = jnp.dot(x, wg_ref[...].astype(BF16), preferred_element_type=F32) + bg_ref[...]
            lin = jnp.dot(x, wl_ref[...].astype(BF16), preferred_element_type=F32) + bl_ref[...]
            glu = jnp.minimum(glu, SWIGLU_LIMIT)
            lin = jnp.clip(lin, -SWIGLU_LIMIT, SWIGLU_LIMIT)
            act = glu * _sigmoid(SWIGLU_ALPHA * glu) * (lin + 1.0)
            acc_scr[pl.ds(r0, rows), :] += jnp.dot(act.astype(BF16), wd_ref[...].astype(BF16),
                                                   preferred_element_type=F32)

            @pl.when(j == last_j)
            def _():
                for b in range(nb):
                    out_copy(blk0, b0 + b).start()

        full = nblk // MOE_PASS_BLOCKS
        loop(full, lambda p: mlp(p * MOE_PASS_BLOCKS, MOE_PASS_BLOCKS))
        done = full * MOE_PASS_BLOCKS
        part = MOE_PASS_BLOCKS // 2
        while part >= 1:
            take = ((nblk - done) // part) * part

            @pl.when(take > 0)
            def _(done=done, part=part):
                mlp(done, part)

            done = done + take
            part //= 2

    @pl.when((s == last_s) & (j == last_j))
    def _():
        loop(nblk, lambda b: out_copy(blk0, b).wait())
        acc_scr[0:rb, :] = jnp.zeros((rb, D_MODEL), F32)

        def zero_copy(b):
            return pltpu.make_async_copy(acc_scr.at[pl.ds(0, rb)], ys_hbm.at[pl.ds(b * rb, rb)], sem_out)

        def start(b, c):
            zero_copy(b).start()
            return c

        def wait(b, c):
            zero_copy(b).wait()
            return c

        lax.fori_loop(tail_ref[0], tail_ref[1], start, 0)
        lax.fori_loop(tail_ref[0], tail_ref[1], wait, 0)


def _experts(xs, sb_e, sb_blk0, sb_nblk, tail, w_gate_up, b_gate_up, w_down, b_down):
    n_rows = xs.shape[0]
    n_sb = sb_e.shape[0]
    tf = MOE_TF
    nj = D_MODEL // tf
    rmax = MOE_SB_BLOCKS * MOE_ROWS

    def jj(s, j, n):
        return jnp.where(n[s] > 0, j, nj - 1)

    return pl.pallas_call(
        _experts_kernel,
        grid_spec=pltpu.PrefetchScalarGridSpec(
            num_scalar_prefetch=4,
            grid=(n_sb, nj),
            in_specs=[
                pl.BlockSpec(memory_space=pl.ANY),
                pl.BlockSpec((None, D_MODEL, tf), lambda s, j, e, b, n, tl: (e[s], 0, jj(s, j, n))),
                pl.BlockSpec((None, D_MODEL, tf), lambda s, j, e, b, n, tl: (e[s], 0, nj + jj(s, j, n))),
                pl.BlockSpec((None, tf, D_MODEL), lambda s, j, e, b, n, tl: (e[s], jj(s, j, n), 0)),
                pl.BlockSpec((None, 1, tf), lambda s, j, e, b, n, tl: (e[s], 0, jj(s, j, n))),
                pl.BlockSpec((None, 1, tf), lambda s, j, e, b, n, tl: (e[s], 0, nj + jj(s, j, n))),
                pl.BlockSpec((None, 1, D_MODEL), lambda s, j, e, b, n, tl: (e[s], 0, 0)),
            ],
            out_specs=pl.BlockSpec(memory_space=pl.ANY),
            scratch_shapes=[
                pltpu.VMEM((rmax, D_MODEL), F32),
                pltpu.VMEM((rmax, D_MODEL), BF16),
                pltpu.VMEM((rmax, D_MODEL), F32),
                pltpu.SemaphoreType.DMA(()),
                pltpu.SemaphoreType.DMA(()),
            ],
        ),
        out_shape=jax.ShapeDtypeStruct((n_rows, D_MODEL), F32),
        compiler_params=_cp(("arbitrary", "arbitrary")),
        name="moe_experts",
    )(sb_e, sb_blk0, sb_nblk, tail, xs, w_gate_up, w_gate_up, w_down,
      b_gate_up.reshape(N_EXPERTS, 1, 2 * D_MODEL), b_gate_up.reshape(N_EXPERTS, 1, 2 * D_MODEL),
      b_down.reshape(N_EXPERTS, 1, D_MODEL))


COMBINE_TOK = 128


def _combine_kernel(pos_ref, posn_ref, ys_hbm, x_ref, gt_ref, w_ref, o_ref, buf, sem):
    tm = x_ref.shape[0]
    n = tm * TOP_K
    i = pl.program_id(0)
    slot = lax.rem(i, 2)

    def issue(p_ref, sl):
        def start(a, c):
            pltpu.make_async_copy(ys_hbm.at[pl.ds(p_ref[0, a], 1)], buf.at[sl, pl.ds(a, 1)], sem.at[sl]).start()
            return c
        lax.fori_loop(0, n, start, 0, unroll=DMA_ROWS)

    @pl.when(i == 0)
    def _():
        issue(pos_ref, 0)

    @pl.when(i + 1 < pl.num_programs(0))
    def _():
        issue(posn_ref, 1 - slot)

    pltpu.make_async_copy(ys_hbm.at[pl.ds(0, n)], buf.at[slot], sem.at[slot]).wait()
    w = w_ref[...]
    lane = lax.broadcasted_iota(jnp.int32, w.shape, 1)
    y = jnp.zeros((tm, D_MODEL), F32)
    for k in range(TOP_K):
        wk = jnp.sum(jnp.where(lane == k, w, 0.0), axis=-1, keepdims=True)
        y = y + wk * buf[slot, k * tm:(k + 1) * tm, :]
    o_ref[...] = x_ref[...] + gt_ref[...] * y


def _combine(ys, pos, x2d, gt, top_w, rows_per_mod, tm):
    m = x2d.shape[0]
    steps = m // tm
    pos_kmajor = _kmajor(pos, tm)
    if rows_per_mod == 1:
        gt = gt.reshape(m, D_MODEL)
        gt_spec = pl.BlockSpec((tm, D_MODEL), lambda i: (i, 0))
    else:
        gt_spec = pl.BlockSpec((None, 1, D_MODEL), lambda i: (i // (rows_per_mod // tm), 0, 0))
    return pl.pallas_call(
        _combine_kernel,
        grid=(steps,),
        in_specs=[
            pl.BlockSpec((None, 1, TOP_K * tm), lambda i: (i, 0, 0), memory_space=pltpu.SMEM),
            pl.BlockSpec((None, 1, TOP_K * tm), lambda i: (jnp.minimum(i + 1, steps - 1), 0, 0), memory_space=pltpu.SMEM),
            pl.BlockSpec(memory_space=pl.ANY),
            pl.BlockSpec((tm, D_MODEL), lambda i: (i, 0)),
            gt_spec,
            pl.BlockSpec((tm, LANE), lambda i: (i, 0)),
        ],
        out_specs=pl.BlockSpec((tm, D_MODEL), lambda i: (i, 0)),
        out_shape=jax.ShapeDtypeStruct((m, D_MODEL), F32),
        scratch_shapes=[pltpu.VMEM((2, TOP_K * tm, D_MODEL), F32), pltpu.SemaphoreType.DMA((2,))],
        compiler_params=_cp(("arbitrary",)),
        name="moe_combine",
    )(pos_kmajor, pos_kmajor, ys, x2d, gt, top_w)


def _routing_tables(top_idx):
    n_tok = top_idx.shape[0]
    n_assign = n_tok * TOP_K
    rb = MOE_ROWS
    n_blocks = -(-(n_assign + N_EXPERTS * (rb - 1)) // rb)
    n_rows = n_blocks * rb
    flat_e = top_idx.reshape(-1)
    onehot = (flat_e[:, None] == jnp.arange(N_EXPERTS, dtype=jnp.int32)[None, :]).astype(jnp.int32)
    csum = jnp.cumsum(onehot, axis=0)
    rank = jnp.sum((csum - onehot) * onehot, axis=1)
    counts = csum[-1]
    nblk_e = (counts + rb - 1) // rb
    blk_start = jnp.cumsum(nblk_e) - nblk_e
    dest = (blk_start * rb)[flat_e] + rank
    total_blk = jnp.sum(nblk_e)
    last_blk = jnp.where(nblk_e > 0, blk_start + nblk_e - 1, -1)
    bidx = jnp.arange(n_blocks, dtype=jnp.int32)
    zero_blocks = jnp.concatenate([last_blk, jnp.where(bidx >= total_blk, bidx, -1)]).astype(jnp.int32)
    n_sb_max = (n_blocks + N_EXPERTS * (MOE_SB_BLOCKS - 1)) // MOE_SB_BLOCKS
    sb_per_e = (nblk_e + MOE_SB_BLOCKS - 1) // MOE_SB_BLOCKS
    sb_start = jnp.cumsum(sb_per_e) - sb_per_e
    total_sb = jnp.sum(sb_per_e)
    sidx = jnp.arange(n_sb_max, dtype=jnp.int32)
    sb_end = jnp.cumsum(sb_per_e)
    e_of = jnp.minimum(jnp.sum((sb_end[None, :] <= sidx[:, None]).astype(jnp.int32), axis=1), N_EXPERTS - 1)
    local = sidx - sb_start[e_of]
    active = sidx < total_sb
    last_e = e_of[jnp.maximum(total_sb - 1, 0)]
    sb_e = jnp.where(active, e_of, last_e).astype(jnp.int32)
    sb_blk0 = jnp.where(active, blk_start[e_of] + local * MOE_SB_BLOCKS, 0).astype(jnp.int32)
    sb_nblk = jnp.where(active, jnp.minimum(nblk_e[e_of] - local * MOE_SB_BLOCKS, MOE_SB_BLOCKS), 0).astype(jnp.int32)
    tail = jnp.stack([total_blk, jnp.int32(n_blocks)]).astype(jnp.int32)
    return dest.astype(jnp.int32), zero_blocks, n_rows, sb_e, sb_blk0, sb_nblk, tail


def _kmajor(pos, tm):
    m = pos.shape[0]
    return pos.reshape(m // tm, tm, TOP_K).transpose(0, 2, 1).reshape(m // tm, 1, TOP_K * tm)


def _repack_w_in(w_in):
    a = DN_CONV_CH + DN_VW
    b = a + 2 * DN_HEADS
    c = b + SW_QW
    e = c + 2 * SW_KVW
    parts = [w_in[:, :a], w_in[:, b:c], w_in[:, e:], w_in[:, c:e], w_in[:, a:b]]
    pad = jnp.zeros((D_MODEL, PROJ_W - w_in.shape[1]), BF16)
    return jnp.concatenate([p.astype(BF16) for p in parts] + [pad], axis=1)


def _lane_vec(v, offset):
    return jnp.zeros((1, LANE), F32).at[0, offset:offset + v.shape[0]].set(v.astype(F32))


def kernel(x_prompt, x_sample, state_conv, state_delta, cache_swa_k, cache_swa_v, c_prompt, c_sample, w_ada, b_ada, ln1_w, w_in, conv_w, dn_a_log, dn_dt_bias, dn_norm_w, sw_q_norm_w, sw_k_norm_w, sw_sinks, w_branch_a, w_branch_b, w_out, ln2_w, router_w, router_b, w_gate_up, b_gate_up, w_down, b_down):
    assert w_ada.shape[0] == 1, "single-layer step"
    bp, t, d = x_prompt.shape
    bs = x_sample.shape[0]
    np_tok = bp * t
    l = 0

    n_c = bp + bs
    c_all = jnp.concatenate([c_prompt, c_sample, jnp.zeros((-n_c % 8, d), F32)], axis=0)
    mod = _ada_mod(c_all, w_ada[l], b_ada[l])
    mods_p = [m.reshape(bp, 1, d) for m in jnp.split(mod[:bp], 6, axis=-1)]
    mods_s = [m.reshape(bs, 1, d) for m in jnp.split(mod[bp:n_c], 6, axis=-1)]

    w_in_r = _repack_w_in(w_in[l])
    wa, wb, wo = w_branch_a[l].astype(BF16), w_branch_b[l].astype(BF16), w_out[l].astype(BF16)
    alog_lane = _lane_vec(dn_a_log[l], DN_HEADS)
    dtb_lane = _lane_vec(dn_dt_bias[l], DN_HEADS)
    rw_pad = jnp.zeros((d, LANE), F32).at[:, :N_EXPERTS].set(router_w[l])
    rb_pad = jnp.zeros((1, LANE), F32).at[0, :N_EXPERTS].set(router_b[l])
    sinks = sw_sinks[l].astype(F32)

    xp = x_prompt.reshape(np_tok, d)
    proj_p = _in_proj(xp, ln1_w[l], mods_p[1], mods_p[0], w_in_r, t, 1024)
    proj3 = proj_p.reshape(bp, t, PROJ_W)
    gates = _gdn_gates(proj3, alog_lane, dtb_lane)
    ya_p, delta_p = _gdn_prompt(proj3, gates, conv_w[l], dn_norm_w[l])
    yb_p, kn_p = _swa_prompt(proj3, sinks, sw_q_norm_w[l], sw_k_norm_w[l])
    x1_p, h2_p, idx_p, tw_p = _post_attention(
        ya_p.reshape(np_tok, DN_VW), yb_p.reshape(np_tok, SW_QW), proj_p, xp, mods_p[2], ln2_w[l], mods_p[4], mods_p[3],
        wa, wb, wo, rw_pad, rb_pad, t, POST_TM)

    xs_ = x_sample.reshape(bs, d)
    proj_s = _in_proj(xs_, ln1_w[l], mods_s[1], mods_s[0], w_in_r, 1, bs)
    ya_s, conv_s, delta_s = _gdn_step(proj_s, state_conv[l], state_delta[l], conv_w[l], alog_lane, dtb_lane, dn_norm_w[l])
    w_buf = cache_swa_k.shape[2]
    yb_s, k_s, v_s = _swa_step(proj_s, cache_swa_k[l].reshape(bs, w_buf, SW_KVW), cache_swa_v[l].reshape(bs, w_buf, SW_KVW),
                               sinks, sw_q_norm_w[l], sw_k_norm_w[l])
    x1_s, h2_s, idx_s, tw_s = _post_attention(
        ya_s.reshape(bs, DN_VW), yb_s.reshape(bs, SW_QW), proj_s, xs_, mods_s[2], ln2_w[l], mods_s[4], mods_s[3],
        wa, wb, wo, rw_pad, rb_pad, 1, bs)

    top_idx = jnp.concatenate([idx_p[:, :TOP_K], idx_s[:, :TOP_K]], axis=0)
    dest, zero_blocks, n_rows, sb_e, sb_blk0, sb_nblk, tail = _routing_tables(top_idx)
    xs_sorted = _scatter_rows(h2_p, h2_s, dest, zero_blocks, n_rows)
    ys = _experts(xs_sorted, sb_e, sb_blk0, sb_nblk, tail, w_gate_up[l], b_gate_up[l], w_down[l], b_down[l])
    pos = dest.reshape(np_tok + bs, TOP_K)
    y_p = _combine(ys, pos[:np_tok], x1_p, mods_p[5], tw_p, t, COMBINE_TOK)
    y_s = _combine(ys, pos[np_tok:], x1_s, mods_s[5], tw_s, 1, bs)

    conv_p = proj3[:, t - (DN_CONV - 1):, C_QKV:C_QKV + DN_CONV_CH]
    kp_out = kn_p[:, t - WINDOW:].reshape(bp, WINDOW, SW_KV_HEADS, SW_HD)
    vp_out = proj3[:, t - WINDOW:, C_SV:C_SV + SW_KVW].reshape(bp, WINDOW, SW_KV_HEADS, SW_HD)
    return (
        y_p.reshape(bp, t, d),
        y_s.reshape(bs, 1, d),
        conv_p[None],
        conv_s[None],
        delta_p[None],
        delta_s[None],
        kp_out[None],
        k_s.reshape(bs, w_buf, SW_KV_HEADS, SW_HD)[None],
        vp_out[None],
        v_s.reshape(bs, w_buf, SW_KV_HEADS, SW_HD)[None],
    )
```

```python
import functools

import jax
import jax.numpy as jnp
import numpy as np
from jax import lax
from jax.experimental import pallas as pl
from jax.experimental.pallas import tpu as pltpu

F32 = jnp.float32
BF16 = jnp.bfloat16

D_MODEL = 2048
PAST_LEN = 16384
DN_HEADS = 8
DN_DK = 128
DN_DV = 128
DN_CONV = 4
SW_HEADS = 16
SW_KV_HEADS = 2
SW_HD = 64
SW_GROUP = SW_HEADS // SW_KV_HEADS
WINDOW = 128
N_EXPERTS = 32
TOP_K = 4
SWIGLU_ALPHA = 1.702
SWIGLU_LIMIT = 7.0
EPS = 1e-6

DN_QK = DN_HEADS * DN_DK
DN_VW = DN_HEADS * DN_DV
DN_CONV_CH = 2 * DN_QK + DN_VW
SW_QW = SW_HEADS * SW_HD
SW_KVW = SW_KV_HEADS * SW_HD

LANE = 128
SUBLANE = 8
C_QKV = 0
C_Z = DN_CONV_CH
C_SQ = C_Z + DN_VW
C_GA = C_SQ + SW_QW
C_GB = C_GA + D_MODEL
C_SK = C_GB + D_MODEL
C_SV = C_SK + SW_KVW
C_BA = C_SV + SW_KVW
PROJ_W = 10240

GDN_GROUP = 256
GDN_CHUNK = 256
GDN_LEVELS = 8
MOE_ROWS = 128
MOE_SB_BLOCKS = 10
MOE_PASS_BLOCKS = 8
MOE_TF = 512
STEP_B = 4
POST_TM = 256
VMEM_LIMIT = 56 * 1024 * 1024


def _cp(sem, vmem=VMEM_LIMIT):
    return pltpu.CompilerParams(dimension_semantics=sem, vmem_limit_bytes=vmem)


def _dot(a, b):
    return jnp.dot(a.astype(BF16), b.astype(BF16), preferred_element_type=F32)


def _dot_nt(a, b):
    return lax.dot_general(a.astype(BF16), b.astype(BF16), (((1,), (1,)), ((), ())), preferred_element_type=F32)


def _split(a):
    hi = a.astype(BF16)
    lo = (a - hi.astype(F32)).astype(BF16)
    return hi, lo


def _dot3(a, b):
    ah, al = _split(a)
    bh, bl = _split(b)
    d = functools.partial(jnp.dot, preferred_element_type=F32)
    return d(ah, bh) + (d(ah, bl) + d(al, bh))


def _dot3_nt(a, b):
    ah, al = _split(a)
    bh, bl = _split(b)
    d = functools.partial(lax.dot_general, dimension_numbers=(((1,), (1,)), ((), ())), preferred_element_type=F32)
    return d(ah, bh) + (d(ah, bl) + d(al, bh))


def _dot_exact_lhs01(m01, b):
    b1 = b.astype(BF16)
    r = b - b1.astype(F32)
    b2 = r.astype(BF16)
    b3 = (r - b2.astype(F32)).astype(BF16)
    d = functools.partial(jnp.dot, preferred_element_type=F32)
    m = m01.astype(BF16)
    return d(m, b1) + (d(m, b2) + d(m, b3))


def _sigmoid(x):
    return 1.0 / (1.0 + jnp.exp(-x))


def _silu(x):
    return x * _sigmoid(x)


def _softplus(x):
    return jnp.maximum(x, 0.0) + jnp.log(1.0 + jnp.exp(-jnp.abs(x)))


def _ada_kernel(c_ref, w_ref, b_ref, o_ref):
    o_ref[...] = _dot(_silu(c_ref[...]), w_ref[...]) + b_ref[...]


def _ada_mod(c_all, w_ada, b_ada):
    m = c_all.shape[0]
    n = w_ada.shape[1]
    tn = 1024
    return pl.pallas_call(
        _ada_kernel,
        grid=(n // tn,),
        in_specs=[
            pl.BlockSpec((m, D_MODEL), lambda j: (0, 0)),
            pl.BlockSpec((D_MODEL, tn), lambda j: (0, j)),
            pl.BlockSpec((1, tn), lambda j: (0, j)),
        ],
        out_specs=pl.BlockSpec((m, tn), lambda j: (0, j)),
        out_shape=jax.ShapeDtypeStruct((m, n), F32),
        compiler_params=_cp(("arbitrary",)),
        name="ada_mod",
    )(c_all, w_ada, b_ada.reshape(1, n))


def _norm_mod(x, lnw, sc, sh):
    y = x * lax.rsqrt(jnp.mean(x * x, axis=-1, keepdims=True) + EPS)
    return (y * lnw) * (1.0 + sc) + sh


def _inproj_kernel(x_ref, lnw_ref, sc_ref, sh_ref, w_ref, o_ref, h_scr):
    @pl.when(pl.program_id(1) == 0)
    def _():
        h_scr[...] = _norm_mod(x_ref[...], lnw_ref[...], sc_ref[...], sh_ref[...]).astype(BF16)

    o_ref[...] = jnp.dot(h_scr[...], w_ref[...], preferred_element_type=F32)


def _in_proj(x2d, lnw, sc, sh, w_bf16, rows_per_mod, tm):
    m = x2d.shape[0]
    tn = 1024
    if rows_per_mod == 1:
        mod_spec = pl.BlockSpec((tm, D_MODEL), lambda i, j: (i, 0))
        sc, sh = sc.reshape(m, D_MODEL), sh.reshape(m, D_MODEL)
    else:
        assert rows_per_mod % tm == 0
        mod_spec = pl.BlockSpec((None, 1, D_MODEL), lambda i, j: (i // (rows_per_mod // tm), 0, 0))
    return pl.pallas_call(
        _inproj_kernel,
        grid=(m // tm, PROJ_W // tn),
        in_specs=[
            pl.BlockSpec((tm, D_MODEL), lambda i, j: (i, 0)),
            pl.BlockSpec((1, D_MODEL), lambda i, j: (0, 0)),
            mod_spec,
            mod_spec,
            pl.BlockSpec((D_MODEL, tn), lambda i, j: (0, j)),
        ],
        out_specs=pl.BlockSpec((tm, tn), lambda i, j: (i, j)),
        out_shape=jax.ShapeDtypeStruct((m, PROJ_W), F32),
        scratch_shapes=[pltpu.VMEM((tm, D_MODEL), BF16)],
        compiler_params=_cp(("arbitrary", "arbitrary")),
        name="in_proj",
    )(x2d, lnw.reshape(1, D_MODEL), sc, sh, w_bf16)


def _tri_masks(n, chunk):
    r = lax.broadcasted_iota(jnp.int32, (n, n), 0)
    c = lax.broadcasted_iota(jnp.int32, (n, n), 1)
    same = (r // chunk) == (c // chunk)
    return same, same & (r >= c), same & (r > c)


def _gates_kernel(ba_ref, alog_ref, dtb_ref, beta_ref, gc_ref, eg_ref, ek_ref, el_ref, gcrow_ref):
    same, causal, _ = _tri_masks(GDN_GROUP, GDN_CHUNK)
    lower01 = jnp.where(causal, 1.0, 0.0)
    ones01 = jnp.where(same, 1.0, 0.0)
    nega = -jnp.exp(alog_ref[...])
    dtb = dtb_ref[...]
    t = ba_ref.shape[0]

    def body(i, carry):
        r0 = pl.multiple_of(i * GDN_GROUP, GDN_GROUP)
        x = ba_ref[pl.ds(r0, GDN_GROUP), :]
        g = nega * _softplus(x + dtb)
        gc = _dot_exact_lhs01(lower01, g)
        gl = _dot_exact_lhs01(ones01, g)
        beta_ref[pl.ds(r0, GDN_GROUP), :] = _sigmoid(x)
        gc_ref[pl.ds(r0, GDN_GROUP), :] = gc
        eg_ref[pl.ds(r0, GDN_GROUP), :] = jnp.exp(gc)
        ek_ref[pl.ds(r0, GDN_GROUP), :] = jnp.exp(gl - gc)
        el_ref[pl.ds(r0, GDN_GROUP), :] = jnp.exp(gl)
        gct = gc.T
        for h in range(DN_HEADS):
            gcrow_ref[h, :, pl.ds(r0, GDN_GROUP)] = gct[DN_HEADS + h:DN_HEADS + h + 1, :]
        return carry

    lax.fori_loop(0, t // GDN_GROUP, body, 0)


def _gdn_gates(proj3, alog_lane, dtb_lane):
    b, t, _ = proj3.shape
    col = pl.BlockSpec((None, t, LANE), lambda i: (i, 0, 0))
    shp = jax.ShapeDtypeStruct((b, t, LANE), F32)
    return pl.pallas_call(
        _gates_kernel,
        grid=(b,),
        in_specs=[
            pl.BlockSpec((None, t, LANE), lambda i: (i, 0, C_BA // LANE)),
            pl.BlockSpec((1, LANE), lambda i: (0, 0)),
            pl.BlockSpec((1, LANE), lambda i: (0, 0)),
        ],
        out_specs=[col, col, col, col, col, pl.BlockSpec((None, DN_HEADS, 1, t), lambda i: (i, 0, 0, 0))],
        out_shape=[shp, shp, shp, shp, shp, jax.ShapeDtypeStruct((b, DN_HEADS, 1, t), F32)],
        compiler_params=_cp(("arbitrary",)),
        name="gdn_gates",
    )(proj3, alog_lane, dtb_lane)


def _l2norm(x):
    return x * lax.rsqrt(jnp.sum(x * x, axis=-1, keepdims=True) + EPS)


GDN_HPS = 4
GDN_TILE = 1024


def _gdn_kernel(q_ref, k_ref, v_ref, z_ref, beta_ref, gc_ref, eg_ref, ek_ref, el_ref, gcrow_ref,
                cwq_ref, cwk_ref, cwv_ref, nw_ref, o_ref, s_ref,
                pad_scr, hist_scr, qn_scr, kn_scr, vn_scr, oacc_scr, s_scr):
    t = q_ref.shape[0]
    wdt = GDN_HPS * LANE
    h0 = pl.program_id(1) * GDN_HPS
    pad = 8

    @pl.when(pl.program_id(2) == 0)
    def _():
        hist_scr[...] = jnp.zeros(hist_scr.shape, F32)
        s_scr[...] = jnp.zeros(s_scr.shape, F32)

    def conv_silu(stream, u_ref, cw_ref):
        pad_scr[0:pad, :] = hist_scr[stream]
        pad_scr[pad:pad + t, :] = u_ref[...]
        hist_scr[stream] = pad_scr[t:t + pad, :]
        y = cw_ref[DN_CONV - 1:DN_CONV, :] * pad_scr[pad:pad + t, :]
        for i in range(DN_CONV - 1):
            off = pad - (DN_CONV - 1) + i
            y = y + cw_ref[i:i + 1, :] * pad_scr[off:off + t, :]
        return _silu(y)

    yq = conv_silu(0, q_ref, cwq_ref)
    for hh in range(GDN_HPS):
        sl = slice(hh * LANE, (hh + 1) * LANE)
        qn_scr[:, sl] = _l2norm(yq[:, sl]) * (DN_DK ** -0.5)
    yk = conv_silu(1, k_ref, cwk_ref)
    for hh in range(GDN_HPS):
        sl = slice(hh * LANE, (hh + 1) * LANE)
        kn_scr[:, sl] = _l2norm(yk[:, sl])
    vn_scr[...] = conv_silu(2, v_ref, cwv_ref)

    n = GDN_GROUP
    c = GDN_CHUNK
    _, causal, strict = _tri_masks(n, c)
    rr = lax.broadcasted_iota(jnp.int32, (n, n), 0)
    cc = lax.broadcasted_iota(jnp.int32, (n, n), 1)
    eye = jnp.where(rr == cc, 1.0, 0.0)
    lane = lax.broadcasted_iota(jnp.int32, (n, LANE), 1)

    def pick(ref, r0, sel):
        return jnp.sum(jnp.where(sel, ref[pl.ds(r0, n), :], 0.0), axis=-1, keepdims=True)

    def head_group(hh, r0):
        sl = slice(hh * LANE, (hh + 1) * LANE)
        sel_b = lane == h0 + hh
        sel_g = lane == h0 + hh + DN_HEADS
        q = qn_scr[pl.ds(r0, n), sl]
        k = kn_scr[pl.ds(r0, n), sl]
        v = vn_scr[pl.ds(r0, n), sl]
        beta = pick(beta_ref, r0, sel_b)
        gc = pick(gc_ref, r0, sel_g)
        eg = pick(eg_ref, r0, sel_g)
        ek = pick(ek_ref, r0, sel_g)
        el = pick(el_ref, r0, sel_g)
        gcrow = gcrow_ref[hh, :, pl.ds(r0, n)]
        decay = jnp.where(causal, jnp.exp(gc - gcrow), 0.0)
        a_low = jnp.where(strict, beta * _dot_nt(k, k) * decay, 0.0)
        pw = [-a_low]
        for _ in range(GDN_LEVELS - 1):
            pw.append(_dot(pw[-1], pw[-1]))
        fs = [eye + pw[i] + pw[i + 1] + _dot(pw[i], pw[i + 1]) for i in range(0, GDN_LEVELS, 2)]
        while len(fs) > 1:
            fs = [_dot(fs[i], fs[i + 1]) for i in range(0, len(fs), 2)]
        rhs = jnp.concatenate([v * beta, k * (beta * eg)], axis=1)
        sol = _dot(fs[0], rhs)
        value = sol[:, :DN_DV]
        kcum = sol[:, DN_DV:]
        intra = _dot_nt(q, k) * decay
        q_dec = q * eg
        k_dec = k * ek
        for j in range(n // c):
            lo, hi = j * c, (j + 1) * c
            s = s_scr[hh]
            r = _dot(jnp.concatenate([kcum[lo:hi], q_dec[lo:hi]], axis=0), s)
            v_new = value[lo:hi] - r[:c]
            parts = []
            if lo:
                parts.append(jnp.zeros((lo, DN_DV), F32))
            parts.append(v_new)
            if hi < n:
                parts.append(jnp.zeros((n - hi, DN_DV), F32))
            o = r[c:] + _dot(intra[lo:hi], jnp.concatenate(parts, axis=0))
            oacc_scr[pl.ds(r0 + lo, c), sl] = o
            s_scr[hh] = s * el[lo:lo + 1] + _dot(k_dec[lo:hi].T, v_new)

    def body(i, carry):
        r0 = pl.multiple_of(i * n, n)
        for hh in range(GDN_HPS):
            head_group(hh, r0)
        return carry

    lax.fori_loop(0, t // n, body, 0)
    for hh in range(GDN_HPS):
        sl = slice(hh * LANE, (hh + 1) * LANE)
        o = oacc_scr[:, sl]
        y = o * lax.rsqrt(jnp.mean(o * o, axis=-1, keepdims=True) + EPS)
        o_ref[:, sl] = (y * nw_ref[...]) * _silu(z_ref[:, sl])
    s_ref[...] = s_scr[...]


def _gdn_prompt(proj3, gates, conv_w, norm_w):
    b, t, _ = proj3.shape
    beta, gc, eg, ek, el, gcrow = gates
    hps = GDN_HPS
    wdt = hps * LANE
    steps = DN_HEADS // hps
    tt = GDN_TILE
    assert t % tt == 0 and tt % GDN_GROUP == 0

    def colspec(base):
        return pl.BlockSpec((None, tt, wdt), lambda i, j, r, base=base: (i, r, base + j))

    gate = pl.BlockSpec((None, tt, LANE), lambda i, j, r: (i, r, 0))

    def cwspec(base):
        return pl.BlockSpec((DN_CONV, wdt), lambda i, j, r, base=base: (0, base + j))

    return pl.pallas_call(
        _gdn_kernel,
        grid=(b, steps, t // tt),
        in_specs=[
            colspec(0), colspec(steps), colspec(2 * steps), colspec(C_Z // wdt),
            gate, gate, gate, gate, gate,
            pl.BlockSpec((None, hps, 1, tt), lambda i, j, r: (i, j, 0, r)),
            cwspec(0), cwspec(steps), cwspec(2 * steps),
            pl.BlockSpec((1, DN_DV), lambda i, j, r: (0, 0)),
        ],
        out_specs=[
            pl.BlockSpec((None, tt, wdt), lambda i, j, r: (i, r, j)),
            pl.BlockSpec((None, hps, DN_DK, DN_DV), lambda i, j, r: (i, j, 0, 0)),
        ],
        out_shape=[
            jax.ShapeDtypeStruct((b, t, DN_VW), F32),
            jax.ShapeDtypeStruct((b, DN_HEADS, DN_DK, DN_DV), F32),
        ],
        scratch_shapes=[
            pltpu.VMEM((tt + 8, wdt), F32),
            pltpu.VMEM((3, 8, wdt), F32),
            pltpu.VMEM((tt, wdt), F32),
            pltpu.VMEM((tt, wdt), F32),
            pltpu.VMEM((tt, wdt), F32),
            pltpu.VMEM((tt, wdt), F32),
            pltpu.VMEM((hps, DN_DK, DN_DV), F32),
        ],
        compiler_params=_cp(("arbitrary", "arbitrary", "arbitrary")),
        name="gdn_prompt",
    )(proj3, proj3, proj3, proj3, beta, gc, eg, ek, el, gcrow, conv_w, conv_w, conv_w, norm_w.reshape(1, DN_DV))


def _gdn_step_one(p_ref, cprev_ref, s_ref, cw_ref, alog_ref, dtb_ref, nw_ref, o_ref, cnew_ref, snew_ref):
    u = p_ref[:, C_QKV:C_QKV + DN_CONV_CH]
    prev = cprev_ref[...]
    y = cw_ref[DN_CONV - 1:DN_CONV, :] * u
    for i in range(DN_CONV - 1):
        y = y + cw_ref[i:i + 1, :] * prev[i:i + 1, :]
    y = _silu(y)
    cnew_ref[0:DN_CONV - 2, :] = prev[1:DN_CONV - 1, :]
    cnew_ref[DN_CONV - 2:DN_CONV - 1, :] = u
    ba = p_ref[:, C_BA:C_BA + LANE]
    beta_l = _sigmoid(ba)
    a_l = jnp.exp(-jnp.exp(alog_ref[...]) * _softplus(ba + dtb_ref[...]))
    lane = lax.broadcasted_iota(jnp.int32, (1, LANE), 1)
    row8 = lax.broadcasted_iota(jnp.int32, (8, LANE), 0)
    for h in range(DN_HEADS):
        q = _l2norm(y[:, h * DN_DK:(h + 1) * DN_DK]) * (DN_DK ** -0.5)
        k = _l2norm(y[:, DN_QK + h * DN_DK:DN_QK + (h + 1) * DN_DK])
        v = y[:, 2 * DN_QK + h * DN_DV:2 * DN_QK + (h + 1) * DN_DV]
        beta = jnp.sum(jnp.where(lane == h, beta_l, 0.0), axis=-1, keepdims=True)
        a = jnp.sum(jnp.where(lane == h + DN_HEADS, a_l, 0.0), axis=-1, keepdims=True)
        s = s_ref[h]
        kq = jnp.where(row8 == 0, k, jnp.where(row8 == 1, q, 0.0)).T
        kcol, qcol = kq[:, 0:1], kq[:, 1:2]
        v_new = beta * (v - a * jnp.sum(s * kcol, axis=0, keepdims=True))
        o = a * jnp.sum(s * qcol, axis=0, keepdims=True) + jnp.sum(q * k, axis=-1, keepdims=True) * v_new
        snew_ref[h] = s * a + kcol * v_new
        yo = o * lax.rsqrt(jnp.mean(o * o, axis=-1, keepdims=True) + EPS)
        z = p_ref[:, C_Z + h * DN_DV:C_Z + (h + 1) * DN_DV]
        o_ref[:, h * DN_DV:(h + 1) * DN_DV] = (yo * nw_ref[...]) * _silu(z)


def _gdn_step_kernel(p_ref, cprev_ref, s_ref, cw_ref, alog_ref, dtb_ref, nw_ref, o_ref, cnew_ref, snew_ref):
    for bi in range(p_ref.shape[0]):
        _gdn_step_one(p_ref.at[bi], cprev_ref.at[bi], s_ref.at[bi], cw_ref, alog_ref, dtb_ref, nw_ref,
                      o_ref.at[bi], cnew_ref.at[bi], snew_ref.at[bi])


def _gdn_step(proj_s, conv_prev, s0, conv_w, alog_lane, dtb_lane, norm_w):
    b = proj_s.shape[0]
    sb = STEP_B
    assert b % sb == 0
    return pl.pallas_call(
        _gdn_step_kernel,
        grid=(b // sb,),
        in_specs=[
            pl.BlockSpec((sb, 1, PROJ_W), lambda i: (i, 0, 0)),
            pl.BlockSpec((sb, DN_CONV - 1, DN_CONV_CH), lambda i: (i, 0, 0)),
            pl.BlockSpec((sb, DN_HEADS, DN_DK, DN_DV), lambda i: (i, 0, 0, 0)),
            pl.BlockSpec((DN_CONV, DN_CONV_CH), lambda i: (0, 0)),
            pl.BlockSpec((1, LANE), lambda i: (0, 0)),
            pl.BlockSpec((1, LANE), lambda i: (0, 0)),
            pl.BlockSpec((1, DN_DV), lambda i: (0, 0)),
        ],
        out_specs=[
            pl.BlockSpec((sb, 1, DN_VW), lambda i: (i, 0, 0)),
            pl.BlockSpec((sb, DN_CONV - 1, DN_CONV_CH), lambda i: (i, 0, 0)),
            pl.BlockSpec((sb, DN_HEADS, DN_DK, DN_DV), lambda i: (i, 0, 0, 0)),
        ],
        out_shape=[
            jax.ShapeDtypeStruct((b, 1, DN_VW), F32),
            jax.ShapeDtypeStruct((b, DN_CONV - 1, DN_CONV_CH), F32),
            jax.ShapeDtypeStruct((b, DN_HEADS, DN_DK, DN_DV), F32),
        ],
        compiler_params=_cp(("arbitrary",)),
        name="gdn_step",
    )(proj_s.reshape(b, 1, PROJ_W), conv_prev, s0, conv_w, alog_lane, dtb_lane, norm_w.reshape(1, DN_DV))


def _alibi_slope(h):
    return float(2.0 ** (-8.0 * (h + 1) / SW_HEADS))


def _head_rms(x, w):
    return (x * lax.rsqrt(jnp.mean(x * x, axis=-1, keepdims=True) + EPS)) * w


SWA_HB = 4


def _swa_kernel(sinks_ref, q_ref, kc_ref, kp_ref, vc_ref, vp_ref, qw_ref, kw_ref, o_ref, kn_ref):
    blk = pl.program_id(1)
    w = WINDOW
    rows = SWA_HB * w
    qi = lax.broadcasted_iota(jnp.int32, (rows, 2 * w), 0)
    kj = lax.broadcasted_iota(jnp.int32, (rows, 2 * w), 1)
    dist = (qi & (w - 1)) + w - kj
    valid = (dist >= 0) & (dist < w) & ((kj >= w) | (blk > 0))
    distf = dist.astype(F32)
    stripe = lax.broadcasted_iota(jnp.int32, (rows, 1), 0) // w
    kc = kc_ref[...]
    kp = kp_ref[...]
    kbands, vbands = [], []
    for g in range(SW_KV_HEADS):
        sl = slice(g * SW_HD, (g + 1) * SW_HD)
        kcn = _head_rms(kc[:, sl], kw_ref[...])
        kn_ref[:, sl] = kcn
        kbands.append(jnp.concatenate([_head_rms(kp[:, sl], kw_ref[...]), kcn], axis=0))
        vbands.append(jnp.concatenate([vp_ref[:, sl], vc_ref[:, sl]], axis=0))
    for hb in range(SW_HEADS // SWA_HB):
        heads = range(hb * SWA_HB, (hb + 1) * SWA_HB)
        g = heads[0] // SW_GROUP
        qs = jnp.concatenate([_head_rms(q_ref[:, h * SW_HD:(h + 1) * SW_HD], qw_ref[...]) for h in heads], axis=0)
        slope = jnp.zeros((rows, 1), F32)
        sink = jnp.zeros((rows, 1), F32)
        for i, h in enumerate(heads):
            slope = jnp.where(stripe == i, _alibi_slope(h), slope)
            sink = jnp.where(stripe == i, sinks_ref[h], sink)
        s = _dot_nt(qs, kbands[g]) * (SW_HD ** -0.5) - slope * distf
        s = jnp.where(valid, s, -jnp.inf)
        m = jnp.maximum(jnp.max(s, axis=-1, keepdims=True), sink)
        p = jnp.exp(s - m)
        den = jnp.sum(p, axis=-1, keepdims=True) + jnp.exp(sink - m)
        o = _dot(p / den, vbands[g])
        for i, h in enumerate(heads):
            o_ref[:, h * SW_HD:(h + 1) * SW_HD] = o[i * w:(i + 1) * w]


def _swa_prompt(proj3, sinks, qw, kw):
    b, t, _ = proj3.shape
    nb = t // WINDOW
    kcol, vcol = C_SK // LANE, C_SV // LANE

    def cur(col):
        return pl.BlockSpec((None, WINDOW, SW_KVW), lambda i, j, s, col=col: (i, j, col))

    def prev(col):
        return pl.BlockSpec((None, WINDOW, SW_KVW), lambda i, j, s, col=col: (i, jnp.maximum(j - 1, 0), col))

    return pl.pallas_call(
        _swa_kernel,
        grid_spec=pltpu.PrefetchScalarGridSpec(
            num_scalar_prefetch=1,
            grid=(b, nb),
            in_specs=[
                pl.BlockSpec((None, WINDOW, SW_QW), lambda i, j, s: (i, j, C_SQ // SW_QW)),
                cur(kcol), prev(kcol), cur(vcol), prev(vcol),
                pl.BlockSpec((1, SW_HD), lambda i, j, s: (0, 0)),
                pl.BlockSpec((1, SW_HD), lambda i, j, s: (0, 0)),
            ],
            out_specs=[
                pl.BlockSpec((None, WINDOW, SW_QW), lambda i, j, s: (i, j, 0)),
                pl.BlockSpec((None, WINDOW, SW_KVW), lambda i, j, s: (i, j, 0)),
            ],
        ),
        out_shape=[
            jax.ShapeDtypeStruct((b, t, SW_QW), F32),
            jax.ShapeDtypeStruct((b, t, SW_KVW), F32),
        ],
        compiler_params=_cp(("arbitrary", "arbitrary")),
        name="swa_prompt",
    )(sinks, proj3, proj3, proj3, proj3, proj3, qw.reshape(1, SW_HD), kw.reshape(1, SW_HD))


def _swa_step_one(sinks_ref, p_ref, kbuf_ref, vbuf_ref, qw_ref, kw_ref, o_ref, knew_ref, vnew_ref, kcat, vcat):
    w = kbuf_ref.shape[0]
    rows = kcat.shape[0]
    knew = p_ref[:, C_SK:C_SK + SW_KVW]
    vnew = p_ref[:, C_SV:C_SV + SW_KVW]
    kcat[...] = jnp.zeros(kcat.shape, F32)
    vcat[...] = jnp.zeros(vcat.shape, F32)
    kcat[0:w, :] = kbuf_ref[...]
    vcat[0:w, :] = vbuf_ref[...]
    for g in range(SW_KV_HEADS):
        sl = slice(g * SW_HD, (g + 1) * SW_HD)
        kcat[w:w + 1, sl] = _head_rms(knew[:, sl], kw_ref[...])
    vcat[w:w + 1, :] = vnew
    knew_ref[...] = kcat[1:w + 1, :]
    vnew_ref[...] = vcat[1:w + 1, :]
    j = lax.broadcasted_iota(jnp.int32, (SW_GROUP, rows), 1)
    dist = w - j
    valid = (dist >= 0) & (dist < WINDOW)
    distf = dist.astype(F32)
    hrow = lax.broadcasted_iota(jnp.int32, (SW_GROUP, 1), 0)
    for g in range(SW_KV_HEADS):
        sl = slice(g * SW_HD, (g + 1) * SW_HD)
        qs = jnp.zeros((SW_GROUP, SW_HD), F32)
        slope = jnp.zeros((SW_GROUP, 1), F32)
        sink = jnp.zeros((SW_GROUP, 1), F32)
        for i in range(SW_GROUP):
            h = g * SW_GROUP + i
            qh = _head_rms(p_ref[:, C_SQ + h * SW_HD:C_SQ + (h + 1) * SW_HD], qw_ref[...])
            qs = jnp.where(hrow == i, qh, qs)
            slope = jnp.where(hrow == i, _alibi_slope(h), slope)
            sink = jnp.where(hrow == i, sinks_ref[h], sink)
        s = _dot_nt(qs, kcat[:, sl]) * (SW_HD ** -0.5) - slope * distf
        s = jnp.where(valid, s, -jnp.inf)
        m = jnp.maximum(jnp.max(s, axis=-1, keepdims=True), sink)
        p = jnp.exp(s - m)
        den = jnp.sum(p, axis=-1, keepdims=True) + jnp.exp(sink - m)
        o = _dot(p / den, vcat[:, sl])
        for i in range(SW_GROUP):
            h = g * SW_GROUP + i
            o_ref[:, h * SW_HD:(h + 1) * SW_HD] = o[i:i + 1]


def _swa_step_kernel(sinks_ref, p_ref, kbuf_ref, vbuf_ref, qw_ref, kw_ref, o_ref, knew_ref, vnew_ref, kcat, vcat):
    for bi in range(p_ref.shape[0]):
        _swa_step_one(sinks_ref, p_ref.at[bi], kbuf_ref.at[bi], vbuf_ref.at[bi], qw_ref, kw_ref,
                      o_ref.at[bi], knew_ref.at[bi], vnew_ref.at[bi], kcat.at[bi], vcat.at[bi])


def _swa_step(proj_s, kbuf, vbuf, sinks, qw, kw):
    b = proj_s.shape[0]
    w = kbuf.shape[1]
    sb = STEP_B
    assert b % sb == 0
    rows = 2 * w
    buf = pl.BlockSpec((sb, w, SW_KVW), lambda i, s: (i, 0, 0))
    return pl.pallas_call(
        _swa_step_kernel,
        grid_spec=pltpu.PrefetchScalarGridSpec(
            num_scalar_prefetch=1,
            grid=(b // sb,),
            in_specs=[
                pl.BlockSpec((sb, 1, PROJ_W), lambda i, s: (i, 0, 0)),
                buf, buf,
                pl.BlockSpec((1, SW_HD), lambda i, s: (0, 0)),
                pl.BlockSpec((1, SW_HD), lambda i, s: (0, 0)),
            ],
            out_specs=[pl.BlockSpec((sb, 1, SW_QW), lambda i, s: (i, 0, 0)), buf, buf],
            scratch_shapes=[pltpu.VMEM((sb, rows, SW_KVW), F32), pltpu.VMEM((sb, rows, SW_KVW), F32)],
        ),
        out_shape=[
            jax.ShapeDtypeStruct((b, 1, SW_QW), F32),
            jax.ShapeDtypeStruct((b, w, SW_KVW), F32),
            jax.ShapeDtypeStruct((b, w, SW_KVW), F32),
        ],
        compiler_params=_cp(("arbitrary",)),
        name="swa_step",
    )(sinks, proj_s.reshape(b, 1, PROJ_W), kbuf, vbuf, qw.reshape(1, SW_HD), kw.reshape(1, SW_HD))


def _route_top_k(hmod, rw_ref, rb_ref, idx_ref, w_ref):
    logits = _dot3(hmod, rw_ref[...]) + rb_ref[...]
    lane = lax.broadcasted_iota(jnp.int32, logits.shape, 1)
    cur = jnp.where(lane < N_EXPERTS, logits, -jnp.inf)
    vals, idxs = [], []
    for _ in range(TOP_K):
        m = jnp.max(cur, axis=-1, keepdims=True)
        ix = jnp.min(jnp.where(cur == m, lane, LANE), axis=-1, keepdims=True)
        vals.append(m)
        idxs.append(ix)
        cur = jnp.where(lane == ix, -jnp.inf, cur)
    es = [jnp.exp(v - vals[0]) for v in vals]
    den = es[0] + es[1] + es[2] + es[3]
    idx_out = jnp.zeros(logits.shape, jnp.int32)
    w_out = jnp.zeros(logits.shape, F32)
    for k in range(TOP_K):
        idx_out = jnp.where(lane == k, idxs[k], idx_out)
        w_out = jnp.where(lane == k, es[k] / den, w_out)
    idx_ref[...] = idx_out
    w_ref[...] = w_out


def _post_kernel(ya_ref, yb_ref, ga0_ref, ga1_ref, gb0_ref, gb1_ref, x_ref, gt_ref, lnw_ref, sc_ref, sh_ref,
                 wa_ref, wb_ref, wo_ref, rw_ref, rb_ref, x1_ref, h_ref, idx_ref, w_ref):
    a = _dot(ya_ref[...], wa_ref[...])
    b = _dot(yb_ref[...], wb_ref[...])
    ga = jnp.concatenate([ga0_ref[...], ga1_ref[...]], axis=1)
    gb = jnp.concatenate([gb0_ref[...], gb1_ref[...]], axis=1)
    merged = _sigmoid(ga) * a + _sigmoid(gb) * b
    x1 = x_ref[...] + gt_ref[...] * _dot(merged, wo_ref[...])
    x1_ref[...] = x1
    hmod = _norm_mod(x1, lnw_ref[...], sc_ref[...], sh_ref[...])
    h_ref[...] = hmod
    _route_top_k(hmod, rw_ref, rb_ref, idx_ref, w_ref)


def _post_attention(ya, yb, proj, x2d, gt, lnw, sc, sh, wa, wb, wo, rw_pad, rb_pad, rows_per_mod, tm):
    m = x2d.shape[0]
    half = D_MODEL // 2
    assert C_GA % half == 0 and C_GB % half == 0
    if rows_per_mod == 1:
        mod_spec = pl.BlockSpec((tm, D_MODEL), lambda i: (i, 0))
        gt, sc, sh = (v.reshape(m, D_MODEL) for v in (gt, sc, sh))
    else:
        mod_spec = pl.BlockSpec((None, 1, D_MODEL), lambda i: (i // (rows_per_mod // tm), 0, 0))
    row = pl.BlockSpec((tm, D_MODEL), lambda i: (i, 0))
    small = pl.BlockSpec((tm, LANE), lambda i: (i, 0))

    def gate(col):
        return pl.BlockSpec((tm, half), lambda i, col=col: (i, col))

    def resident(shape):
        return pl.BlockSpec(shape, lambda i: (0, 0), pipeline_mode=pl.Buffered(1))

    return pl.pallas_call(
        _post_kernel,
        grid=(m // tm,),
        in_specs=[
            pl.BlockSpec((tm, DN_VW), lambda i: (i, 0)),
            pl.BlockSpec((tm, SW_QW), lambda i: (i, 0)),
            gate(C_GA // half), gate(C_GA // half + 1), gate(C_GB // half), gate(C_GB // half + 1),
            row, mod_spec,
            pl.BlockSpec((1, D_MODEL), lambda i: (0, 0)),
            mod_spec, mod_spec,
            resident((DN_VW, D_MODEL)), resident((SW_QW, D_MODEL)), resident((D_MODEL, D_MODEL)),
            pl.BlockSpec((D_MODEL, LANE), lambda i: (0, 0)),
            pl.BlockSpec((1, LANE), lambda i: (0, 0)),
        ],
        out_specs=[row, row, small, small],
        out_shape=[
            jax.ShapeDtypeStruct((m, D_MODEL), F32),
            jax.ShapeDtypeStruct((m, D_MODEL), F32),
            jax.ShapeDtypeStruct((m, LANE), jnp.int32),
            jax.ShapeDtypeStruct((m, LANE), F32),
        ],
        compiler_params=_cp(("arbitrary",)),
        name="post_attention",
    )(ya, yb, proj, proj, proj, proj, x2d, gt, lnw.reshape(1, D_MODEL), sc, sh, wa, wb, wo, rw_pad, rb_pad)


SCATTER_TOK = 256
DMA_ROWS = 16


def _scatter_kernel(zl_ref, dest_ref, hp_ref, hs_ref, xs_hbm, zbuf, sem, zsem):
    i = pl.program_id(0)
    rb = MOE_ROWS
    n_prompt_steps = pl.num_programs(0) - 1

    @pl.when(i == 0)
    def _():
        zbuf[...] = jnp.zeros(zbuf.shape, F32)

        def zero_copy(n):
            return pltpu.make_async_copy(zbuf, xs_hbm.at[pl.ds(zl_ref[n] * rb, rb)], zsem)

        def start(n, c):
            @pl.when(zl_ref[n] >= 0)
            def _():
                zero_copy(n).start()
            return c

        def wait(n, c):
            @pl.when(zl_ref[n] >= 0)
            def _():
                zero_copy(n).wait()
            return c

        lax.fori_loop(0, zl_ref.shape[0], start, 0)
        lax.fori_loop(0, zl_ref.shape[0], wait, 0)

    def scatter(src_ref):
        def group(g, c):
            for r in range(SUBLANE):
                for k in range(TOP_K):
                    dst = dest_ref[0, g * (SUBLANE * TOP_K) + (r * TOP_K + k)]
                    pltpu.make_async_copy(src_ref.at[g, pl.ds(r, 1)], xs_hbm.at[pl.ds(dst, 1)], sem).start()
            return c

        lax.fori_loop(0, src_ref.shape[0], group, 0)
        for _ in range(TOP_K):
            pltpu.make_async_copy(src_ref, src_ref, sem).wait()

    @pl.when(i < n_prompt_steps)
    def _():
        scatter(hp_ref)

    @pl.when(i == n_prompt_steps)
    def _():
        scatter(hs_ref)


def _scatter_rows(h_p, h_s, dest, zero_blocks, n_rows):
    assert TOP_K == 4
    n_p, n_s = h_p.shape[0], h_s.shape[0]
    steps_p = n_p // SCATTER_TOK
    per = SCATTER_TOK * TOP_K
    dest_s = jnp.concatenate([dest[n_p * TOP_K:], jnp.zeros((per - n_s * TOP_K,), jnp.int32)])
    dest3 = jnp.concatenate([dest[:n_p * TOP_K], dest_s]).reshape(steps_p + 1, 1, per)
    return pl.pallas_call(
        _scatter_kernel,
        grid_spec=pltpu.PrefetchScalarGridSpec(
            num_scalar_prefetch=1,
            grid=(steps_p + 1,),
            in_specs=[
                pl.BlockSpec((None, 1, per), lambda i, zl: (i, 0, 0), memory_space=pltpu.SMEM),
                pl.BlockSpec((SCATTER_TOK // SUBLANE, SUBLANE, D_MODEL), lambda i, zl: (jnp.minimum(i, steps_p - 1), 0, 0)),
                pl.BlockSpec((n_s // SUBLANE, SUBLANE, D_MODEL), lambda i, zl: (0, 0, 0)),
            ],
            out_specs=pl.BlockSpec(memory_space=pl.ANY),
            scratch_shapes=[
                pltpu.VMEM((MOE_ROWS, D_MODEL), F32),
                pltpu.SemaphoreType.DMA(()),
                pltpu.SemaphoreType.DMA(()),
            ],
        ),
        out_shape=jax.ShapeDtypeStruct((n_rows, D_MODEL), F32),
        compiler_params=_cp(("arbitrary",)),
        name="moe_scatter",
    )(zero_blocks, dest3, h_p.reshape(n_p // SUBLANE, SUBLANE, D_MODEL), h_s.reshape(n_s // SUBLANE, SUBLANE, D_MODEL))


def _experts_kernel(sbe_ref, sbb_ref, sbn_ref, tail_ref, xs_hbm, wg_ref, wl_ref, wd_ref, bg_ref, bl_ref, bd_ref,
                    ys_hbm, xf_scr, xb_scr, acc_scr, sem_in, sem_out):
    s = pl.program_id(0)
    j = pl.program_id(1)
    last_s = pl.num_programs(0) - 1
    last_j = pl.num_programs(1) - 1
    nblk = sbn_ref[s]
    blk0 = sbb_ref[s]
    rb = MOE_ROWS

    def in_copy(first_blk, b):
        return pltpu.make_async_copy(xs_hbm.at[pl.ds((first_blk + b) * rb, rb)], xf_scr.at[pl.ds(b * rb, rb)], sem_in)

    def out_copy(first_blk, b):
        return pltpu.make_async_copy(acc_scr.at[pl.ds(b * rb, rb)], ys_hbm.at[pl.ds((first_blk + b) * rb, rb)], sem_out)

    def loop(n, fn):
        def body(b, c):
            fn(b)
            return c
        lax.fori_loop(0, n, body, 0)

    @pl.when(j == 0)
    def _():
        @pl.when(s == 0)
        def _():
            loop(nblk, lambda b: in_copy(blk0, b).start())

        loop(nblk, lambda b: in_copy(blk0, b).wait())

        @pl.when(s > 0)
        def _():
            prev0 = sbb_ref[s - 1]
            loop(sbn_ref[s - 1], lambda b: out_copy(prev0, b).wait())

        def cast(b):
            r0 = pl.multiple_of(b * rb, rb)
            xb_scr[pl.ds(r0, rb), :] = xf_scr[pl.ds(r0, rb), :].astype(BF16)
            acc_scr[pl.ds(r0, rb), :] = jnp.broadcast_to(bd_ref[...], (rb, D_MODEL))
        loop(nblk, cast)

        @pl.when(s < last_s)
        def _():
            nxt0 = sbb_ref[s + 1]
            loop(sbn_ref[s + 1], lambda b: in_copy(nxt0, b).start())

    @pl.when(nblk > 0)
    def _():
        def mlp(b0, nb):
            rows = nb * rb
            r0 = pl.multiple_of(b0 * rb, rb)
            x = xb_scr[pl.ds(r0, rows), :]
            glu = jnp.dot(x, wg_ref[...].astype(BF16), preferred_element_type=F32) + bg_ref[...]
            lin = jnp.dot(x, wl_ref[...].astype(BF16), preferred_element_type=F32) + bl_ref[...]
            glu = jnp.minimum(glu, SWIGLU_LIMIT)
            lin = jnp.clip(lin, -SWIGLU_LIMIT, SWIGLU_LIMIT)
            act = glu * _sigmoid(SWIGLU_ALPHA * glu) * (lin + 1.0)
            acc_scr[pl.ds(r0, rows), :] += jnp.dot(act.astype(BF16), wd_ref[...].astype(BF16),
                                                   preferred_element_type=F32)

            @pl.when(j == last_j)
            def _():
                for b in range(nb):
                    out_copy(blk0, b0 + b).start()

        full = nblk // MOE_PASS_BLOCKS
        loop(full, lambda p: mlp(p * MOE_PASS_BLOCKS, MOE_PASS_BLOCKS))
        done = full * MOE_PASS_BLOCKS
        part = MOE_PASS_BLOCKS // 2
        while part >= 1:
            take = ((nblk - done) // part) * part

            @pl.when(take > 0)
            def _(done=done, part=part):
                mlp(done, part)

            done = done + take
            part //= 2

    @pl.when((s == last_s) & (j == last_j))
    def _():
        loop(nblk, lambda b: out_copy(blk0, b).wait())
        acc_scr[0:rb, :] = jnp.zeros((rb, D_MODEL), F32)

        def zero_copy(b):
            return pltpu.make_async_copy(acc_scr.at[pl.ds(0, rb)], ys_hbm.at[pl.ds(b * rb, rb)], sem_out)

        def start(b, c):
            zero_copy(b).start()
            return c

        def wait(b, c):
            zero_copy(b).wait()
            return c

        lax.fori_loop(tail_ref[0], tail_ref[1], start, 0)
        lax.fori_loop(tail_ref[0], tail_ref[1], wait, 0)


def _experts(xs, sb_e, sb_blk0, sb_nblk, tail, w_gate_up, b_gate_up, w_down, b_down):
    n_rows = xs.shape[0]
    n_sb = sb_e.shape[0]
    tf = MOE_TF
    nj = D_MODEL // tf
    rmax = MOE_SB_BLOCKS * MOE_ROWS

    def jj(s, j, n):
        return jnp.where(n[s] > 0, j, nj - 1)

    return pl.pallas_call(
        _experts_kernel,
        grid_spec=pltpu.PrefetchScalarGridSpec(
            num_scalar_prefetch=4,
            grid=(n_sb, nj),
            in_specs=[
                pl.BlockSpec(memory_space=pl.ANY),
                pl.BlockSpec((None, D_MODEL, tf), lambda s, j, e, b, n, tl: (e[s], 0, jj(s, j, n))),
                pl.BlockSpec((None, D_MODEL, tf), lambda s, j, e, b, n, tl: (e[s], 0, nj + jj(s, j, n))),
                pl.BlockSpec((None, tf, D_MODEL), lambda s, j, e, b, n, tl: (e[s], jj(s, j, n), 0)),
                pl.BlockSpec((None, 1, tf), lambda s, j, e, b, n, tl: (e[s], 0, jj(s, j, n))),
                pl.BlockSpec((None, 1, tf), lambda s, j, e, b, n, tl: (e[s], 0, nj + jj(s, j, n))),
                pl.BlockSpec((None, 1, D_MODEL), lambda s, j, e, b, n, tl: (e[s], 0, 0)),
            ],
            out_specs=pl.BlockSpec(memory_space=pl.ANY),
            scratch_shapes=[
                pltpu.VMEM((rmax, D_MODEL), F32),
                pltpu.VMEM((rmax, D_MODEL), BF16),
                pltpu.VMEM((rmax, D_MODEL), F32),
                pltpu.SemaphoreType.DMA(()),
                pltpu.SemaphoreType.DMA(()),
            ],
        ),
        out_shape=jax.ShapeDtypeStruct((n_rows, D_MODEL), F32),
        compiler_params=_cp(("arbitrary", "arbitrary")),
        name="moe_experts",
    )(sb_e, sb_blk0, sb_nblk, tail, xs, w_gate_up, w_gate_up, w_down,
      b_gate_up.reshape(N_EXPERTS, 1, 2 * D_MODEL), b_gate_up.reshape(N_EXPERTS, 1, 2 * D_MODEL),
      b_down.reshape(N_EXPERTS, 1, D_MODEL))


COMBINE_TOK = 128


def _combine_kernel(pos_ref, posn_ref, ys_hbm, x_ref, gt_ref, w_ref, o_ref, buf0, buf1, sem):
    tm = x_ref.shape[0]
    n = tm * TOP_K
    i = pl.program_id(0)
    slot = lax.rem(i, 2)
    bufs = (buf0, buf1)

    def issue(p_ref, sl):
        def group(g, c):
            for r in range(DMA_ROWS):
                src = ys_hbm.at[pl.ds(p_ref[0, g * DMA_ROWS + r], 1)]
                dst = bufs[sl].at[g * (DMA_ROWS // SUBLANE) + r // SUBLANE, pl.ds(r % SUBLANE, 1)]
                pltpu.make_async_copy(src, dst, sem.at[sl]).start()
            return c
        lax.fori_loop(0, n // DMA_ROWS, group, 0)

    @pl.when(i == 0)
    def _():
        issue(pos_ref, 0)

    for sl in range(2):
        @pl.when((i + 1 < pl.num_programs(0)) & (slot == 1 - sl))
        def _(sl=sl):
            issue(posn_ref, sl)

    for sl in range(2):
        @pl.when(slot == sl)
        def _(sl=sl):
            buf = bufs[sl]
            pltpu.make_async_copy(buf, buf, sem.at[sl]).wait()
            w = w_ref[...]
            lane = lax.broadcasted_iota(jnp.int32, w.shape, 1)
            y = jnp.zeros((tm, D_MODEL), F32)
            tiles = tm // SUBLANE
            for k in range(TOP_K):
                wk = jnp.sum(jnp.where(lane == k, w, 0.0), axis=-1, keepdims=True)
                y = y + wk * buf[k * tiles:(k + 1) * tiles].reshape(tm, D_MODEL)
            o_ref[...] = x_ref[...] + gt_ref[...] * y


def _combine(ys, pos, x2d, gt, top_w, rows_per_mod, tm):
    m = x2d.shape[0]
    steps = m // tm
    pos_kmajor = _kmajor(pos, tm)
    if rows_per_mod == 1:
        gt = gt.reshape(m, D_MODEL)
        gt_spec = pl.BlockSpec((tm, D_MODEL), lambda i: (i, 0))
    else:
        gt_spec = pl.BlockSpec((None, 1, D_MODEL), lambda i: (i // (rows_per_mod // tm), 0, 0))
    return pl.pallas_call(
        _combine_kernel,
        grid=(steps,),
        in_specs=[
            pl.BlockSpec((None, 1, TOP_K * tm), lambda i: (i, 0, 0), memory_space=pltpu.SMEM),
            pl.BlockSpec((None, 1, TOP_K * tm), lambda i: (jnp.minimum(i + 1, steps - 1), 0, 0), memory_space=pltpu.SMEM),
            pl.BlockSpec(memory_space=pl.ANY),
            pl.BlockSpec((tm, D_MODEL), lambda i: (i, 0)),
            gt_spec,
            pl.BlockSpec((tm, LANE), lambda i: (i, 0)),
        ],
        out_specs=pl.BlockSpec((tm, D_MODEL), lambda i: (i, 0)),
        out_shape=jax.ShapeDtypeStruct((m, D_MODEL), F32),
        scratch_shapes=[pltpu.VMEM((TOP_K * tm // SUBLANE, SUBLANE, D_MODEL), F32),
                        pltpu.VMEM((TOP_K * tm // SUBLANE, SUBLANE, D_MODEL), F32),
                        pltpu.SemaphoreType.DMA((2,))],
        compiler_params=_cp(("arbitrary",)),
        name="moe_combine",
    )(pos_kmajor, pos_kmajor, ys, x2d, gt, top_w)


def _routing_tables(top_idx):
    n_tok = top_idx.shape[0]
    n_assign = n_tok * TOP_K
    rb = MOE_ROWS
    n_blocks = -(-(n_assign + N_EXPERTS * (rb - 1)) // rb)
    n_rows = n_blocks * rb
    flat_e = top_idx.reshape(-1)
    onehot = (flat_e[:, None] == jnp.arange(N_EXPERTS, dtype=jnp.int32)[None, :]).astype(jnp.int32)
    csum = jnp.cumsum(onehot, axis=0)
    rank = jnp.sum((csum - onehot) * onehot, axis=1)
    counts = csum[-1]
    nblk_e = (counts + rb - 1) // rb
    blk_start = jnp.cumsum(nblk_e) - nblk_e
    dest = (blk_start * rb)[flat_e] + rank
    total_blk = jnp.sum(nblk_e)
    last_blk = jnp.where(nblk_e > 0, blk_start + nblk_e - 1, -1)
    bidx = jnp.arange(n_blocks, dtype=jnp.int32)
    zero_blocks = jnp.concatenate([last_blk, jnp.where(bidx >= total_blk, bidx, -1)]).astype(jnp.int32)
    n_sb_max = (n_blocks + N_EXPERTS * (MOE_SB_BLOCKS - 1)) // MOE_SB_BLOCKS
    sb_per_e = (nblk_e + MOE_SB_BLOCKS - 1) // MOE_SB_BLOCKS
    sb_start = jnp.cumsum(sb_per_e) - sb_per_e
    total_sb = jnp.sum(sb_per_e)
    sidx = jnp.arange(n_sb_max, dtype=jnp.int32)
    sb_end = jnp.cumsum(sb_per_e)
    e_of = jnp.minimum(jnp.sum((sb_end[None, :] <= sidx[:, None]).astype(jnp.int32), axis=1), N_EXPERTS - 1)
    local = sidx - sb_start[e_of]
    active = sidx < total_sb
    last_e = e_of[jnp.maximum(total_sb - 1, 0)]
    sb_e = jnp.where(active, e_of, last_e).astype(jnp.int32)
    sb_blk0 = jnp.where(active, blk_start[e_of] + local * MOE_SB_BLOCKS, 0).astype(jnp.int32)
    sb_nblk = jnp.where(active, jnp.minimum(nblk_e[e_of] - local * MOE_SB_BLOCKS, MOE_SB_BLOCKS), 0).astype(jnp.int32)
    tail = jnp.stack([total_blk, jnp.int32(n_blocks)]).astype(jnp.int32)
    return dest.astype(jnp.int32), zero_blocks, n_rows, sb_e, sb_blk0, sb_nblk, tail


def _kmajor(pos, tm):
    m = pos.shape[0]
    return pos.reshape(m // tm, tm, TOP_K).transpose(0, 2, 1).reshape(m // tm, 1, TOP_K * tm)


def _repack_w_in(w_in):
    a = DN_CONV_CH + DN_VW
    b = a + 2 * DN_HEADS
    c = b + SW_QW
    e = c + 2 * SW_KVW
    parts = [w_in[:, :a], w_in[:, b:c], w_in[:, e:], w_in[:, c:e], w_in[:, a:b]]
    pad = jnp.zeros((D_MODEL, PROJ_W - w_in.shape[1]), BF16)
    return jnp.concatenate([p.astype(BF16) for p in parts] + [pad], axis=1)


def _lane_vec(v, offset):
    return jnp.zeros((1, LANE), F32).at[0, offset:offset + v.shape[0]].set(v.astype(F32))


def kernel(x_prompt, x_sample, state_conv, state_delta, cache_swa_k, cache_swa_v, c_prompt, c_sample, w_ada, b_ada, ln1_w, w_in, conv_w, dn_a_log, dn_dt_bias, dn_norm_w, sw_q_norm_w, sw_k_norm_w, sw_sinks, w_branch_a, w_branch_b, w_out, ln2_w, router_w, router_b, w_gate_up, b_gate_up, w_down, b_down):
    assert w_ada.shape[0] == 1, "single-layer step"
    bp, t, d = x_prompt.shape
    bs = x_sample.shape[0]
    np_tok = bp * t
    l = 0

    n_c = bp + bs
    c_all = jnp.concatenate([c_prompt, c_sample, jnp.zeros((-n_c % 8, d), F32)], axis=0)
    mod = _ada_mod(c_all, w_ada[l], b_ada[l])
    mods_p = [m.reshape(bp, 1, d) for m in jnp.split(mod[:bp], 6, axis=-1)]
    mods_s = [m.reshape(bs, 1, d) for m in jnp.split(mod[bp:n_c], 6, axis=-1)]

    w_in_r = _repack_w_in(w_in[l])
    wa, wb, wo = w_branch_a[l].astype(BF16), w_branch_b[l].astype(BF16), w_out[l].astype(BF16)
    alog_lane = _lane_vec(dn_a_log[l], DN_HEADS)
    dtb_lane = _lane_vec(dn_dt_bias[l], DN_HEADS)
    rw_pad = jnp.zeros((d, LANE), F32).at[:, :N_EXPERTS].set(router_w[l])
    rb_pad = jnp.zeros((1, LANE), F32).at[0, :N_EXPERTS].set(router_b[l])
    sinks = sw_sinks[l].astype(F32)

    xp = x_prompt.reshape(np_tok, d)
    proj_p = _in_proj(xp, ln1_w[l], mods_p[1], mods_p[0], w_in_r, t, 1024)
    proj3 = proj_p.reshape(bp, t, PROJ_W)
    gates = _gdn_gates(proj3, alog_lane, dtb_lane)
    ya_p, delta_p = _gdn_prompt(proj3, gates, conv_w[l], dn_norm_w[l])
    yb_p, kn_p = _swa_prompt(proj3, sinks, sw_q_norm_w[l], sw_k_norm_w[l])
    x1_p, h2_p, idx_p, tw_p = _post_attention(
        ya_p.reshape(np_tok, DN_VW), yb_p.reshape(np_tok, SW_QW), proj_p, xp, mods_p[2], ln2_w[l], mods_p[4], mods_p[3],
        wa, wb, wo, rw_pad, rb_pad, t, POST_TM)

    xs_ = x_sample.reshape(bs, d)
    proj_s = _in_proj(xs_, ln1_w[l], mods_s[1], mods_s[0], w_in_r, 1, bs)
    ya_s, conv_s, delta_s = _gdn_step(proj_s, state_conv[l], state_delta[l], conv_w[l], alog_lane, dtb_lane, dn_norm_w[l])
    w_buf = cache_swa_k.shape[2]
    yb_s, k_s, v_s = _swa_step(proj_s, cache_swa_k[l].reshape(bs, w_buf, SW_KVW), cache_swa_v[l].reshape(bs, w_buf, SW_KVW),
                               sinks, sw_q_norm_w[l], sw_k_norm_w[l])
    x1_s, h2_s, idx_s, tw_s = _post_attention(
        ya_s.reshape(bs, DN_VW), yb_s.reshape(bs, SW_QW), proj_s, xs_, mods_s[2], ln2_w[l], mods_s[4], mods_s[3],
        wa, wb, wo, rw_pad, rb_pad, 1, bs)

    top_idx = jnp.concatenate([idx_p[:, :TOP_K], idx_s[:, :TOP_K]], axis=0)
    dest, zero_blocks, n_rows, sb_e, sb_blk0, sb_nblk, tail = _routing_tables(top_idx)
    xs_sorted = _scatter_rows(h2_p, h2_s, dest, zero_blocks, n_rows)
    ys = _experts(xs_sorted, sb_e, sb_blk0, sb_nblk, tail, w_gate_up[l], b_gate_up[l], w_down[l], b_down[l])
    pos = dest.reshape(np_tok + bs, TOP_K)
    y_p = _combine(ys, pos[:np_tok], x1_p, mods_p[5], tw_p, t, COMBINE_TOK)
    y_s = _combine(ys, pos[np_tok:], x1_s, mods_s[5], tw_s, 1, bs)

    conv_p = proj3[:, t - (DN_CONV - 1):, C_QKV:C_QKV + DN_CONV_CH]
    kp_out = kn_p[:, t - WINDOW:].reshape(bp, WINDOW, SW_KV_HEADS, SW_HD)
    vp_out = proj3[:, t - WINDOW:, C_SV:C_SV + SW_KVW].reshape(bp, WINDOW, SW_KV_HEADS, SW_HD)
    return (
        y_p.reshape(bp, t, d),
        y_s.reshape(bs, 1, d),
        conv_p[None],
        conv_s[None],
        delta_p[None],
        delta_s[None],
        kp_out[None],
        k_s.reshape(bs, w_buf, SW_KV_HEADS, SW_HD)[None],
        vp_out[None],
        v_s.reshape(bs, w_buf, SW_KV_HEADS, SW_HD)[None],
    )
```

```python
import functools

import jax
import jax.numpy as jnp
import numpy as np
from jax import lax
from jax.experimental import pallas as pl
from jax.experimental.pallas import tpu as pltpu

F32 = jnp.float32
BF16 = jnp.bfloat16

D_MODEL = 2048
PAST_LEN = 16384
DN_HEADS = 8
DN_DK = 128
DN_DV = 128
DN_CONV = 4
SW_HEADS = 16
SW_KV_HEADS = 2
SW_HD = 64
SW_GROUP = SW_HEADS // SW_KV_HEADS
WINDOW = 128
N_EXPERTS = 32
TOP_K = 4
SWIGLU_ALPHA = 1.702
SWIGLU_LIMIT = 7.0
EPS = 1e-6

DN_QK = DN_HEADS * DN_DK
DN_VW = DN_HEADS * DN_DV
DN_CONV_CH = 2 * DN_QK + DN_VW
SW_QW = SW_HEADS * SW_HD
SW_KVW = SW_KV_HEADS * SW_HD

LANE = 128
SUBLANE = 8
C_QKV = 0
C_Z = DN_CONV_CH
C_SQ = C_Z + DN_VW
C_GA = C_SQ + SW_QW
C_GB = C_GA + D_MODEL
C_SK = C_GB + D_MODEL
C_SV = C_SK + SW_KVW
C_BA = C_SV + SW_KVW
PROJ_W = 10240

GDN_GROUP = 256
GDN_CHUNK = 256
GDN_LEVELS = 8
MOE_ROWS = 128
MOE_SB_BLOCKS = 10
MOE_PASS_BLOCKS = 8
MOE_TF = 512
STEP_B = 4
POST_TM = 256
VMEM_LIMIT = 56 * 1024 * 1024


def _cp(sem, vmem=VMEM_LIMIT):
    return pltpu.CompilerParams(dimension_semantics=sem, vmem_limit_bytes=vmem)


def _dot(a, b):
    return jnp.dot(a.astype(BF16), b.astype(BF16), preferred_element_type=F32)


def _dot_nt(a, b):
    return lax.dot_general(a.astype(BF16), b.astype(BF16), (((1,), (1,)), ((), ())), preferred_element_type=F32)


def _split(a):
    hi = a.astype(BF16)
    lo = (a - hi.astype(F32)).astype(BF16)
    return hi, lo


def _dot3(a, b):
    ah, al = _split(a)
    bh, bl = _split(b)
    d = functools.partial(jnp.dot, preferred_element_type=F32)
    return d(ah, bh) + (d(ah, bl) + d(al, bh))


def _dot3_nt(a, b):
    ah, al = _split(a)
    bh, bl = _split(b)
    d = functools.partial(lax.dot_general, dimension_numbers=(((1,), (1,)), ((), ())), preferred_element_type=F32)
    return d(ah, bh) + (d(ah, bl) + d(al, bh))


def _dot_exact_lhs01(m01, b):
    b1 = b.astype(BF16)
    r = b - b1.astype(F32)
    b2 = r.astype(BF16)
    b3 = (r - b2.astype(F32)).astype(BF16)
    d = functools.partial(jnp.dot, preferred_element_type=F32)
    m = m01.astype(BF16)
    return d(m, b1) + (d(m, b2) + d(m, b3))


def _sigmoid(x):
    return 1.0 / (1.0 + jnp.exp(-x))


def _silu(x):
    return x * _sigmoid(x)


def _softplus(x):
    return jnp.maximum(x, 0.0) + jnp.log(1.0 + jnp.exp(-jnp.abs(x)))


def _ada_kernel(c_ref, w_ref, b_ref, o_ref):
    o_ref[...] = _dot(_silu(c_ref[...]), w_ref[...]) + b_ref[...]


def _ada_mod(c_all, w_ada, b_ada):
    m = c_all.shape[0]
    n = w_ada.shape[1]
    tn = 1024
    return pl.pallas_call(
        _ada_kernel,
        grid=(n // tn,),
        in_specs=[
            pl.BlockSpec((m, D_MODEL), lambda j: (0, 0)),
            pl.BlockSpec((D_MODEL, tn), lambda j: (0, j)),
            pl.BlockSpec((1, tn), lambda j: (0, j)),
        ],
        out_specs=pl.BlockSpec((m, tn), lambda j: (0, j)),
        out_shape=jax.ShapeDtypeStruct((m, n), F32),
        compiler_params=_cp(("arbitrary",)),
        name="ada_mod",
    )(c_all, w_ada, b_ada.reshape(1, n))


def _norm_mod(x, lnw, sc, sh):
    y = x * lax.rsqrt(jnp.mean(x * x, axis=-1, keepdims=True) + EPS)
    return (y * lnw) * (1.0 + sc) + sh


def _inproj_kernel(x_ref, lnw_ref, sc_ref, sh_ref, w_ref, o_ref, h_scr):
    @pl.when(pl.program_id(1) == 0)
    def _():
        h_scr[...] = _norm_mod(x_ref[...], lnw_ref[...], sc_ref[...], sh_ref[...]).astype(BF16)

    o_ref[...] = jnp.dot(h_scr[...], w_ref[...], preferred_element_type=F32)


def _in_proj(x2d, lnw, sc, sh, w_bf16, rows_per_mod, tm):
    m = x2d.shape[0]
    tn = 1024
    if rows_per_mod == 1:
        mod_spec = pl.BlockSpec((tm, D_MODEL), lambda i, j: (i, 0))
        sc, sh = sc.reshape(m, D_MODEL), sh.reshape(m, D_MODEL)
    else:
        assert rows_per_mod % tm == 0
        mod_spec = pl.BlockSpec((None, 1, D_MODEL), lambda i, j: (i // (rows_per_mod // tm), 0, 0))
    return pl.pallas_call(
        _inproj_kernel,
        grid=(m // tm, PROJ_W // tn),
        in_specs=[
            pl.BlockSpec((tm, D_MODEL), lambda i, j: (i, 0)),
            pl.BlockSpec((1, D_MODEL), lambda i, j: (0, 0)),
            mod_spec,
            mod_spec,
            pl.BlockSpec((D_MODEL, tn), lambda i, j: (0, j)),
        ],
        out_specs=pl.BlockSpec((tm, tn), lambda i, j: (i, j)),
        out_shape=jax.ShapeDtypeStruct((m, PROJ_W), F32),
        scratch_shapes=[pltpu.VMEM((tm, D_MODEL), BF16)],
        compiler_params=_cp(("arbitrary", "arbitrary")),
        name="in_proj",
    )(x2d, lnw.reshape(1, D_MODEL), sc, sh, w_bf16)


def _tri_masks(n, chunk):
    r = lax.broadcasted_iota(jnp.int32, (n, n), 0)
    c = lax.broadcasted_iota(jnp.int32, (n, n), 1)
    same = (r // chunk) == (c // chunk)
    return same, same & (r >= c), same & (r > c)


def _gates_kernel(ba_ref, alog_ref, dtb_ref, beta_ref, gc_ref, eg_ref, ek_ref, el_ref, gcrow_ref):
    same, causal, _ = _tri_masks(GDN_GROUP, GDN_CHUNK)
    lower01 = jnp.where(causal, 1.0, 0.0)
    ones01 = jnp.where(same, 1.0, 0.0)
    nega = -jnp.exp(alog_ref[...])
    dtb = dtb_ref[...]
    t = ba_ref.shape[0]

    def body(i, carry):
        r0 = pl.multiple_of(i * GDN_GROUP, GDN_GROUP)
        x = ba_ref[pl.ds(r0, GDN_GROUP), :]
        g = nega * _softplus(x + dtb)
        gc = _dot_exact_lhs01(lower01, g)
        gl = _dot_exact_lhs01(ones01, g)
        beta_ref[pl.ds(r0, GDN_GROUP), :] = _sigmoid(x)
        gc_ref[pl.ds(r0, GDN_GROUP), :] = gc
        eg_ref[pl.ds(r0, GDN_GROUP), :] = jnp.exp(gc)
        ek_ref[pl.ds(r0, GDN_GROUP), :] = jnp.exp(gl - gc)
        el_ref[pl.ds(r0, GDN_GROUP), :] = jnp.exp(gl)
        gct = gc.T
        for h in range(DN_HEADS):
            gcrow_ref[h, :, pl.ds(r0, GDN_GROUP)] = gct[DN_HEADS + h:DN_HEADS + h + 1, :]
        return carry

    lax.fori_loop(0, t // GDN_GROUP, body, 0)


def _gdn_gates(proj3, alog_lane, dtb_lane):
    b, t, _ = proj3.shape
    col = pl.BlockSpec((None, t, LANE), lambda i: (i, 0, 0))
    shp = jax.ShapeDtypeStruct((b, t, LANE), F32)
    return pl.pallas_call(
        _gates_kernel,
        grid=(b,),
        in_specs=[
            pl.BlockSpec((None, t, LANE), lambda i: (i, 0, C_BA // LANE)),
            pl.BlockSpec((1, LANE), lambda i: (0, 0)),
            pl.BlockSpec((1, LANE), lambda i: (0, 0)),
        ],
        out_specs=[col, col, col, col, col, pl.BlockSpec((None, DN_HEADS, 1, t), lambda i: (i, 0, 0, 0))],
        out_shape=[shp, shp, shp, shp, shp, jax.ShapeDtypeStruct((b, DN_HEADS, 1, t), F32)],
        compiler_params=_cp(("arbitrary",)),
        name="gdn_gates",
    )(proj3, alog_lane, dtb_lane)


def _l2norm(x):
    return x * lax.rsqrt(jnp.sum(x * x, axis=-1, keepdims=True) + EPS)


GDN_HPS = 4
GDN_TILE = 1024


def _gdn_kernel(q_ref, k_ref, v_ref, z_ref, beta_ref, gc_ref, eg_ref, ek_ref, el_ref, gcrow_ref,
                cwq_ref, cwk_ref, cwv_ref, nw_ref, o_ref, s_ref,
                pad_scr, hist_scr, qn_scr, kn_scr, vn_scr, oacc_scr, s_scr):
    t = q_ref.shape[0]
    wdt = GDN_HPS * LANE
    h0 = pl.program_id(1) * GDN_HPS
    pad = 8

    @pl.when(pl.program_id(2) == 0)
    def _():
        hist_scr[...] = jnp.zeros(hist_scr.shape, F32)
        s_scr[...] = jnp.zeros(s_scr.shape, F32)

    def conv_silu(stream, u_ref, cw_ref):
        pad_scr[0:pad, :] = hist_scr[stream]
        pad_scr[pad:pad + t, :] = u_ref[...]
        hist_scr[stream] = pad_scr[t:t + pad, :]
        y = cw_ref[DN_CONV - 1:DN_CONV, :] * pad_scr[pad:pad + t, :]
        for i in range(DN_CONV - 1):
            off = pad - (DN_CONV - 1) + i
            y = y + cw_ref[i:i + 1, :] * pad_scr[off:off + t, :]
        return _silu(y)

    yq = conv_silu(0, q_ref, cwq_ref)
    for hh in range(GDN_HPS):
        sl = slice(hh * LANE, (hh + 1) * LANE)
        qn_scr[:, sl] = _l2norm(yq[:, sl]) * (DN_DK ** -0.5)
    yk = conv_silu(1, k_ref, cwk_ref)
    for hh in range(GDN_HPS):
        sl = slice(hh * LANE, (hh + 1) * LANE)
        kn_scr[:, sl] = _l2norm(yk[:, sl])
    vn_scr[...] = conv_silu(2, v_ref, cwv_ref)

    n = GDN_GROUP
    c = GDN_CHUNK
    _, causal, strict = _tri_masks(n, c)
    rr = lax.broadcasted_iota(jnp.int32, (n, n), 0)
    cc = lax.broadcasted_iota(jnp.int32, (n, n), 1)
    eye = jnp.where(rr == cc, 1.0, 0.0)
    lane = lax.broadcasted_iota(jnp.int32, (n, LANE), 1)

    def pick(ref, r0, sel):
        return jnp.sum(jnp.where(sel, ref[pl.ds(r0, n), :], 0.0), axis=-1, keepdims=True)

    def head_group(hh, r0):
        sl = slice(hh * LANE, (hh + 1) * LANE)
        sel_b = lane == h0 + hh
        sel_g = lane == h0 + hh + DN_HEADS
        q = qn_scr[pl.ds(r0, n), sl]
        k = kn_scr[pl.ds(r0, n), sl]
        v = vn_scr[pl.ds(r0, n), sl]
        beta = pick(beta_ref, r0, sel_b)
        gc = pick(gc_ref, r0, sel_g)
        eg = pick(eg_ref, r0, sel_g)
        ek = pick(ek_ref, r0, sel_g)
        el = pick(el_ref, r0, sel_g)
        gcrow = gcrow_ref[hh, :, pl.ds(r0, n)]
        decay = jnp.where(causal, jnp.exp(gc - gcrow), 0.0)
        a_low = jnp.where(strict, beta * _dot_nt(k, k) * decay, 0.0)
        pw = [-a_low]
        for _ in range(GDN_LEVELS - 1):
            pw.append(_dot(pw[-1], pw[-1]))
        fs = [eye + pw[i] + pw[i + 1] + _dot(pw[i], pw[i + 1]) for i in range(0, GDN_LEVELS, 2)]
        while len(fs) > 1:
            fs = [_dot(fs[i], fs[i + 1]) for i in range(0, len(fs), 2)]
        rhs = jnp.concatenate([v * beta, k * (beta * eg)], axis=1)
        sol = _dot(fs[0], rhs)
        value = sol[:, :DN_DV]
        kcum = sol[:, DN_DV:]
        intra = _dot_nt(q, k) * decay
        q_dec = q * eg
        k_dec = k * ek
        for j in range(n // c):
            lo, hi = j * c, (j + 1) * c
            s = s_scr[hh]
            r = _dot(jnp.concatenate([kcum[lo:hi], q_dec[lo:hi]], axis=0), s)
            v_new = value[lo:hi] - r[:c]
            parts = []
            if lo:
                parts.append(jnp.zeros((lo, DN_DV), F32))
            parts.append(v_new)
            if hi < n:
                parts.append(jnp.zeros((n - hi, DN_DV), F32))
            o = r[c:] + _dot(intra[lo:hi], jnp.concatenate(parts, axis=0))
            oacc_scr[pl.ds(r0 + lo, c), sl] = o
            s_scr[hh] = s * el[lo:lo + 1] + _dot(k_dec[lo:hi].T, v_new)

    def body(i, carry):
        r0 = pl.multiple_of(i * n, n)
        for hh in range(GDN_HPS):
            head_group(hh, r0)
        return carry

    lax.fori_loop(0, t // n, body, 0)
    for hh in range(GDN_HPS):
        sl = slice(hh * LANE, (hh + 1) * LANE)
        o = oacc_scr[:, sl]
        y = o * lax.rsqrt(jnp.mean(o * o, axis=-1, keepdims=True) + EPS)
        o_ref[:, sl] = (y * nw_ref[...]) * _silu(z_ref[:, sl])
    s_ref[...] = s_scr[...]


def _gdn_prompt(proj3, gates, conv_w, norm_w):
    b, t, _ = proj3.shape
    beta, gc, eg, ek, el, gcrow = gates
    hps = GDN_HPS
    wdt = hps * LANE
    steps = DN_HEADS // hps
    tt = GDN_TILE
    assert t % tt == 0 and tt % GDN_GROUP == 0

    def colspec(base):
        return pl.BlockSpec((None, tt, wdt), lambda i, j, r, base=base: (i, r, base + j))

    gate = pl.BlockSpec((None, tt, LANE), lambda i, j, r: (i, r, 0))

    def cwspec(base):
        return pl.BlockSpec((DN_CONV, wdt), lambda i, j, r, base=base: (0, base + j))

    return pl.pallas_call(
        _gdn_kernel,
        grid=(b, steps, t // tt),
        in_specs=[
            colspec(0), colspec(steps), colspec(2 * steps), colspec(C_Z // wdt),
            gate, gate, gate, gate, gate,
            pl.BlockSpec((None, hps, 1, tt), lambda i, j, r: (i, j, 0, r)),
            cwspec(0), cwspec(steps), cwspec(2 * steps),
            pl.BlockSpec((1, DN_DV), lambda i, j, r: (0, 0)),
        ],
        out_specs=[
            pl.BlockSpec((None, tt, wdt), lambda i, j, r: (i, r, j)),
            pl.BlockSpec((None, hps, DN_DK, DN_DV), lambda i, j, r: (i, j, 0, 0)),
        ],
        out_shape=[
            jax.ShapeDtypeStruct((b, t, DN_VW), F32),
            jax.ShapeDtypeStruct((b, DN_HEADS, DN_DK, DN_DV), F32),
        ],
        scratch_shapes=[
            pltpu.VMEM((tt + 8, wdt), F32),
            pltpu.VMEM((3, 8, wdt), F32),
            pltpu.VMEM((tt, wdt), F32),
            pltpu.VMEM((tt, wdt), F32),
            pltpu.VMEM((tt, wdt), F32),
            pltpu.VMEM((tt, wdt), F32),
            pltpu.VMEM((hps, DN_DK, DN_DV), F32),
        ],
        compiler_params=_cp(("arbitrary", "arbitrary", "arbitrary")),
        name="gdn_prompt",
    )(proj3, proj3, proj3, proj3, beta, gc, eg, ek, el, gcrow, conv_w, conv_w, conv_w, norm_w.reshape(1, DN_DV))


def _gdn_step_one(p_ref, cprev_ref, s_ref, cw_ref, alog_ref, dtb_ref, nw_ref, o_ref, cnew_ref, snew_ref):
    u = p_ref[:, C_QKV:C_QKV + DN_CONV_CH]
    prev = cprev_ref[...]
    y = cw_ref[DN_CONV - 1:DN_CONV, :] * u
    for i in range(DN_CONV - 1):
        y = y + cw_ref[i:i + 1, :] * prev[i:i + 1, :]
    y = _silu(y)
    cnew_ref[0:DN_CONV - 2, :] = prev[1:DN_CONV - 1, :]
    cnew_ref[DN_CONV - 2:DN_CONV - 1, :] = u
    ba = p_ref[:, C_BA:C_BA + LANE]
    beta_l = _sigmoid(ba)
    a_l = jnp.exp(-jnp.exp(alog_ref[...]) * _softplus(ba + dtb_ref[...]))
    lane = lax.broadcasted_iota(jnp.int32, (1, LANE), 1)
    row8 = lax.broadcasted_iota(jnp.int32, (8, LANE), 0)
    for h in range(DN_HEADS):
        q = _l2norm(y[:, h * DN_DK:(h + 1) * DN_DK]) * (DN_DK ** -0.5)
        k = _l2norm(y[:, DN_QK + h * DN_DK:DN_QK + (h + 1) * DN_DK])
        v = y[:, 2 * DN_QK + h * DN_DV:2 * DN_QK + (h + 1) * DN_DV]
        beta = jnp.sum(jnp.where(lane == h, beta_l, 0.0), axis=-1, keepdims=True)
        a = jnp.sum(jnp.where(lane == h + DN_HEADS, a_l, 0.0), axis=-1, keepdims=True)
        s = s_ref[h]
        kq = jnp.where(row8 == 0, k, jnp.where(row8 == 1, q, 0.0)).T
        kcol, qcol = kq[:, 0:1], kq[:, 1:2]
        v_new = beta * (v - a * jnp.sum(s * kcol, axis=0, keepdims=True))
        o = a * jnp.sum(s * qcol, axis=0, keepdims=True) + jnp.sum(q * k, axis=-1, keepdims=True) * v_new
        snew_ref[h] = s * a + kcol * v_new
        yo = o * lax.rsqrt(jnp.mean(o * o, axis=-1, keepdims=True) + EPS)
        z = p_ref[:, C_Z + h * DN_DV:C_Z + (h + 1) * DN_DV]
        o_ref[:, h * DN_DV:(h + 1) * DN_DV] = (yo * nw_ref[...]) * _silu(z)


def _gdn_step_kernel(p_ref, cprev_ref, s_ref, cw_ref, alog_ref, dtb_ref, nw_ref, o_ref, cnew_ref, snew_ref):
    for bi in range(p_ref.shape[0]):
        _gdn_step_one(p_ref.at[bi], cprev_ref.at[bi], s_ref.at[bi], cw_ref, alog_ref, dtb_ref, nw_ref,
                      o_ref.at[bi], cnew_ref.at[bi], snew_ref.at[bi])


def _gdn_step(proj_s, conv_prev, s0, conv_w, alog_lane, dtb_lane, norm_w):
    b = proj_s.shape[0]
    sb = STEP_B
    assert b % sb == 0
    return pl.pallas_call(
        _gdn_step_kernel,
        grid=(b // sb,),
        in_specs=[
            pl.BlockSpec((sb, 1, PROJ_W), lambda i: (i, 0, 0)),
            pl.BlockSpec((sb, DN_CONV - 1, DN_CONV_CH), lambda i: (i, 0, 0)),
            pl.BlockSpec((sb, DN_HEADS, DN_DK, DN_DV), lambda i: (i, 0, 0, 0)),
            pl.BlockSpec((DN_CONV, DN_CONV_CH), lambda i: (0, 0)),
            pl.BlockSpec((1, LANE), lambda i: (0, 0)),
            pl.BlockSpec((1, LANE), lambda i: (0, 0)),
            pl.BlockSpec((1, DN_DV), lambda i: (0, 0)),
        ],
        out_specs=[
            pl.BlockSpec((sb, 1, DN_VW), lambda i: (i, 0, 0)),
            pl.BlockSpec((sb, DN_CONV - 1, DN_CONV_CH), lambda i: (i, 0, 0)),
            pl.BlockSpec((sb, DN_HEADS, DN_DK, DN_DV), lambda i: (i, 0, 0, 0)),
        ],
        out_shape=[
            jax.ShapeDtypeStruct((b, 1, DN_VW), F32),
            jax.ShapeDtypeStruct((b, DN_CONV - 1, DN_CONV_CH), F32),
            jax.ShapeDtypeStruct((b, DN_HEADS, DN_DK, DN_DV), F32),
        ],
        compiler_params=_cp(("arbitrary",)),
        name="gdn_step",
    )(proj_s.reshape(b, 1, PROJ_W), conv_prev, s0, conv_w, alog_lane, dtb_lane, norm_w.reshape(1, DN_DV))


def _alibi_slope(h):
    return float(2.0 ** (-8.0 * (h + 1) / SW_HEADS))


def _head_rms(x, w):
    return (x * lax.rsqrt(jnp.mean(x * x, axis=-1, keepdims=True) + EPS)) * w


SWA_HB = 4


def _swa_kernel(sinks_ref, q_ref, kc_ref, kp_ref, vc_ref, vp_ref, qw_ref, kw_ref, o_ref, kn_ref):
    blk = pl.program_id(1)
    w = WINDOW
    rows = SWA_HB * w
    qi = lax.broadcasted_iota(jnp.int32, (rows, 2 * w), 0)
    kj = lax.broadcasted_iota(jnp.int32, (rows, 2 * w), 1)
    dist = (qi & (w - 1)) + w - kj
    valid = (dist >= 0) & (dist < w) & ((kj >= w) | (blk > 0))
    distf = dist.astype(F32)
    stripe = lax.broadcasted_iota(jnp.int32, (rows, 1), 0) // w
    kc = kc_ref[...]
    kp = kp_ref[...]
    kbands, vbands = [], []
    for g in range(SW_KV_HEADS):
        sl = slice(g * SW_HD, (g + 1) * SW_HD)
        kcn = _head_rms(kc[:, sl], kw_ref[...])
        kn_ref[:, sl] = kcn
        kbands.append(jnp.concatenate([_head_rms(kp[:, sl], kw_ref[...]), kcn], axis=0))
        vbands.append(jnp.concatenate([vp_ref[:, sl], vc_ref[:, sl]], axis=0))
    for hb in range(SW_HEADS // SWA_HB):
        heads = range(hb * SWA_HB, (hb + 1) * SWA_HB)
        g = heads[0] // SW_GROUP
        qs = jnp.concatenate([_head_rms(q_ref[:, h * SW_HD:(h + 1) * SW_HD], qw_ref[...]) for h in heads], axis=0)
        slope = jnp.zeros((rows, 1), F32)
        sink = jnp.zeros((rows, 1), F32)
        for i, h in enumerate(heads):
            slope = jnp.where(stripe == i, _alibi_slope(h), slope)
            sink = jnp.where(stripe == i, sinks_ref[h], sink)
        s = _dot_nt(qs, kbands[g]) * (SW_HD ** -0.5) - slope * distf
        s = jnp.where(valid, s, -jnp.inf)
        m = jnp.maximum(jnp.max(s, axis=-1, keepdims=True), sink)
        p = jnp.exp(s - m)
        den = jnp.sum(p, axis=-1, keepdims=True) + jnp.exp(sink - m)
        o = _dot(p / den, vbands[g])
        for i, h in enumerate(heads):
            o_ref[:, h * SW_HD:(h + 1) * SW_HD] = o[i * w:(i + 1) * w]


def _swa_prompt(proj3, sinks, qw, kw):
    b, t, _ = proj3.shape
    nb = t // WINDOW
    kcol, vcol = C_SK // LANE, C_SV // LANE

    def cur(col):
        return pl.BlockSpec((None, WINDOW, SW_KVW), lambda i, j, s, col=col: (i, j, col))

    def prev(col):
        return pl.BlockSpec((None, WINDOW, SW_KVW), lambda i, j, s, col=col: (i, jnp.maximum(j - 1, 0), col))

    return pl.pallas_call(
        _swa_kernel,
        grid_spec=pltpu.PrefetchScalarGridSpec(
            num_scalar_prefetch=1,
            grid=(b, nb),
            in_specs=[
                pl.BlockSpec((None, WINDOW, SW_QW), lambda i, j, s: (i, j, C_SQ // SW_QW)),
                cur(kcol), prev(kcol), cur(vcol), prev(vcol),
                pl.BlockSpec((1, SW_HD), lambda i, j, s: (0, 0)),
                pl.BlockSpec((1, SW_HD), lambda i, j, s: (0, 0)),
            ],
            out_specs=[
                pl.BlockSpec((None, WINDOW, SW_QW), lambda i, j, s: (i, j, 0)),
                pl.BlockSpec((None, WINDOW, SW_KVW), lambda i, j, s: (i, j, 0)),
            ],
        ),
        out_shape=[
            jax.ShapeDtypeStruct((b, t, SW_QW), F32),
            jax.ShapeDtypeStruct((b, t, SW_KVW), F32),
        ],
        compiler_params=_cp(("arbitrary", "arbitrary")),
        name="swa_prompt",
    )(sinks, proj3, proj3, proj3, proj3, proj3, qw.reshape(1, SW_HD), kw.reshape(1, SW_HD))


def _swa_step_one(sinks_ref, p_ref, kbuf_ref, vbuf_ref, qw_ref, kw_ref, o_ref, knew_ref, vnew_ref, kcat, vcat):
    w = kbuf_ref.shape[0]
    rows = kcat.shape[0]
    knew = p_ref[:, C_SK:C_SK + SW_KVW]
    vnew = p_ref[:, C_SV:C_SV + SW_KVW]
    kcat[...] = jnp.zeros(kcat.shape, F32)
    vcat[...] = jnp.zeros(vcat.shape, F32)
    kcat[0:w, :] = kbuf_ref[...]
    vcat[0:w, :] = vbuf_ref[...]
    for g in range(SW_KV_HEADS):
        sl = slice(g * SW_HD, (g + 1) * SW_HD)
        kcat[w:w + 1, sl] = _head_rms(knew[:, sl], kw_ref[...])
    vcat[w:w + 1, :] = vnew
    knew_ref[...] = kcat[1:w + 1, :]
    vnew_ref[...] = vcat[1:w + 1, :]
    j = lax.broadcasted_iota(jnp.int32, (SW_GROUP, rows), 1)
    dist = w - j
    valid = (dist >= 0) & (dist < WINDOW)
    distf = dist.astype(F32)
    hrow = lax.broadcasted_iota(jnp.int32, (SW_GROUP, 1), 0)
    for g in range(SW_KV_HEADS):
        sl = slice(g * SW_HD, (g + 1) * SW_HD)
        qs = jnp.zeros((SW_GROUP, SW_HD), F32)
        slope = jnp.zeros((SW_GROUP, 1), F32)
        sink = jnp.zeros((SW_GROUP, 1), F32)
        for i in range(SW_GROUP):
            h = g * SW_GROUP + i
            qh = _head_rms(p_ref[:, C_SQ + h * SW_HD:C_SQ + (h + 1) * SW_HD], qw_ref[...])
            qs = jnp.where(hrow == i, qh, qs)
            slope = jnp.where(hrow == i, _alibi_slope(h), slope)
            sink = jnp.where(hrow == i, sinks_ref[h], sink)
        s = _dot_nt(qs, kcat[:, sl]) * (SW_HD ** -0.5) - slope * distf
        s = jnp.where(valid, s, -jnp.inf)
        m = jnp.maximum(jnp.max(s, axis=-1, keepdims=True), sink)
        p = jnp.exp(s - m)
        den = jnp.sum(p, axis=-1, keepdims=True) + jnp.exp(sink - m)
        o = _dot(p / den, vcat[:, sl])
        for i in range(SW_GROUP):
            h = g * SW_GROUP + i
            o_ref[:, h * SW_HD:(h + 1) * SW_HD] = o[i:i + 1]


def _swa_step_kernel(sinks_ref, p_ref, kbuf_ref, vbuf_ref, qw_ref, kw_ref, o_ref, knew_ref, vnew_ref, kcat, vcat):
    for bi in range(p_ref.shape[0]):
        _swa_step_one(sinks_ref, p_ref.at[bi], kbuf_ref.at[bi], vbuf_ref.at[bi], qw_ref, kw_ref,
                      o_ref.at[bi], knew_ref.at[bi], vnew_ref.at[bi], kcat.at[bi], vcat.at[bi])


def _swa_step(proj_s, kbuf, vbuf, sinks, qw, kw):
    b = proj_s.shape[0]
    w = kbuf.shape[1]
    sb = STEP_B
    assert b % sb == 0
    rows = 2 * w
    buf = pl.BlockSpec((sb, w, SW_KVW), lambda i, s: (i, 0, 0))
    return pl.pallas_call(
        _swa_step_kernel,
        grid_spec=pltpu.PrefetchScalarGridSpec(
            num_scalar_prefetch=1,
            grid=(b // sb,),
            in_specs=[
                pl.BlockSpec((sb, 1, PROJ_W), lambda i, s: (i, 0, 0)),
                buf, buf,
                pl.BlockSpec((1, SW_HD), lambda i, s: (0, 0)),
                pl.BlockSpec((1, SW_HD), lambda i, s: (0, 0)),
            ],
            out_specs=[pl.BlockSpec((sb, 1, SW_QW), lambda i, s: (i, 0, 0)), buf, buf],
            scratch_shapes=[pltpu.VMEM((sb, rows, SW_KVW), F32), pltpu.VMEM((sb, rows, SW_KVW), F32)],
        ),
        out_shape=[
            jax.ShapeDtypeStruct((b, 1, SW_QW), F32),
            jax.ShapeDtypeStruct((b, w, SW_KVW), F32),
            jax.ShapeDtypeStruct((b, w, SW_KVW), F32),
        ],
        compiler_params=_cp(("arbitrary",)),
        name="swa_step",
    )(sinks, proj_s.reshape(b, 1, PROJ_W), kbuf, vbuf, qw.reshape(1, SW_HD), kw.reshape(1, SW_HD))


def _route_top_k(hmod, rw_ref, rb_ref, idx_ref, w_ref):
    logits = _dot3(hmod, rw_ref[...]) + rb_ref[...]
    lane = lax.broadcasted_iota(jnp.int32, logits.shape, 1)
    cur = jnp.where(lane < N_EXPERTS, logits, -jnp.inf)
    vals, idxs = [], []
    for _ in range(TOP_K):
        m = jnp.max(cur, axis=-1, keepdims=True)
        ix = jnp.min(jnp.where(cur == m, lane, LANE), axis=-1, keepdims=True)
        vals.append(m)
        idxs.append(ix)
        cur = jnp.where(lane == ix, -jnp.inf, cur)
    es = [jnp.exp(v - vals[0]) for v in vals]
    den = es[0] + es[1] + es[2] + es[3]
    idx_out = jnp.zeros(logits.shape, jnp.int32)
    w_out = jnp.zeros(logits.shape, F32)
    for k in range(TOP_K):
        idx_out = jnp.where(lane == k, idxs[k], idx_out)
        w_out = jnp.where(lane == k, es[k] / den, w_out)
    idx_ref[...] = idx_out
    w_ref[...] = w_out


def _post_kernel(ya_ref, yb_ref, ga0_ref, ga1_ref, gb0_ref, gb1_ref, x_ref, gt_ref, lnw_ref, sc_ref, sh_ref,
                 wa_ref, wb_ref, wo_ref, rw_ref, rb_ref, x1_ref, h_ref, idx_ref, w_ref):
    a = _dot(ya_ref[...], wa_ref[...])
    b = _dot(yb_ref[...], wb_ref[...])
    ga = jnp.concatenate([ga0_ref[...], ga1_ref[...]], axis=1)
    gb = jnp.concatenate([gb0_ref[...], gb1_ref[...]], axis=1)
    merged = _sigmoid(ga) * a + _sigmoid(gb) * b
    x1 = x_ref[...] + gt_ref[...] * _dot(merged, wo_ref[...])
    x1_ref[...] = x1
    hmod = _norm_mod(x1, lnw_ref[...], sc_ref[...], sh_ref[...])
    h_ref[...] = hmod
    _route_top_k(hmod, rw_ref, rb_ref, idx_ref, w_ref)


def _post_attention(ya, yb, proj, x2d, gt, lnw, sc, sh, wa, wb, wo, rw_pad, rb_pad, rows_per_mod, tm):
    m = x2d.shape[0]
    half = D_MODEL // 2
    assert C_GA % half == 0 and C_GB % half == 0
    if rows_per_mod == 1:
        mod_spec = pl.BlockSpec((tm, D_MODEL), lambda i: (i, 0))
        gt, sc, sh = (v.reshape(m, D_MODEL) for v in (gt, sc, sh))
    else:
        mod_spec = pl.BlockSpec((None, 1, D_MODEL), lambda i: (i // (rows_per_mod // tm), 0, 0))
    row = pl.BlockSpec((tm, D_MODEL), lambda i: (i, 0))
    small = pl.BlockSpec((tm, LANE), lambda i: (i, 0))

    def gate(col):
        return pl.BlockSpec((tm, half), lambda i, col=col: (i, col))

    def resident(shape):
        return pl.BlockSpec(shape, lambda i: (0, 0), pipeline_mode=pl.Buffered(1))

    return pl.pallas_call(
        _post_kernel,
        grid=(m // tm,),
        in_specs=[
            pl.BlockSpec((tm, DN_VW), lambda i: (i, 0)),
            pl.BlockSpec((tm, SW_QW), lambda i: (i, 0)),
            gate(C_GA // half), gate(C_GA // half + 1), gate(C_GB // half), gate(C_GB // half + 1),
            row, mod_spec,
            pl.BlockSpec((1, D_MODEL), lambda i: (0, 0)),
            mod_spec, mod_spec,
            resident((DN_VW, D_MODEL)), resident((SW_QW, D_MODEL)), resident((D_MODEL, D_MODEL)),
            pl.BlockSpec((D_MODEL, LANE), lambda i: (0, 0)),
            pl.BlockSpec((1, LANE), lambda i: (0, 0)),
        ],
        out_specs=[row, row, small, small],
        out_shape=[
            jax.ShapeDtypeStruct((m, D_MODEL), F32),
            jax.ShapeDtypeStruct((m, D_MODEL), F32),
            jax.ShapeDtypeStruct((m, LANE), jnp.int32),
            jax.ShapeDtypeStruct((m, LANE), F32),
        ],
        compiler_params=_cp(("arbitrary",)),
        name="post_attention",
    )(ya, yb, proj, proj, proj, proj, x2d, gt, lnw.reshape(1, D_MODEL), sc, sh, wa, wb, wo, rw_pad, rb_pad)


SCATTER_TOK = 512
DMA_ROWS = 16


def _scatter_kernel(zl_ref, dest_ref, hp_ref, hs_ref, xs_hbm, zbuf, sem, zsem):
    i = pl.program_id(0)
    rb = MOE_ROWS
    n_prompt_steps = pl.num_programs(0) - 1

    @pl.when(i == 0)
    def _():
        zbuf[...] = jnp.zeros(zbuf.shape, F32)

        def zero_copy(n):
            return pltpu.make_async_copy(zbuf, xs_hbm.at[pl.ds(zl_ref[n] * rb, rb)], zsem)

        def start(n, c):
            @pl.when(zl_ref[n] >= 0)
            def _():
                zero_copy(n).start()
            return c

        def wait(n, c):
            @pl.when(zl_ref[n] >= 0)
            def _():
                zero_copy(n).wait()
            return c

        lax.fori_loop(0, zl_ref.shape[0], start, 0)
        lax.fori_loop(0, zl_ref.shape[0], wait, 0)

    def scatter(src_ref):
        def group(g, c):
            for r in range(SUBLANE):
                for k in range(TOP_K):
                    dst = dest_ref[0, g * (SUBLANE * TOP_K) + (r * TOP_K + k)]
                    pltpu.make_async_copy(src_ref.at[g, pl.ds(r, 1)], xs_hbm.at[pl.ds(dst, 1)], sem).start()
            return c

        lax.fori_loop(0, src_ref.shape[0], group, 0)
        for _ in range(TOP_K):
            pltpu.make_async_copy(src_ref, src_ref, sem).wait()

    @pl.when(i < n_prompt_steps)
    def _():
        scatter(hp_ref)

    @pl.when(i == n_prompt_steps)
    def _():
        scatter(hs_ref)


def _scatter_rows(h_p, h_s, dest, zero_blocks, n_rows):
    assert TOP_K == 4
    n_p, n_s = h_p.shape[0], h_s.shape[0]
    steps_p = n_p // SCATTER_TOK
    per = SCATTER_TOK * TOP_K
    dest_s = jnp.concatenate([dest[n_p * TOP_K:], jnp.zeros((per - n_s * TOP_K,), jnp.int32)])
    dest3 = jnp.concatenate([dest[:n_p * TOP_K], dest_s]).reshape(steps_p + 1, 1, per)
    return pl.pallas_call(
        _scatter_kernel,
        grid_spec=pltpu.PrefetchScalarGridSpec(
            num_scalar_prefetch=1,
            grid=(steps_p + 1,),
            in_specs=[
                pl.BlockSpec((None, 1, per), lambda i, zl: (i, 0, 0), memory_space=pltpu.SMEM),
                pl.BlockSpec((SCATTER_TOK // SUBLANE, SUBLANE, D_MODEL), lambda i, zl: (jnp.minimum(i, steps_p - 1), 0, 0)),
                pl.BlockSpec((n_s // SUBLANE, SUBLANE, D_MODEL), lambda i, zl: (0, 0, 0)),
            ],
            out_specs=pl.BlockSpec(memory_space=pl.ANY),
            scratch_shapes=[
                pltpu.VMEM((MOE_ROWS, D_MODEL), F32),
                pltpu.SemaphoreType.DMA(()),
                pltpu.SemaphoreType.DMA(()),
            ],
        ),
        out_shape=jax.ShapeDtypeStruct((n_rows, D_MODEL), F32),
        compiler_params=_cp(("arbitrary",)),
        name="moe_scatter",
    )(zero_blocks, dest3, h_p.reshape(n_p // SUBLANE, SUBLANE, D_MODEL), h_s.reshape(n_s // SUBLANE, SUBLANE, D_MODEL))


def _experts_kernel(sbe_ref, sbb_ref, sbn_ref, tail_ref, xs_hbm, wg_ref, wl_ref, wd_ref, bg_ref, bl_ref, bd_ref,
                    ys_hbm, xf_scr, xb_scr, acc_scr, sem_in, sem_out):
    s = pl.program_id(0)
    j = pl.program_id(1)
    last_s = pl.num_programs(0) - 1
    last_j = pl.num_programs(1) - 1
    nblk = sbn_ref[s]
    blk0 = sbb_ref[s]
    rb = MOE_ROWS

    def in_copy(first_blk, b):
        return pltpu.make_async_copy(xs_hbm.at[pl.ds((first_blk + b) * rb, rb)], xf_scr.at[pl.ds(b * rb, rb)], sem_in)

    def out_copy(first_blk, b):
        return pltpu.make_async_copy(acc_scr.at[pl.ds(b * rb, rb)], ys_hbm.at[pl.ds((first_blk + b) * rb, rb)], sem_out)

    def loop(n, fn):
        def body(b, c):
            fn(b)
            return c
        lax.fori_loop(0, n, body, 0)

    @pl.when(j == 0)
    def _():
        @pl.when(s == 0)
        def _():
            loop(nblk, lambda b: in_copy(blk0, b).start())

        loop(nblk, lambda b: in_copy(blk0, b).wait())

        @pl.when(s > 0)
        def _():
            prev0 = sbb_ref[s - 1]
            loop(sbn_ref[s - 1], lambda b: out_copy(prev0, b).wait())

        def cast(b):
            r0 = pl.multiple_of(b * rb, rb)
            xb_scr[pl.ds(r0, rb), :] = xf_scr[pl.ds(r0, rb), :].astype(BF16)
            acc_scr[pl.ds(r0, rb), :] = jnp.broadcast_to(bd_ref[...], (rb, D_MODEL))
        loop(nblk, cast)

        @pl.when(s < last_s)
        def _():
            nxt0 = sbb_ref[s + 1]
            loop(sbn_ref[s + 1], lambda b: in_copy(nxt0, b).start())

    @pl.when(nblk > 0)
    def _():
        def mlp(b0, nb):
            rows = nb * rb
            r0 = pl.multiple_of(b0 * rb, rb)
            x = xb_scr[pl.ds(r0, rows), :]
            glu = jnp.dot(x, wg_ref[...].astype(BF16), preferred_element_type=F32) + bg_ref[...]
            lin = jnp.dot(x, wl_ref[...].astype(BF16), preferred_element_type=F32) + bl_ref[...]
            glu = jnp.minimum(glu, SWIGLU_LIMIT)
            lin = jnp.clip(lin, -SWIGLU_LIMIT, SWIGLU_LIMIT)
            act = glu * _sigmoid(SWIGLU_ALPHA * glu) * (lin + 1.0)
            acc_scr[pl.ds(r0, rows), :] += jnp.dot(act.astype(BF16), wd_ref[...].astype(BF16),
                                                   preferred_element_type=F32)

            @pl.when(j == last_j)
            def _():
                for b in range(nb):
                    out_copy(blk0, b0 + b).start()

        full = nblk // MOE_PASS_BLOCKS
        loop(full, lambda p: mlp(p * MOE_PASS_BLOCKS, MOE_PASS_BLOCKS))
        done = full * MOE_PASS_BLOCKS
        part = MOE_PASS_BLOCKS // 2
        while part >= 1:
            take = ((nblk - done) // part) * part

            @pl.when(take > 0)
            def _(done=done, part=part):
                mlp(done, part)

            done = done + take
            part //= 2

    @pl.when((s == last_s) & (j == last_j))
    def _():
        loop(nblk, lambda b: out_copy(blk0, b).wait())
        acc_scr[0:rb, :] = jnp.zeros((rb, D_MODEL), F32)

        def zero_copy(b):
            return pltpu.make_async_copy(acc_scr.at[pl.ds(0, rb)], ys_hbm.at[pl.ds(b * rb, rb)], sem_out)

        def start(b, c):
            zero_copy(b).start()
            return c

        def wait(b, c):
            zero_copy(b).wait()
            return c

        lax.fori_loop(tail_ref[0], tail_ref[1], start, 0)
        lax.fori_loop(tail_ref[0], tail_ref[1], wait, 0)


def _experts(xs, sb_e, sb_blk0, sb_nblk, tail, w_gate_up, b_gate_up, w_down, b_down):
    n_rows = xs.shape[0]
    n_sb = sb_e.shape[0]
    tf = MOE_TF
    nj = D_MODEL // tf
    rmax = MOE_SB_BLOCKS * MOE_ROWS

    def jj(s, j, n):
        return jnp.where(n[s] > 0, j, nj - 1)

    return pl.pallas_call(
        _experts_kernel,
        grid_spec=pltpu.PrefetchScalarGridSpec(
            num_scalar_prefetch=4,
            grid=(n_sb, nj),
            in_specs=[
                pl.BlockSpec(memory_space=pl.ANY),
                pl.BlockSpec((None, D_MODEL, tf), lambda s, j, e, b, n, tl: (e[s], 0, jj(s, j, n))),
                pl.BlockSpec((None, D_MODEL, tf), lambda s, j, e, b, n, tl: (e[s], 0, nj + jj(s, j, n))),
                pl.BlockSpec((None, tf, D_MODEL), lambda s, j, e, b, n, tl: (e[s], jj(s, j, n), 0)),
                pl.BlockSpec((None, 1, tf), lambda s, j, e, b, n, tl: (e[s], 0, jj(s, j, n))),
                pl.BlockSpec((None, 1, tf), lambda s, j, e, b, n, tl: (e[s], 0, nj + jj(s, j, n))),
                pl.BlockSpec((None, 1, D_MODEL), lambda s, j, e, b, n, tl: (e[s], 0, 0)),
            ],
            out_specs=pl.BlockSpec(memory_space=pl.ANY),
            scratch_shapes=[
                pltpu.VMEM((rmax, D_MODEL), F32),
                pltpu.VMEM((rmax, D_MODEL), BF16),
                pltpu.VMEM((rmax, D_MODEL), F32),
                pltpu.SemaphoreType.DMA(()),
                pltpu.SemaphoreType.DMA(()),
            ],
        ),
        out_shape=jax.ShapeDtypeStruct((n_rows, D_MODEL), F32),
        compiler_params=_cp(("arbitrary", "arbitrary")),
        name="moe_experts",
    )(sb_e, sb_blk0, sb_nblk, tail, xs, w_gate_up, w_gate_up, w_down,
      b_gate_up.reshape(N_EXPERTS, 1, 2 * D_MODEL), b_gate_up.reshape(N_EXPERTS, 1, 2 * D_MODEL),
      b_down.reshape(N_EXPERTS, 1, D_MODEL))


COMBINE_TOK = 256


def _combine_kernel(pos_ref, posn_ref, ys_hbm, x_ref, gt_ref, w_ref, o_ref, buf0, buf1, sem):
    tm = x_ref.shape[0]
    n = tm * TOP_K
    i = pl.program_id(0)
    slot = lax.rem(i, 2)
    bufs = (buf0, buf1)

    def issue(p_ref, sl):
        def group(g, c):
            for r in range(DMA_ROWS):
                src = ys_hbm.at[pl.ds(p_ref[0, g * DMA_ROWS + r], 1)]
                dst = bufs[sl].at[g * (DMA_ROWS // SUBLANE) + r // SUBLANE, pl.ds(r % SUBLANE, 1)]
                pltpu.make_async_copy(src, dst, sem.at[sl]).start()
            return c
        lax.fori_loop(0, n // DMA_ROWS, group, 0)

    @pl.when(i == 0)
    def _():
        issue(pos_ref, 0)

    for sl in range(2):
        @pl.when((i + 1 < pl.num_programs(0)) & (slot == 1 - sl))
        def _(sl=sl):
            issue(posn_ref, sl)

    for sl in range(2):
        @pl.when(slot == sl)
        def _(sl=sl):
            buf = bufs[sl]
            pltpu.make_async_copy(buf, buf, sem.at[sl]).wait()
            w = w_ref[...]
            lane = lax.broadcasted_iota(jnp.int32, w.shape, 1)
            y = jnp.zeros((tm, D_MODEL), F32)
            tiles = tm // SUBLANE
            for k in range(TOP_K):
                wk = jnp.sum(jnp.where(lane == k, w, 0.0), axis=-1, keepdims=True)
                y = y + wk * buf[k * tiles:(k + 1) * tiles].reshape(tm, D_MODEL)
            o_ref[...] = x_ref[...] + gt_ref[...] * y


def _combine(ys, pos, x2d, gt, top_w, rows_per_mod, tm):
    m = x2d.shape[0]
    steps = m // tm
    pos_kmajor = _kmajor(pos, tm)
    if rows_per_mod == 1:
        gt = gt.reshape(m, D_MODEL)
        gt_spec = pl.BlockSpec((tm, D_MODEL), lambda i: (i, 0))
    else:
        gt_spec = pl.BlockSpec((None, 1, D_MODEL), lambda i: (i // (rows_per_mod // tm), 0, 0))
    return pl.pallas_call(
        _combine_kernel,
        grid=(steps,),
        in_specs=[
            pl.BlockSpec((None, 1, TOP_K * tm), lambda i: (i, 0, 0), memory_space=pltpu.SMEM),
            pl.BlockSpec((None, 1, TOP_K * tm), lambda i: (jnp.minimum(i + 1, steps - 1), 0, 0), memory_space=pltpu.SMEM),
            pl.BlockSpec(memory_space=pl.ANY),
            pl.BlockSpec((tm, D_MODEL), lambda i: (i, 0)),
            gt_spec,
            pl.BlockSpec((tm, LANE), lambda i: (i, 0)),
        ],
        out_specs=pl.BlockSpec((tm, D_MODEL), lambda i: (i, 0)),
        out_shape=jax.ShapeDtypeStruct((m, D_MODEL), F32),
        scratch_shapes=[pltpu.VMEM((TOP_K * tm // SUBLANE, SUBLANE, D_MODEL), F32),
                        pltpu.VMEM((TOP_K * tm // SUBLANE, SUBLANE, D_MODEL), F32),
                        pltpu.SemaphoreType.DMA((2,))],
        compiler_params=_cp(("arbitrary",)),
        name="moe_combine",
    )(pos_kmajor, pos_kmajor, ys, x2d, gt, top_w)


def _routing_tables(top_idx):
    n_tok = top_idx.shape[0]
    n_assign = n_tok * TOP_K
    rb = MOE_ROWS
    n_blocks = -(-(n_assign + N_EXPERTS * (rb - 1)) // rb)
    n_rows = n_blocks * rb
    flat_e = top_idx.reshape(-1)
    onehot = (flat_e[:, None] == jnp.arange(N_EXPERTS, dtype=jnp.int32)[None, :]).astype(jnp.int32)
    csum = jnp.cumsum(onehot, axis=0)
    rank = jnp.sum((csum - onehot) * onehot, axis=1)
    counts = csum[-1]
    nblk_e = (counts + rb - 1) // rb
    blk_start = jnp.cumsum(nblk_e) - nblk_e
    dest = (blk_start * rb)[flat_e] + rank
    total_blk = jnp.sum(nblk_e)
    last_blk = jnp.where(nblk_e > 0, blk_start + nblk_e - 1, -1)
    bidx = jnp.arange(n_blocks, dtype=jnp.int32)
    zero_blocks = jnp.concatenate([last_blk, jnp.where(bidx >= total_blk, bidx, -1)]).astype(jnp.int32)
    n_sb_max = (n_blocks + N_EXPERTS * (MOE_SB_BLOCKS - 1)) // MOE_SB_BLOCKS
    sb_per_e = (nblk_e + MOE_SB_BLOCKS - 1) // MOE_SB_BLOCKS
    sb_start = jnp.cumsum(sb_per_e) - sb_per_e
    total_sb = jnp.sum(sb_per_e)
    sidx = jnp.arange(n_sb_max, dtype=jnp.int32)
    sb_end = jnp.cumsum(sb_per_e)
    e_of = jnp.minimum(jnp.sum((sb_end[None, :] <= sidx[:, None]).astype(jnp.int32), axis=1), N_EXPERTS - 1)
    local = sidx - sb_start[e_of]
    active = sidx < total_sb
    last_e = e_of[jnp.maximum(total_sb - 1, 0)]
    sb_e = jnp.where(active, e_of, last_e).astype(jnp.int32)
    sb_blk0 = jnp.where(active, blk_start[e_of] + local * MOE_SB_BLOCKS, 0).astype(jnp.int32)
    sb_nblk = jnp.where(active, jnp.minimum(nblk_e[e_of] - local * MOE_SB_BLOCKS, MOE_SB_BLOCKS), 0).astype(jnp.int32)
    tail = jnp.stack([total_blk, jnp.int32(n_blocks)]).astype(jnp.int32)
    return dest.astype(jnp.int32), zero_blocks, n_rows, sb_e, sb_blk0, sb_nblk, tail


def _kmajor(pos, tm):
    m = pos.shape[0]
    return pos.reshape(m // tm, tm, TOP_K).transpose(0, 2, 1).reshape(m // tm, 1, TOP_K * tm)


def _repack_w_in(w_in):
    a = DN_CONV_CH + DN_VW
    b = a + 2 * DN_HEADS
    c = b + SW_QW
    e = c + 2 * SW_KVW
    parts = [w_in[:, :a], w_in[:, b:c], w_in[:, e:], w_in[:, c:e], w_in[:, a:b]]
    pad = jnp.zeros((D_MODEL, PROJ_W - w_in.shape[1]), BF16)
    return jnp.concatenate([p.astype(BF16) for p in parts] + [pad], axis=1)


def _lane_vec(v, offset):
    return jnp.zeros((1, LANE), F32).at[0, offset:offset + v.shape[0]].set(v.astype(F32))


def kernel(x_prompt, x_sample, state_conv, state_delta, cache_swa_k, cache_swa_v, c_prompt, c_sample, w_ada, b_ada, ln1_w, w_in, conv_w, dn_a_log, dn_dt_bias, dn_norm_w, sw_q_norm_w, sw_k_norm_w, sw_sinks, w_branch_a, w_branch_b, w_out, ln2_w, router_w, router_b, w_gate_up, b_gate_up, w_down, b_down):
    assert w_ada.shape[0] == 1, "single-layer step"
    bp, t, d = x_prompt.shape
    bs = x_sample.shape[0]
    np_tok = bp * t
    l = 0

    n_c = bp + bs
    c_all = jnp.concatenate([c_prompt, c_sample, jnp.zeros((-n_c % 8, d), F32)], axis=0)
    mod = _ada_mod(c_all, w_ada[l], b_ada[l])
    mods_p = [m.reshape(bp, 1, d) for m in jnp.split(mod[:bp], 6, axis=-1)]
    mods_s = [m.reshape(bs, 1, d) for m in jnp.split(mod[bp:n_c], 6, axis=-1)]

    w_in_r = _repack_w_in(w_in[l])
    wa, wb, wo = w_branch_a[l].astype(BF16), w_branch_b[l].astype(BF16), w_out[l].astype(BF16)
    alog_lane = _lane_vec(dn_a_log[l], DN_HEADS)
    dtb_lane = _lane_vec(dn_dt_bias[l], DN_HEADS)
    rw_pad = jnp.zeros((d, LANE), F32).at[:, :N_EXPERTS].set(router_w[l])
    rb_pad = jnp.zeros((1, LANE), F32).at[0, :N_EXPERTS].set(router_b[l])
    sinks = sw_sinks[l].astype(F32)

    xp = x_prompt.reshape(np_tok, d)
    proj_p = _in_proj(xp, ln1_w[l], mods_p[1], mods_p[0], w_in_r, t, 1024)
    proj3 = proj_p.reshape(bp, t, PROJ_W)
    gates = _gdn_gates(proj3, alog_lane, dtb_lane)
    ya_p, delta_p = _gdn_prompt(proj3, gates, conv_w[l], dn_norm_w[l])
    yb_p, kn_p = _swa_prompt(proj3, sinks, sw_q_norm_w[l], sw_k_norm_w[l])
    x1_p, h2_p, idx_p, tw_p = _post_attention(
        ya_p.reshape(np_tok, DN_VW), yb_p.reshape(np_tok, SW_QW), proj_p, xp, mods_p[2], ln2_w[l], mods_p[4], mods_p[3],
        wa, wb, wo, rw_pad, rb_pad, t, POST_TM)

    xs_ = x_sample.reshape(bs, d)
    proj_s = _in_proj(xs_, ln1_w[l], mods_s[1], mods_s[0], w_in_r, 1, bs)
    ya_s, conv_s, delta_s = _gdn_step(proj_s, state_conv[l], state_delta[l], conv_w[l], alog_lane, dtb_lane, dn_norm_w[l])
    w_buf = cache_swa_k.shape[2]
    yb_s, k_s, v_s = _swa_step(proj_s, cache_swa_k[l].reshape(bs, w_buf, SW_KVW), cache_swa_v[l].reshape(bs, w_buf, SW_KVW),
                               sinks, sw_q_norm_w[l], sw_k_norm_w[l])
    x1_s, h2_s, idx_s, tw_s = _post_attention(
        ya_s.reshape(bs, DN_VW), yb_s.reshape(bs, SW_QW), proj_s, xs_, mods_s[2], ln2_w[l], mods_s[4], mods_s[3],
        wa, wb, wo, rw_pad, rb_pad, 1, bs)

    top_idx = jnp.concatenate([idx_p[:, :TOP_K], idx_s[:, :TOP_K]], axis=0)
    dest, zero_blocks, n_rows, sb_e, sb_blk0, sb_nblk, tail = _routing_tables(top_idx)
    xs_sorted = _scatter_rows(h2_p, h2_s, dest, zero_blocks, n_rows)
    ys = _experts(xs_sorted, sb_e, sb_blk0, sb_nblk, tail, w_gate_up[l], b_gate_up[l], w_down[l], b_down[l])
    pos = dest.reshape(np_tok + bs, TOP_K)
    y_p = _combine(ys, pos[:np_tok], x1_p, mods_p[5], tw_p, t, COMBINE_TOK)
    y_s = _combine(ys, pos[np_tok:], x1_s, mods_s[5], tw_s, 1, bs)

    conv_p = proj3[:, t - (DN_CONV - 1):, C_QKV:C_QKV + DN_CONV_CH]
    kp_out = kn_p[:, t - WINDOW:].reshape(bp, WINDOW, SW_KV_HEADS, SW_HD)
    vp_out = proj3[:, t - WINDOW:, C_SV:C_SV + SW_KVW].reshape(bp, WINDOW, SW_KV_HEADS, SW_HD)
    return (
        y_p.reshape(bp, t, d),
        y_s.reshape(bs, 1, d),
        conv_p[None],
        conv_s[None],
        delta_p[None],
        delta_s[None],
        kp_out[None],
        k_s.reshape(bs, w_buf, SW_KV_HEADS, SW_HD)[None],
        vp_out[None],
        v_s.reshape(bs, w_buf, SW_KV_HEADS, SW_HD)[None],
    )
```

```python
import functools

import jax
import jax.numpy as jnp
import numpy as np
from jax import lax
from jax.experimental import pallas as pl
from jax.experimental.pallas import tpu as pltpu

F32 = jnp.float32
BF16 = jnp.bfloat16

D_MODEL = 2048
PAST_LEN = 16384
DN_HEADS = 8
DN_DK = 128
DN_DV = 128
DN_CONV = 4
SW_HEADS = 16
SW_KV_HEADS = 2
SW_HD = 64
SW_GROUP = SW_HEADS // SW_KV_HEADS
WINDOW = 128
N_EXPERTS = 32
TOP_K = 4
SWIGLU_ALPHA = 1.702
SWIGLU_LIMIT = 7.0
EPS = 1e-6

DN_QK = DN_HEADS * DN_DK
DN_VW = DN_HEADS * DN_DV
DN_CONV_CH = 2 * DN_QK + DN_VW
SW_QW = SW_HEADS * SW_HD
SW_KVW = SW_KV_HEADS * SW_HD

LANE = 128
SUBLANE = 8
C_QKV = 0
C_Z = DN_CONV_CH
C_SQ = C_Z + DN_VW
C_GA = C_SQ + SW_QW
C_GB = C_GA + D_MODEL
C_SK = C_GB + D_MODEL
C_SV = C_SK + SW_KVW
C_BA = C_SV + SW_KVW
PROJ_W = 10240

GDN_GROUP = 256
GDN_CHUNK = 256
GDN_LEVELS = 8
MOE_ROWS = 128
MOE_SB_BLOCKS = 10
MOE_PASS_BLOCKS = 8
MOE_TF = 512
STEP_B = 4
GDN_STEP_B = 1
POST_TM = 256
VMEM_LIMIT = 56 * 1024 * 1024


def _cp(sem, vmem=VMEM_LIMIT):
    return pltpu.CompilerParams(dimension_semantics=sem, vmem_limit_bytes=vmem)


def _dot(a, b):
    return jnp.dot(a.astype(BF16), b.astype(BF16), preferred_element_type=F32)


def _dot_nt(a, b):
    return lax.dot_general(a.astype(BF16), b.astype(BF16), (((1,), (1,)), ((), ())), preferred_element_type=F32)


def _split(a):
    hi = a.astype(BF16)
    lo = (a - hi.astype(F32)).astype(BF16)
    return hi, lo


def _dot3(a, b):
    ah, al = _split(a)
    bh, bl = _split(b)
    d = functools.partial(jnp.dot, preferred_element_type=F32)
    return d(ah, bh) + (d(ah, bl) + d(al, bh))


def _dot3_nt(a, b):
    ah, al = _split(a)
    bh, bl = _split(b)
    d = functools.partial(lax.dot_general, dimension_numbers=(((1,), (1,)), ((), ())), preferred_element_type=F32)
    return d(ah, bh) + (d(ah, bl) + d(al, bh))


def _dot_exact_lhs01(m01, b):
    b1 = b.astype(BF16)
    r = b - b1.astype(F32)
    b2 = r.astype(BF16)
    b3 = (r - b2.astype(F32)).astype(BF16)
    d = functools.partial(jnp.dot, preferred_element_type=F32)
    m = m01.astype(BF16)
    return d(m, b1) + (d(m, b2) + d(m, b3))


def _sigmoid(x):
    return 1.0 / (1.0 + jnp.exp(-x))


def _silu(x):
    return x * _sigmoid(x)


def _softplus(x):
    return jnp.maximum(x, 0.0) + jnp.log(1.0 + jnp.exp(-jnp.abs(x)))


def _ada_kernel(c_ref, w_ref, b_ref, o_ref):
    o_ref[...] = _dot(_silu(c_ref[...]), w_ref[...]) + b_ref[...]


def _ada_mod(c_all, w_ada, b_ada):
    m = c_all.shape[0]
    n = w_ada.shape[1]
    tn = 1024
    return pl.pallas_call(
        _ada_kernel,
        grid=(n // tn,),
        in_specs=[
            pl.BlockSpec((m, D_MODEL), lambda j: (0, 0)),
            pl.BlockSpec((D_MODEL, tn), lambda j: (0, j)),
            pl.BlockSpec((1, tn), lambda j: (0, j)),
        ],
        out_specs=pl.BlockSpec((m, tn), lambda j: (0, j)),
        out_shape=jax.ShapeDtypeStruct((m, n), F32),
        compiler_params=_cp(("arbitrary",)),
        name="ada_mod",
    )(c_all, w_ada, b_ada.reshape(1, n))


def _norm_mod(x, lnw, sc, sh):
    y = x * lax.rsqrt(jnp.mean(x * x, axis=-1, keepdims=True) + EPS)
    return (y * lnw) * (1.0 + sc) + sh


def _inproj_kernel(x_ref, lnw_ref, sc_ref, sh_ref, w_ref, o_ref, h_scr):
    @pl.when(pl.program_id(1) == 0)
    def _():
        h_scr[...] = _norm_mod(x_ref[...], lnw_ref[...], sc_ref[...], sh_ref[...]).astype(BF16)

    o_ref[...] = jnp.dot(h_scr[...], w_ref[...], preferred_element_type=F32)


def _in_proj(x2d, lnw, sc, sh, w_bf16, rows_per_mod, tm):
    m = x2d.shape[0]
    tn = 1024
    if rows_per_mod == 1:
        mod_spec = pl.BlockSpec((tm, D_MODEL), lambda i, j: (i, 0))
        sc, sh = sc.reshape(m, D_MODEL), sh.reshape(m, D_MODEL)
    else:
        assert rows_per_mod % tm == 0
        mod_spec = pl.BlockSpec((None, 1, D_MODEL), lambda i, j: (i // (rows_per_mod // tm), 0, 0))
    return pl.pallas_call(
        _inproj_kernel,
        grid=(m // tm, PROJ_W // tn),
        in_specs=[
            pl.BlockSpec((tm, D_MODEL), lambda i, j: (i, 0)),
            pl.BlockSpec((1, D_MODEL), lambda i, j: (0, 0)),
            mod_spec,
            mod_spec,
            pl.BlockSpec((D_MODEL, tn), lambda i, j: (0, j)),
        ],
        out_specs=pl.BlockSpec((tm, tn), lambda i, j: (i, j)),
        out_shape=jax.ShapeDtypeStruct((m, PROJ_W), F32),
        scratch_shapes=[pltpu.VMEM((tm, D_MODEL), BF16)],
        compiler_params=_cp(("arbitrary", "arbitrary")),
        name="in_proj",
    )(x2d, lnw.reshape(1, D_MODEL), sc, sh, w_bf16)


def _tri_masks(n, chunk):
    r = lax.broadcasted_iota(jnp.int32, (n, n), 0)
    c = lax.broadcasted_iota(jnp.int32, (n, n), 1)
    same = (r // chunk) == (c // chunk)
    return same, same & (r >= c), same & (r > c)


def _gates_kernel(ba_ref, alog_ref, dtb_ref, beta_ref, gc_ref, eg_ref, ek_ref, el_ref, gcrow_ref):
    same, causal, _ = _tri_masks(GDN_GROUP, GDN_CHUNK)
    lower01 = jnp.where(causal, 1.0, 0.0)
    ones01 = jnp.where(same, 1.0, 0.0)
    nega = -jnp.exp(alog_ref[...])
    dtb = dtb_ref[...]
    t = ba_ref.shape[0]

    def body(i, carry):
        r0 = pl.multiple_of(i * GDN_GROUP, GDN_GROUP)
        x = ba_ref[pl.ds(r0, GDN_GROUP), :]
        g = nega * _softplus(x + dtb)
        gc = _dot_exact_lhs01(lower01, g)
        gl = _dot_exact_lhs01(ones01, g)
        beta_ref[pl.ds(r0, GDN_GROUP), :] = _sigmoid(x)
        gc_ref[pl.ds(r0, GDN_GROUP), :] = gc
        eg_ref[pl.ds(r0, GDN_GROUP), :] = jnp.exp(gc)
        ek_ref[pl.ds(r0, GDN_GROUP), :] = jnp.exp(gl - gc)
        el_ref[pl.ds(r0, GDN_GROUP), :] = jnp.exp(gl)
        gct = gc.T
        for h in range(DN_HEADS):
            gcrow_ref[h, :, pl.ds(r0, GDN_GROUP)] = gct[DN_HEADS + h:DN_HEADS + h + 1, :]
        return carry

    lax.fori_loop(0, t // GDN_GROUP, body, 0)


def _gdn_gates(proj3, alog_lane, dtb_lane):
    b, t, _ = proj3.shape
    col = pl.BlockSpec((None, t, LANE), lambda i: (i, 0, 0))
    shp = jax.ShapeDtypeStruct((b, t, LANE), F32)
    return pl.pallas_call(
        _gates_kernel,
        grid=(b,),
        in_specs=[
            pl.BlockSpec((None, t, LANE), lambda i: (i, 0, C_BA // LANE)),
            pl.BlockSpec((1, LANE), lambda i: (0, 0)),
            pl.BlockSpec((1, LANE), lambda i: (0, 0)),
        ],
        out_specs=[col, col, col, col, col, pl.BlockSpec((None, DN_HEADS, 1, t), lambda i: (i, 0, 0, 0))],
        out_shape=[shp, shp, shp, shp, shp, jax.ShapeDtypeStruct((b, DN_HEADS, 1, t), F32)],
        compiler_params=_cp(("arbitrary",)),
        name="gdn_gates",
    )(proj3, alog_lane, dtb_lane)


def _l2norm(x):
    return x * lax.rsqrt(jnp.sum(x * x, axis=-1, keepdims=True) + EPS)


GDN_HPS = 4
GDN_TILE = 1024


def _gdn_kernel(q_ref, k_ref, v_ref, z_ref, beta_ref, gc_ref, eg_ref, ek_ref, el_ref, gcrow_ref,
                cwq_ref, cwk_ref, cwv_ref, nw_ref, o_ref, s_ref,
                pad_scr, hist_scr, qn_scr, kn_scr, vn_scr, oacc_scr, s_scr):
    t = q_ref.shape[0]
    wdt = GDN_HPS * LANE
    h0 = pl.program_id(1) * GDN_HPS
    pad = 8

    @pl.when(pl.program_id(2) == 0)
    def _():
        hist_scr[...] = jnp.zeros(hist_scr.shape, F32)
        s_scr[...] = jnp.zeros(s_scr.shape, F32)

    def conv_silu(stream, u_ref, cw_ref):
        pad_scr[0:pad, :] = hist_scr[stream]
        pad_scr[pad:pad + t, :] = u_ref[...]
        hist_scr[stream] = pad_scr[t:t + pad, :]
        y = cw_ref[DN_CONV - 1:DN_CONV, :] * pad_scr[pad:pad + t, :]
        for i in range(DN_CONV - 1):
            off = pad - (DN_CONV - 1) + i
            y = y + cw_ref[i:i + 1, :] * pad_scr[off:off + t, :]
        return _silu(y)

    yq = conv_silu(0, q_ref, cwq_ref)
    for hh in range(GDN_HPS):
        sl = slice(hh * LANE, (hh + 1) * LANE)
        qn_scr[:, sl] = _l2norm(yq[:, sl]) * (DN_DK ** -0.5)
    yk = conv_silu(1, k_ref, cwk_ref)
    for hh in range(GDN_HPS):
        sl = slice(hh * LANE, (hh + 1) * LANE)
        kn_scr[:, sl] = _l2norm(yk[:, sl])
    vn_scr[...] = conv_silu(2, v_ref, cwv_ref)

    n = GDN_GROUP
    c = GDN_CHUNK
    _, causal, strict = _tri_masks(n, c)
    rr = lax.broadcasted_iota(jnp.int32, (n, n), 0)
    cc = lax.broadcasted_iota(jnp.int32, (n, n), 1)
    eye = jnp.where(rr == cc, 1.0, 0.0)
    lane = lax.broadcasted_iota(jnp.int32, (n, LANE), 1)

    def pick(ref, r0, sel):
        return jnp.sum(jnp.where(sel, ref[pl.ds(r0, n), :], 0.0), axis=-1, keepdims=True)

    def head_group(hh, r0):
        sl = slice(hh * LANE, (hh + 1) * LANE)
        sel_b = lane == h0 + hh
        sel_g = lane == h0 + hh + DN_HEADS
        q = qn_scr[pl.ds(r0, n), sl]
        k = kn_scr[pl.ds(r0, n), sl]
        v = vn_scr[pl.ds(r0, n), sl]
        beta = pick(beta_ref, r0, sel_b)
        gc = pick(gc_ref, r0, sel_g)
        eg = pick(eg_ref, r0, sel_g)
        ek = pick(ek_ref, r0, sel_g)
        el = pick(el_ref, r0, sel_g)
        gcrow = gcrow_ref[hh, :, pl.ds(r0, n)]
        decay = jnp.where(causal, jnp.exp(gc - gcrow), 0.0)
        a_low = jnp.where(strict, beta * _dot_nt(k, k) * decay, 0.0)
        pw = [-a_low]
        for _ in range(GDN_LEVELS - 1):
            pw.append(_dot(pw[-1], pw[-1]))
        fs = [eye + pw[i] + pw[i + 1] + _dot(pw[i], pw[i + 1]) for i in range(0, GDN_LEVELS, 2)]
        while len(fs) > 1:
            fs = [_dot(fs[i], fs[i + 1]) for i in range(0, len(fs), 2)]
        rhs = jnp.concatenate([v * beta, k * (beta * eg)], axis=1)
        sol = _dot(fs[0], rhs)
        value = sol[:, :DN_DV]
        kcum = sol[:, DN_DV:]
        intra = _dot_nt(q, k) * decay
        q_dec = q * eg
        k_dec = k * ek
        for j in range(n // c):
            lo, hi = j * c, (j + 1) * c
            s = s_scr[hh]
            r = _dot(jnp.concatenate([kcum[lo:hi], q_dec[lo:hi]], axis=0), s)
            v_new = value[lo:hi] - r[:c]
            parts = []
            if lo:
                parts.append(jnp.zeros((lo, DN_DV), F32))
            parts.append(v_new)
            if hi < n:
                parts.append(jnp.zeros((n - hi, DN_DV), F32))
            o = r[c:] + _dot(intra[lo:hi], jnp.concatenate(parts, axis=0))
            oacc_scr[pl.ds(r0 + lo, c), sl] = o
            s_scr[hh] = s * el[lo:lo + 1] + _dot(k_dec[lo:hi].T, v_new)

    def body(i, carry):
        r0 = pl.multiple_of(i * n, n)
        for hh in range(GDN_HPS):
            head_group(hh, r0)
        return carry

    lax.fori_loop(0, t // n, body, 0)
    for hh in range(GDN_HPS):
        sl = slice(hh * LANE, (hh + 1) * LANE)
        o = oacc_scr[:, sl]
        y = o * lax.rsqrt(jnp.mean(o * o, axis=-1, keepdims=True) + EPS)
        o_ref[:, sl] = (y * nw_ref[...]) * _silu(z_ref[:, sl])
    s_ref[...] = s_scr[...]


def _gdn_prompt(proj3, gates, conv_w, norm_w):
    b, t, _ = proj3.shape
    beta, gc, eg, ek, el, gcrow = gates
    hps = GDN_HPS
    wdt = hps * LANE
    steps = DN_HEADS // hps
    tt = GDN_TILE
    assert t % tt == 0 and tt % GDN_GROUP == 0

    def colspec(base):
        return pl.BlockSpec((None, tt, wdt), lambda i, j, r, base=base: (i, r, base + j))

    gate = pl.BlockSpec((None, tt, LANE), lambda i, j, r: (i, r, 0))

    def cwspec(base):
        return pl.BlockSpec((DN_CONV, wdt), lambda i, j, r, base=base: (0, base + j))

    return pl.pallas_call(
        _gdn_kernel,
        grid=(b, steps, t // tt),
        in_specs=[
            colspec(0), colspec(steps), colspec(2 * steps), colspec(C_Z // wdt),
            gate, gate, gate, gate, gate,
            pl.BlockSpec((None, hps, 1, tt), lambda i, j, r: (i, j, 0, r)),
            cwspec(0), cwspec(steps), cwspec(2 * steps),
            pl.BlockSpec((1, DN_DV), lambda i, j, r: (0, 0)),
        ],
        out_specs=[
            pl.BlockSpec((None, tt, wdt), lambda i, j, r: (i, r, j)),
            pl.BlockSpec((None, hps, DN_DK, DN_DV), lambda i, j, r: (i, j, 0, 0)),
        ],
        out_shape=[
            jax.ShapeDtypeStruct((b, t, DN_VW), F32),
            jax.ShapeDtypeStruct((b, DN_HEADS, DN_DK, DN_DV), F32),
        ],
        scratch_shapes=[
            pltpu.VMEM((tt + 8, wdt), F32),
            pltpu.VMEM((3, 8, wdt), F32),
            pltpu.VMEM((tt, wdt), F32),
            pltpu.VMEM((tt, wdt), F32),
            pltpu.VMEM((tt, wdt), F32),
            pltpu.VMEM((tt, wdt), F32),
            pltpu.VMEM((hps, DN_DK, DN_DV), F32),
        ],
        compiler_params=_cp(("arbitrary", "arbitrary", "arbitrary")),
        name="gdn_prompt",
    )(proj3, proj3, proj3, proj3, beta, gc, eg, ek, el, gcrow, conv_w, conv_w, conv_w, norm_w.reshape(1, DN_DV))


def _gdn_step_one(p_ref, cprev_ref, s_ref, cw_ref, alog_ref, dtb_ref, nw_ref, o_ref, cnew_ref, snew_ref):
    u = p_ref[:, C_QKV:C_QKV + DN_CONV_CH]
    prev = cprev_ref[...]
    y = cw_ref[DN_CONV - 1:DN_CONV, :] * u
    for i in range(DN_CONV - 1):
        y = y + cw_ref[i:i + 1, :] * prev[i:i + 1, :]
    y = _silu(y)
    cnew_ref[0:DN_CONV - 2, :] = prev[1:DN_CONV - 1, :]
    cnew_ref[DN_CONV - 2:DN_CONV - 1, :] = u
    ba = p_ref[:, C_BA:C_BA + LANE]
    beta_l = _sigmoid(ba)
    a_l = jnp.exp(-jnp.exp(alog_ref[...]) * _softplus(ba + dtb_ref[...]))
    lane = lax.broadcasted_iota(jnp.int32, (1, LANE), 1)
    row8 = lax.broadcasted_iota(jnp.int32, (8, LANE), 0)
    for h in range(DN_HEADS):
        q = _l2norm(y[:, h * DN_DK:(h + 1) * DN_DK]) * (DN_DK ** -0.5)
        k = _l2norm(y[:, DN_QK + h * DN_DK:DN_QK + (h + 1) * DN_DK])
        v = y[:, 2 * DN_QK + h * DN_DV:2 * DN_QK + (h + 1) * DN_DV]
        beta = jnp.sum(jnp.where(lane == h, beta_l, 0.0), axis=-1, keepdims=True)
        a = jnp.sum(jnp.where(lane == h + DN_HEADS, a_l, 0.0), axis=-1, keepdims=True)
        s = s_ref[h]
        kq = jnp.where(row8 == 0, k, jnp.where(row8 == 1, q, 0.0)).T
        kcol, qcol = kq[:, 0:1], kq[:, 1:2]
        v_new = beta * (v - a * jnp.sum(s * kcol, axis=0, keepdims=True))
        o = a * jnp.sum(s * qcol, axis=0, keepdims=True) + jnp.sum(q * k, axis=-1, keepdims=True) * v_new
        snew_ref[h] = s * a + kcol * v_new
        yo = o * lax.rsqrt(jnp.mean(o * o, axis=-1, keepdims=True) + EPS)
        z = p_ref[:, C_Z + h * DN_DV:C_Z + (h + 1) * DN_DV]
        o_ref[:, h * DN_DV:(h + 1) * DN_DV] = (yo * nw_ref[...]) * _silu(z)


def _gdn_step_kernel(p_ref, cprev_ref, s_ref, cw_ref, alog_ref, dtb_ref, nw_ref, o_ref, cnew_ref, snew_ref):
    for bi in range(p_ref.shape[0]):
        _gdn_step_one(p_ref.at[bi], cprev_ref.at[bi], s_ref.at[bi], cw_ref, alog_ref, dtb_ref, nw_ref,
                      o_ref.at[bi], cnew_ref.at[bi], snew_ref.at[bi])


def _gdn_step(proj_s, conv_prev, s0, conv_w, alog_lane, dtb_lane, norm_w):
    b = proj_s.shape[0]
    sb = GDN_STEP_B
    assert b % sb == 0
    return pl.pallas_call(
        _gdn_step_kernel,
        grid=(b // sb,),
        in_specs=[
            pl.BlockSpec((sb, 1, PROJ_W), lambda i: (i, 0, 0)),
            pl.BlockSpec((sb, DN_CONV - 1, DN_CONV_CH), lambda i: (i, 0, 0)),
            pl.BlockSpec((sb, DN_HEADS, DN_DK, DN_DV), lambda i: (i, 0, 0, 0)),
            pl.BlockSpec((DN_CONV, DN_CONV_CH), lambda i: (0, 0)),
            pl.BlockSpec((1, LANE), lambda i: (0, 0)),
            pl.BlockSpec((1, LANE), lambda i: (0, 0)),
            pl.BlockSpec((1, DN_DV), lambda i: (0, 0)),
        ],
        out_specs=[
            pl.BlockSpec((sb, 1, DN_VW), lambda i: (i, 0, 0)),
            pl.BlockSpec((sb, DN_CONV - 1, DN_CONV_CH), lambda i: (i, 0, 0)),
            pl.BlockSpec((sb, DN_HEADS, DN_DK, DN_DV), lambda i: (i, 0, 0, 0)),
        ],
        out_shape=[
            jax.ShapeDtypeStruct((b, 1, DN_VW), F32),
            jax.ShapeDtypeStruct((b, DN_CONV - 1, DN_CONV_CH), F32),
            jax.ShapeDtypeStruct((b, DN_HEADS, DN_DK, DN_DV), F32),
        ],
        compiler_params=_cp(("arbitrary",)),
        name="gdn_step",
    )(proj_s.reshape(b, 1, PROJ_W), conv_prev, s0, conv_w, alog_lane, dtb_lane, norm_w.reshape(1, DN_DV))


def _alibi_slope(h):
    return float(2.0 ** (-8.0 * (h + 1) / SW_HEADS))


def _head_rms(x, w):
    return (x * lax.rsqrt(jnp.mean(x * x, axis=-1, keepdims=True) + EPS)) * w


SWA_HB = 4


def _swa_kernel(sinks_ref, q_ref, kc_ref, kp_ref, vc_ref, vp_ref, qw_ref, kw_ref, o_ref, kn_ref):
    blk = pl.program_id(1)
    w = WINDOW
    rows = SWA_HB * w
    qi = lax.broadcasted_iota(jnp.int32, (rows, 2 * w), 0)
    kj = lax.broadcasted_iota(jnp.int32, (rows, 2 * w), 1)
    dist = (qi & (w - 1)) + w - kj
    valid = (dist >= 0) & (dist < w) & ((kj >= w) | (blk > 0))
    distf = dist.astype(F32)
    stripe = lax.broadcasted_iota(jnp.int32, (rows, 1), 0) // w
    kc = kc_ref[...]
    kp = kp_ref[...]
    kbands, vbands = [], []
    for g in range(SW_KV_HEADS):
        sl = slice(g * SW_HD, (g + 1) * SW_HD)
        kcn = _head_rms(kc[:, sl], kw_ref[...])
        kn_ref[:, sl] = kcn
        kbands.append(jnp.concatenate([_head_rms(kp[:, sl], kw_ref[...]), kcn], axis=0))
        vbands.append(jnp.concatenate([vp_ref[:, sl], vc_ref[:, sl]], axis=0))
    for hb in range(SW_HEADS // SWA_HB):
        heads = range(hb * SWA_HB, (hb + 1) * SWA_HB)
        g = heads[0] // SW_GROUP
        qs = jnp.concatenate([_head_rms(q_ref[:, h * SW_HD:(h + 1) * SW_HD], qw_ref[...]) for h in heads], axis=0)
        slope = jnp.zeros((rows, 1), F32)
        sink = jnp.zeros((rows, 1), F32)
        for i, h in enumerate(heads):
            slope = jnp.where(stripe == i, _alibi_slope(h), slope)
            sink = jnp.where(stripe == i, sinks_ref[h], sink)
        s = _dot_nt(qs, kbands[g]) * (SW_HD ** -0.5) - slope * distf
        s = jnp.where(valid, s, -jnp.inf)
        m = jnp.maximum(jnp.max(s, axis=-1, keepdims=True), sink)
        p = jnp.exp(s - m)
        den = jnp.sum(p, axis=-1, keepdims=True) + jnp.exp(sink - m)
        o = _dot(p / den, vbands[g])
        for i, h in enumerate(heads):
            o_ref[:, h * SW_HD:(h + 1) * SW_HD] = o[i * w:(i + 1) * w]


def _swa_prompt(proj3, sinks, qw, kw):
    b, t, _ = proj3.shape
    nb = t // WINDOW
    kcol, vcol = C_SK // LANE, C_SV // LANE

    def cur(col):
        return pl.BlockSpec((None, WINDOW, SW_KVW), lambda i, j, s, col=col: (i, j, col))

    def prev(col):
        return pl.BlockSpec((None, WINDOW, SW_KVW), lambda i, j, s, col=col: (i, jnp.maximum(j - 1, 0), col))

    return pl.pallas_call(
        _swa_kernel,
        grid_spec=pltpu.PrefetchScalarGridSpec(
            num_scalar_prefetch=1,
            grid=(b, nb),
            in_specs=[
                pl.BlockSpec((None, WINDOW, SW_QW), lambda i, j, s: (i, j, C_SQ // SW_QW)),
                cur(kcol), prev(kcol), cur(vcol), prev(vcol),
                pl.BlockSpec((1, SW_HD), lambda i, j, s: (0, 0)),
                pl.BlockSpec((1, SW_HD), lambda i, j, s: (0, 0)),
            ],
            out_specs=[
                pl.BlockSpec((None, WINDOW, SW_QW), lambda i, j, s: (i, j, 0)),
                pl.BlockSpec((None, WINDOW, SW_KVW), lambda i, j, s: (i, j, 0)),
            ],
        ),
        out_shape=[
            jax.ShapeDtypeStruct((b, t, SW_QW), F32),
            jax.ShapeDtypeStruct((b, t, SW_KVW), F32),
        ],
        compiler_params=_cp(("arbitrary", "arbitrary")),
        name="swa_prompt",
    )(sinks, proj3, proj3, proj3, proj3, proj3, qw.reshape(1, SW_HD), kw.reshape(1, SW_HD))


def _swa_step_one(sinks_ref, p_ref, kbuf_ref, vbuf_ref, qw_ref, kw_ref, o_ref, knew_ref, vnew_ref, kcat, vcat):
    w = kbuf_ref.shape[0]
    rows = kcat.shape[0]
    knew = p_ref[:, C_SK:C_SK + SW_KVW]
    vnew = p_ref[:, C_SV:C_SV + SW_KVW]
    kcat[...] = jnp.zeros(kcat.shape, F32)
    vcat[...] = jnp.zeros(vcat.shape, F32)
    kcat[0:w, :] = kbuf_ref[...]
    vcat[0:w, :] = vbuf_ref[...]
    for g in range(SW_KV_HEADS):
        sl = slice(g * SW_HD, (g + 1) * SW_HD)
        kcat[w:w + 1, sl] = _head_rms(knew[:, sl], kw_ref[...])
    vcat[w:w + 1, :] = vnew
    knew_ref[...] = kcat[1:w + 1, :]
    vnew_ref[...] = vcat[1:w + 1, :]
    j = lax.broadcasted_iota(jnp.int32, (SW_GROUP, rows), 1)
    dist = w - j
    valid = (dist >= 0) & (dist < WINDOW)
    distf = dist.astype(F32)
    hrow = lax.broadcasted_iota(jnp.int32, (SW_GROUP, 1), 0)
    for g in range(SW_KV_HEADS):
        sl = slice(g * SW_HD, (g + 1) * SW_HD)
        qs = jnp.zeros((SW_GROUP, SW_HD), F32)
        slope = jnp.zeros((SW_GROUP, 1), F32)
        sink = jnp.zeros((SW_GROUP, 1), F32)
        for i in range(SW_GROUP):
            h = g * SW_GROUP + i
            qh = _head_rms(p_ref[:, C_SQ + h * SW_HD:C_SQ + (h + 1) * SW_HD], qw_ref[...])
            qs = jnp.where(hrow == i, qh, qs)
            slope = jnp.where(hrow == i, _alibi_slope(h), slope)
            sink = jnp.where(hrow == i, sinks_ref[h], sink)
        s = _dot_nt(qs, kcat[:, sl]) * (SW_HD ** -0.5) - slope * distf
        s = jnp.where(valid, s, -jnp.inf)
        m = jnp.maximum(jnp.max(s, axis=-1, keepdims=True), sink)
        p = jnp.exp(s - m)
        den = jnp.sum(p, axis=-1, keepdims=True) + jnp.exp(sink - m)
        o = _dot(p / den, vcat[:, sl])
        for i in range(SW_GROUP):
            h = g * SW_GROUP + i
            o_ref[:, h * SW_HD:(h + 1) * SW_HD] = o[i:i + 1]


def _swa_step_kernel(sinks_ref, p_ref, kbuf_ref, vbuf_ref, qw_ref, kw_ref, o_ref, knew_ref, vnew_ref, kcat, vcat):
    for bi in range(p_ref.shape[0]):
        _swa_step_one(sinks_ref, p_ref.at[bi], kbuf_ref.at[bi], vbuf_ref.at[bi], qw_ref, kw_ref,
                      o_ref.at[bi], knew_ref.at[bi], vnew_ref.at[bi], kcat.at[bi], vcat.at[bi])


def _swa_step(proj_s, kbuf, vbuf, sinks, qw, kw):
    b = proj_s.shape[0]
    w = kbuf.shape[1]
    sb = STEP_B
    assert b % sb == 0
    rows = 2 * w
    buf = pl.BlockSpec((sb, w, SW_KVW), lambda i, s: (i, 0, 0))
    return pl.pallas_call(
        _swa_step_kernel,
        grid_spec=pltpu.PrefetchScalarGridSpec(
            num_scalar_prefetch=1,
            grid=(b // sb,),
            in_specs=[
                pl.BlockSpec((sb, 1, PROJ_W), lambda i, s: (i, 0, 0)),
                buf, buf,
                pl.BlockSpec((1, SW_HD), lambda i, s: (0, 0)),
                pl.BlockSpec((1, SW_HD), lambda i, s: (0, 0)),
            ],
            out_specs=[pl.BlockSpec((sb, 1, SW_QW), lambda i, s: (i, 0, 0)), buf, buf],
            scratch_shapes=[pltpu.VMEM((sb, rows, SW_KVW), F32), pltpu.VMEM((sb, rows, SW_KVW), F32)],
        ),
        out_shape=[
            jax.ShapeDtypeStruct((b, 1, SW_QW), F32),
            jax.ShapeDtypeStruct((b, w, SW_KVW), F32),
            jax.ShapeDtypeStruct((b, w, SW_KVW), F32),
        ],
        compiler_params=_cp(("arbitrary",)),
        name="swa_step",
    )(sinks, proj_s.reshape(b, 1, PROJ_W), kbuf, vbuf, qw.reshape(1, SW_HD), kw.reshape(1, SW_HD))


def _route_top_k(hmod, rw_ref, rb_ref, idx_ref, w_ref):
    logits = _dot3(hmod, rw_ref[...]) + rb_ref[...]
    lane = lax.broadcasted_iota(jnp.int32, logits.shape, 1)
    cur = jnp.where(lane < N_EXPERTS, logits, -jnp.inf)
    vals, idxs = [], []
    for _ in range(TOP_K):
        m = jnp.max(cur, axis=-1, keepdims=True)
        ix = jnp.min(jnp.where(cur == m, lane, LANE), axis=-1, keepdims=True)
        vals.append(m)
        idxs.append(ix)
        cur = jnp.where(lane == ix, -jnp.inf, cur)
    es = [jnp.exp(v - vals[0]) for v in vals]
    den = es[0] + es[1] + es[2] + es[3]
    idx_out = jnp.zeros(logits.shape, jnp.int32)
    w_out = jnp.zeros(logits.shape, F32)
    for k in range(TOP_K):
        idx_out = jnp.where(lane == k, idxs[k], idx_out)
        w_out = jnp.where(lane == k, es[k] / den, w_out)
    idx_ref[...] = idx_out
    w_ref[...] = w_out


def _post_kernel(ya_ref, yb_ref, ga0_ref, ga1_ref, gb0_ref, gb1_ref, x_ref, gt_ref, lnw_ref, sc_ref, sh_ref,
                 wa_ref, wb_ref, wo_ref, rw_ref, rb_ref, x1_ref, h_ref, idx_ref, w_ref):
    a = _dot(ya_ref[...], wa_ref[...])
    b = _dot(yb_ref[...], wb_ref[...])
    ga = jnp.concatenate([ga0_ref[...], ga1_ref[...]], axis=1)
    gb = jnp.concatenate([gb0_ref[...], gb1_ref[...]], axis=1)
    merged = _sigmoid(ga) * a + _sigmoid(gb) * b
    x1 = x_ref[...] + gt_ref[...] * _dot(merged, wo_ref[...])
    x1_ref[...] = x1
    hmod = _norm_mod(x1, lnw_ref[...], sc_ref[...], sh_ref[...])
    h_ref[...] = hmod
    _route_top_k(hmod, rw_ref, rb_ref, idx_ref, w_ref)


def _post_attention(ya, yb, proj, x2d, gt, lnw, sc, sh, wa, wb, wo, rw_pad, rb_pad, rows_per_mod, tm):
    m = x2d.shape[0]
    half = D_MODEL // 2
    assert C_GA % half == 0 and C_GB % half == 0
    if rows_per_mod == 1:
        mod_spec = pl.BlockSpec((tm, D_MODEL), lambda i: (i, 0))
        gt, sc, sh = (v.reshape(m, D_MODEL) for v in (gt, sc, sh))
    else:
        mod_spec = pl.BlockSpec((None, 1, D_MODEL), lambda i: (i // (rows_per_mod // tm), 0, 0))
    row = pl.BlockSpec((tm, D_MODEL), lambda i: (i, 0))
    small = pl.BlockSpec((tm, LANE), lambda i: (i, 0))

    def gate(col):
        return pl.BlockSpec((tm, half), lambda i, col=col: (i, col))

    def resident(shape):
        return pl.BlockSpec(shape, lambda i: (0, 0), pipeline_mode=pl.Buffered(1))

    return pl.pallas_call(
        _post_kernel,
        grid=(m // tm,),
        in_specs=[
            pl.BlockSpec((tm, DN_VW), lambda i: (i, 0)),
            pl.BlockSpec((tm, SW_QW), lambda i: (i, 0)),
            gate(C_GA // half), gate(C_GA // half + 1), gate(C_GB // half), gate(C_GB // half + 1),
            row, mod_spec,
            pl.BlockSpec((1, D_MODEL), lambda i: (0, 0)),
            mod_spec, mod_spec,
            resident((DN_VW, D_MODEL)), resident((SW_QW, D_MODEL)), resident((D_MODEL, D_MODEL)),
            pl.BlockSpec((D_MODEL, LANE), lambda i: (0, 0)),
            pl.BlockSpec((1, LANE), lambda i: (0, 0)),
        ],
        out_specs=[row, row, small, small],
        out_shape=[
            jax.ShapeDtypeStruct((m, D_MODEL), F32),
            jax.ShapeDtypeStruct((m, D_MODEL), F32),
            jax.ShapeDtypeStruct((m, LANE), jnp.int32),
            jax.ShapeDtypeStruct((m, LANE), F32),
        ],
        compiler_params=_cp(("arbitrary",)),
        name="post_attention",
    )(ya, yb, proj, proj, proj, proj, x2d, gt, lnw.reshape(1, D_MODEL), sc, sh, wa, wb, wo, rw_pad, rb_pad)


SCATTER_TOK = 512
DMA_ROWS = 16


def _scatter_kernel(zl_ref, dest_ref, hp_ref, hs_ref, xs_hbm, zbuf, sem, zsem):
    i = pl.program_id(0)
    rb = MOE_ROWS
    n_prompt_steps = pl.num_programs(0) - 1

    @pl.when(i == 0)
    def _():
        zbuf[...] = jnp.zeros(zbuf.shape, F32)

        def zero_copy(n):
            return pltpu.make_async_copy(zbuf, xs_hbm.at[pl.ds(zl_ref[n] * rb, rb)], zsem)

        def start(n, c):
            @pl.when(zl_ref[n] >= 0)
            def _():
                zero_copy(n).start()
            return c

        def wait(n, c):
            @pl.when(zl_ref[n] >= 0)
            def _():
                zero_copy(n).wait()
            return c

        lax.fori_loop(0, zl_ref.shape[0], start, 0)
        lax.fori_loop(0, zl_ref.shape[0], wait, 0)

    def scatter(src_ref):
        def group(g, c):
            for r in range(SUBLANE):
                for k in range(TOP_K):
                    dst = dest_ref[0, g * (SUBLANE * TOP_K) + (r * TOP_K + k)]
                    pltpu.make_async_copy(src_ref.at[g, pl.ds(r, 1)], xs_hbm.at[pl.ds(dst, 1)], sem).start()
            return c

        lax.fori_loop(0, src_ref.shape[0], group, 0)
        for _ in range(TOP_K):
            pltpu.make_async_copy(src_ref, src_ref, sem).wait()

    @pl.when(i < n_prompt_steps)
    def _():
        scatter(hp_ref)

    @pl.when(i == n_prompt_steps)
    def _():
        scatter(hs_ref)


def _scatter_rows(h_p, h_s, dest, zero_blocks, n_rows):
    assert TOP_K == 4
    n_p, n_s = h_p.shape[0], h_s.shape[0]
    steps_p = n_p // SCATTER_TOK
    per = SCATTER_TOK * TOP_K
    dest_s = jnp.concatenate([dest[n_p * TOP_K:], jnp.zeros((per - n_s * TOP_K,), jnp.int32)])
    dest3 = jnp.concatenate([dest[:n_p * TOP_K], dest_s]).reshape(steps_p + 1, 1, per)
    return pl.pallas_call(
        _scatter_kernel,
        grid_spec=pltpu.PrefetchScalarGridSpec(
            num_scalar_prefetch=1,
            grid=(steps_p + 1,),
            in_specs=[
                pl.BlockSpec((None, 1, per), lambda i, zl: (i, 0, 0), memory_space=pltpu.SMEM),
                pl.BlockSpec((SCATTER_TOK // SUBLANE, SUBLANE, D_MODEL), lambda i, zl: (jnp.minimum(i, steps_p - 1), 0, 0)),
                pl.BlockSpec((n_s // SUBLANE, SUBLANE, D_MODEL), lambda i, zl: (0, 0, 0)),
            ],
            out_specs=pl.BlockSpec(memory_space=pl.ANY),
            scratch_shapes=[
                pltpu.VMEM((MOE_ROWS, D_MODEL), F32),
                pltpu.SemaphoreType.DMA(()),
                pltpu.SemaphoreType.DMA(()),
            ],
        ),
        out_shape=jax.ShapeDtypeStruct((n_rows, D_MODEL), F32),
        compiler_params=_cp(("arbitrary",)),
        name="moe_scatter",
    )(zero_blocks, dest3, h_p.reshape(n_p // SUBLANE, SUBLANE, D_MODEL), h_s.reshape(n_s // SUBLANE, SUBLANE, D_MODEL))


def _experts_kernel(sbe_ref, sbb_ref, sbn_ref, tail_ref, xs_hbm, wg_ref, wl_ref, wd_ref, bg_ref, bl_ref, bd_ref,
                    ys_hbm, xf_scr, xb_scr, acc_scr, sem_in, sem_out):
    s = pl.program_id(0)
    j = pl.program_id(1)
    last_s = pl.num_programs(0) - 1
    last_j = pl.num_programs(1) - 1
    nblk = sbn_ref[s]
    blk0 = sbb_ref[s]
    rb = MOE_ROWS

    def in_copy(first_blk, b):
        return pltpu.make_async_copy(xs_hbm.at[pl.ds((first_blk + b) * rb, rb)], xf_scr.at[pl.ds(b * rb, rb)], sem_in)

    def out_copy(first_blk, b):
        return pltpu.make_async_copy(acc_scr.at[pl.ds(b * rb, rb)], ys_hbm.at[pl.ds((first_blk + b) * rb, rb)], sem_out)

    def loop(n, fn):
        def body(b, c):
            fn(b)
            return c
        lax.fori_loop(0, n, body, 0)

    @pl.when(j == 0)
    def _():
        @pl.when(s == 0)
        def _():
            loop(nblk, lambda b: in_copy(blk0, b).start())

        loop(nblk, lambda b: in_copy(blk0, b).wait())

        @pl.when(s > 0)
        def _():
            prev0 = sbb_ref[s - 1]
            loop(sbn_ref[s - 1], lambda b: out_copy(prev0, b).wait())

        def cast(b):
            r0 = pl.multiple_of(b * rb, rb)
            xb_scr[pl.ds(r0, rb), :] = xf_scr[pl.ds(r0, rb), :].astype(BF16)
            acc_scr[pl.ds(r0, rb), :] = jnp.broadcast_to(bd_ref[...], (rb, D_MODEL))
        loop(nblk, cast)

        @pl.when(s < last_s)
        def _():
            nxt0 = sbb_ref[s + 1]
            loop(sbn_ref[s + 1], lambda b: in_copy(nxt0, b).start())

    @pl.when(nblk > 0)
    def _():
        def mlp(b0, nb):
            rows = nb * rb
            r0 = pl.multiple_of(b0 * rb, rb)
            x = xb_scr[pl.ds(r0, rows), :]
            glu = jnp.dot(x, wg_ref[...].astype(BF16), preferred_element_type=F32) + bg_ref[...]
            lin = jnp.dot(x, wl_ref[...].astype(BF16), preferred_element_type=F32) + bl_ref[...]
            glu = jnp.minimum(glu, SWIGLU_LIMIT)
            lin = jnp.clip(lin, -SWIGLU_LIMIT, SWIGLU_LIMIT)
            act = glu * _sigmoid(SWIGLU_ALPHA * glu) * (lin + 1.0)
            acc_scr[pl.ds(r0, rows), :] += jnp.dot(act.astype(BF16), wd_ref[...].astype(BF16),
                                                   preferred_element_type=F32)

            @pl.when(j == last_j)
            def _():
                for b in range(nb):
                    out_copy(blk0, b0 + b).start()

        full = nblk // MOE_PASS_BLOCKS
        loop(full, lambda p: mlp(p * MOE_PASS_BLOCKS, MOE_PASS_BLOCKS))
        done = full * MOE_PASS_BLOCKS
        part = MOE_PASS_BLOCKS // 2
        while part >= 1:
            take = ((nblk - done) // part) * part

            @pl.when(take > 0)
            def _(done=done, part=part):
                mlp(done, part)

            done = done + take
            part //= 2

    @pl.when((s == last_s) & (j == last_j))
    def _():
        loop(nblk, lambda b: out_copy(blk0, b).wait())
        acc_scr[0:rb, :] = jnp.zeros((rb, D_MODEL), F32)

        def zero_copy(b):
            return pltpu.make_async_copy(acc_scr.at[pl.ds(0, rb)], ys_hbm.at[pl.ds(b * rb, rb)], sem_out)

        def start(b, c):
            zero_copy(b).start()
            return c

        def wait(b, c):
            zero_copy(b).wait()
            return c

        lax.fori_loop(tail_ref[0], tail_ref[1], start, 0)
        lax.fori_loop(tail_ref[0], tail_ref[1], wait, 0)


def _experts(xs, sb_e, sb_blk0, sb_nblk, tail, w_gate_up, b_gate_up, w_down, b_down):
    n_rows = xs.shape[0]
    n_sb = sb_e.shape[0]
    tf = MOE_TF
    nj = D_MODEL // tf
    rmax = MOE_SB_BLOCKS * MOE_ROWS

    def jj(s, j, n):
        return jnp.where(n[s] > 0, j, nj - 1)

    return pl.pallas_call(
        _experts_kernel,
        grid_spec=pltpu.PrefetchScalarGridSpec(
            num_scalar_prefetch=4,
            grid=(n_sb, nj),
            in_specs=[
                pl.BlockSpec(memory_space=pl.ANY),
                pl.BlockSpec((None, D_MODEL, tf), lambda s, j, e, b, n, tl: (e[s], 0, jj(s, j, n))),
                pl.BlockSpec((None, D_MODEL, tf), lambda s, j, e, b, n, tl: (e[s], 0, nj + jj(s, j, n))),
                pl.BlockSpec((None, tf, D_MODEL), lambda s, j, e, b, n, tl: (e[s], jj(s, j, n), 0)),
                pl.BlockSpec((None, 1, tf), lambda s, j, e, b, n, tl: (e[s], 0, jj(s, j, n))),
                pl.BlockSpec((None, 1, tf), lambda s, j, e, b, n, tl: (e[s], 0, nj + jj(s, j, n))),
                pl.BlockSpec((None, 1, D_MODEL), lambda s, j, e, b, n, tl: (e[s], 0, 0)),
            ],
            out_specs=pl.BlockSpec(memory_space=pl.ANY),
            scratch_shapes=[
                pltpu.VMEM((rmax, D_MODEL), F32),
                pltpu.VMEM((rmax, D_MODEL), BF16),
                pltpu.VMEM((rmax, D_MODEL), F32),
                pltpu.SemaphoreType.DMA(()),
                pltpu.SemaphoreType.DMA(()),
            ],
        ),
        out_shape=jax.ShapeDtypeStruct((n_rows, D_MODEL), F32),
        compiler_params=_cp(("arbitrary", "arbitrary")),
        name="moe_experts",
    )(sb_e, sb_blk0, sb_nblk, tail, xs, w_gate_up, w_gate_up, w_down,
      b_gate_up.reshape(N_EXPERTS, 1, 2 * D_MODEL), b_gate_up.reshape(N_EXPERTS, 1, 2 * D_MODEL),
      b_down.reshape(N_EXPERTS, 1, D_MODEL))


COMBINE_TOK = 256


def _combine_kernel(pos_ref, posn_ref, ys_hbm, x_ref, gt_ref, w_ref, o_ref, buf0, buf1, sem):
    tm = x_ref.shape[0]
    n = tm * TOP_K
    i = pl.program_id(0)
    slot = lax.rem(i, 2)
    bufs = (buf0, buf1)

    def issue(p_ref, sl):
        def group(g, c):
            for r in range(DMA_ROWS):
                src = ys_hbm.at[pl.ds(p_ref[0, g * DMA_ROWS + r], 1)]
                dst = bufs[sl].at[g * (DMA_ROWS // SUBLANE) + r // SUBLANE, pl.ds(r % SUBLANE, 1)]
                pltpu.make_async_copy(src, dst, sem.at[sl]).start()
            return c
        lax.fori_loop(0, n // DMA_ROWS, group, 0)

    @pl.when(i == 0)
    def _():
        issue(pos_ref, 0)

    for sl in range(2):
        @pl.when((i + 1 < pl.num_programs(0)) & (slot == 1 - sl))
        def _(sl=sl):
            issue(posn_ref, sl)

    for sl in range(2):
        @pl.when(slot == sl)
        def _(sl=sl):
            buf = bufs[sl]
            pltpu.make_async_copy(buf, buf, sem.at[sl]).wait()
            w = w_ref[...]
            lane = lax.broadcasted_iota(jnp.int32, w.shape, 1)
            y = jnp.zeros((tm, D_MODEL), F32)
            tiles = tm // SUBLANE
            for k in range(TOP_K):
                wk = jnp.sum(jnp.where(lane == k, w, 0.0), axis=-1, keepdims=True)
                y = y + wk * buf[k * tiles:(k + 1) * tiles].reshape(tm, D_MODEL)
            o_ref[...] = x_ref[...] + gt_ref[...] * y


def _combine(ys, pos, x2d, gt, top_w, rows_per_mod, tm):
    m = x2d.shape[0]
    steps = m // tm
    pos_kmajor = _kmajor(pos, tm)
    if rows_per_mod == 1:
        gt = gt.reshape(m, D_MODEL)
        gt_spec = pl.BlockSpec((tm, D_MODEL), lambda i: (i, 0))
    else:
        gt_spec = pl.BlockSpec((None, 1, D_MODEL), lambda i: (i // (rows_per_mod // tm), 0, 0))
    return pl.pallas_call(
        _combine_kernel,
        grid=(steps,),
        in_specs=[
            pl.BlockSpec((None, 1, TOP_K * tm), lambda i: (i, 0, 0), memory_space=pltpu.SMEM),
            pl.BlockSpec((None, 1, TOP_K * tm), lambda i: (jnp.minimum(i + 1, steps - 1), 0, 0), memory_space=pltpu.SMEM),
            pl.BlockSpec(memory_space=pl.ANY),
            pl.BlockSpec((tm, D_MODEL), lambda i: (i, 0)),
            gt_spec,
            pl.BlockSpec((tm, LANE), lambda i: (i, 0)),
        ],
        out_specs=pl.BlockSpec((tm, D_MODEL), lambda i: (i, 0)),
        out_shape=jax.ShapeDtypeStruct((m, D_MODEL), F32),
        scratch_shapes=[pltpu.VMEM((TOP_K * tm // SUBLANE, SUBLANE, D_MODEL), F32),
                        pltpu.VMEM((TOP_K * tm // SUBLANE, SUBLANE, D_MODEL), F32),
                        pltpu.SemaphoreType.DMA((2,))],
        compiler_params=_cp(("arbitrary",)),
        name="moe_combine",
    )(pos_kmajor, pos_kmajor, ys, x2d, gt, top_w)


def _routing_tables(top_idx):
    n_tok = top_idx.shape[0]
    n_assign = n_tok * TOP_K
    rb = MOE_ROWS
    n_blocks = -(-(n_assign + N_EXPERTS * (rb - 1)) // rb)
    n_rows = n_blocks * rb
    flat_e = top_idx.reshape(-1)
    onehot = (flat_e[:, None] == jnp.arange(N_EXPERTS, dtype=jnp.int32)[None, :]).astype(jnp.int32)
    csum = jnp.cumsum(onehot, axis=0)
    rank = jnp.sum((csum - onehot) * onehot, axis=1)
    counts = csum[-1]
    nblk_e = (counts + rb - 1) // rb
    blk_start = jnp.cumsum(nblk_e) - nblk_e
    dest = (blk_start * rb)[flat_e] + rank
    total_blk = jnp.sum(nblk_e)
    last_blk = jnp.where(nblk_e > 0, blk_start + nblk_e - 1, -1)
    bidx = jnp.arange(n_blocks, dtype=jnp.int32)
    zero_blocks = jnp.concatenate([last_blk, jnp.where(bidx >= total_blk, bidx, -1)]).astype(jnp.int32)
    n_sb_max = (n_blocks + N_EXPERTS * (MOE_SB_BLOCKS - 1)) // MOE_SB_BLOCKS
    sb_per_e = (nblk_e + MOE_SB_BLOCKS - 1) // MOE_SB_BLOCKS
    sb_start = jnp.cumsum(sb_per_e) - sb_per_e
    total_sb = jnp.sum(sb_per_e)
    sidx = jnp.arange(n_sb_max, dtype=jnp.int32)
    sb_end = jnp.cumsum(sb_per_e)
    e_of = jnp.minimum(jnp.sum((sb_end[None, :] <= sidx[:, None]).astype(jnp.int32), axis=1), N_EXPERTS - 1)
    local = sidx - sb_start[e_of]
    active = sidx < total_sb
    last_e = e_of[jnp.maximum(total_sb - 1, 0)]
    sb_e = jnp.where(active, e_of, last_e).astype(jnp.int32)
    sb_blk0 = jnp.where(active, blk_start[e_of] + local * MOE_SB_BLOCKS, 0).astype(jnp.int32)
    sb_nblk = jnp.where(active, jnp.minimum(nblk_e[e_of] - local * MOE_SB_BLOCKS, MOE_SB_BLOCKS), 0).astype(jnp.int32)
    tail = jnp.stack([total_blk, jnp.int32(n_blocks)]).astype(jnp.int32)
    return dest.astype(jnp.int32), zero_blocks, n_rows, sb_e, sb_blk0, sb_nblk, tail


def _kmajor(pos, tm):
    m = pos.shape[0]
    return pos.reshape(m // tm, tm, TOP_K).transpose(0, 2, 1).reshape(m // tm, 1, TOP_K * tm)


def _repack_w_in(w_in):
    a = DN_CONV_CH + DN_VW
    b = a + 2 * DN_HEADS
    c = b + SW_QW
    e = c + 2 * SW_KVW
    parts = [w_in[:, :a], w_in[:, b:c], w_in[:, e:], w_in[:, c:e], w_in[:, a:b]]
    pad = jnp.zeros((D_MODEL, PROJ_W - w_in.shape[1]), BF16)
    return jnp.concatenate([p.astype(BF16) for p in parts] + [pad], axis=1)


def _lane_vec(v, offset):
    return jnp.zeros((1, LANE), F32).at[0, offset:offset + v.shape[0]].set(v.astype(F32))


def kernel(x_prompt, x_sample, state_conv, state_delta, cache_swa_k, cache_swa_v, c_prompt, c_sample, w_ada, b_ada, ln1_w, w_in, conv_w, dn_a_log, dn_dt_bias, dn_norm_w, sw_q_norm_w, sw_k_norm_w, sw_sinks, w_branch_a, w_branch_b, w_out, ln2_w, router_w, router_b, w_gate_up, b_gate_up, w_down, b_down):
    assert w_ada.shape[0] == 1, "single-layer step"
    bp, t, d = x_prompt.shape
    bs = x_sample.shape[0]
    np_tok = bp * t
    l = 0

    n_c = bp + bs
    c_all = jnp.concatenate([c_prompt, c_sample, jnp.zeros((-n_c % 8, d), F32)], axis=0)
    mod = _ada_mod(c_all, w_ada[l], b_ada[l])
    mods_p = [m.reshape(bp, 1, d) for m in jnp.split(mod[:bp], 6, axis=-1)]
    mods_s = [m.reshape(bs, 1, d) for m in jnp.split(mod[bp:n_c], 6, axis=-1)]

    w_in_r = _repack_w_in(w_in[l])
    wa, wb, wo = w_branch_a[l].astype(BF16), w_branch_b[l].astype(BF16), w_out[l].astype(BF16)
    alog_lane = _lane_vec(dn_a_log[l], DN_HEADS)
    dtb_lane = _lane_vec(dn_dt_bias[l], DN_HEADS)
    rw_pad = jnp.zeros((d, LANE), F32).at[:, :N_EXPERTS].set(router_w[l])
    rb_pad = jnp.zeros((1, LANE), F32).at[0, :N_EXPERTS].set(router_b[l])
    sinks = sw_sinks[l].astype(F32)

    xp = x_prompt.reshape(np_tok, d)
    proj_p = _in_proj(xp, ln1_w[l], mods_p[1], mods_p[0], w_in_r, t, 1024)
    proj3 = proj_p.reshape(bp, t, PROJ_W)
    gates = _gdn_gates(proj3, alog_lane, dtb_lane)
    ya_p, delta_p = _gdn_prompt(proj3, gates, conv_w[l], dn_norm_w[l])
    yb_p, kn_p = _swa_prompt(proj3, sinks, sw_q_norm_w[l], sw_k_norm_w[l])
    x1_p, h2_p, idx_p, tw_p = _post_attention(
        ya_p.reshape(np_tok, DN_VW), yb_p.reshape(np_tok, SW_QW), proj_p, xp, mods_p[2], ln2_w[l], mods_p[4], mods_p[3],
        wa, wb, wo, rw_pad, rb_pad, t, POST_TM)

    xs_ = x_sample.reshape(bs, d)
    proj_s = _in_proj(xs_, ln1_w[l], mods_s[1], mods_s[0], w_in_r, 1, bs)
    ya_s, conv_s, delta_s = _gdn_step(proj_s, state_conv[l], state_delta[l], conv_w[l], alog_lane, dtb_lane, dn_norm_w[l])
    w_buf = cache_swa_k.shape[2]
    yb_s, k_s, v_s = _swa_step(proj_s, cache_swa_k[l].reshape(bs, w_buf, SW_KVW), cache_swa_v[l].reshape(bs, w_buf, SW_KVW),
                               sinks, sw_q_norm_w[l], sw_k_norm_w[l])
    x1_s, h2_s, idx_s, tw_s = _post_attention(
        ya_s.reshape(bs, DN_VW), yb_s.reshape(bs, SW_QW), proj_s, xs_, mods_s[2], ln2_w[l], mods_s[4], mods_s[3],
        wa, wb, wo, rw_pad, rb_pad, 1, bs)

    top_idx = jnp.concatenate([idx_p[:, :TOP_K], idx_s[:, :TOP_K]], axis=0)
    dest, zero_blocks, n_rows, sb_e, sb_blk0, sb_nblk, tail = _routing_tables(top_idx)
    xs_sorted = _scatter_rows(h2_p, h2_s, dest, zero_blocks, n_rows)
    ys = _experts(xs_sorted, sb_e, sb_blk0, sb_nblk, tail, w_gate_up[l], b_gate_up[l], w_down[l], b_down[l])
    pos = dest.reshape(np_tok + bs, TOP_K)
    y_p = _combine(ys, pos[:np_tok], x1_p, mods_p[5], tw_p, t, COMBINE_TOK)
    y_s = _combine(ys, pos[np_tok:], x1_s, mods_s[5], tw_s, 1, bs)

    conv_p = proj3[:, t - (DN_CONV - 1):, C_QKV:C_QKV + DN_CONV_CH]
    kp_out = kn_p[:, t - WINDOW:].reshape(bp, WINDOW, SW_KV_HEADS, SW_HD)
    vp_out = proj3[:, t - WINDOW:, C_SV:C_SV + SW_KVW].reshape(bp, WINDOW, SW_KV_HEADS, SW_HD)
    return (
        y_p.reshape(bp, t, d),
        y_s.reshape(bs, 1, d),
        conv_p[None],
        conv_s[None],
        delta_p[None],
        delta_s[None],
        kp_out[None],
        k_s.reshape(bs, w_buf, SW_KV_HEADS, SW_HD)[None],
        vp_out[None],
        v_s.reshape(bs, w_buf, SW_KV_HEADS, SW_HD)[None],
    )
```
